```python
import math
import jax, jax.numpy as jnp
from jax import lax
import numpy as np

D_MODEL = 2048
BATCH = 4
SEQ = 2048
DEPTH = 1
DEC_BATCH = 16
DEC_SEQ = 16
PAST_LEN = 2048

CHUNK = 64
N_META = 16
PROMPT_PAD = (-N_META) % CHUNK
RW_HEAD = 64
RW_DIM = D_MODEL // 2
RW_HEADS = RW_DIM // RW_HEAD
RW_DECAY_LORA = 96
RW_AAA_LORA = 96
RW_GATE_LORA = 256
RW_SHIFT_COLS = 3 * RW_DIM + RW_DECAY_LORA + RW_AAA_LORA + RW_GATE_LORA
RW_SPLITS = (RW_DIM, 2 * RW_DIM, 3 * RW_DIM, 3 * RW_DIM + RW_DECAY_LORA, 3 * RW_DIM + RW_DECAY_LORA + RW_AAA_LORA)
RW_GN_EPS = 64e-5
SSM_HEAD = 64
SSM_DIM = D_MODEL
SSM_HEADS = SSM_DIM // SSM_HEAD
SSM_GROUPS = 4
SSM_HPG = SSM_HEADS // SSM_GROUPS
SSM_STATE = 128
CONV_W = 4
CONV_DIM = SSM_DIM + 2 * SSM_GROUPS * SSM_STATE
RMS_EPS = 1e-5
N_IN = RW_SHIFT_COLS + SSM_DIM + CONV_DIM + SSM_HEADS + 2 * D_MODEL
IN_SPLITS = (RW_SHIFT_COLS, RW_SHIFT_COLS + SSM_DIM, RW_SHIFT_COLS + SSM_DIM + CONV_DIM, RW_SHIFT_COLS + SSM_DIM + CONV_DIM + SSM_HEADS)
D_FF = 5632
LN_EPS = 1e-5
ALPHA = (2 * DEPTH) ** 0.25
BETA = (8 * DEPTH) ** -0.25

kernel_name = 'rwkv7_mamba2_gated_hybrid_stream_step'


def layer_norm(x, g, b):
    xf = x.astype(jnp.float32)
    mu = jnp.mean(xf, axis=-1, keepdims=True)
    var = jnp.mean(jnp.square(xf - mu), axis=-1, keepdims=True)
    return ((xf - mu) * lax.rsqrt(var + LN_EPS) * g + b).astype(x.dtype)


def swiglu(x, w_gu, w_dn):
    g, u = jnp.split(x @ w_gu, 2, axis=-1)
    return (jax.nn.silu(g) * u) @ w_dn


def token_shift(p, hist, mu):
    prev = jnp.concatenate([hist.astype(p.dtype), p[:, :-1]], axis=1)
    return p + (prev - p) * mu, p[:, -1:]


def causal_dwconv(u, hist, w, b):
    l = u.shape[1]
    full = jnp.concatenate([hist.astype(u.dtype), u], axis=1)
    out = b + sum(full[:, i:i + l] * w[i] for i in range(CONV_W))
    return out, full[:, l:]


def rwkv7_mix(p_rw, shift_hist, wkv0, rw_mu, rw_w0, rw_w2, rw_a0, rw_a2, rw_g2, rw_kk, rw_ka, rw_rk, rw_gn_w, rw_gn_b):
    f32 = jnp.float32
    ps, new_shift = token_shift(p_rw, shift_hist, rw_mu)
    r, k, v, wd, ad, gd = jnp.split(ps, RW_SPLITS, axis=-1)
    b, l = r.shape[:2]
    w_log = -jax.nn.softplus(-(rw_w0 + jnp.tanh(wd) @ rw_w2).astype(f32)) - 0.5
    decay = jnp.exp(-jnp.exp(w_log))
    a = jax.nn.sigmoid((rw_a0 + ad @ rw_a2).astype(f32))
    g = jax.nn.sigmoid(gd) @ rw_g2
    hd = lambda t: t.reshape(b, l, RW_HEADS, RW_HEAD).astype(f32)
    r, k, v, decay, a = hd(r), hd(k), hd(v), hd(decay), hd(a)
    kk = k * rw_kk
    kk = kk / jnp.maximum(jnp.linalg.norm(kk, axis=-1, keepdims=True), 1e-12)
    k = k * (1.0 + (a - 1.0) * rw_ka)

    def step(S, inp):
        r_t, w_t, k_t, v_t, kk_t, a_t = inp
        sa = jnp.einsum('bhvk,bhk->bhv', S, -kk_t)
        S = S * w_t[:, :, None, :] + sa[..., None] * (kk_t * a_t)[:, :, None, :] + v_t[..., None] * k_t[:, :, None, :]
        return S, jnp.einsum('bhvk,bhk->bhv', S, r_t)

    tm = lambda t: jnp.moveaxis(t, 1, 0)
    wkv_t, y = lax.scan(step, wkv0.astype(f32), (tm(r), tm(decay), tm(k), tm(v), tm(kk), tm(a)))
    y = jnp.moveaxis(y, 0, 1)
    mu = jnp.mean(y, axis=-1, keepdims=True)
    var = jnp.mean(jnp.square(y - mu), axis=-1, keepdims=True)
    yn = ((y - mu) * lax.rsqrt(var + RW_GN_EPS)).reshape(b, l, RW_DIM) * rw_gn_w + rw_gn_b
    bonus = (jnp.sum(r * k * rw_rk, axis=-1, keepdims=True) * v).reshape(b, l, RW_DIM)
    out = ((yn + bonus) * g).astype(p_rw.dtype)
    return out, new_shift, wkv_t.astype(wkv0.dtype)


def ssd_scan(x, dt, a, bm, cm, h0, chunk):
    b, l = x.shape[:2]
    nc = l // chunk
    x = x.reshape(b, nc, chunk, SSM_GROUPS, SSM_HPG, SSM_HEAD)
    dt = dt.reshape(b, nc, chunk, SSM_GROUPS, SSM_HPG)
    bm = bm.reshape(b, nc, chunk, SSM_GROUPS, SSM_STATE)
    cm = cm.reshape(b, nc, chunk, SSM_GROUPS, SSM_STATE)
    acs = jnp.cumsum(dt * a.reshape(SSM_GROUPS, SSM_HPG), axis=2)
    xdt = x * dt[..., None]
    causal = jnp.tril(jnp.ones((chunk, chunk), dtype=bool))[:, :, None, None]
    seg = acs[:, :, :, None] - acs[:, :, None, :]
    lmat = jnp.exp(jnp.where(causal, seg, -jnp.inf))
    cb = jnp.einsum('bcign,bcjgn->bcijg', cm, bm)
    y_diag = jnp.einsum('bcijgh,bcjghp->bcighp', cb[..., None] * lmat, xdt)
    to_end = jnp.exp(acs[:, :, -1:] - acs)
    states = jnp.einsum('bcjgn,bcjghp->bcghpn', bm, xdt * to_end[..., None])
    chunk_decay = jnp.exp(acs[:, :, -1])

    def carry(h, inp):
        st, dec = inp
        return h * dec[..., None, None] + st, h

    h0g = h0.reshape(b, SSM_GROUPS, SSM_HPG, SSM_HEAD, SSM_STATE)
    h_t, h_in = lax.scan(carry, h0g, (jnp.moveaxis(states, 1, 0), jnp.moveaxis(chunk_decay, 1, 0)))
    h_in = jnp.moveaxis(h_in, 0, 1)
    y_off = jnp.einsum('bcign,bcghpn->bcighp', cm, h_in) * jnp.exp(acs)[..., None]
    y = (y_diag + y_off).reshape(b, l, SSM_HEADS, SSM_HEAD)
    return y, h_t.reshape(b, SSM_HEADS, SSM_HEAD, SSM_STATE)


def mamba2_mix(p_z, p_xbc, p_dt, conv_hist, ssm0, pad, chunk, conv_w, conv_b, dt_bias, a_log, d_skip, ssm_norm_w):
    f32 = jnp.float32
    xbc, new_conv = causal_dwconv(p_xbc, conv_hist, conv_w, conv_b)
    xbc = jax.nn.silu(xbc)
    xs, bm, cm = jnp.split(xbc, (SSM_DIM, SSM_DIM + SSM_GROUPS * SSM_STATE), axis=-1)
    b, l = xs.shape[:2]
    dt = jax.nn.softplus((p_dt + dt_bias).astype(f32))
    a = -jnp.exp(a_log.astype(f32))
    xh = xs.reshape(b, l, SSM_HEADS, SSM_HEAD).astype(f32)
    bm = bm.reshape(b, l, SSM_GROUPS, SSM_STATE).astype(f32)
    cm = cm.reshape(b, l, SSM_GROUPS, SSM_STATE).astype(f32)
    padf = lambda t: jnp.pad(t, ((0, 0), (pad, 0)) + ((0, 0),) * (t.ndim - 2))
    y, ssm_t = ssd_scan(padf(xh), padf(dt), a, padf(bm), padf(cm), ssm0.astype(f32), chunk)
    y = y[:, pad:] + xh * d_skip[:, None]
    y = y.reshape(b, l, SSM_DIM) * jax.nn.silu(p_z.astype(f32))
    yg = y.reshape(b, l, SSM_GROUPS, SSM_DIM // SSM_GROUPS)
    yg = yg * lax.rsqrt(jnp.mean(jnp.square(yg), axis=-1, keepdims=True) + RMS_EPS)
    out = (yg.reshape(b, l, SSM_DIM) * ssm_norm_w).astype(p_z.dtype)
    return out, new_conv, ssm_t.astype(ssm0.dtype)


def trunk_layer(x, shift_hist, wkv0, conv_hist, ssm0, pad, chunk, lw):
    x = layer_norm(ALPHA * x + 0.5 * swiglu(x, lw['ffn1_gu'], lw['ffn1_dn']), lw['ln1_g'], lw['ln1_b'])
    proj = x @ lw['w_in']
    p_rw, p_z, p_xbc, p_dt, p_gate = jnp.split(proj, IN_SPLITS, axis=-1)
    y_rw, new_shift, wkv_t = rwkv7_mix(p_rw, shift_hist, wkv0, lw['rw_mu'], lw['rw_w0'], lw['rw_w2'], lw['rw_a0'], lw['rw_a2'], lw['rw_g2'], lw['rw_kk'], lw['rw_ka'], lw['rw_rk'], lw['rw_gn_w'], lw['rw_gn_b'])
    y_ssm, new_conv, ssm_t = mamba2_mix(p_z, p_xbc, p_dt, conv_hist, ssm0, pad, chunk, lw['conv_w'], lw['conv_b'], lw['dt_bias'], lw['a_log'], lw['d_skip'], lw['ssm_norm_w'])
    g_a, g_b = jnp.split(jax.nn.sigmoid(p_gate + lw['b_gate']), 2, axis=-1)
    merged = g_a * (y_rw @ lw['w_rw_out']) + g_b * (y_ssm @ lw['w_ssm_out'])
    x = layer_norm(ALPHA * x + merged @ lw['w_out'], lw['ln2_g'], lw['ln2_b'])
    x = layer_norm(ALPHA * x + 0.5 * swiglu(x, lw['ffn2_gu'], lw['ffn2_dn']), lw['ln3_g'], lw['ln3_b'])
    return x, new_shift, wkv_t, new_conv, ssm_t


def setup_inputs(seed: int = 0) -> dict:
    key = jax.random.key(seed)
    ks = iter(jax.random.split(key, 64))

    def nrm(shape, scale):
        return jax.random.normal(next(ks), shape, jnp.float32) * scale

    def unif(shape, lo, hi):
        return jax.random.uniform(next(ks), shape, jnp.float32, lo, hi)

    L = DEPTH
    dt0 = jnp.exp(unif((L, SSM_HEADS), math.log(1e-3), math.log(1e-1)))
    return {
        'x_prompt': nrm((BATCH, SEQ, D_MODEL), 1.0),
        'x_sample': nrm((DEC_BATCH, DEC_SEQ, D_MODEL), 1.0),
        'state_rwkv_shift': nrm((L, DEC_BATCH, 1, RW_SHIFT_COLS), 1.0),
        'state_wkv': nrm((L, DEC_BATCH, RW_HEADS, RW_HEAD, RW_HEAD), 0.3),
        'state_conv': nrm((L, DEC_BATCH, CONV_W - 1, CONV_DIM), 1.0),
        'state_ssm': nrm((L, DEC_BATCH, SSM_HEADS, SSM_HEAD, SSM_STATE), 0.1),
        'meta_tokens': nrm((N_META, D_MODEL), 1.0),
        'ffn1_gu': nrm((L, D_MODEL, 2 * D_FF), D_MODEL ** -0.5),
        'ffn1_dn': nrm((L, D_FF, D_MODEL), BETA * D_FF ** -0.5),
        'ln1_g': 1.0 + nrm((L, D_MODEL), 0.02),
        'ln1_b': nrm((L, D_MODEL), 0.02),
        'w_in': nrm((L, D_MODEL, N_IN), D_MODEL ** -0.5),
        'b_gate': nrm((L, 2 * D_MODEL), 0.1),
        'rw_mu': unif((L, RW_SHIFT_COLS), 0.0, 1.0),
        'rw_w0': unif((L, RW_DIM), -6.0, -1.0),
        'rw_w2': nrm((L, RW_DECAY_LORA, RW_DIM), 0.1 * RW_DECAY_LORA ** -0.5),
        'rw_a0': nrm((L, RW_DIM), 0.1),
        'rw_a2': nrm((L, RW_AAA_LORA, RW_DIM), RW_AAA_LORA ** -0.5),
        'rw_g2': nrm((L, RW_GATE_LORA, RW_DIM), RW_GATE_LORA ** -0.5),
        'rw_kk': 0.85 + nrm((L, RW_HEADS, RW_HEAD), 0.02),
        'rw_ka': 1.0 + nrm((L, RW_HEADS, RW_HEAD), 0.02),
        'rw_rk': nrm((L, RW_HEADS, RW_HEAD), 0.1),
        'rw_gn_w': 1.0 + nrm((L, RW_DIM), 0.02),
        'rw_gn_b': nrm((L, RW_DIM), 0.02),
        'conv_w': nrm((L, CONV_W, CONV_DIM), CONV_W ** -0.5),
        'conv_b': nrm((L, CONV_DIM), 0.02),
        'dt_bias': dt0 + jnp.log(-jnp.expm1(-dt0)),
        'a_log': jnp.log(unif((L, SSM_HEADS), 1.0, 16.0)),
        'd_skip': 1.0 + nrm((L, SSM_HEADS), 0.1),
        'ssm_norm_w': 1.0 + nrm((L, SSM_DIM), 0.02),
        'w_rw_out': nrm((L, RW_DIM, D_MODEL), RW_DIM ** -0.5),
        'w_ssm_out': nrm((L, SSM_DIM, D_MODEL), SSM_DIM ** -0.5),
        'w_out': nrm((L, D_MODEL, D_MODEL), BETA * D_MODEL ** -0.5),
        'ln2_g': 1.0 + nrm((L, D_MODEL), 0.02),
        'ln2_b': nrm((L, D_MODEL), 0.02),
        'ffn2_gu': nrm((L, D_MODEL, 2 * D_FF), D_MODEL ** -0.5),
        'ffn2_dn': nrm((L, D_FF, D_MODEL), BETA * D_FF ** -0.5),
        'ln3_g': 1.0 + nrm((L, D_MODEL), 0.02),
        'ln3_b': nrm((L, D_MODEL), 0.02),
    }


def reference(x_prompt, x_sample, state_rwkv_shift, state_wkv, state_conv, state_ssm, meta_tokens,
              ffn1_gu, ffn1_dn, ln1_g, ln1_b, w_in, b_gate, rw_mu, rw_w0, rw_w2, rw_a0, rw_a2, rw_g2,
              rw_kk, rw_ka, rw_rk, rw_gn_w, rw_gn_b, conv_w, conv_b, dt_bias, a_log, d_skip, ssm_norm_w,
              w_rw_out, w_ssm_out, w_out, ln2_g, ln2_b, ffn2_gu, ffn2_dn, ln3_g, ln3_b):
    weights = {
        'ffn1_gu': ffn1_gu, 'ffn1_dn': ffn1_dn, 'ln1_g': ln1_g, 'ln1_b': ln1_b, 'w_in': w_in, 'b_gate': b_gate,
        'rw_mu': rw_mu, 'rw_w0': rw_w0, 'rw_w2': rw_w2, 'rw_a0': rw_a0, 'rw_a2': rw_a2, 'rw_g2': rw_g2,
        'rw_kk': rw_kk, 'rw_ka': rw_ka, 'rw_rk': rw_rk, 'rw_gn_w': rw_gn_w, 'rw_gn_b': rw_gn_b,
        'conv_w': conv_w, 'conv_b': conv_b, 'dt_bias': dt_bias, 'a_log': a_log, 'd_skip': d_skip,
        'ssm_norm_w': ssm_norm_w, 'w_rw_out': w_rw_out, 'w_ssm_out': w_ssm_out, 'w_out': w_out,
        'ln2_g': ln2_g, 'ln2_b': ln2_b, 'ffn2_gu': ffn2_gu, 'ffn2_dn': ffn2_dn, 'ln3_g': ln3_g, 'ln3_b': ln3_b,
    }
    b = x_prompt.shape[0]
    dtp = x_prompt.dtype
    xp = jnp.concatenate([jnp.broadcast_to(meta_tokens.astype(dtp)[None], (b, N_META, D_MODEL)), x_prompt], axis=1)
    xs = x_sample
    p_shift, p_wkv, p_conv, p_ssm = [], [], [], []
    s_shift, s_wkv, s_conv, s_ssm = [], [], [], []
    for layer in range(DEPTH):
        lw = {name: w[layer] for name, w in weights.items()}
        xp, ps_, pw_, pc_, pm_ = trunk_layer(
            xp,
            jnp.zeros((b, 1, RW_SHIFT_COLS), dtp),
            jnp.zeros((b, RW_HEADS, RW_HEAD, RW_HEAD), dtp),
            jnp.zeros((b, CONV_W - 1, CONV_DIM), dtp),
            jnp.zeros((b, SSM_HEADS, SSM_HEAD, SSM_STATE), dtp),
            PROMPT_PAD, CHUNK, lw)
        xs, ss_, sw_, sc_, sm_ = trunk_layer(
            xs, state_rwkv_shift[layer], state_wkv[layer], state_conv[layer], state_ssm[layer],
            0, xs.shape[1], lw)
        p_shift.append(ps_); p_wkv.append(pw_); p_conv.append(pc_); p_ssm.append(pm_)
        s_shift.append(ss_); s_wkv.append(sw_); s_conv.append(sc_); s_ssm.append(sm_)
    y_prompt = xp[:, N_META:]
    return (y_prompt, xs, jnp.stack(p_shift), jnp.stack(p_wkv), jnp.stack(p_conv), jnp.stack(p_ssm),
            jnp.stack(s_shift), jnp.stack(s_wkv), jnp.stack(s_conv), jnp.stack(s_ssm))
```

```python
import functools
import math

import jax
import jax.numpy as jnp
from jax import lax
from jax.experimental import pallas as pl
from jax.experimental.pallas import tpu as pltpu

F32 = jnp.float32
BF16 = jnp.bfloat16

D_MODEL = 2048
BATCH = 4
SEQ = 2048
DEC_BATCH = 16
DEC_SEQ = 16
CHUNK = 64
N_META = 16
PAD = (-N_META) % CHUNK
SEQ_ROWS = PAD + N_META + SEQ
P_ROWS = BATCH * SEQ_ROWS
S_ROWS = DEC_BATCH * DEC_SEQ
M_ROWS = P_ROWS + S_ROWS

HEAD = 64
RW_DIM = D_MODEL // 2
RW_HEADS = RW_DIM // HEAD
RW_DECAY_LORA = 96
RW_AAA_LORA = 96
RW_GATE_LORA = 256
RW_SHIFT_COLS = 3 * RW_DIM + RW_DECAY_LORA + RW_AAA_LORA + RW_GATE_LORA
RW_GN_EPS = 64e-5
SSM_DIM = D_MODEL
SSM_HEADS = SSM_DIM // HEAD
SSM_GROUPS = 4
SSM_HPG = SSM_HEADS // SSM_GROUPS
SSM_STATE = 128
CONV_W = 4
BC_DIM = SSM_GROUPS * SSM_STATE
CONV_DIM = SSM_DIM + 2 * BC_DIM
RMS_EPS = 1e-5
D_FF = 5632
LN_EPS = 1e-5
DEPTH = 1
ALPHA = (2 * DEPTH) ** 0.25

LANES = 128
SUBLANES = 8
VMEM_LIMIT = 56 * 1024 * 1024

LORA_PAD = LANES
RW_OFF_WD = 3 * RW_DIM
RW_OFF_AD = RW_OFF_WD + LORA_PAD
RW_OFF_GD = RW_OFF_AD + LORA_PAD
RW_COLS = RW_OFF_GD + RW_GATE_LORA
DT_COLS = LANES
CONV_HIST = SUBLANES

FFN_TM = 512
FFN_TF = 512
MM_TM = 512
MM_TN = 512
MERGE_TM = 256


def _dot(a, b):
    return jnp.dot(a, b, preferred_element_type=F32)


def _dot_nt(a, b):
    return lax.dot_general(a, b, (((1,), (1,)), ((), ())), preferred_element_type=F32)


def _dot_tn(a, b):
    return lax.dot_general(a, b, (((0,), (0,)), ((), ())), preferred_element_type=F32)


def _split3(x):
    hi = x.astype(BF16)
    r1 = x - hi.astype(F32)
    mid = r1.astype(BF16)
    lo = (r1 - mid.astype(F32)).astype(BF16)
    return hi, mid, lo


def _dot_sel_r(x, sel):
    hi, mid, lo = _split3(x)
    return _dot(hi, sel) + _dot(mid, sel) + _dot(lo, sel)


def _dot_sel_l(sel, x):
    hi, mid, lo = _split3(x)
    return _dot(sel, hi) + _dot(sel, mid) + _dot(sel, lo)


def _sigmoid(x):
    return jax.nn.sigmoid(x)


def _softplus(x):
    return jnp.maximum(x, 0.0) + jnp.log1p(jnp.exp(-jnp.abs(x)))


def _layer_norm(s, g, b):
    mu = jnp.mean(s, axis=-1, keepdims=True)
    d = s - mu
    var = jnp.mean(d * d, axis=-1, keepdims=True)
    return d * lax.rsqrt(var + LN_EPS) * g + b


def _ffn_ln_kernel(x_ref, wg_ref, wu_ref, wd_ref, g_ref, b_ref, o_ref, ob_ref, xb_ref, acc_ref):
    f = pl.program_id(1)

    @pl.when(f == 0)
    def _():
        xb_ref[...] = x_ref[...].astype(BF16)
        acc_ref[...] = jnp.zeros_like(acc_ref)

    xb = xb_ref[...]
    gate = _dot(xb, wg_ref[...])
    up = _dot(xb, wu_ref[...])
    h = (gate * _sigmoid(gate) * up).astype(BF16)
    acc_ref[...] += _dot(h, wd_ref[...])

    @pl.when(f == pl.num_programs(1) - 1)
    def _():
        y = _layer_norm(ALPHA * x_ref[...] + 0.5 * acc_ref[...], g_ref[...], b_ref[...])
        o_ref[...] = y
        ob_ref[...] = y.astype(BF16)


def _ffn_ln(x, w_gu, w_dn, ln_g, ln_b):
    m, d = x.shape
    nf = D_FF // FFN_TF
    return pl.pallas_call(
        _ffn_ln_kernel,
        grid=(m // FFN_TM, nf),
        in_specs=[
            pl.BlockSpec((FFN_TM, d), lambda i, f: (i, 0)),
            pl.BlockSpec((d, FFN_TF), lambda i, f: (0, f)),
            pl.BlockSpec((d, FFN_TF), lambda i, f: (0, f + nf)),
            pl.BlockSpec((FFN_TF, d), lambda i, f: (f, 0)),
            pl.BlockSpec((1, d), lambda i, f: (0, 0)),
            pl.BlockSpec((1, d), lambda i, f: (0, 0)),
        ],
        out_specs=[
            pl.BlockSpec((FFN_TM, d), lambda i, f: (i, 0)),
            pl.BlockSpec((FFN_TM, d), lambda i, f: (i, 0)),
        ],
        out_shape=[jax.ShapeDtypeStruct((m, d), F32), jax.ShapeDtypeStruct((m, d), BF16)],
        scratch_shapes=[pltpu.VMEM((FFN_TM, d), BF16), pltpu.VMEM((FFN_TM, d), F32)],
        compiler_params=pltpu.CompilerParams(
            dimension_semantics=("parallel", "arbitrary"), vmem_limit_bytes=VMEM_LIMIT),
        name="ffn_ln",
    )(x, w_gu, w_gu, w_dn, ln_g, ln_b)


def _mm_kernel(x_ref, w_ref, o_ref):
    o_ref[...] = _dot(x_ref[...], w_ref[...])


def _matmul(x, w, name):
    m, k = x.shape
    n = w.shape[1]
    tn = min(MM_TN, n)
    return pl.pallas_call(
        _mm_kernel,
        grid=(m // MM_TM, n // tn),
        in_specs=[pl.BlockSpec((MM_TM, k), lambda i, j: (i, 0)),
                  pl.BlockSpec((k, tn), lambda i, j: (0, j))],
        out_specs=pl.BlockSpec((MM_TM, tn), lambda i, j: (i, j)),
        out_shape=jax.ShapeDtypeStruct((m, n), F32),
        compiler_params=pltpu.CompilerParams(
            dimension_semantics=("parallel", "parallel"), vmem_limit_bytes=VMEM_LIMIT),
        name=name,
    )(x, w)


def _rwkv_kernel(p_ref, hist_ref, s0_ref, mu_ref, w0_ref, w2_ref, a0_ref, a2_ref, g2_ref,
                 kk_ref, ka_ref, rk_ref, gnw_ref, gnb_ref, seg_ref,
                 y_ref, s_ref, prev_ref, yacc_ref, *, chunk, pad):
    c = pl.program_id(1)
    rows = lax.broadcasted_iota(jnp.int32, (chunk, 1), 0)

    @pl.when(c == 0)
    def _():
        prev_ref[...] = hist_ref[0]
        s_ref[...] = s0_ref[...]

    p = p_ref[...]
    if pad:
        p = jnp.where(jnp.logical_and(c == 0, rows < pad), 0.0, p)
    prev = jnp.where(rows == 0, prev_ref[...], pltpu.roll(p, 1, 0))
    prev_ref[...] = p[chunk - 1:chunk, :]
    ps = p + (prev - p) * mu_ref[...]

    r = ps[:, 0:RW_DIM]
    k = ps[:, RW_DIM:2 * RW_DIM]
    v = ps[:, 2 * RW_DIM:3 * RW_DIM]
    wd = ps[:, RW_OFF_WD:RW_OFF_WD + LORA_PAD]
    ad = ps[:, RW_OFF_AD:RW_OFF_AD + LORA_PAD]
    gd = ps[:, RW_OFF_GD:RW_OFF_GD + RW_GATE_LORA]

    w_log = -_softplus(-(w0_ref[...] + _dot(jnp.tanh(wd).astype(BF16), w2_ref[...]))) - 0.5
    lw = -jnp.exp(w_log)
    a = _sigmoid(a0_ref[...] + _dot(ad.astype(BF16), a2_ref[...]))
    g = _dot(_sigmoid(gd).astype(BF16), g2_ref[...])

    seg = seg_ref[...]

    def head_sum(x):
        w = seg.shape[0]
        return jnp.concatenate(
            [_dot_sel_r(x[:, j * w:(j + 1) * w], seg) for j in range(RW_DIM // w)], axis=1)

    kk = k * kk_ref[...]
    kk = kk / jnp.maximum(jnp.sqrt(head_sum(kk * kk)), 1e-12)
    k = k * (1.0 + (a - 1.0) * ka_ref[...])
    b = kk * a

    ci = lax.broadcasted_iota(jnp.int32, (chunk, chunk), 0)
    cj = lax.broadcasted_iota(jnp.int32, (chunk, chunk), 1)
    tri_incl = ci >= cj
    tri_strict = ci > cj
    cum = _dot_sel_l(tri_incl.astype(BF16), lw)
    cum_end = cum[chunk - 1:chunk, :]
    e_neg = jnp.exp(-cum)
    kt = (kk * jnp.exp(cum - lw)).astype(BF16)
    rt = (r * jnp.exp(cum)).astype(BF16)
    kd = (k * e_neg).astype(BF16)
    bd = (b * e_neg).astype(BF16)
    e_end = jnp.exp(cum_end - cum)
    k_end = (k * e_end).astype(BF16)
    b_end_neg = (-(b * e_end)).astype(BF16)
    p_end = jnp.exp(cum_end)
    vb = v.astype(BF16)
    eye = (ci == cj).astype(F32)

    for h in range(RW_HEADS):
        sl = slice(h * HEAD, (h + 1) * HEAD)
        s_h = s_ref[0, h]
        s_hb = s_h.astype(BF16)
        amat = _dot_nt(jnp.concatenate([kt[:, sl], rt[:, sl]], axis=0),
                       jnp.concatenate([bd[:, sl], kd[:, sl]], axis=0))
        a_kb = jnp.where(tri_strict, amat[:chunk, :chunk], 0.0)
        a_kk = jnp.where(tri_strict, amat[:chunk, chunk:], 0.0)
        a_rb = jnp.where(tri_incl, amat[chunk:, :chunk], 0.0)
        a_rk = jnp.where(tri_incl, amat[chunk:, chunk:], 0.0)
        q = -a_kb
        t = eye + q
        n = 1
        while 2 * n < chunk:
            qb = q.astype(BF16)
            q = _dot(qb, qb)
            t = t + _dot(t.astype(BF16), q.astype(BF16))
            n *= 2
        v_h = vb[:, sl]
        rhs = _dot_nt(kt[:, sl], s_hb) + _dot(a_kk.astype(BF16), v_h)
        u = _dot(t.astype(BF16), rhs.astype(BF16))
        ub = u.astype(BF16)
        y_h = (_dot_nt(rt[:, sl], s_hb) + _dot(a_rk.astype(BF16), v_h)
               - _dot(a_rb.astype(BF16), ub))
        yacc_ref[:, sl] = y_h
        s_new = s_h * p_end[:, sl] + _dot_tn(jnp.concatenate([v_h, ub], axis=0),
                                             jnp.concatenate([k_end[:, sl], b_end_neg[:, sl]], axis=0))
        s_ref[0, h] = s_new

    y = yacc_ref[...]
    inv_n = 1.0 / HEAD
    mean = head_sum(y) * inv_n
    yc = y - mean
    var = head_sum(yc * yc) * inv_n
    yn = yc * lax.rsqrt(var + RW_GN_EPS) * gnw_ref[...] + gnb_ref[...]
    bonus = head_sum(r * k * rk_ref[...]) * v
    y_ref[...] = ((yn + bonus) * g).astype(y_ref.dtype)


def _rwkv_mix(p_rw, hist, s0, wts, *, row0, n_seq, n_chunks, chunk, pad):
    blk0 = row0 // chunk
    const2 = lambda s, c: (0, 0)
    wspecs = [pl.BlockSpec(w.shape, const2) for w in wts]
    return pl.pallas_call(
        functools.partial(_rwkv_kernel, chunk=chunk, pad=pad),
        grid=(n_seq, n_chunks),
        in_specs=[
            pl.BlockSpec((chunk, RW_COLS), lambda s, c: (blk0 + s * n_chunks + c, 0)),
            pl.BlockSpec((1, 1, RW_COLS), lambda s, c: (s, 0, 0)),
            pl.BlockSpec((1, RW_HEADS, HEAD, HEAD), lambda s, c: (s, 0, 0, 0)),
        ] + wspecs,
        out_specs=[
            pl.BlockSpec((chunk, RW_DIM), lambda s, c: (s * n_chunks + c, 0)),
            pl.BlockSpec((1, RW_HEADS, HEAD, HEAD), lambda s, c: (s, 0, 0, 0)),
        ],
        out_shape=[jax.ShapeDtypeStruct((n_seq * n_chunks * chunk, RW_DIM), BF16),
                   jax.ShapeDtypeStruct((n_seq, RW_HEADS, HEAD, HEAD), F32)],
        scratch_shapes=[pltpu.VMEM((1, RW_COLS), F32), pltpu.VMEM((chunk, RW_DIM), F32)],
        compiler_params=pltpu.CompilerParams(
            dimension_semantics=("parallel", "arbitrary"), vmem_limit_bytes=VMEM_LIMIT),
        name="rwkv_mix_c%d" % chunk,
    )(p_rw, hist, s0, *wts)


def _ssd_kernel(xbc_ref, z_ref, dt_ref, hist_ref, h0_ref, cw_ref, cb_ref, dtb_ref, alog_ref,
                dskip_ref, nw_ref, exp_ref, gsum_ref,
                y_ref, h_ref, xpad_ref, yacc_ref, *, chunk, pad):
    c = pl.program_id(1)
    rows = lax.broadcasted_iota(jnp.int32, (chunk, 1), 0)

    @pl.when(c == 0)
    def _():
        xpad_ref[0:CONV_HIST, :] = hist_ref[0]
        h_ref[...] = h0_ref[...]

    u = xbc_ref[...]
    if pad:
        is_pad = jnp.logical_and(c == 0, rows < pad)
        u = jnp.where(is_pad, 0.0, u)
    xpad_ref[CONV_HIST:CONV_HIST + chunk, :] = u
    conv = cb_ref[...] + u * cw_ref[CONV_W - 1:CONV_W, :]
    for i in range(CONV_W - 1):
        back = CONV_W - 1 - i
        conv = conv + xpad_ref[CONV_HIST - back:CONV_HIST - back + chunk, :] * cw_ref[i:i + 1, :]
    xpad_ref[0:CONV_HIST, :] = xpad_ref[chunk:chunk + CONV_HIST, :]
    xbc = conv * _sigmoid(conv)
    xs = xbc[:, 0:SSM_DIM]
    bm = xbc[:, SSM_DIM:SSM_DIM + BC_DIM].astype(BF16)
    cm = xbc[:, SSM_DIM + BC_DIM:CONV_DIM].astype(BF16)

    dt = _softplus(dt_ref[...] + dtb_ref[...])
    if pad:
        dt = jnp.where(is_pad, 0.0, dt)
    da = dt * (-jnp.exp(alog_ref[...]))
    ci = lax.broadcasted_iota(jnp.int32, (chunk, chunk), 0)
    cj = lax.broadcasted_iota(jnp.int32, (chunk, chunk), 1)
    causal = ci >= cj
    acs = _dot_sel_l(causal.astype(BF16), da)
    acs_t = acs.T
    acs_end = acs[chunk - 1:chunk, :]

    expand = exp_ref[...]
    xdt = xs * _dot_sel_r(dt, expand)
    xdt_b = xdt.astype(BF16)
    xdt_end = (xdt * _dot_sel_r(jnp.exp(acs_end - acs), expand)).astype(BF16)
    e_acs = _dot_sel_r(jnp.exp(acs), expand)
    decay_end = jnp.exp(acs_t[:, chunk - 1:chunk])

    gw = SSM_HPG * HEAD
    for g in range(SSM_GROUPS):
        bm_g = bm[:, g * SSM_STATE:(g + 1) * SSM_STATE]
        cm_g = cm[:, g * SSM_STATE:(g + 1) * SSM_STATE]
        cb = _dot_nt(cm_g, bm_g)
        for hh in range(SSM_HPG):
            hd = g * SSM_HPG + hh
            sl = slice(hd * HEAD, (hd + 1) * HEAD)
            seg = acs[:, hd:hd + 1] - acs_t[hd:hd + 1, :]
            lmat = jnp.exp(jnp.where(causal, seg, -jnp.inf))
            h_in = h_ref[0, hd]
            y_h = (_dot((cb * lmat).astype(BF16), xdt_b[:, sl])
                   + _dot_nt(cm_g, h_in.astype(BF16)) * e_acs[:, sl])
            yacc_ref[:, sl] = y_h
            h_ref[0, hd] = h_in * decay_end[hd:hd + 1, :] + _dot_tn(xdt_end[:, sl], bm_g)

    y = yacc_ref[...] + xs * dskip_ref[...]
    zz = z_ref[...]
    y = y * (zz * _sigmoid(zz))
    ms = _dot_sel_r(y * y, gsum_ref[...]) * (1.0 / gw)
    y_ref[...] = (y * lax.rsqrt(ms + RMS_EPS) * nw_ref[...]).astype(y_ref.dtype)


def _ssd_mix(p_xbc, p_z, p_dt, hist, h0, wts, *, row0, n_seq, n_chunks, chunk, pad):
    blk0 = row0 // chunk
    const2 = lambda s, c: (0, 0)
    row_map = lambda s, c: (blk0 + s * n_chunks + c, 0)
    wspecs = [pl.BlockSpec(w.shape, const2) for w in wts]
    return pl.pallas_call(
        functools.partial(_ssd_kernel, chunk=chunk, pad=pad),
        grid=(n_seq, n_chunks),
        in_specs=[
            pl.BlockSpec((chunk, CONV_DIM), row_map),
            pl.BlockSpec((chunk, SSM_DIM), row_map),
            pl.BlockSpec((chunk, DT_COLS), row_map),
            pl.BlockSpec((1, CONV_HIST, CONV_DIM), lambda s, c: (s, 0, 0)),
            pl.BlockSpec((1, SSM_HEADS, HEAD, SSM_STATE), lambda s, c: (s, 0, 0, 0)),
        ] + wspecs,
        out_specs=[
            pl.BlockSpec((chunk, SSM_DIM), lambda s, c: (s * n_chunks + c, 0)),
            pl.BlockSpec((1, SSM_HEADS, HEAD, SSM_STATE), lambda s, c: (s, 0, 0, 0)),
        ],
        out_shape=[jax.ShapeDtypeStruct((n_seq * n_chunks * chunk, SSM_DIM), BF16),
                   jax.ShapeDtypeStruct((n_seq, SSM_HEADS, HEAD, SSM_STATE), F32)],
        scratch_shapes=[pltpu.VMEM((CONV_HIST + chunk, CONV_DIM), F32),
                        pltpu.VMEM((chunk, SSM_DIM), F32)],
        compiler_params=pltpu.CompilerParams(
            dimension_semantics=("parallel", "arbitrary"), vmem_limit_bytes=VMEM_LIMIT),
        name="ssd_mix_c%d" % chunk,
    )(p_xbc, p_z, p_dt, hist, h0, *wts)


def _merge_kernel(x_ref, yrw_ref, yssm_ref, gate_ref, bg_ref, wrw_ref, wssm_ref, wout_ref,
                  g_ref, b_ref, o_ref, ob_ref):
    gates = _sigmoid(gate_ref[...] + bg_ref[...])
    merged = (gates[:, 0:D_MODEL] * _dot(yrw_ref[...], wrw_ref[...])
              + gates[:, D_MODEL:2 * D_MODEL] * _dot(yssm_ref[...], wssm_ref[...]))
    s = ALPHA * x_ref[...] + _dot(merged.astype(BF16), wout_ref[...])
    y = _layer_norm(s, g_ref[...], b_ref[...])
    o_ref[...] = y
    ob_ref[...] = y.astype(BF16)


def _merge(x, y_rw, y_ssm, p_gate, b_gate, w_rw_out, w_ssm_out, w_out, ln_g, ln_b):
    m, d = x.shape
    tm = MERGE_TM
    row = lambda i: (i, 0)
    const = lambda i: (0, 0)
    resident = lambda w: pl.BlockSpec(w.shape, const, pipeline_mode=pl.Buffered(1))
    return pl.pallas_call(
        _merge_kernel,
        grid=(m // tm,),
        in_specs=[
            pl.BlockSpec((tm, d), row),
            pl.BlockSpec((tm, RW_DIM), row),
            pl.BlockSpec((tm, SSM_DIM), row),
            pl.BlockSpec((tm, 2 * d), row),
            pl.BlockSpec((1, 2 * d), const),
            resident(w_rw_out), resident(w_ssm_out), resident(w_out),
            pl.BlockSpec((1, d), const),
            pl.BlockSpec((1, d), const),
        ],
        out_specs=[pl.BlockSpec((tm, d), row), pl.BlockSpec((tm, d), row)],
        out_shape=[jax.ShapeDtypeStruct((m, d), F32), jax.ShapeDtypeStruct((m, d), BF16)],
        compiler_params=pltpu.CompilerParams(
            dimension_semantics=("parallel",), vmem_limit_bytes=VMEM_LIMIT),
        name="merge_ln",
    )(x, y_rw, y_ssm, p_gate, b_gate, w_rw_out, w_ssm_out, w_out, ln_g, ln_b)


def _pad_cols(x, width):
    return jnp.pad(x, [(0, 0)] * (x.ndim - 1) + [(0, width - x.shape[-1])])


def _rw_cols(x):
    o_wd, o_ad, o_gd = 3 * RW_DIM, 3 * RW_DIM + RW_DECAY_LORA, 3 * RW_DIM + RW_DECAY_LORA + RW_AAA_LORA
    return jnp.concatenate([
        x[..., :o_wd],
        _pad_cols(x[..., o_wd:o_ad], LORA_PAD),
        _pad_cols(x[..., o_ad:o_gd], LORA_PAD),
        x[..., o_gd:],
    ], axis=-1)


def _rw_cols_inv(x):
    return jnp.concatenate([
        x[..., :RW_OFF_WD],
        x[..., RW_OFF_WD:RW_OFF_WD + RW_DECAY_LORA],
        x[..., RW_OFF_AD:RW_OFF_AD + RW_AAA_LORA],
        x[..., RW_OFF_GD:],
    ], axis=-1)


def _block_ones(n, blk):
    i = jnp.arange(n) // blk
    return (i[:, None] == i[None, :]).astype(BF16)


def kernel(x_prompt, x_sample, state_rwkv_shift, state_wkv, state_conv, state_ssm, meta_tokens, ffn1_gu, ffn1_dn, ln1_g, ln1_b, w_in, b_gate, rw_mu, rw_w0, rw_w2, rw_a0, rw_a2, rw_g2, rw_kk, rw_ka, rw_rk, rw_gn_w, rw_gn_b, conv_w, conv_b, dt_bias, a_log, d_skip, ssm_norm_w, w_rw_out, w_ssm_out, w_out, ln2_g, ln2_b, ffn2_gu, ffn2_dn, ln3_g, ln3_b):
    lyr = 0
    row = lambda t: t[lyr].reshape(1, -1).astype(F32)

    head_rows = jnp.concatenate([jnp.zeros((PAD, D_MODEL), F32), meta_tokens.astype(F32)], axis=0)
    xp = jnp.concatenate([jnp.broadcast_to(head_rows[None], (BATCH, PAD + N_META, D_MODEL)), x_prompt], axis=1)
    x0 = jnp.concatenate([xp.reshape(P_ROWS, D_MODEL), x_sample.reshape(S_ROWS, D_MODEL)], axis=0)

    x1, x1b = _ffn_ln(x0, ffn1_gu[lyr].astype(BF16), ffn1_dn[lyr].astype(BF16), row(ln1_g), row(ln1_b))

    w = w_in[lyr]
    o_z = RW_SHIFT_COLS
    o_xbc = o_z + SSM_DIM
    o_dt = o_xbc + CONV_DIM
    o_gate = o_dt + SSM_HEADS
    p_rw = _matmul(x1b, _rw_cols(w[:, :o_z]).astype(BF16), "proj_rw")
    p_z = _matmul(x1b, w[:, o_z:o_xbc].astype(BF16), "proj_z")
    p_xbc = _matmul(x1b, w[:, o_xbc:o_dt].astype(BF16), "proj_xbc")
    p_dt = _matmul(x1b, _pad_cols(w[:, o_dt:o_gate], DT_COLS).astype(BF16), "proj_dt")
    p_gate = _matmul(x1b, w[:, o_gate:].astype(BF16), "proj_gate")

    pad_rows = lambda t, n: jnp.pad(t, ((0, n - t.shape[0]), (0, 0)))
    rw_wts = [
        _rw_cols(rw_mu[lyr]).reshape(1, RW_COLS), row(rw_w0),
        pad_rows(rw_w2[lyr], LORA_PAD).astype(BF16), row(rw_a0),
        pad_rows(rw_a2[lyr], LORA_PAD).astype(BF16), rw_g2[lyr].astype(BF16),
        row(rw_kk), row(rw_ka), row(rw_rk), row(rw_gn_w), row(rw_gn_b),
        _block_ones(2 * LANES, HEAD),
    ]
    y_rw_p, wkv_p = _rwkv_mix(
        p_rw, jnp.zeros((BATCH, 1, RW_COLS), F32), jnp.zeros((BATCH, RW_HEADS, HEAD, HEAD), F32), rw_wts,
        row0=0, n_seq=BATCH, n_chunks=SEQ_ROWS // CHUNK, chunk=CHUNK, pad=PAD)
    y_rw_s, wkv_s = _rwkv_mix(
        p_rw, _rw_cols(state_rwkv_shift[lyr]), state_wkv[lyr], rw_wts,
        row0=P_ROWS, n_seq=DEC_BATCH, n_chunks=1, chunk=DEC_SEQ, pad=0)

    head_of_lane = jnp.arange(SSM_DIM) // HEAD
    expand = (jnp.arange(DT_COLS)[:, None] == head_of_lane[None, :]).astype(BF16)
    ssd_wts = [
        conv_w[lyr], row(conv_b), _pad_cols(row(dt_bias), DT_COLS), _pad_cols(row(a_log), DT_COLS),
        jnp.repeat(d_skip[lyr], HEAD).reshape(1, SSM_DIM), row(ssm_norm_w),
        expand, _block_ones(SSM_DIM, SSM_HPG * HEAD),
    ]
    hist_rows = lambda t: jnp.pad(t, ((0, 0), (CONV_HIST - (CONV_W - 1), 0), (0, 0)))
    y_ssm_p, ssm_p = _ssd_mix(
        p_xbc, p_z, p_dt, jnp.zeros((BATCH, CONV_HIST, CONV_DIM), F32),
        jnp.zeros((BATCH, SSM_HEADS, HEAD, SSM_STATE), F32), ssd_wts,
        row0=0, n_seq=BATCH, n_chunks=SEQ_ROWS // CHUNK, chunk=CHUNK, pad=PAD)
    y_ssm_s, ssm_s = _ssd_mix(
        p_xbc, p_z, p_dt, hist_rows(state_conv[lyr]), state_ssm[lyr], ssd_wts,
        row0=P_ROWS, n_seq=DEC_BATCH, n_chunks=1, chunk=DEC_SEQ, pad=0)

    y_rw = jnp.concatenate([y_rw_p, y_rw_s], axis=0)
    y_ssm = jnp.concatenate([y_ssm_p, y_ssm_s], axis=0)
    x2, _ = _merge(x1, y_rw, y_ssm, p_gate, row(b_gate), w_rw_out[lyr].astype(BF16),
                   w_ssm_out[lyr].astype(BF16), w_out[lyr].astype(BF16), row(ln2_g), row(ln2_b))
    x3, _ = _ffn_ln(x2, ffn2_gu[lyr].astype(BF16), ffn2_dn[lyr].astype(BF16), row(ln3_g), row(ln3_b))

    y_prompt = x3[:P_ROWS].reshape(BATCH, SEQ_ROWS, D_MODEL)[:, PAD + N_META:]
    y_sample = x3[P_ROWS:].reshape(DEC_BATCH, DEC_SEQ, D_MODEL)
    prw_p = p_rw[:P_ROWS].reshape(BATCH, SEQ_ROWS, RW_COLS)
    prw_s = p_rw[P_ROWS:].reshape(DEC_BATCH, DEC_SEQ, RW_COLS)
    pxbc_p = p_xbc[:P_ROWS].reshape(BATCH, SEQ_ROWS, CONV_DIM)
    pxbc_s = p_xbc[P_ROWS:].reshape(DEC_BATCH, DEC_SEQ, CONV_DIM)
    return (y_prompt, y_sample,
            _rw_cols_inv(prw_p[:, -1:])[None], wkv_p[None], pxbc_p[:, -(CONV_W - 1):][None], ssm_p[None],
            _rw_cols_inv(prw_s[:, -1:])[None], wkv_s[None], pxbc_s[:, -(CONV_W - 1):][None], ssm_s[None])
```

```python
import functools
import math

import jax
import jax.numpy as jnp
from jax import lax
from jax.experimental import pallas as pl
from jax.experimental.pallas import tpu as pltpu

F32 = jnp.float32
BF16 = jnp.bfloat16

D_MODEL = 2048
BATCH = 4
SEQ = 2048
DEC_BATCH = 16
DEC_SEQ = 16
CHUNK = 64
N_META = 16
PAD = (-N_META) % CHUNK
SEQ_ROWS = PAD + N_META + SEQ
P_ROWS = BATCH * SEQ_ROWS
S_ROWS = DEC_BATCH * DEC_SEQ
M_ROWS = P_ROWS + S_ROWS

HEAD = 64
RW_DIM = D_MODEL // 2
RW_HEADS = RW_DIM // HEAD
RW_DECAY_LORA = 96
RW_AAA_LORA = 96
RW_GATE_LORA = 256
RW_SHIFT_COLS = 3 * RW_DIM + RW_DECAY_LORA + RW_AAA_LORA + RW_GATE_LORA
RW_GN_EPS = 64e-5
SSM_DIM = D_MODEL
SSM_HEADS = SSM_DIM // HEAD
SSM_GROUPS = 4
SSM_HPG = SSM_HEADS // SSM_GROUPS
SSM_STATE = 128
CONV_W = 4
BC_DIM = SSM_GROUPS * SSM_STATE
CONV_DIM = SSM_DIM + 2 * BC_DIM
RMS_EPS = 1e-5
D_FF = 5632
LN_EPS = 1e-5
DEPTH = 1
ALPHA = (2 * DEPTH) ** 0.25

LANES = 128
SUBLANES = 8
VMEM_LIMIT = 56 * 1024 * 1024

LORA_PAD = LANES
RW_OFF_WD = 3 * RW_DIM
RW_OFF_AD = RW_OFF_WD + LORA_PAD
RW_OFF_GD = RW_OFF_AD + LORA_PAD
RW_COLS = RW_OFF_GD + RW_GATE_LORA
DT_COLS = LANES
CONV_HIST = SUBLANES

FFN_TM = 512
FFN_TF = 512
MM_TM = 512
MM_TN = 512
MERGE_TM = 256


def _dot(a, b):
    return jnp.dot(a, b, preferred_element_type=F32)


def _dot_nt(a, b):
    return lax.dot_general(a, b, (((1,), (1,)), ((), ())), preferred_element_type=F32)


def _dot_tn(a, b):
    return lax.dot_general(a, b, (((0,), (0,)), ((), ())), preferred_element_type=F32)


def _split3(x):
    hi = x.astype(BF16)
    r1 = x - hi.astype(F32)
    mid = r1.astype(BF16)
    lo = (r1 - mid.astype(F32)).astype(BF16)
    return hi, mid, lo


def _dot_sel_r(x, sel):
    hi, mid, lo = _split3(x)
    return _dot(hi, sel) + _dot(mid, sel) + _dot(lo, sel)


def _dot_sel_l(sel, x):
    hi, mid, lo = _split3(x)
    return _dot(sel, hi) + _dot(sel, mid) + _dot(sel, lo)


def _sigmoid(x):
    return jax.nn.sigmoid(x)


def _softplus(x):
    return jnp.maximum(x, 0.0) + jnp.log1p(jnp.exp(-jnp.abs(x)))


def _layer_norm(s, g, b):
    mu = jnp.mean(s, axis=-1, keepdims=True)
    d = s - mu
    var = jnp.mean(d * d, axis=-1, keepdims=True)
    return d * lax.rsqrt(var + LN_EPS) * g + b


def _ffn_ln_kernel(x_ref, wg_ref, wu_ref, wd_ref, g_ref, b_ref, o_ref, ob_ref, xb_ref, acc_ref):
    f = pl.program_id(1)

    @pl.when(f == 0)
    def _():
        xb_ref[...] = x_ref[...].astype(BF16)
        acc_ref[...] = jnp.zeros_like(acc_ref)

    xb = xb_ref[...]
    gate = _dot(xb, wg_ref[...])
    up = _dot(xb, wu_ref[...])
    h = (gate * _sigmoid(gate) * up).astype(BF16)
    acc_ref[...] += _dot(h, wd_ref[...])

    @pl.when(f == pl.num_programs(1) - 1)
    def _():
        y = _layer_norm(ALPHA * x_ref[...] + 0.5 * acc_ref[...], g_ref[...], b_ref[...])
        o_ref[...] = y
        ob_ref[...] = y.astype(BF16)


def _ffn_ln(x, w_gu, w_dn, ln_g, ln_b):
    m, d = x.shape
    nf = D_FF // FFN_TF
    return pl.pallas_call(
        _ffn_ln_kernel,
        grid=(m // FFN_TM, nf),
        in_specs=[
            pl.BlockSpec((FFN_TM, d), lambda i, f: (i, 0)),
            pl.BlockSpec((d, FFN_TF), lambda i, f: (0, f)),
            pl.BlockSpec((d, FFN_TF), lambda i, f: (0, f + nf)),
            pl.BlockSpec((FFN_TF, d), lambda i, f: (f, 0)),
            pl.BlockSpec((1, d), lambda i, f: (0, 0)),
            pl.BlockSpec((1, d), lambda i, f: (0, 0)),
        ],
        out_specs=[
            pl.BlockSpec((FFN_TM, d), lambda i, f: (i, 0)),
            pl.BlockSpec((FFN_TM, d), lambda i, f: (i, 0)),
        ],
        out_shape=[jax.ShapeDtypeStruct((m, d), F32), jax.ShapeDtypeStruct((m, d), BF16)],
        scratch_shapes=[pltpu.VMEM((FFN_TM, d), BF16), pltpu.VMEM((FFN_TM, d), F32)],
        compiler_params=pltpu.CompilerParams(
            dimension_semantics=("parallel", "arbitrary"), vmem_limit_bytes=VMEM_LIMIT),
        name="ffn_ln",
    )(x, w_gu, w_gu, w_dn, ln_g, ln_b)


def _mm_kernel(x_ref, w_ref, o_ref):
    o_ref[...] = _dot(x_ref[...], w_ref[...])


def _matmul(x, w, name):
    m, k = x.shape
    n = w.shape[1]
    tn = min(MM_TN, n)
    return pl.pallas_call(
        _mm_kernel,
        grid=(m // MM_TM, n // tn),
        in_specs=[pl.BlockSpec((MM_TM, k), lambda i, j: (i, 0)),
                  pl.BlockSpec((k, tn), lambda i, j: (0, j))],
        out_specs=pl.BlockSpec((MM_TM, tn), lambda i, j: (i, j)),
        out_shape=jax.ShapeDtypeStruct((m, n), F32),
        compiler_params=pltpu.CompilerParams(
            dimension_semantics=("parallel", "parallel"), vmem_limit_bytes=VMEM_LIMIT),
        name=name,
    )(x, w)


def _rwkv_kernel(p_ref, hist_ref, s0_ref, mu_ref, w0_ref, w2_ref, a0_ref, a2_ref, g2_ref,
                 kk_ref, ka_ref, rk_ref, gnw_ref, gnb_ref, seg_ref,
                 y_ref, s_ref, prev_ref, yacc_ref, *, chunk, pad):
    c = pl.program_id(1)
    rows = lax.broadcasted_iota(jnp.int32, (chunk, 1), 0)

    @pl.when(c == 0)
    def _():
        prev_ref[...] = hist_ref[0]
        s_ref[...] = s0_ref[...]

    p = p_ref[...]
    if pad:
        p = jnp.where(jnp.logical_and(c == 0, rows < pad), 0.0, p)
    prev = jnp.where(rows == 0, prev_ref[...], pltpu.roll(p, 1, 0))
    prev_ref[...] = p[chunk - 1:chunk, :]
    ps = p + (prev - p) * mu_ref[...]

    r = ps[:, 0:RW_DIM]
    k = ps[:, RW_DIM:2 * RW_DIM]
    v = ps[:, 2 * RW_DIM:3 * RW_DIM]
    wd = ps[:, RW_OFF_WD:RW_OFF_WD + LORA_PAD]
    ad = ps[:, RW_OFF_AD:RW_OFF_AD + LORA_PAD]
    gd = ps[:, RW_OFF_GD:RW_OFF_GD + RW_GATE_LORA]

    w_log = -_softplus(-(w0_ref[...] + _dot(jnp.tanh(wd).astype(BF16), w2_ref[...]))) - 0.5
    lw = -jnp.exp(w_log)
    a = _sigmoid(a0_ref[...] + _dot(ad.astype(BF16), a2_ref[...]))
    g = _dot(_sigmoid(gd).astype(BF16), g2_ref[...])

    seg = seg_ref[...]

    def head_sum(x):
        w = seg.shape[0]
        return jnp.concatenate(
            [_dot_sel_r(x[:, j * w:(j + 1) * w], seg) for j in range(RW_DIM // w)], axis=1)

    kk = k * kk_ref[...]
    kk = kk / jnp.maximum(jnp.sqrt(head_sum(kk * kk)), 1e-12)
    k = k * (1.0 + (a - 1.0) * ka_ref[...])
    b = kk * a

    ci = lax.broadcasted_iota(jnp.int32, (chunk, chunk), 0)
    cj = lax.broadcasted_iota(jnp.int32, (chunk, chunk), 1)
    tri_incl = ci >= cj
    tri_strict = ci > cj
    cum = _dot_sel_l(tri_incl.astype(BF16), lw)
    cum_end = cum[chunk - 1:chunk, :]
    e_neg = jnp.exp(-cum)
    kt = (kk * jnp.exp(cum - lw)).astype(BF16)
    rt = (r * jnp.exp(cum)).astype(BF16)
    kd = (k * e_neg).astype(BF16)
    bd = (b * e_neg).astype(BF16)
    e_end = jnp.exp(cum_end - cum)
    k_end = (k * e_end).astype(BF16)
    b_end_neg = (-(b * e_end)).astype(BF16)
    p_end = jnp.exp(cum_end)
    vb = v.astype(BF16)
    eye = (ci == cj).astype(F32)

    heads = range(RW_HEADS)
    sls = [slice(h * HEAD, (h + 1) * HEAD) for h in heads]
    s_old = [s_ref[0, h] for h in heads]
    s_b = [s.astype(BF16) for s in s_old]
    amats = [_dot_nt(jnp.concatenate([kt[:, sl], rt[:, sl]], axis=0),
                     jnp.concatenate([bd[:, sl], kd[:, sl]], axis=0)) for sl in sls]
    a_kk = [jnp.where(tri_strict, m[:chunk, chunk:], 0.0).astype(BF16) for m in amats]
    a_rb_neg = [jnp.where(tri_incl, -m[chunk:, :chunk], 0.0) for m in amats]
    a_rk = [jnp.where(tri_incl, m[chunk:, chunk:], 0.0) for m in amats]
    a_y = [jnp.concatenate([x, y], axis=1).astype(BF16) for x, y in zip(a_rk, a_rb_neg)]
    nb = [jnp.where(tri_strict, -m[:chunk, :chunk], 0.0) for m in amats]
    ts = [eye + x for x in nb]
    nbb = [x.astype(BF16) for x in nb]
    qs = [_dot(x, x) for x in nbb]
    n = 2
    while n < chunk:
        last = 2 * n >= chunk
        if last:
            ts = [t + _dot(q.astype(BF16), t.astype(BF16)) for t, q in zip(ts, qs)]
        else:
            both = [_dot(q.astype(BF16), jnp.concatenate([t, q], axis=1).astype(BF16))
                    for t, q in zip(ts, qs)]
            ts = [t + x[:, :chunk] for t, x in zip(ts, both)]
            qs = [x[:, chunk:] for x in both]
        n *= 2
    v_h = [vb[:, sl] for sl in sls]
    rhs = [_dot_nt(kt[:, sl], sb) + _dot(akk, vh) for sl, sb, akk, vh in zip(sls, s_b, a_kk, v_h)]
    ub = [_dot(t.astype(BF16), x.astype(BF16)).astype(BF16) for t, x in zip(ts, rhs)]
    vu = [jnp.concatenate([vh, u], axis=0) for vh, u in zip(v_h, ub)]
    for h in heads:
        sl = sls[h]
        yacc_ref[:, sl] = _dot_nt(rt[:, sl], s_b[h]) + _dot(a_y[h], vu[h])
        s_ref[0, h] = s_old[h] * p_end[:, sl] + _dot_tn(
            vu[h], jnp.concatenate([k_end[:, sl], b_end_neg[:, sl]], axis=0))

    y = yacc_ref[...]
    inv_n = 1.0 / HEAD
    mean = head_sum(y) * inv_n
    yc = y - mean
    var = head_sum(yc * yc) * inv_n
    yn = yc * lax.rsqrt(var + RW_GN_EPS) * gnw_ref[...] + gnb_ref[...]
    bonus = head_sum(r * k * rk_ref[...]) * v
    y_ref[...] = ((yn + bonus) * g).astype(y_ref.dtype)


def _rwkv_mix(p_rw, hist, s0, wts, *, row0, n_seq, n_chunks, chunk, pad):
    blk0 = row0 // chunk
    const2 = lambda s, c: (0, 0)
    wspecs = [pl.BlockSpec(w.shape, const2) for w in wts]
    return pl.pallas_call(
        functools.partial(_rwkv_kernel, chunk=chunk, pad=pad),
        grid=(n_seq, n_chunks),
        in_specs=[
            pl.BlockSpec((chunk, RW_COLS), lambda s, c: (blk0 + s * n_chunks + c, 0)),
            pl.BlockSpec((1, 1, RW_COLS), lambda s, c: (s, 0, 0)),
            pl.BlockSpec((1, RW_HEADS, HEAD, HEAD), lambda s, c: (s, 0, 0, 0)),
        ] + wspecs,
        out_specs=[
            pl.BlockSpec((chunk, RW_DIM), lambda s, c: (s * n_chunks + c, 0)),
            pl.BlockSpec((1, RW_HEADS, HEAD, HEAD), lambda s, c: (s, 0, 0, 0)),
        ],
        out_shape=[jax.ShapeDtypeStruct((n_seq * n_chunks * chunk, RW_DIM), BF16),
                   jax.ShapeDtypeStruct((n_seq, RW_HEADS, HEAD, HEAD), F32)],
        scratch_shapes=[pltpu.VMEM((1, RW_COLS), F32), pltpu.VMEM((chunk, RW_DIM), F32)],
        compiler_params=pltpu.CompilerParams(
            dimension_semantics=("parallel", "arbitrary"), vmem_limit_bytes=VMEM_LIMIT),
        name="rwkv_mix_c%d" % chunk,
    )(p_rw, hist, s0, *wts)


def _ssd_kernel(xbc_ref, z_ref, dt_ref, hist_ref, h0_ref, cw_ref, cb_ref, dtb_ref, alog_ref,
                dskip_ref, nw_ref, exp_ref,
                y_ref, h_ref, xpad_ref, yacc_ref, *, chunk, pad):
    c = pl.program_id(1)
    rows = lax.broadcasted_iota(jnp.int32, (chunk, 1), 0)

    @pl.when(c == 0)
    def _():
        xpad_ref[0:CONV_HIST, :] = hist_ref[0]
        h_ref[...] = h0_ref[...]

    u = xbc_ref[...]
    if pad:
        is_pad = jnp.logical_and(c == 0, rows < pad)
        u = jnp.where(is_pad, 0.0, u)
    xpad_ref[CONV_HIST:CONV_HIST + chunk, :] = u
    conv = cb_ref[...] + u * cw_ref[CONV_W - 1:CONV_W, :]
    for i in range(CONV_W - 1):
        back = CONV_W - 1 - i
        conv = conv + xpad_ref[CONV_HIST - back:CONV_HIST - back + chunk, :] * cw_ref[i:i + 1, :]
    xpad_ref[0:CONV_HIST, :] = xpad_ref[chunk:chunk + CONV_HIST, :]
    xbc = conv * _sigmoid(conv)
    xs = xbc[:, 0:SSM_DIM]
    bm = xbc[:, SSM_DIM:SSM_DIM + BC_DIM].astype(BF16)
    cm = xbc[:, SSM_DIM + BC_DIM:CONV_DIM].astype(BF16)

    dt = _softplus(dt_ref[...] + dtb_ref[...])
    if pad:
        dt = jnp.where(is_pad, 0.0, dt)
    da = dt * (-jnp.exp(alog_ref[...]))
    ci = lax.broadcasted_iota(jnp.int32, (chunk, chunk), 0)
    cj = lax.broadcasted_iota(jnp.int32, (chunk, chunk), 1)
    causal = ci >= cj
    acs = _dot_sel_l(causal.astype(BF16), da)
    acs_t = acs.T
    acs_end = acs[chunk - 1:chunk, :]

    expand = exp_ref[...]
    xdt = xs * _dot_sel_r(dt, expand)
    xdt_b = xdt.astype(BF16)
    xdt_end = (xdt * _dot_sel_r(jnp.exp(acs_end - acs), expand)).astype(BF16)
    e_acs = _dot_sel_r(jnp.exp(acs), expand)
    decay_end = jnp.exp(acs_t[:, chunk - 1:chunk])

    gw = SSM_HPG * HEAD
    groups = range(SSM_GROUPS)
    heads = range(SSM_HEADS)
    bm_g = [bm[:, g * SSM_STATE:(g + 1) * SSM_STATE] for g in groups]
    cm_g = [cm[:, g * SSM_STATE:(g + 1) * SSM_STATE] for g in groups]
    h_old = [h_ref[0, hd] for hd in heads]
    cb = [_dot_nt(cm_g[g], bm_g[g]) for g in groups]
    y_off = [_dot_nt(cm_g[g], jnp.concatenate(
        [h_old[g * SSM_HPG + hh].astype(BF16) for hh in range(SSM_HPG)], axis=0)) for g in groups]
    st = [_dot_tn(xdt_end[:, g * gw:(g + 1) * gw], bm_g[g]) for g in groups]
    for hd in heads:
        sl = slice(hd * HEAD, (hd + 1) * HEAD)
        seg = acs[:, hd:hd + 1] - acs_t[hd:hd + 1, :]
        lmat = jnp.exp(jnp.where(causal, seg, -jnp.inf))
        yacc_ref[:, sl] = _dot((cb[hd // SSM_HPG] * lmat).astype(BF16), xdt_b[:, sl])
    for hd in heads:
        g, hh = divmod(hd, SSM_HPG)
        h_ref[0, hd] = h_old[hd] * decay_end[hd:hd + 1, :] + st[g][hh * HEAD:(hh + 1) * HEAD, :]

    y = yacc_ref[...] + jnp.concatenate(y_off, axis=1) * e_acs + xs * dskip_ref[...]
    zz = z_ref[...]
    y = y * (zz * _sigmoid(zz))
    parts = []
    for g in groups:
        yg = y[:, g * gw:(g + 1) * gw]
        ms = jnp.mean(yg * yg, axis=-1, keepdims=True)
        parts.append(yg * lax.rsqrt(ms + RMS_EPS))
    y_ref[...] = (jnp.concatenate(parts, axis=1) * nw_ref[...]).astype(y_ref.dtype)


def _ssd_mix(p_xbc, p_z, p_dt, hist, h0, wts, *, row0, n_seq, n_chunks, chunk, pad):
    blk0 = row0 // chunk
    const2 = lambda s, c: (0, 0)
    row_map = lambda s, c: (blk0 + s * n_chunks + c, 0)
    wspecs = [pl.BlockSpec(w.shape, const2) for w in wts]
    return pl.pallas_call(
        functools.partial(_ssd_kernel, chunk=chunk, pad=pad),
        grid=(n_seq, n_chunks),
        in_specs=[
            pl.BlockSpec((chunk, CONV_DIM), row_map),
            pl.BlockSpec((chunk, SSM_DIM), row_map),
            pl.BlockSpec((chunk, DT_COLS), row_map),
            pl.BlockSpec((1, CONV_HIST, CONV_DIM), lambda s, c: (s, 0, 0)),
            pl.BlockSpec((1, SSM_HEADS, HEAD, SSM_STATE), lambda s, c: (s, 0, 0, 0)),
        ] + wspecs,
        out_specs=[
            pl.BlockSpec((chunk, SSM_DIM), lambda s, c: (s * n_chunks + c, 0)),
            pl.BlockSpec((1, SSM_HEADS, HEAD, SSM_STATE), lambda s, c: (s, 0, 0, 0)),
        ],
        out_shape=[jax.ShapeDtypeStruct((n_seq * n_chunks * chunk, SSM_DIM), BF16),
                   jax.ShapeDtypeStruct((n_seq, SSM_HEADS, HEAD, SSM_STATE), F32)],
        scratch_shapes=[pltpu.VMEM((CONV_HIST + chunk, CONV_DIM), F32),
                        pltpu.VMEM((chunk, SSM_DIM), F32)],
        compiler_params=pltpu.CompilerParams(
            dimension_semantics=("parallel", "arbitrary"), vmem_limit_bytes=VMEM_LIMIT),
        name="ssd_mix_c%d" % chunk,
    )(p_xbc, p_z, p_dt, hist, h0, *wts)


def _merge_kernel(x_ref, yrw_ref, yssm_ref, gate_ref, bg_ref, wrw_ref, wssm_ref, wout_ref,
                  g_ref, b_ref, o_ref, ob_ref):
    gates = _sigmoid(gate_ref[...] + bg_ref[...])
    merged = (gates[:, 0:D_MODEL] * _dot(yrw_ref[...], wrw_ref[...])
              + gates[:, D_MODEL:2 * D_MODEL] * _dot(yssm_ref[...], wssm_ref[...]))
    s = ALPHA * x_ref[...] + _dot(merged.astype(BF16), wout_ref[...])
    y = _layer_norm(s, g_ref[...], b_ref[...])
    o_ref[...] = y
    ob_ref[...] = y.astype(BF16)


def _merge(x, y_rw, y_ssm, p_gate, b_gate, w_rw_out, w_ssm_out, w_out, ln_g, ln_b):
    m, d = x.shape
    tm = MERGE_TM
    row = lambda i: (i, 0)
    const = lambda i: (0, 0)
    resident = lambda w: pl.BlockSpec(w.shape, const, pipeline_mode=pl.Buffered(1))
    return pl.pallas_call(
        _merge_kernel,
        grid=(m // tm,),
        in_specs=[
            pl.BlockSpec((tm, d), row),
            pl.BlockSpec((tm, RW_DIM), row),
            pl.BlockSpec((tm, SSM_DIM), row),
            pl.BlockSpec((tm, 2 * d), row),
            pl.BlockSpec((1, 2 * d), const),
            resident(w_rw_out), resident(w_ssm_out), resident(w_out),
            pl.BlockSpec((1, d), const),
            pl.BlockSpec((1, d), const),
        ],
        out_specs=[pl.BlockSpec((tm, d), row), pl.BlockSpec((tm, d), row)],
        out_shape=[jax.ShapeDtypeStruct((m, d), F32), jax.ShapeDtypeStruct((m, d), BF16)],
        compiler_params=pltpu.CompilerParams(
            dimension_semantics=("parallel",), vmem_limit_bytes=VMEM_LIMIT),
        name="merge_ln",
    )(x, y_rw, y_ssm, p_gate, b_gate, w_rw_out, w_ssm_out, w_out, ln_g, ln_b)


def _pad_cols(x, width):
    return jnp.pad(x, [(0, 0)] * (x.ndim - 1) + [(0, width - x.shape[-1])])


def _rw_cols(x):
    o_wd, o_ad, o_gd = 3 * RW_DIM, 3 * RW_DIM + RW_DECAY_LORA, 3 * RW_DIM + RW_DECAY_LORA + RW_AAA_LORA
    return jnp.concatenate([
        x[..., :o_wd],
        _pad_cols(x[..., o_wd:o_ad], LORA_PAD),
        _pad_cols(x[..., o_ad:o_gd], LORA_PAD),
        x[..., o_gd:],
    ], axis=-1)


def _rw_cols_inv(x):
    return jnp.concatenate([
        x[..., :RW_OFF_WD],
        x[..., RW_OFF_WD:RW_OFF_WD + RW_DECAY_LORA],
        x[..., RW_OFF_AD:RW_OFF_AD + RW_AAA_LORA],
        x[..., RW_OFF_GD:],
    ], axis=-1)


def _block_ones(n, blk):
    i = jnp.arange(n) // blk
    return (i[:, None] == i[None, :]).astype(BF16)


def kernel(x_prompt, x_sample, state_rwkv_shift, state_wkv, state_conv, state_ssm, meta_tokens, ffn1_gu, ffn1_dn, ln1_g, ln1_b, w_in, b_gate, rw_mu, rw_w0, rw_w2, rw_a0, rw_a2, rw_g2, rw_kk, rw_ka, rw_rk, rw_gn_w, rw_gn_b, conv_w, conv_b, dt_bias, a_log, d_skip, ssm_norm_w, w_rw_out, w_ssm_out, w_out, ln2_g, ln2_b, ffn2_gu, ffn2_dn, ln3_g, ln3_b):
    lyr = 0
    row = lambda t: t[lyr].reshape(1, -1).astype(F32)

    head_rows = jnp.concatenate([jnp.zeros((PAD, D_MODEL), F32), meta_tokens.astype(F32)], axis=0)
    xp = jnp.concatenate([jnp.broadcast_to(head_rows[None], (BATCH, PAD + N_META, D_MODEL)), x_prompt], axis=1)
    x0 = jnp.concatenate([xp.reshape(P_ROWS, D_MODEL), x_sample.reshape(S_ROWS, D_MODEL)], axis=0)

    x1, x1b = _ffn_ln(x0, ffn1_gu[lyr].astype(BF16), ffn1_dn[lyr].astype(BF16), row(ln1_g), row(ln1_b))

    w = w_in[lyr]
    o_z = RW_SHIFT_COLS
    o_xbc = o_z + SSM_DIM
    o_dt = o_xbc + CONV_DIM
    o_gate = o_dt + SSM_HEADS
    p_rw = _matmul(x1b, _rw_cols(w[:, :o_z]).astype(BF16), "proj_rw")
    p_z = _matmul(x1b, w[:, o_z:o_xbc].astype(BF16), "proj_z")
    p_xbc = _matmul(x1b, w[:, o_xbc:o_dt].astype(BF16), "proj_xbc")
    p_dt = _matmul(x1b, _pad_cols(w[:, o_dt:o_gate], DT_COLS).astype(BF16), "proj_dt")
    p_gate = _matmul(x1b, w[:, o_gate:].astype(BF16), "proj_gate")

    pad_rows = lambda t, n: jnp.pad(t, ((0, n - t.shape[0]), (0, 0)))
    rw_wts = [
        _rw_cols(rw_mu[lyr]).reshape(1, RW_COLS), row(rw_w0),
        pad_rows(rw_w2[lyr], LORA_PAD).astype(BF16), row(rw_a0),
        pad_rows(rw_a2[lyr], LORA_PAD).astype(BF16), rw_g2[lyr].astype(BF16),
        row(rw_kk), row(rw_ka), row(rw_rk), row(rw_gn_w), row(rw_gn_b),
        _block_ones(2 * LANES, HEAD),
    ]
    y_rw_p, wkv_p = _rwkv_mix(
        p_rw, jnp.zeros((BATCH, 1, RW_COLS), F32), jnp.zeros((BATCH, RW_HEADS, HEAD, HEAD), F32), rw_wts,
        row0=0, n_seq=BATCH, n_chunks=SEQ_ROWS // CHUNK, chunk=CHUNK, pad=PAD)
    y_rw_s, wkv_s = _rwkv_mix(
        p_rw, _rw_cols(state_rwkv_shift[lyr]), state_wkv[lyr], rw_wts,
        row0=P_ROWS, n_seq=DEC_BATCH, n_chunks=1, chunk=DEC_SEQ, pad=0)

    head_of_lane = jnp.arange(SSM_DIM) // HEAD
    expand = (jnp.arange(DT_COLS)[:, None] == head_of_lane[None, :]).astype(BF16)
    ssd_wts = [
        conv_w[lyr], row(conv_b), _pad_cols(row(dt_bias), DT_COLS), _pad_cols(row(a_log), DT_COLS),
        jnp.repeat(d_skip[lyr], HEAD).reshape(1, SSM_DIM), row(ssm_norm_w),
        expand,
    ]
    hist_rows = lambda t: jnp.pad(t, ((0, 0), (CONV_HIST - (CONV_W - 1), 0), (0, 0)))
    y_ssm_p, ssm_p = _ssd_mix(
        p_xbc, p_z, p_dt, jnp.zeros((BATCH, CONV_HIST, CONV_DIM), F32),
        jnp.zeros((BATCH, SSM_HEADS, HEAD, SSM_STATE), F32), ssd_wts,
        row0=0, n_seq=BATCH, n_chunks=SEQ_ROWS // CHUNK, chunk=CHUNK, pad=PAD)
    y_ssm_s, ssm_s = _ssd_mix(
        p_xbc, p_z, p_dt, hist_rows(state_conv[lyr]), state_ssm[lyr], ssd_wts,
        row0=P_ROWS, n_seq=DEC_BATCH, n_chunks=1, chunk=DEC_SEQ, pad=0)

    y_rw = jnp.concatenate([y_rw_p, y_rw_s], axis=0)
    y_ssm = jnp.concatenate([y_ssm_p, y_ssm_s], axis=0)
    x2, _ = _merge(x1, y_rw, y_ssm, p_gate, row(b_gate), w_rw_out[lyr].astype(BF16),
                   w_ssm_out[lyr].astype(BF16), w_out[lyr].astype(BF16), row(ln2_g), row(ln2_b))
    x3, _ = _ffn_ln(x2, ffn2_gu[lyr].astype(BF16), ffn2_dn[lyr].astype(BF16), row(ln3_g), row(ln3_b))

    y_prompt = x3[:P_ROWS].reshape(BATCH, SEQ_ROWS, D_MODEL)[:, PAD + N_META:]
    y_sample = x3[P_ROWS:].reshape(DEC_BATCH, DEC_SEQ, D_MODEL)
    prw_p = p_rw[:P_ROWS].reshape(BATCH, SEQ_ROWS, RW_COLS)
    prw_s = p_rw[P_ROWS:].reshape(DEC_BATCH, DEC_SEQ, RW_COLS)
    pxbc_p = p_xbc[:P_ROWS].reshape(BATCH, SEQ_ROWS, CONV_DIM)
    pxbc_s = p_xbc[P_ROWS:].reshape(DEC_BATCH, DEC_SEQ, CONV_DIM)
    return (y_prompt, y_sample,
            _rw_cols_inv(prw_p[:, -1:])[None], wkv_p[None], pxbc_p[:, -(CONV_W - 1):][None], ssm_p[None],
            _rw_cols_inv(prw_s[:, -1:])[None], wkv_s[None], pxbc_s[:, -(CONV_W - 1):][None], ssm_s[None])
```

```python
import functools
import math

import jax
import jax.numpy as jnp
from jax import lax
from jax.experimental import pallas as pl
from jax.experimental.pallas import tpu as pltpu

F32 = jnp.float32
BF16 = jnp.bfloat16

D_MODEL = 2048
BATCH = 4
SEQ = 2048
DEC_BATCH = 16
DEC_SEQ = 16
CHUNK = 64
N_META = 16
PAD = (-N_META) % CHUNK
HEAD_ROWS = PAD + N_META
SEQ_CHUNKS = (HEAD_ROWS + SEQ) // CHUNK
F_ROWS = BATCH * SEQ
S_ROWS = DEC_BATCH * DEC_SEQ
ROW_S0 = F_ROWS
ROW_H0 = F_ROWS + S_ROWS
X_ROWS = S_ROWS + BATCH * HEAD_ROWS
M_ROWS = F_ROWS + X_ROWS

HEAD = 64
RW_DIM = D_MODEL // 2
RW_HEADS = RW_DIM // HEAD
RW_DECAY_LORA = 96
RW_AAA_LORA = 96
RW_GATE_LORA = 256
RW_SHIFT_COLS = 3 * RW_DIM + RW_DECAY_LORA + RW_AAA_LORA + RW_GATE_LORA
RW_GN_EPS = 64e-5
SSM_DIM = D_MODEL
SSM_HEADS = SSM_DIM // HEAD
SSM_GROUPS = 4
SSM_HPG = SSM_HEADS // SSM_GROUPS
SSM_STATE = 128
CONV_W = 4
BC_DIM = SSM_GROUPS * SSM_STATE
CONV_DIM = SSM_DIM + 2 * BC_DIM
RMS_EPS = 1e-5
D_FF = 5632
LN_EPS = 1e-5
DEPTH = 1
ALPHA = (2 * DEPTH) ** 0.25

LANES = 128
SUBLANES = 8
VMEM_LIMIT = 56 * 1024 * 1024

LORA_PAD = LANES
RW_OFF_WD = 3 * RW_DIM
RW_OFF_AD = RW_OFF_WD + LORA_PAD
RW_OFF_GD = RW_OFF_AD + LORA_PAD
RW_COLS = RW_OFF_GD + RW_GATE_LORA
DT_COLS = LANES
CONV_HIST = SUBLANES

FFN_TM = 512
FFN_TF = 512
MM_TM = 512
MM_TN = 1024
MERGE_TM = 256

DT_SLOT = 512
PC_RW = 0
PC_DT = PC_RW + RW_COLS
PC_GATE = PC_DT + DT_SLOT
PC_Z = PC_GATE + 2 * D_MODEL
PC_XS = PC_Z + SSM_DIM
PC_BC = PC_XS + SSM_DIM
P_COLS = PC_BC + 2 * BC_DIM
assert PC_DT % DT_COLS == 0 and PC_GATE % D_MODEL == 0 and PC_Z % SSM_DIM == 0
assert PC_XS % SSM_DIM == 0 and PC_BC % (2 * BC_DIM) == 0 and P_COLS % MM_TN == 0
assert F_ROWS % FFN_TM == 0 and X_ROWS == FFN_TM and M_ROWS % MM_TM == 0 and ROW_H0 % MERGE_TM == 0


def _dot(a, b):
    return jnp.dot(a, b, preferred_element_type=F32)


def _dot_nt(a, b):
    return lax.dot_general(a, b, (((1,), (1,)), ((), ())), preferred_element_type=F32)


def _dot_tn(a, b):
    return lax.dot_general(a, b, (((0,), (0,)), ((), ())), preferred_element_type=F32)


def _split3(x):
    hi = x.astype(BF16)
    r1 = x - hi.astype(F32)
    mid = r1.astype(BF16)
    lo = (r1 - mid.astype(F32)).astype(BF16)
    return hi, mid, lo


def _dot_sel_r(x, sel):
    hi, mid, lo = _split3(x)
    return _dot(hi, sel) + _dot(mid, sel) + _dot(lo, sel)


def _dot_sel_l(sel, x):
    hi, mid, lo = _split3(x)
    return _dot(sel, hi) + _dot(sel, mid) + _dot(sel, lo)


def _sigmoid(x):
    return jax.nn.sigmoid(x)


def _softplus(x):
    return jnp.maximum(x, 0.0) + jnp.log1p(jnp.exp(-jnp.abs(x)))


def _layer_norm(s, g, b):
    mu = jnp.mean(s, axis=-1, keepdims=True)
    d = s - mu
    var = jnp.mean(d * d, axis=-1, keepdims=True)
    return d * lax.rsqrt(var + LN_EPS) * g + b


N_MAIN_TILES = F_ROWS // FFN_TM


def _ffn_step(xb_ref, wg_ref, wu_ref, wd_ref, acc_ref):
    xb = xb_ref[...]
    gate = _dot(xb, wg_ref[...])
    up = _dot(xb, wu_ref[...])
    h = (gate * _sigmoid(gate) * up).astype(BF16)
    acc_ref[...] += _dot(h, wd_ref[...])


def _ffn_in_kernel(xm_ref, xe_ref, wg_ref, wu_ref, wd_ref, g_ref, b_ref, o_ref, ob_ref,
                   xb_ref, acc_ref):
    i, f = pl.program_id(0), pl.program_id(1)
    load_x = lambda: jnp.where(i < N_MAIN_TILES, xm_ref[...], xe_ref[...])

    @pl.when(f == 0)
    def _():
        xb_ref[...] = load_x().astype(BF16)
        acc_ref[...] = jnp.zeros_like(acc_ref)

    _ffn_step(xb_ref, wg_ref, wu_ref, wd_ref, acc_ref)

    @pl.when(f == pl.num_programs(1) - 1)
    def _():
        y = _layer_norm(ALPHA * load_x() + 0.5 * acc_ref[...], g_ref[...], b_ref[...])
        o_ref[...] = y
        ob_ref[...] = y.astype(BF16)


def _ffn_out_kernel(x_ref, wg_ref, wu_ref, wd_ref, g_ref, b_ref, om_ref, oe_ref, xb_ref, acc_ref):
    i, f = pl.program_id(0), pl.program_id(1)

    @pl.when(f == 0)
    def _():
        xb_ref[...] = x_ref[...].astype(BF16)
        acc_ref[...] = jnp.zeros_like(acc_ref)

    _ffn_step(xb_ref, wg_ref, wu_ref, wd_ref, acc_ref)

    @pl.when(f == pl.num_programs(1) - 1)
    def _():
        y = _layer_norm(ALPHA * x_ref[...] + 0.5 * acc_ref[...], g_ref[...], b_ref[...])

        @pl.when(i < N_MAIN_TILES)
        def _():
            om_ref[...] = y

        @pl.when(i == N_MAIN_TILES)
        def _():
            oe_ref[...] = y


def _ffn_specs(d):
    nf = D_FF // FFN_TF
    return nf, [
        pl.BlockSpec((d, FFN_TF), lambda i, f: (0, f)),
        pl.BlockSpec((d, FFN_TF), lambda i, f: (0, f + nf)),
        pl.BlockSpec((FFN_TF, d), lambda i, f: (f, 0)),
        pl.BlockSpec((1, d), lambda i, f: (0, 0)),
        pl.BlockSpec((1, d), lambda i, f: (0, 0)),
    ]


_FFN_PARAMS = dict(dimension_semantics=("arbitrary", "arbitrary"), vmem_limit_bytes=VMEM_LIMIT)
_main_tile = lambda i, f: (jnp.minimum(i, N_MAIN_TILES - 1), 0)
_last_tile = lambda i, f: (0, 0)
_any_tile = lambda i, f: (i, 0)


def _ffn_ln_in(x_main, x_extra, w_gu, w_dn, ln_g, ln_b):
    d = x_main.shape[1]
    nf, wspecs = _ffn_specs(d)
    tile = (FFN_TM, d)
    return pl.pallas_call(
        _ffn_in_kernel,
        grid=(M_ROWS // FFN_TM, nf),
        in_specs=[pl.BlockSpec(tile, _main_tile),
                  pl.BlockSpec(tile, _last_tile, pipeline_mode=pl.Buffered(1))] + wspecs,
        out_specs=[pl.BlockSpec(tile, _any_tile), pl.BlockSpec(tile, _any_tile)],
        out_shape=[jax.ShapeDtypeStruct((M_ROWS, d), F32), jax.ShapeDtypeStruct((M_ROWS, d), BF16)],
        scratch_shapes=[pltpu.VMEM(tile, BF16), pltpu.VMEM(tile, F32)],
        compiler_params=pltpu.CompilerParams(**_FFN_PARAMS),
        name="ffn_ln_in",
    )(x_main, x_extra, w_gu, w_gu, w_dn, ln_g, ln_b)


def _ffn_ln_out(x, w_gu, w_dn, ln_g, ln_b):
    d = x.shape[1]
    nf, wspecs = _ffn_specs(d)
    tile = (FFN_TM, d)
    return pl.pallas_call(
        _ffn_out_kernel,
        grid=(M_ROWS // FFN_TM, nf),
        in_specs=[pl.BlockSpec(tile, _any_tile)] + wspecs,
        out_specs=[pl.BlockSpec(tile, _main_tile), pl.BlockSpec(tile, _last_tile)],
        out_shape=[jax.ShapeDtypeStruct((F_ROWS, d), F32), jax.ShapeDtypeStruct((X_ROWS, d), F32)],
        scratch_shapes=[pltpu.VMEM(tile, BF16), pltpu.VMEM(tile, F32)],
        compiler_params=pltpu.CompilerParams(**_FFN_PARAMS),
        name="ffn_ln_out",
    )(x, w_gu, w_gu, w_dn, ln_g, ln_b)


def _mm_kernel(x_ref, w_ref, o_ref):
    o_ref[...] = _dot(x_ref[...], w_ref[...])


def _project(x, w):
    m, k = x.shape
    n = w.shape[1]
    return pl.pallas_call(
        _mm_kernel,
        grid=(n // MM_TN, m // MM_TM),
        in_specs=[pl.BlockSpec((MM_TM, k), lambda j, i: (i, 0)),
                  pl.BlockSpec((k, MM_TN), lambda j, i: (0, j))],
        out_specs=pl.BlockSpec((MM_TM, MM_TN), lambda j, i: (i, j)),
        out_shape=jax.ShapeDtypeStruct((m, n), F32),
        compiler_params=pltpu.CompilerParams(
            dimension_semantics=("parallel", "parallel"), vmem_limit_bytes=VMEM_LIMIT),
        name="proj_in",
    )(x, w)


def _rwkv_kernel(p_ref, hist_ref, s0_ref, mu_ref, w0_ref, w2_ref, a0_ref, a2_ref, g2_ref,
                 kk_ref, ka_ref, rk_ref, gnw_ref, gnb_ref, seg_ref, y_other_ref,
                 y_ref, s_ref, prev_ref, yacc_ref, *, chunk, pad):
    del y_other_ref
    c = pl.program_id(1)
    rows = lax.broadcasted_iota(jnp.int32, (chunk, 1), 0)

    @pl.when(c == 0)
    def _():
        prev_ref[...] = hist_ref[0]
        s_ref[...] = s0_ref[...]

    p = p_ref[...]
    if pad:
        p = jnp.where(jnp.logical_and(c == 0, rows < pad), 0.0, p)
    prev = jnp.where(rows == 0, prev_ref[...], pltpu.roll(p, 1, 0))
    prev_ref[...] = p[chunk - 1:chunk, :]
    ps = p + (prev - p) * mu_ref[...]

    r = ps[:, 0:RW_DIM]
    k = ps[:, RW_DIM:2 * RW_DIM]
    v = ps[:, 2 * RW_DIM:3 * RW_DIM]
    wd = ps[:, RW_OFF_WD:RW_OFF_WD + LORA_PAD]
    ad = ps[:, RW_OFF_AD:RW_OFF_AD + LORA_PAD]
    gd = ps[:, RW_OFF_GD:RW_OFF_GD + RW_GATE_LORA]

    w_log = -_softplus(-(w0_ref[...] + _dot(jnp.tanh(wd).astype(BF16), w2_ref[...]))) - 0.5
    lw = -jnp.exp(w_log)
    a = _sigmoid(a0_ref[...] + _dot(ad.astype(BF16), a2_ref[...]))
    g = _dot(_sigmoid(gd).astype(BF16), g2_ref[...])

    seg = seg_ref[...]

    def head_sum(x):
        w = seg.shape[0]
        return jnp.concatenate(
            [_dot_sel_r(x[:, j * w:(j + 1) * w], seg) for j in range(RW_DIM // w)], axis=1)

    kk = k * kk_ref[...]
    kk = kk / jnp.maximum(jnp.sqrt(head_sum(kk * kk)), 1e-12)
    k = k * (1.0 + (a - 1.0) * ka_ref[...])
    b = kk * a

    ci = lax.broadcasted_iota(jnp.int32, (chunk, chunk), 0)
    cj = lax.broadcasted_iota(jnp.int32, (chunk, chunk), 1)
    tri_incl = ci >= cj
    tri_strict = ci > cj
    cum = _dot_sel_l(tri_incl.astype(BF16), lw)
    cum_end = cum[chunk - 1:chunk, :]
    e_neg = jnp.exp(-cum)
    kt = (kk * jnp.exp(cum - lw)).astype(BF16)
    rt = (r * jnp.exp(cum)).astype(BF16)
    kd = (k * e_neg).astype(BF16)
    bd = (b * e_neg).astype(BF16)
    e_end = jnp.exp(cum_end - cum)
    k_end = (k * e_end).astype(BF16)
    b_end_neg = (-(b * e_end)).astype(BF16)
    p_end = jnp.exp(cum_end)
    vb = v.astype(BF16)
    eye = (ci == cj).astype(F32)

    heads = range(RW_HEADS)
    sls = [slice(h * HEAD, (h + 1) * HEAD) for h in heads]
    s_old = [s_ref[0, h] for h in heads]
    s_b = [s.astype(BF16) for s in s_old]
    amats = [_dot_nt(jnp.concatenate([kt[:, sl], rt[:, sl]], axis=0),
                     jnp.concatenate([bd[:, sl], kd[:, sl]], axis=0)) for sl in sls]
    a_kk = [jnp.where(tri_strict, m[:chunk, chunk:], 0.0).astype(BF16) for m in amats]
    a_rb_neg = [jnp.where(tri_incl, -m[chunk:, :chunk], 0.0) for m in amats]
    a_rk = [jnp.where(tri_incl, m[chunk:, chunk:], 0.0) for m in amats]
    a_y = [jnp.concatenate([x, y], axis=1).astype(BF16) for x, y in zip(a_rk, a_rb_neg)]
    nb = [jnp.where(tri_strict, -m[:chunk, :chunk], 0.0) for m in amats]
    ts = [eye + x for x in nb]
    nbb = [x.astype(BF16) for x in nb]
    qs = [_dot(x, x) for x in nbb]
    n = 2
    while n < chunk:
        last = 2 * n >= chunk
        if last:
            ts = [t + _dot(q.astype(BF16), t.astype(BF16)) for t, q in zip(ts, qs)]
        else:
            both = [_dot(q.astype(BF16), jnp.concatenate([t, q], axis=1).astype(BF16))
                    for t, q in zip(ts, qs)]
            ts = [t + x[:, :chunk] for t, x in zip(ts, both)]
            qs = [x[:, chunk:] for x in both]
        n *= 2
    v_h = [vb[:, sl] for sl in sls]
    rhs = [_dot_nt(kt[:, sl], sb) + _dot(akk, vh) for sl, sb, akk, vh in zip(sls, s_b, a_kk, v_h)]
    ub = [_dot(t.astype(BF16), x.astype(BF16)).astype(BF16) for t, x in zip(ts, rhs)]
    vu = [jnp.concatenate([vh, u], axis=0) for vh, u in zip(v_h, ub)]
    for h in heads:
        sl = sls[h]
        yacc_ref[:, sl] = _dot_nt(rt[:, sl], s_b[h]) + _dot(a_y[h], vu[h])
        s_ref[0, h] = s_old[h] * p_end[:, sl] + _dot_tn(
            vu[h], jnp.concatenate([k_end[:, sl], b_end_neg[:, sl]], axis=0))

    y = yacc_ref[...]
    inv_n = 1.0 / HEAD
    mean = head_sum(y) * inv_n
    yc = y - mean
    var = head_sum(yc * yc) * inv_n
    yn = yc * lax.rsqrt(var + RW_GN_EPS) * gnw_ref[...] + gnb_ref[...]
    bonus = head_sum(r * k * rk_ref[...]) * v
    y_ref[...] = ((yn + bonus) * g).astype(y_ref.dtype)


def _seq_rows(prompt):
    if prompt:
        blk = lambda s, c: jnp.where(c == 0, ROW_H0 // CHUNK + s, s * (SEQ // CHUNK) + c - 1)
        return (BATCH, SEQ_CHUNKS), CHUNK, PAD, blk
    return (DEC_BATCH, 1), DEC_SEQ, 0, lambda s, c: ROW_S0 // DEC_SEQ + s


_MIX_PARAMS = dict(dimension_semantics=("parallel", "arbitrary"), vmem_limit_bytes=VMEM_LIMIT)


def _rwkv_mix(p_all, hist, s0, wts, y_other, *, prompt):
    grid, chunk, pad, blk = _seq_rows(prompt)
    const2 = lambda s, c: (0, 0)
    wspecs = [pl.BlockSpec(w.shape, const2) for w in wts]
    n_in = 3 + len(wts)
    return pl.pallas_call(
        functools.partial(_rwkv_kernel, chunk=chunk, pad=pad),
        grid=grid,
        in_specs=[
            pl.BlockSpec((chunk, RW_COLS), lambda s, c: (blk(s, c), PC_RW // RW_COLS)),
            pl.BlockSpec((1, 1, RW_COLS), lambda s, c: (s, 0, 0)),
            pl.BlockSpec((1, RW_HEADS, HEAD, HEAD), lambda s, c: (s, 0, 0, 0)),
        ] + wspecs + [pl.BlockSpec(memory_space=pl.ANY)],
        out_specs=[
            pl.BlockSpec((chunk, RW_DIM), lambda s, c: (blk(s, c), 0)),
            pl.BlockSpec((1, RW_HEADS, HEAD, HEAD), lambda s, c: (s, 0, 0, 0)),
        ],
        out_shape=[jax.ShapeDtypeStruct(y_other.shape, y_other.dtype),
                   jax.ShapeDtypeStruct((grid[0], RW_HEADS, HEAD, HEAD), F32)],
        scratch_shapes=[pltpu.VMEM((1, RW_COLS), F32), pltpu.VMEM((chunk, RW_DIM), F32)],
        input_output_aliases={n_in: 0},
        compiler_params=pltpu.CompilerParams(**_MIX_PARAMS),
        name="rwkv_mix_c%d" % chunk,
    )(p_all, hist, s0, *wts, y_other)


def _ssd_kernel(xs_ref, bc_ref, z_ref, dt_ref, hist_ref, h0_ref, cw_ref, cb_ref, dtb_ref, alog_ref,
                dskip_ref, nw_ref, exp_ref, y_other_ref,
                y_ref, h_ref, xpad_ref, yacc_ref, *, chunk, pad):
    del y_other_ref
    c = pl.program_id(1)
    rows = lax.broadcasted_iota(jnp.int32, (chunk, 1), 0)

    @pl.when(c == 0)
    def _():
        xpad_ref[0:CONV_HIST, :] = hist_ref[0]
        h_ref[...] = h0_ref[...]

    u = jnp.concatenate([xs_ref[...], bc_ref[...]], axis=1)
    if pad:
        is_pad = jnp.logical_and(c == 0, rows < pad)
        u = jnp.where(is_pad, 0.0, u)
    xpad_ref[CONV_HIST:CONV_HIST + chunk, :] = u
    conv = cb_ref[...] + u * cw_ref[CONV_W - 1:CONV_W, :]
    for i in range(CONV_W - 1):
        back = CONV_W - 1 - i
        conv = conv + xpad_ref[CONV_HIST - back:CONV_HIST - back + chunk, :] * cw_ref[i:i + 1, :]
    xpad_ref[0:CONV_HIST, :] = xpad_ref[chunk:chunk + CONV_HIST, :]
    xbc = conv * _sigmoid(conv)
    xs = xbc[:, 0:SSM_DIM]
    bm = xbc[:, SSM_DIM:SSM_DIM + BC_DIM].astype(BF16)
    cm = xbc[:, SSM_DIM + BC_DIM:CONV_DIM].astype(BF16)

    dt = _softplus(dt_ref[...] + dtb_ref[...])
    if pad:
        dt = jnp.where(is_pad, 0.0, dt)
    da = dt * (-jnp.exp(alog_ref[...]))
    ci = lax.broadcasted_iota(jnp.int32, (chunk, chunk), 0)
    cj = lax.broadcasted_iota(jnp.int32, (chunk, chunk), 1)
    causal = ci >= cj
    acs = _dot_sel_l(causal.astype(BF16), da)
    acs_t = acs.T
    acs_end = acs[chunk - 1:chunk, :]

    expand = exp_ref[...]
    xdt = xs * _dot_sel_r(dt, expand)
    xdt_b = xdt.astype(BF16)
    xdt_end = (xdt * _dot_sel_r(jnp.exp(acs_end - acs), expand)).astype(BF16)
    e_acs = _dot_sel_r(jnp.exp(acs), expand)
    decay_end = jnp.exp(acs_t[:, chunk - 1:chunk])

    gw = SSM_HPG * HEAD
    groups = range(SSM_GROUPS)
    heads = range(SSM_HEADS)
    bm_g = [bm[:, g * SSM_STATE:(g + 1) * SSM_STATE] for g in groups]
    cm_g = [cm[:, g * SSM_STATE:(g + 1) * SSM_STATE] for g in groups]
    h_old = [h_ref[0, hd] for hd in heads]
    cb = [_dot_nt(cm_g[g], bm_g[g]) for g in groups]
    y_off = [_dot_nt(cm_g[g], jnp.concatenate(
        [h_old[g * SSM_HPG + hh].astype(BF16) for hh in range(SSM_HPG)], axis=0)) for g in groups]
    st = [_dot_tn(xdt_end[:, g * gw:(g + 1) * gw], bm_g[g]) for g in groups]
    for hd in heads:
        sl = slice(hd * HEAD, (hd + 1) * HEAD)
        seg = acs[:, hd:hd + 1] - acs_t[hd:hd + 1, :]
        lmat = jnp.exp(jnp.where(causal, seg, -jnp.inf))
        yacc_ref[:, sl] = _dot((cb[hd // SSM_HPG] * lmat).astype(BF16), xdt_b[:, sl])
    for hd in heads:
        g, hh = divmod(hd, SSM_HPG)
        h_ref[0, hd] = h_old[hd] * decay_end[hd:hd + 1, :] + st[g][hh * HEAD:(hh + 1) * HEAD, :]

    y = yacc_ref[...] + jnp.concatenate(y_off, axis=1) * e_acs + xs * dskip_ref[...]
    zz = z_ref[...]
    y = y * (zz * _sigmoid(zz))
    parts = []
    for g in groups:
        yg = y[:, g * gw:(g + 1) * gw]
        ms = jnp.mean(yg * yg, axis=-1, keepdims=True)
        parts.append(yg * lax.rsqrt(ms + RMS_EPS))
    y_ref[...] = (jnp.concatenate(parts, axis=1) * nw_ref[...]).astype(y_ref.dtype)


def _ssd_mix(p_all, hist, h0, wts, y_other, *, prompt):
    grid, chunk, pad, blk = _seq_rows(prompt)
    const2 = lambda s, c: (0, 0)
    wspecs = [pl.BlockSpec(w.shape, const2) for w in wts]
    n_in = 6 + len(wts)
    return pl.pallas_call(
        functools.partial(_ssd_kernel, chunk=chunk, pad=pad),
        grid=grid,
        in_specs=[
            pl.BlockSpec((chunk, SSM_DIM), lambda s, c: (blk(s, c), PC_XS // SSM_DIM)),
            pl.BlockSpec((chunk, 2 * BC_DIM), lambda s, c: (blk(s, c), PC_BC // (2 * BC_DIM))),
            pl.BlockSpec((chunk, SSM_DIM), lambda s, c: (blk(s, c), PC_Z // SSM_DIM)),
            pl.BlockSpec((chunk, DT_COLS), lambda s, c: (blk(s, c), PC_DT // DT_COLS)),
            pl.BlockSpec((1, CONV_HIST, CONV_DIM), lambda s, c: (s, 0, 0)),
            pl.BlockSpec((1, SSM_HEADS, HEAD, SSM_STATE), lambda s, c: (s, 0, 0, 0)),
        ] + wspecs + [pl.BlockSpec(memory_space=pl.ANY)],
        out_specs=[
            pl.BlockSpec((chunk, SSM_DIM), lambda s, c: (blk(s, c), 0)),
            pl.BlockSpec((1, SSM_HEADS, HEAD, SSM_STATE), lambda s, c: (s, 0, 0, 0)),
        ],
        out_shape=[jax.ShapeDtypeStruct(y_other.shape, y_other.dtype),
                   jax.ShapeDtypeStruct((grid[0], SSM_HEADS, HEAD, SSM_STATE), F32)],
        scratch_shapes=[pltpu.VMEM((CONV_HIST + chunk, CONV_DIM), F32),
                        pltpu.VMEM((chunk, SSM_DIM), F32)],
        input_output_aliases={n_in: 0},
        compiler_params=pltpu.CompilerParams(**_MIX_PARAMS),
        name="ssd_mix_c%d" % chunk,
    )(p_all, p_all, p_all, p_all, hist, h0, *wts, y_other)


def _merge_kernel(x_ref, yrw_ref, yssm_ref, ga_ref, gb_ref, bga_ref, bgb_ref, wrw_ref, wssm_ref,
                  wout_ref, g_ref, b_ref, o_ref):
    merged = (_sigmoid(ga_ref[...] + bga_ref[...]) * _dot(yrw_ref[...], wrw_ref[...])
              + _sigmoid(gb_ref[...] + bgb_ref[...]) * _dot(yssm_ref[...], wssm_ref[...]))
    s = ALPHA * x_ref[...] + _dot(merged.astype(BF16), wout_ref[...])
    o_ref[...] = _layer_norm(s, g_ref[...], b_ref[...])


def _merge(x, y_rw, y_ssm, p_all, b_gate, w_rw_out, w_ssm_out, w_out, ln_g, ln_b):
    m, d = x.shape
    tm = MERGE_TM
    row = lambda i: (i, 0)
    const = lambda i: (0, 0)
    resident = lambda w: pl.BlockSpec(w.shape, const, pipeline_mode=pl.Buffered(1))
    return pl.pallas_call(
        _merge_kernel,
        grid=(m // tm,),
        in_specs=[
            pl.BlockSpec((tm, d), row),
            pl.BlockSpec((tm, RW_DIM), row),
            pl.BlockSpec((tm, SSM_DIM), row),
            pl.BlockSpec((tm, d), lambda i: (i, PC_GATE // d)),
            pl.BlockSpec((tm, d), lambda i: (i, PC_GATE // d + 1)),
            pl.BlockSpec((1, d), lambda i: (0, 0)),
            pl.BlockSpec((1, d), lambda i: (0, 1)),
            resident(w_rw_out), resident(w_ssm_out), resident(w_out),
            pl.BlockSpec((1, d), const),
            pl.BlockSpec((1, d), const),
        ],
        out_specs=pl.BlockSpec((tm, d), row),
        out_shape=jax.ShapeDtypeStruct((m, d), F32),
        compiler_params=pltpu.CompilerParams(
            dimension_semantics=("parallel",), vmem_limit_bytes=VMEM_LIMIT),
        name="merge_ln",
    )(x, y_rw, y_ssm, p_all, p_all, b_gate, b_gate, w_rw_out, w_ssm_out, w_out, ln_g, ln_b)


def _pad_cols(x, width):
    return jnp.pad(x, [(0, 0)] * (x.ndim - 1) + [(0, width - x.shape[-1])])


def _rw_cols(x):
    o_wd, o_ad, o_gd = 3 * RW_DIM, 3 * RW_DIM + RW_DECAY_LORA, 3 * RW_DIM + RW_DECAY_LORA + RW_AAA_LORA
    return jnp.concatenate([
        x[..., :o_wd],
        _pad_cols(x[..., o_wd:o_ad], LORA_PAD),
        _pad_cols(x[..., o_ad:o_gd], LORA_PAD),
        x[..., o_gd:],
    ], axis=-1)


def _rw_cols_inv(x):
    return jnp.concatenate([
        x[..., :RW_OFF_WD],
        x[..., RW_OFF_WD:RW_OFF_WD + RW_DECAY_LORA],
        x[..., RW_OFF_AD:RW_OFF_AD + RW_AAA_LORA],
        x[..., RW_OFF_GD:],
    ], axis=-1)


def _block_ones(n, blk):
    i = jnp.arange(n) // blk
    return (i[:, None] == i[None, :]).astype(BF16)


def kernel(x_prompt, x_sample, state_rwkv_shift, state_wkv, state_conv, state_ssm, meta_tokens, ffn1_gu, ffn1_dn, ln1_g, ln1_b, w_in, b_gate, rw_mu, rw_w0, rw_w2, rw_a0, rw_a2, rw_g2, rw_kk, rw_ka, rw_rk, rw_gn_w, rw_gn_b, conv_w, conv_b, dt_bias, a_log, d_skip, ssm_norm_w, w_rw_out, w_ssm_out, w_out, ln2_g, ln2_b, ffn2_gu, ffn2_dn, ln3_g, ln3_b):
    lyr = 0
    row = lambda t: t[lyr].reshape(1, -1).astype(F32)

    head_rows = jnp.concatenate([jnp.zeros((PAD, D_MODEL), F32), meta_tokens.astype(F32)], axis=0)
    x_extra = jnp.concatenate([x_sample.reshape(S_ROWS, D_MODEL)] + [head_rows] * BATCH, axis=0)
    x1, x1b = _ffn_ln_in(x_prompt.reshape(F_ROWS, D_MODEL), x_extra, ffn1_gu[lyr].astype(BF16),
                         ffn1_dn[lyr].astype(BF16), row(ln1_g), row(ln1_b))

    w = w_in[lyr]
    o_z = RW_SHIFT_COLS
    o_xbc = o_z + SSM_DIM
    o_dt = o_xbc + CONV_DIM
    o_gate = o_dt + SSM_HEADS
    w_all = jnp.concatenate([
        _rw_cols(w[:, :o_z]), _pad_cols(w[:, o_dt:o_gate], DT_SLOT), w[:, o_gate:],
        w[:, o_z:o_xbc], w[:, o_xbc:o_dt]], axis=1).astype(BF16)
    p_all = _project(x1b, w_all)

    pad_rows = lambda t, n: jnp.pad(t, ((0, n - t.shape[0]), (0, 0)))
    rw_wts = [
        _rw_cols(rw_mu[lyr]).reshape(1, RW_COLS), row(rw_w0),
        pad_rows(rw_w2[lyr], LORA_PAD).astype(BF16), row(rw_a0),
        pad_rows(rw_a2[lyr], LORA_PAD).astype(BF16), rw_g2[lyr].astype(BF16),
        row(rw_kk), row(rw_ka), row(rw_rk), row(rw_gn_w), row(rw_gn_b),
        _block_ones(2 * LANES, HEAD),
    ]
    y_rw, wkv_s = _rwkv_mix(
        p_all, _rw_cols(state_rwkv_shift[lyr]), state_wkv[lyr], rw_wts,
        jnp.zeros((M_ROWS, RW_DIM), BF16), prompt=False)
    y_rw, wkv_p = _rwkv_mix(
        p_all, jnp.zeros((BATCH, 1, RW_COLS), F32), jnp.zeros((BATCH, RW_HEADS, HEAD, HEAD), F32), rw_wts,
        y_rw, prompt=True)

    head_of_lane = jnp.arange(SSM_DIM) // HEAD
    expand = (jnp.arange(DT_COLS)[:, None] == head_of_lane[None, :]).astype(BF16)
    ssd_wts = [
        conv_w[lyr], row(conv_b), _pad_cols(row(dt_bias), DT_COLS), _pad_cols(row(a_log), DT_COLS),
        jnp.repeat(d_skip[lyr], HEAD).reshape(1, SSM_DIM), row(ssm_norm_w),
        expand,
    ]
    hist_rows = lambda t: jnp.pad(t, ((0, 0), (CONV_HIST - (CONV_W - 1), 0), (0, 0)))
    y_ssm, ssm_s = _ssd_mix(
        p_all, hist_rows(state_conv[lyr]), state_ssm[lyr], ssd_wts,
        jnp.zeros((M_ROWS, SSM_DIM), BF16), prompt=False)
    y_ssm, ssm_p = _ssd_mix(
        p_all, jnp.zeros((BATCH, CONV_HIST, CONV_DIM), F32),
        jnp.zeros((BATCH, SSM_HEADS, HEAD, SSM_STATE), F32), ssd_wts, y_ssm, prompt=True)

    x2 = _merge(x1, y_rw, y_ssm, p_all, row(b_gate), w_rw_out[lyr].astype(BF16),
                w_ssm_out[lyr].astype(BF16), w_out[lyr].astype(BF16), row(ln2_g), row(ln2_b))
    y_frames, y_extra = _ffn_ln_out(x2, ffn2_gu[lyr].astype(BF16), ffn2_dn[lyr].astype(BF16),
                                    row(ln3_g), row(ln3_b))

    y_prompt = y_frames.reshape(BATCH, SEQ, D_MODEL)
    y_sample = y_extra[:S_ROWS].reshape(DEC_BATCH, DEC_SEQ, D_MODEL)
    tail = CONV_W - 1
    last_p = p_all[:F_ROWS].reshape(BATCH, SEQ, P_COLS)[:, -tail:]
    last_s = p_all[ROW_S0:ROW_S0 + S_ROWS].reshape(DEC_BATCH, DEC_SEQ, P_COLS)[:, -tail:]
    shift_of = lambda t: _rw_cols_inv(t[:, -1:, PC_RW:PC_RW + RW_COLS])[None]
    conv_of = lambda t: t[:, :, PC_XS:PC_XS + CONV_DIM][None]
    return (y_prompt, y_sample,
            shift_of(last_p), wkv_p[None], conv_of(last_p), ssm_p[None],
            shift_of(last_s), wkv_s[None], conv_of(last_s), ssm_s[None])
```

```python
import functools
import math

import jax
import jax.numpy as jnp
from jax import lax
from jax.experimental import pallas as pl
from jax.experimental.pallas import tpu as pltpu

F32 = jnp.float32
BF16 = jnp.bfloat16

D_MODEL = 2048
BATCH = 4
SEQ = 2048
DEC_BATCH = 16
DEC_SEQ = 16
CHUNK = 64
N_META = 16
PAD = (-N_META) % CHUNK
HEAD_ROWS = PAD + N_META
SEQ_CHUNKS = (HEAD_ROWS + SEQ) // CHUNK
F_ROWS = BATCH * SEQ
S_ROWS = DEC_BATCH * DEC_SEQ
ROW_S0 = F_ROWS
ROW_H0 = F_ROWS + S_ROWS
X_ROWS = S_ROWS + BATCH * HEAD_ROWS
M_ROWS = F_ROWS + X_ROWS

HEAD = 64
RW_DIM = D_MODEL // 2
RW_HEADS = RW_DIM // HEAD
RW_DECAY_LORA = 96
RW_AAA_LORA = 96
RW_GATE_LORA = 256
RW_SHIFT_COLS = 3 * RW_DIM + RW_DECAY_LORA + RW_AAA_LORA + RW_GATE_LORA
RW_GN_EPS = 64e-5
SSM_DIM = D_MODEL
SSM_HEADS = SSM_DIM // HEAD
SSM_GROUPS = 4
SSM_HPG = SSM_HEADS // SSM_GROUPS
SSM_STATE = 128
CONV_W = 4
BC_DIM = SSM_GROUPS * SSM_STATE
CONV_DIM = SSM_DIM + 2 * BC_DIM
RMS_EPS = 1e-5
D_FF = 5632
LN_EPS = 1e-5
DEPTH = 1
ALPHA = (2 * DEPTH) ** 0.25

LANES = 128
SUBLANES = 8
VMEM_LIMIT = 56 * 1024 * 1024

LORA_PAD = LANES
RW_OFF_WD = 3 * RW_DIM
RW_OFF_AD = RW_OFF_WD + LORA_PAD
RW_OFF_GD = RW_OFF_AD + LORA_PAD
RW_COLS = RW_OFF_GD + RW_GATE_LORA
DT_COLS = LANES
CONV_HIST = SUBLANES

FFN_TM = 512
FFN_TF = 512
MM_TM = 512
MM_TN = 1024
MERGE_TM = 256

DT_SLOT = 512
PC_RW = 0
PC_DT = PC_RW + RW_COLS
PC_GATE = PC_DT + DT_SLOT
PC_Z = PC_GATE + 2 * D_MODEL
PC_XS = PC_Z + SSM_DIM
PC_BC = PC_XS + SSM_DIM
P_COLS = PC_BC + 2 * BC_DIM
assert PC_DT % DT_COLS == 0 and PC_GATE % D_MODEL == 0 and PC_Z % SSM_DIM == 0
assert PC_XS % SSM_DIM == 0 and PC_BC % (2 * BC_DIM) == 0 and P_COLS % MM_TN == 0
assert F_ROWS % FFN_TM == 0 and X_ROWS == FFN_TM and M_ROWS % MM_TM == 0 and ROW_H0 % MERGE_TM == 0


def _dot(a, b):
    return jnp.dot(a, b, preferred_element_type=F32)


def _dot_nt(a, b):
    return lax.dot_general(a, b, (((1,), (1,)), ((), ())), preferred_element_type=F32)


def _dot_tn(a, b):
    return lax.dot_general(a, b, (((0,), (0,)), ((), ())), preferred_element_type=F32)


def _split2(x):
    hi = x.astype(BF16)
    lo = (x - hi.astype(F32)).astype(BF16)
    return hi, lo


def _dot_sel_r(x, sel):
    hi, lo = _split2(x)
    return _dot(hi, sel) + _dot(lo, sel)


def _dot_sel_l(sel, x):
    hi, lo = _split2(x)
    return _dot(sel, hi) + _dot(sel, lo)


def _sigmoid(x):
    return jax.nn.sigmoid(x)


def _softplus(x):
    return jnp.maximum(x, 0.0) + jnp.log1p(jnp.exp(-jnp.abs(x)))


def _layer_norm(s, g, b):
    mu = jnp.mean(s, axis=-1, keepdims=True)
    d = s - mu
    var = jnp.mean(d * d, axis=-1, keepdims=True)
    return d * lax.rsqrt(var + LN_EPS) * g + b


N_MAIN_TILES = F_ROWS // FFN_TM


def _ffn_step(xb_ref, wg_ref, wu_ref, wd_ref, acc_ref):
    xb = xb_ref[...]
    gate = _dot(xb, wg_ref[...])
    up = _dot(xb, wu_ref[...])
    h = (gate * _sigmoid(gate) * up).astype(BF16)
    acc_ref[...] += _dot(h, wd_ref[...])


def _ffn_in_kernel(xm_ref, xe_ref, wg_ref, wu_ref, wd_ref, g_ref, b_ref, o_ref, ob_ref,
                   xb_ref, acc_ref):
    i, f = pl.program_id(0), pl.program_id(1)
    load_x = lambda: jnp.where(i < N_MAIN_TILES, xm_ref[...], xe_ref[...])

    @pl.when(f == 0)
    def _():
        xb_ref[...] = load_x().astype(BF16)
        acc_ref[...] = jnp.zeros_like(acc_ref)

    _ffn_step(xb_ref, wg_ref, wu_ref, wd_ref, acc_ref)

    @pl.when(f == pl.num_programs(1) - 1)
    def _():
        y = _layer_norm(ALPHA * load_x() + 0.5 * acc_ref[...], g_ref[...], b_ref[...])
        o_ref[...] = y
        ob_ref[...] = y.astype(BF16)


def _ffn_out_kernel(x_ref, wg_ref, wu_ref, wd_ref, g_ref, b_ref, om_ref, oe_ref, xb_ref, acc_ref):
    i, f = pl.program_id(0), pl.program_id(1)

    @pl.when(f == 0)
    def _():
        xb_ref[...] = x_ref[...].astype(BF16)
        acc_ref[...] = jnp.zeros_like(acc_ref)

    _ffn_step(xb_ref, wg_ref, wu_ref, wd_ref, acc_ref)

    @pl.when(f == pl.num_programs(1) - 1)
    def _():
        y = _layer_norm(ALPHA * x_ref[...] + 0.5 * acc_ref[...], g_ref[...], b_ref[...])

        @pl.when(i < N_MAIN_TILES)
        def _():
            om_ref[...] = y

        @pl.when(i == N_MAIN_TILES)
        def _():
            oe_ref[...] = y


def _ffn_specs(d):
    nf = D_FF // FFN_TF
    return nf, [
        pl.BlockSpec((d, FFN_TF), lambda i, f: (0, f)),
        pl.BlockSpec((d, FFN_TF), lambda i, f: (0, f + nf)),
        pl.BlockSpec((FFN_TF, d), lambda i, f: (f, 0)),
        pl.BlockSpec((1, d), lambda i, f: (0, 0)),
        pl.BlockSpec((1, d), lambda i, f: (0, 0)),
    ]


_FFN_PARAMS = dict(dimension_semantics=("arbitrary", "arbitrary"), vmem_limit_bytes=VMEM_LIMIT)
_main_tile = lambda i, f: (jnp.minimum(i, N_MAIN_TILES - 1), 0)
_last_tile = lambda i, f: (0, 0)
_any_tile = lambda i, f: (i, 0)


def _ffn_ln_in(x_main, x_extra, w_gu, w_dn, ln_g, ln_b):
    d = x_main.shape[1]
    nf, wspecs = _ffn_specs(d)
    tile = (FFN_TM, d)
    return pl.pallas_call(
        _ffn_in_kernel,
        grid=(M_ROWS // FFN_TM, nf),
        in_specs=[pl.BlockSpec(tile, _main_tile),
                  pl.BlockSpec(tile, _last_tile, pipeline_mode=pl.Buffered(1))] + wspecs,
        out_specs=[pl.BlockSpec(tile, _any_tile), pl.BlockSpec(tile, _any_tile)],
        out_shape=[jax.ShapeDtypeStruct((M_ROWS, d), F32), jax.ShapeDtypeStruct((M_ROWS, d), BF16)],
        scratch_shapes=[pltpu.VMEM(tile, BF16), pltpu.VMEM(tile, F32)],
        compiler_params=pltpu.CompilerParams(**_FFN_PARAMS),
        name="ffn_ln_in",
    )(x_main, x_extra, w_gu, w_gu, w_dn, ln_g, ln_b)


def _ffn_ln_out(x, w_gu, w_dn, ln_g, ln_b):
    d = x.shape[1]
    nf, wspecs = _ffn_specs(d)
    tile = (FFN_TM, d)
    return pl.pallas_call(
        _ffn_out_kernel,
        grid=(M_ROWS // FFN_TM, nf),
        in_specs=[pl.BlockSpec(tile, _any_tile)] + wspecs,
        out_specs=[pl.BlockSpec(tile, _main_tile), pl.BlockSpec(tile, _last_tile)],
        out_shape=[jax.ShapeDtypeStruct((F_ROWS, d), F32), jax.ShapeDtypeStruct((X_ROWS, d), F32)],
        scratch_shapes=[pltpu.VMEM(tile, BF16), pltpu.VMEM(tile, F32)],
        compiler_params=pltpu.CompilerParams(**_FFN_PARAMS),
        name="ffn_ln_out",
    )(x, w_gu, w_gu, w_dn, ln_g, ln_b)


def _mm_kernel(x_ref, w_ref, o_ref):
    o_ref[...] = _dot(x_ref[...], w_ref[...])


def _project(x, w):
    m, k = x.shape
    n = w.shape[1]
    return pl.pallas_call(
        _mm_kernel,
        grid=(n // MM_TN, m // MM_TM),
        in_specs=[pl.BlockSpec((MM_TM, k), lambda j, i: (i, 0)),
                  pl.BlockSpec((k, MM_TN), lambda j, i: (0, j))],
        out_specs=pl.BlockSpec((MM_TM, MM_TN), lambda j, i: (i, j)),
        out_shape=jax.ShapeDtypeStruct((m, n), F32),
        compiler_params=pltpu.CompilerParams(
            dimension_semantics=("parallel", "parallel"), vmem_limit_bytes=VMEM_LIMIT),
        name="proj_in",
    )(x, w)


def _rwkv_kernel(p_ref, hist_ref, s0_ref, mu_ref, w0_ref, w2_ref, a0_ref, a2_ref, g2_ref,
                 kk_ref, ka_ref, rk_ref, gnw_ref, gnb_ref, seg_ref, y_other_ref,
                 y_ref, s_ref, prev_ref, yacc_ref, *, chunk, pad):
    del y_other_ref
    c = pl.program_id(1)
    rows = lax.broadcasted_iota(jnp.int32, (chunk, 1), 0)

    @pl.when(c == 0)
    def _():
        prev_ref[...] = hist_ref[0]
        s_ref[...] = s0_ref[...]

    p = p_ref[...]
    if pad:
        p = jnp.where(jnp.logical_and(c == 0, rows < pad), 0.0, p)
    prev = jnp.where(rows == 0, prev_ref[...], pltpu.roll(p, 1, 0))
    prev_ref[...] = p[chunk - 1:chunk, :]
    ps = p + (prev - p) * mu_ref[...]

    r = ps[:, 0:RW_DIM]
    k = ps[:, RW_DIM:2 * RW_DIM]
    v = ps[:, 2 * RW_DIM:3 * RW_DIM]
    wd = ps[:, RW_OFF_WD:RW_OFF_WD + LORA_PAD]
    ad = ps[:, RW_OFF_AD:RW_OFF_AD + LORA_PAD]
    gd = ps[:, RW_OFF_GD:RW_OFF_GD + RW_GATE_LORA]

    lw = -math.exp(-0.5) * _sigmoid(w0_ref[...] + _dot(jnp.tanh(wd).astype(BF16), w2_ref[...]))
    a = _sigmoid(a0_ref[...] + _dot(ad.astype(BF16), a2_ref[...]))
    g = _dot(_sigmoid(gd).astype(BF16), g2_ref[...])

    seg = seg_ref[...]

    def head_sum(x):
        w = seg.shape[0]
        return jnp.concatenate(
            [_dot_sel_r(x[:, j * w:(j + 1) * w], seg) for j in range(RW_DIM // w)], axis=1)

    kk = k * kk_ref[...]
    kk = kk * lax.rsqrt(jnp.maximum(head_sum(kk * kk), 1e-24))
    k = k * (1.0 + (a - 1.0) * ka_ref[...])
    b_neg = -(kk * a)

    ci = lax.broadcasted_iota(jnp.int32, (chunk, chunk), 0)
    cj = lax.broadcasted_iota(jnp.int32, (chunk, chunk), 1)
    cum = _dot_sel_l((ci >= cj).astype(BF16), lw)
    cum_end = cum[chunk - 1:chunk, :]
    e_neg = jnp.exp(-cum)
    kt = (kk * jnp.exp(cum - lw)).astype(BF16)
    rt = (r * jnp.exp(cum)).astype(BF16)
    kd = (k * e_neg).astype(BF16)
    bd_neg = (b_neg * e_neg).astype(BF16)
    e_end = jnp.exp(cum_end - cum)
    k_end = (k * e_end).astype(BF16)
    b_end_neg = (b_neg * e_end).astype(BF16)
    p_end = jnp.exp(cum_end)
    vb = v.astype(BF16)

    ri = lax.broadcasted_iota(jnp.int32, (2 * chunk, 2 * chunk), 0)
    rj = lax.broadcasted_iota(jnp.int32, (2 * chunk, 2 * chunk), 1)
    bi = jnp.where(ri >= chunk, ri - chunk + 1, ri)
    bj = jnp.where(rj >= chunk, rj - chunk, rj)
    keep = bi > bj
    ti = lax.broadcasted_iota(jnp.int32, (chunk, 2 * chunk), 0)
    tj = lax.broadcasted_iota(jnp.int32, (chunk, 2 * chunk), 1)
    hi_lane = tj >= chunk
    eye_hi = (tj == ti + chunk).astype(F32)
    zeros_v = jnp.zeros((chunk, HEAD), BF16)

    heads = range(RW_HEADS)
    sls = [slice(h * HEAD, (h + 1) * HEAD) for h in heads]
    s_old = [s_ref[0, h] for h in heads]
    s_b = [s.astype(BF16) for s in s_old]
    amats = [jnp.where(keep, _dot_nt(jnp.concatenate([kt[:, sl], rt[:, sl]], axis=0),
                                     jnp.concatenate([bd_neg[:, sl], kd[:, sl]], axis=0)), 0.0)
             for sl in sls]
    top = [m[:chunk] for m in amats]
    bot = [m[chunk:].astype(BF16) for m in amats]
    top_b = [x.astype(BF16) for x in top]
    tq = [_dot(xb[:, :chunk], jnp.where(hi_lane, eye_hi, x).astype(BF16)) + eye_hi
          for x, xb in zip(top, top_b)]
    n = 2
    while n < chunk:
        tq = [jnp.where(hi_lane, x, 0.0) + _dot(x[:, :chunk].astype(BF16), x.astype(BF16)) for x in tq]
        n *= 2
    t_b = [x[:, chunk:].astype(BF16) for x in tq]
    v_h = [vb[:, sl] for sl in sls]
    rhs = [_dot_nt(kt[:, sl], sb) + _dot(xb, jnp.concatenate([zeros_v, vh], axis=0))
           for sl, sb, xb, vh in zip(sls, s_b, top_b, v_h)]
    ub = [_dot(t, x.astype(BF16)).astype(BF16) for t, x in zip(t_b, rhs)]
    uv = [jnp.concatenate([u, vh], axis=0) for u, vh in zip(ub, v_h)]
    for h in heads:
        sl = sls[h]
        yacc_ref[:, sl] = _dot_nt(rt[:, sl], s_b[h]) + _dot(bot[h], uv[h])
        s_ref[0, h] = s_old[h] * p_end[:, sl] + _dot_tn(
            uv[h], jnp.concatenate([b_end_neg[:, sl], k_end[:, sl]], axis=0))

    y = yacc_ref[...]
    inv_n = 1.0 / HEAD
    mean = head_sum(y) * inv_n
    yc = y - mean
    var = head_sum(yc * yc) * inv_n
    yn = yc * lax.rsqrt(var + RW_GN_EPS) * gnw_ref[...] + gnb_ref[...]
    bonus = head_sum(r * k * rk_ref[...]) * v
    y_ref[...] = ((yn + bonus) * g).astype(y_ref.dtype)


def _seq_rows(prompt):
    if prompt:
        blk = lambda s, c: jnp.where(c == 0, ROW_H0 // CHUNK + s, s * (SEQ // CHUNK) + c - 1)
        return (BATCH, SEQ_CHUNKS), CHUNK, PAD, blk
    return (DEC_BATCH, 1), DEC_SEQ, 0, lambda s, c: ROW_S0 // DEC_SEQ + s


_MIX_PARAMS = dict(dimension_semantics=("parallel", "arbitrary"), vmem_limit_bytes=VMEM_LIMIT)


def _rwkv_mix(p_all, hist, s0, wts, y_other, *, prompt):
    grid, chunk, pad, blk = _seq_rows(prompt)
    const2 = lambda s, c: (0, 0)
    wspecs = [pl.BlockSpec(w.shape, const2) for w in wts]
    n_in = 3 + len(wts)
    return pl.pallas_call(
        functools.partial(_rwkv_kernel, chunk=chunk, pad=pad),
        grid=grid,
        in_specs=[
            pl.BlockSpec((chunk, RW_COLS), lambda s, c: (blk(s, c), PC_RW // RW_COLS)),
            pl.BlockSpec((1, 1, RW_COLS), lambda s, c: (s, 0, 0)),
            pl.BlockSpec((1, RW_HEADS, HEAD, HEAD), lambda s, c: (s, 0, 0, 0)),
        ] + wspecs + [pl.BlockSpec(memory_space=pl.ANY)],
        out_specs=[
            pl.BlockSpec((chunk, RW_DIM), lambda s, c: (blk(s, c), 0)),
            pl.BlockSpec((1, RW_HEADS, HEAD, HEAD), lambda s, c: (s, 0, 0, 0)),
        ],
        out_shape=[jax.ShapeDtypeStruct(y_other.shape, y_other.dtype),
                   jax.ShapeDtypeStruct((grid[0], RW_HEADS, HEAD, HEAD), F32)],
        scratch_shapes=[pltpu.VMEM((1, RW_COLS), F32), pltpu.VMEM((chunk, RW_DIM), F32)],
        input_output_aliases={n_in: 0},
        compiler_params=pltpu.CompilerParams(**_MIX_PARAMS),
        name="rwkv_mix_c%d" % chunk,
    )(p_all, hist, s0, *wts, y_other)


def _ssd_kernel(xs_ref, bc_ref, z_ref, dt_ref, hist_ref, h0_ref, cw_ref, cb_ref, dtb_ref, alog_ref,
                dskip_ref, nw_ref, exp_ref, y_other_ref,
                y_ref, h_ref, xpad_ref, yacc_ref, *, chunk, pad):
    del y_other_ref
    c = pl.program_id(1)
    rows = lax.broadcasted_iota(jnp.int32, (chunk, 1), 0)

    @pl.when(c == 0)
    def _():
        xpad_ref[0:CONV_HIST, :] = hist_ref[0]
        h_ref[...] = h0_ref[...]

    u = jnp.concatenate([xs_ref[...], bc_ref[...]], axis=1)
    if pad:
        is_pad = jnp.logical_and(c == 0, rows < pad)
        u = jnp.where(is_pad, 0.0, u)
    xpad_ref[CONV_HIST:CONV_HIST + chunk, :] = u
    conv = cb_ref[...] + u * cw_ref[CONV_W - 1:CONV_W, :]
    for i in range(CONV_W - 1):
        back = CONV_W - 1 - i
        conv = conv + xpad_ref[CONV_HIST - back:CONV_HIST - back + chunk, :] * cw_ref[i:i + 1, :]
    xpad_ref[0:CONV_HIST, :] = xpad_ref[chunk:chunk + CONV_HIST, :]
    xbc = conv * _sigmoid(conv)
    xs = xbc[:, 0:SSM_DIM]
    bm = xbc[:, SSM_DIM:SSM_DIM + BC_DIM].astype(BF16)
    cm = xbc[:, SSM_DIM + BC_DIM:CONV_DIM].astype(BF16)

    dt = _softplus(dt_ref[...] + dtb_ref[...])
    if pad:
        dt = jnp.where(is_pad, 0.0, dt)
    da = dt * (-jnp.exp(alog_ref[...]))
    ci = lax.broadcasted_iota(jnp.int32, (chunk, chunk), 0)
    cj = lax.broadcasted_iota(jnp.int32, (chunk, chunk), 1)
    causal = ci >= cj
    acs = _dot_sel_l(causal.astype(BF16), da)
    acs_t = acs.T
    acs_end = acs[chunk - 1:chunk, :]

    expand = exp_ref[...]
    xdt = xs * _dot_sel_r(dt, expand)
    xdt_b = xdt.astype(BF16)
    xdt_end = (xdt * _dot_sel_r(jnp.exp(acs_end - acs), expand)).astype(BF16)
    e_acs = _dot_sel_r(jnp.exp(acs), expand)
    decay_end = jnp.exp(acs_t[:, chunk - 1:chunk])

    gw = SSM_HPG * HEAD
    groups = range(SSM_GROUPS)
    heads = range(SSM_HEADS)
    bm_g = [bm[:, g * SSM_STATE:(g + 1) * SSM_STATE] for g in groups]
    cm_g = [cm[:, g * SSM_STATE:(g + 1) * SSM_STATE] for g in groups]
    h_old = [h_ref[0, hd] for hd in heads]
    cb = [_dot_nt(cm_g[g], bm_g[g]) for g in groups]
    y_off = [_dot_nt(cm_g[g], jnp.concatenate(
        [h_old[g * SSM_HPG + hh].astype(BF16) for hh in range(SSM_HPG)], axis=0)) for g in groups]
    st = [_dot_tn(xdt_end[:, g * gw:(g + 1) * gw], bm_g[g]) for g in groups]
    for hd in heads:
        sl = slice(hd * HEAD, (hd + 1) * HEAD)
        seg = acs[:, hd:hd + 1] - acs_t[hd:hd + 1, :]
        lmat = jnp.exp(jnp.where(causal, seg, -jnp.inf))
        yacc_ref[:, sl] = _dot((cb[hd // SSM_HPG] * lmat).astype(BF16), xdt_b[:, sl])
    for hd in heads:
        g, hh = divmod(hd, SSM_HPG)
        h_ref[0, hd] = h_old[hd] * decay_end[hd:hd + 1, :] + st[g][hh * HEAD:(hh + 1) * HEAD, :]

    y = yacc_ref[...] + jnp.concatenate(y_off, axis=1) * e_acs + xs * dskip_ref[...]
    zz = z_ref[...]
    y = y * (zz * _sigmoid(zz))
    parts = []
    for g in groups:
        yg = y[:, g * gw:(g + 1) * gw]
        ms = jnp.mean(yg * yg, axis=-1, keepdims=True)
        parts.append(yg * lax.rsqrt(ms + RMS_EPS))
    y_ref[...] = (jnp.concatenate(parts, axis=1) * nw_ref[...]).astype(y_ref.dtype)


def _ssd_mix(p_all, hist, h0, wts, y_other, *, prompt):
    grid, chunk, pad, blk = _seq_rows(prompt)
    const2 = lambda s, c: (0, 0)
    wspecs = [pl.BlockSpec(w.shape, const2) for w in wts]
    n_in = 6 + len(wts)
    return pl.pallas_call(
        functools.partial(_ssd_kernel, chunk=chunk, pad=pad),
        grid=grid,
        in_specs=[
            pl.BlockSpec((chunk, SSM_DIM), lambda s, c: (blk(s, c), PC_XS // SSM_DIM)),
            pl.BlockSpec((chunk, 2 * BC_DIM), lambda s, c: (blk(s, c), PC_BC // (2 * BC_DIM))),
            pl.BlockSpec((chunk, SSM_DIM), lambda s, c: (blk(s, c), PC_Z // SSM_DIM)),
            pl.BlockSpec((chunk, DT_COLS), lambda s, c: (blk(s, c), PC_DT // DT_COLS)),
            pl.BlockSpec((1, CONV_HIST, CONV_DIM), lambda s, c: (s, 0, 0)),
            pl.BlockSpec((1, SSM_HEADS, HEAD, SSM_STATE), lambda s, c: (s, 0, 0, 0)),
        ] + wspecs + [pl.BlockSpec(memory_space=pl.ANY)],
        out_specs=[
            pl.BlockSpec((chunk, SSM_DIM), lambda s, c: (blk(s, c), 0)),
            pl.BlockSpec((1, SSM_HEADS, HEAD, SSM_STATE), lambda s, c: (s, 0, 0, 0)),
        ],
        out_shape=[jax.ShapeDtypeStruct(y_other.shape, y_other.dtype),
                   jax.ShapeDtypeStruct((grid[0], SSM_HEADS, HEAD, SSM_STATE), F32)],
        scratch_shapes=[pltpu.VMEM((CONV_HIST + chunk, CONV_DIM), F32),
                        pltpu.VMEM((chunk, SSM_DIM), F32)],
        input_output_aliases={n_in: 0},
        compiler_params=pltpu.CompilerParams(**_MIX_PARAMS),
        name="ssd_mix_c%d" % chunk,
    )(p_all, p_all, p_all, p_all, hist, h0, *wts, y_other)


def _merge_kernel(x_ref, yrw_ref, yssm_ref, ga_ref, gb_ref, bga_ref, bgb_ref, wrw_ref, wssm_ref,
                  wout_ref, g_ref, b_ref, o_ref):
    merged = (_sigmoid(ga_ref[...] + bga_ref[...]) * _dot(yrw_ref[...], wrw_ref[...])
              + _sigmoid(gb_ref[...] + bgb_ref[...]) * _dot(yssm_ref[...], wssm_ref[...]))
    s = ALPHA * x_ref[...] + _dot(merged.astype(BF16), wout_ref[...])
    o_ref[...] = _layer_norm(s, g_ref[...], b_ref[...])


def _merge(x, y_rw, y_ssm, p_all, b_gate, w_rw_out, w_ssm_out, w_out, ln_g, ln_b):
    m, d = x.shape
    tm = MERGE_TM
    row = lambda i: (i, 0)
    const = lambda i: (0, 0)
    resident = lambda w: pl.BlockSpec(w.shape, const, pipeline_mode=pl.Buffered(1))
    return pl.pallas_call(
        _merge_kernel,
        grid=(m // tm,),
        in_specs=[
            pl.BlockSpec((tm, d), row),
            pl.BlockSpec((tm, RW_DIM), row),
            pl.BlockSpec((tm, SSM_DIM), row),
            pl.BlockSpec((tm, d), lambda i: (i, PC_GATE // d)),
            pl.BlockSpec((tm, d), lambda i: (i, PC_GATE // d + 1)),
            pl.BlockSpec((1, d), lambda i: (0, 0)),
            pl.BlockSpec((1, d), lambda i: (0, 1)),
            resident(w_rw_out), resident(w_ssm_out), resident(w_out),
            pl.BlockSpec((1, d), const),
            pl.BlockSpec((1, d), const),
        ],
        out_specs=pl.BlockSpec((tm, d), row),
        out_shape=jax.ShapeDtypeStruct((m, d), F32),
        compiler_params=pltpu.CompilerParams(
            dimension_semantics=("parallel",), vmem_limit_bytes=VMEM_LIMIT),
        name="merge_ln",
    )(x, y_rw, y_ssm, p_all, p_all, b_gate, b_gate, w_rw_out, w_ssm_out, w_out, ln_g, ln_b)


def _pad_cols(x, width):
    return jnp.pad(x, [(0, 0)] * (x.ndim - 1) + [(0, width - x.shape[-1])])


def _rw_cols(x):
    o_wd, o_ad, o_gd = 3 * RW_DIM, 3 * RW_DIM + RW_DECAY_LORA, 3 * RW_DIM + RW_DECAY_LORA + RW_AAA_LORA
    return jnp.concatenate([
        x[..., :o_wd],
        _pad_cols(x[..., o_wd:o_ad], LORA_PAD),
        _pad_cols(x[..., o_ad:o_gd], LORA_PAD),
        x[..., o_gd:],
    ], axis=-1)


def _rw_cols_inv(x):
    return jnp.concatenate([
        x[..., :RW_OFF_WD],
        x[..., RW_OFF_WD:RW_OFF_WD + RW_DECAY_LORA],
        x[..., RW_OFF_AD:RW_OFF_AD + RW_AAA_LORA],
        x[..., RW_OFF_GD:],
    ], axis=-1)


def _block_ones(n, blk):
    i = jnp.arange(n) // blk
    return (i[:, None] == i[None, :]).astype(BF16)


def kernel(x_prompt, x_sample, state_rwkv_shift, state_wkv, state_conv, state_ssm, meta_tokens, ffn1_gu, ffn1_dn, ln1_g, ln1_b, w_in, b_gate, rw_mu, rw_w0, rw_w2, rw_a0, rw_a2, rw_g2, rw_kk, rw_ka, rw_rk, rw_gn_w, rw_gn_b, conv_w, conv_b, dt_bias, a_log, d_skip, ssm_norm_w, w_rw_out, w_ssm_out, w_out, ln2_g, ln2_b, ffn2_gu, ffn2_dn, ln3_g, ln3_b):
    lyr = 0
    row = lambda t: t[lyr].reshape(1, -1).astype(F32)

    head_rows = jnp.concatenate([jnp.zeros((PAD, D_MODEL), F32), meta_tokens.astype(F32)], axis=0)
    x_extra = jnp.concatenate([x_sample.reshape(S_ROWS, D_MODEL)] + [head_rows] * BATCH, axis=0)
    x1, x1b = _ffn_ln_in(x_prompt.reshape(F_ROWS, D_MODEL), x_extra, ffn1_gu[lyr].astype(BF16),
                         ffn1_dn[lyr].astype(BF16), row(ln1_g), row(ln1_b))

    w = w_in[lyr].astype(BF16)
    o_z = RW_SHIFT_COLS
    o_xbc = o_z + SSM_DIM
    o_dt = o_xbc + CONV_DIM
    o_gate = o_dt + SSM_HEADS
    w_all = jnp.concatenate([
        _rw_cols(w[:, :o_z]), _pad_cols(w[:, o_dt:o_gate], DT_SLOT), w[:, o_gate:],
        w[:, o_z:o_xbc], w[:, o_xbc:o_dt]], axis=1)
    p_all = _project(x1b, w_all)

    pad_rows = lambda t, n: jnp.pad(t, ((0, n - t.shape[0]), (0, 0)))
    rw_wts = [
        _rw_cols(rw_mu[lyr]).reshape(1, RW_COLS), row(rw_w0),
        pad_rows(rw_w2[lyr], LORA_PAD).astype(BF16), row(rw_a0),
        pad_rows(rw_a2[lyr], LORA_PAD).astype(BF16), rw_g2[lyr].astype(BF16),
        row(rw_kk), row(rw_ka), row(rw_rk), row(rw_gn_w), row(rw_gn_b),
        _block_ones(2 * LANES, HEAD),
    ]
    y_rw, wkv_s = _rwkv_mix(
        p_all, _rw_cols(state_rwkv_shift[lyr]), state_wkv[lyr], rw_wts,
        jnp.zeros((M_ROWS, RW_DIM), BF16), prompt=False)
    y_rw, wkv_p = _rwkv_mix(
        p_all, jnp.zeros((BATCH, 1, RW_COLS), F32), jnp.zeros((BATCH, RW_HEADS, HEAD, HEAD), F32), rw_wts,
        y_rw, prompt=True)

    head_of_lane = jnp.arange(SSM_DIM) // HEAD
    expand = (jnp.arange(DT_COLS)[:, None] == head_of_lane[None, :]).astype(BF16)
    ssd_wts = [
        conv_w[lyr], row(conv_b), _pad_cols(row(dt_bias), DT_COLS), _pad_cols(row(a_log), DT_COLS),
        jnp.repeat(d_skip[lyr], HEAD).reshape(1, SSM_DIM), row(ssm_norm_w),
        expand,
    ]
    hist_rows = lambda t: jnp.pad(t, ((0, 0), (CONV_HIST - (CONV_W - 1), 0), (0, 0)))
    y_ssm, ssm_s = _ssd_mix(
        p_all, hist_rows(state_conv[lyr]), state_ssm[lyr], ssd_wts,
        jnp.zeros((M_ROWS, SSM_DIM), BF16), prompt=False)
    y_ssm, ssm_p = _ssd_mix(
        p_all, jnp.zeros((BATCH, CONV_HIST, CONV_DIM), F32),
        jnp.zeros((BATCH, SSM_HEADS, HEAD, SSM_STATE), F32), ssd_wts, y_ssm, prompt=True)

    x2 = _merge(x1, y_rw, y_ssm, p_all, row(b_gate), w_rw_out[lyr].astype(BF16),
                w_ssm_out[lyr].astype(BF16), w_out[lyr].astype(BF16), row(ln2_g), row(ln2_b))
    y_frames, y_extra = _ffn_ln_out(x2, ffn2_gu[lyr].astype(BF16), ffn2_dn[lyr].astype(BF16),
                                    row(ln3_g), row(ln3_b))

    y_prompt = y_frames.reshape(BATCH, SEQ, D_MODEL)
    y_sample = y_extra[:S_ROWS].reshape(DEC_BATCH, DEC_SEQ, D_MODEL)
    tail = CONV_W - 1

    def last_rows(row0, seq_len, n_seq):
        picks = [lax.slice(p_all, (row0 + seq_len - tail + t, 0), (row0 + n_seq * seq_len, P_COLS),
                           (seq_len, 1)) for t in range(tail)]
        return jnp.stack(picks, axis=1)

    last_p = last_rows(0, SEQ, BATCH)
    last_s = last_rows(ROW_S0, DEC_SEQ, DEC_BATCH)
    shift_of = lambda t: _rw_cols_inv(t[:, -1:, PC_RW:PC_RW + RW_COLS])[None]
    conv_of = lambda t: t[:, :, PC_XS:PC_XS + CONV_DIM][None]
    return (y_prompt, y_sample,
            shift_of(last_p), wkv_p[None], conv_of(last_p), ssm_p[None],
            shift_of(last_s), wkv_s[None], conv_of(last_s), ssm_s[None])
```

```python
import functools
import math

import jax
import jax.numpy as jnp
from jax import lax
from jax.experimental import pallas as pl
from jax.experimental.pallas import tpu as pltpu

F32 = jnp.float32
BF16 = jnp.bfloat16

D_MODEL = 2048
BATCH = 4
SEQ = 2048
DEC_BATCH = 16
DEC_SEQ = 16
CHUNK = 64
N_META = 16
PAD = (-N_META) % CHUNK
HEAD_ROWS = PAD + N_META
SEQ_CHUNKS = (HEAD_ROWS + SEQ) // CHUNK
F_ROWS = BATCH * SEQ
S_ROWS = DEC_BATCH * DEC_SEQ
ROW_S0 = F_ROWS
ROW_H0 = F_ROWS + S_ROWS
X_ROWS = S_ROWS + BATCH * HEAD_ROWS
M_ROWS = F_ROWS + X_ROWS

HEAD = 64
RW_DIM = D_MODEL // 2
RW_HEADS = RW_DIM // HEAD
RW_DECAY_LORA = 96
RW_AAA_LORA = 96
RW_GATE_LORA = 256
RW_SHIFT_COLS = 3 * RW_DIM + RW_DECAY_LORA + RW_AAA_LORA + RW_GATE_LORA
RW_GN_EPS = 64e-5
SSM_DIM = D_MODEL
SSM_HEADS = SSM_DIM // HEAD
SSM_GROUPS = 4
SSM_HPG = SSM_HEADS // SSM_GROUPS
SSM_STATE = 128
CONV_W = 4
BC_DIM = SSM_GROUPS * SSM_STATE
CONV_DIM = SSM_DIM + 2 * BC_DIM
RMS_EPS = 1e-5
D_FF = 5632
LN_EPS = 1e-5
DEPTH = 1
ALPHA = (2 * DEPTH) ** 0.25

LANES = 128
SUBLANES = 8
VMEM_LIMIT = 56 * 1024 * 1024

LORA_PAD = LANES
RW_OFF_WD = 3 * RW_DIM
RW_OFF_AD = RW_OFF_WD + LORA_PAD
RW_OFF_GD = RW_OFF_AD + LORA_PAD
RW_COLS = RW_OFF_GD + RW_GATE_LORA
DT_COLS = LANES
CONV_HIST = SUBLANES

FFN_TM = 512
FFN_TF = 512
MM_TM = 512
MM_TN = 1024
MERGE_TM = 256

DT_SLOT = 512
PC_RW = 0
PC_DT = PC_RW + RW_COLS
PC_GATE = PC_DT + DT_SLOT
PC_Z = PC_GATE + 2 * D_MODEL
PC_XS = PC_Z + SSM_DIM
PC_BC = PC_XS + SSM_DIM
P_COLS = PC_BC + 2 * BC_DIM
assert PC_DT % DT_COLS == 0 and PC_GATE % D_MODEL == 0 and PC_Z % SSM_DIM == 0
assert PC_XS % SSM_DIM == 0 and PC_BC % (2 * BC_DIM) == 0 and P_COLS % MM_TN == 0
assert F_ROWS % FFN_TM == 0 and X_ROWS == FFN_TM and M_ROWS % MM_TM == 0 and ROW_H0 % MERGE_TM == 0


def _dot(a, b):
    return jnp.dot(a, b, preferred_element_type=F32)


def _dot_nt(a, b):
    return lax.dot_general(a, b, (((1,), (1,)), ((), ())), preferred_element_type=F32)


def _dot_tn(a, b):
    return lax.dot_general(a, b, (((0,), (0,)), ((), ())), preferred_element_type=F32)


def _split2(x):
    hi = x.astype(BF16)
    lo = (x - hi.astype(F32)).astype(BF16)
    return hi, lo


def _dot_sel_r(x, sel):
    hi, lo = _split2(x)
    return _dot(hi, sel) + _dot(lo, sel)


def _dot_sel_l(sel, x):
    hi, lo = _split2(x)
    return _dot(sel, hi) + _dot(sel, lo)


def _sigmoid(x):
    return jax.nn.sigmoid(x)


def _softplus(x):
    return jnp.maximum(x, 0.0) + jnp.log1p(jnp.exp(-jnp.abs(x)))


def _layer_norm(s, g, b):
    mu = jnp.mean(s, axis=-1, keepdims=True)
    d = s - mu
    var = jnp.mean(d * d, axis=-1, keepdims=True)
    return d * lax.rsqrt(var + LN_EPS) * g + b


N_MAIN_TILES = F_ROWS // FFN_TM


def _ffn_step(xb_ref, wg_ref, wu_ref, wd_ref, acc_ref):
    xb = xb_ref[...]
    gate = _dot(xb, wg_ref[...])
    up = _dot(xb, wu_ref[...])
    h = (gate * _sigmoid(gate) * up).astype(BF16)
    acc_ref[...] += _dot(h, wd_ref[...])


def _ffn_in_kernel(xm_ref, xe_ref, wg_ref, wu_ref, wd_ref, g_ref, b_ref, o_ref, ob_ref,
                   xb_ref, acc_ref):
    i, f = pl.program_id(0), pl.program_id(1)
    load_x = lambda: jnp.where(i < N_MAIN_TILES, xm_ref[...], xe_ref[...])

    @pl.when(f == 0)
    def _():
        xb_ref[...] = load_x().astype(BF16)
        acc_ref[...] = jnp.zeros_like(acc_ref)

    _ffn_step(xb_ref, wg_ref, wu_ref, wd_ref, acc_ref)

    @pl.when(f == pl.num_programs(1) - 1)
    def _():
        y = _layer_norm(ALPHA * load_x() + 0.5 * acc_ref[...], g_ref[...], b_ref[...])
        o_ref[...] = y
        ob_ref[...] = y.astype(BF16)


def _ffn_out_kernel(x_ref, wg_ref, wu_ref, wd_ref, g_ref, b_ref, om_ref, oe_ref, xb_ref, acc_ref):
    i, f = pl.program_id(0), pl.program_id(1)

    @pl.when(f == 0)
    def _():
        xb_ref[...] = x_ref[...].astype(BF16)
        acc_ref[...] = jnp.zeros_like(acc_ref)

    _ffn_step(xb_ref, wg_ref, wu_ref, wd_ref, acc_ref)

    @pl.when(f == pl.num_programs(1) - 1)
    def _():
        y = _layer_norm(ALPHA * x_ref[...] + 0.5 * acc_ref[...], g_ref[...], b_ref[...])

        @pl.when(i < N_MAIN_TILES)
        def _():
            om_ref[...] = y

        @pl.when(i == N_MAIN_TILES)
        def _():
            oe_ref[...] = y


def _ffn_specs(d):
    nf = D_FF // FFN_TF
    return nf, [
        pl.BlockSpec((d, FFN_TF), lambda i, f: (0, f)),
        pl.BlockSpec((d, FFN_TF), lambda i, f: (0, f + nf)),
        pl.BlockSpec((FFN_TF, d), lambda i, f: (f, 0)),
        pl.BlockSpec((1, d), lambda i, f: (0, 0)),
        pl.BlockSpec((1, d), lambda i, f: (0, 0)),
    ]


_FFN_PARAMS = dict(dimension_semantics=("arbitrary", "arbitrary"), vmem_limit_bytes=VMEM_LIMIT)
_main_tile = lambda i, f: (jnp.minimum(i, N_MAIN_TILES - 1), 0)
_last_tile = lambda i, f: (0, 0)
_any_tile = lambda i, f: (i, 0)


def _ffn_ln_in(x_main, x_extra, w_gu, w_dn, ln_g, ln_b):
    d = x_main.shape[1]
    nf, wspecs = _ffn_specs(d)
    tile = (FFN_TM, d)
    return pl.pallas_call(
        _ffn_in_kernel,
        grid=(M_ROWS // FFN_TM, nf),
        in_specs=[pl.BlockSpec(tile, _main_tile),
                  pl.BlockSpec(tile, _last_tile, pipeline_mode=pl.Buffered(1))] + wspecs,
        out_specs=[pl.BlockSpec(tile, _any_tile), pl.BlockSpec(tile, _any_tile)],
        out_shape=[jax.ShapeDtypeStruct((M_ROWS, d), F32), jax.ShapeDtypeStruct((M_ROWS, d), BF16)],
        scratch_shapes=[pltpu.VMEM(tile, BF16), pltpu.VMEM(tile, F32)],
        compiler_params=pltpu.CompilerParams(**_FFN_PARAMS),
        name="ffn_ln_in",
    )(x_main, x_extra, w_gu, w_gu, w_dn, ln_g, ln_b)


def _ffn_ln_out(x, w_gu, w_dn, ln_g, ln_b):
    d = x.shape[1]
    nf, wspecs = _ffn_specs(d)
    tile = (FFN_TM, d)
    return pl.pallas_call(
        _ffn_out_kernel,
        grid=(M_ROWS // FFN_TM, nf),
        in_specs=[pl.BlockSpec(tile, _any_tile)] + wspecs,
        out_specs=[pl.BlockSpec(tile, _main_tile), pl.BlockSpec(tile, _last_tile)],
        out_shape=[jax.ShapeDtypeStruct((F_ROWS, d), F32), jax.ShapeDtypeStruct((X_ROWS, d), F32)],
        scratch_shapes=[pltpu.VMEM(tile, BF16), pltpu.VMEM(tile, F32)],
        compiler_params=pltpu.CompilerParams(**_FFN_PARAMS),
        name="ffn_ln_out",
    )(x, w_gu, w_gu, w_dn, ln_g, ln_b)


def _mm_kernel(x_ref, w_ref, o_ref):
    o_ref[...] = _dot(x_ref[...], w_ref[...])


def _project(x, w):
    m, k = x.shape
    n = w.shape[1]
    return pl.pallas_call(
        _mm_kernel,
        grid=(n // MM_TN, m // MM_TM),
        in_specs=[pl.BlockSpec((MM_TM, k), lambda j, i: (i, 0)),
                  pl.BlockSpec((k, MM_TN), lambda j, i: (0, j))],
        out_specs=pl.BlockSpec((MM_TM, MM_TN), lambda j, i: (i, j)),
        out_shape=jax.ShapeDtypeStruct((m, n), F32),
        compiler_params=pltpu.CompilerParams(
            dimension_semantics=("parallel", "parallel"), vmem_limit_bytes=VMEM_LIMIT),
        name="proj_in",
    )(x, w)


def _rwkv_kernel(p_ref, hist_ref, s0_ref, mu_ref, w0_ref, w2_ref, a0_ref, a2_ref, g2_ref,
                 kk_ref, ka_ref, rk_ref, gnw_ref, gnb_ref, seg_ref, y_other_ref,
                 y_ref, s_ref, shift_ref, yacc_ref, *, chunk, pad):
    del y_other_ref
    c = pl.program_id(1)
    rows = lax.broadcasted_iota(jnp.int32, (chunk, 1), 0)

    @pl.when(c == 0)
    def _():
        shift_ref[...] = hist_ref[...]
        s_ref[...] = s0_ref[...]

    p = p_ref[...]
    if pad:
        p = jnp.where(jnp.logical_and(c == 0, rows < pad), 0.0, p)
    prev = jnp.where(rows == 0, shift_ref[0], pltpu.roll(p, 1, 0))
    shift_ref[0] = p[chunk - 1:chunk, :]
    ps = p + (prev - p) * mu_ref[...]

    r = ps[:, 0:RW_DIM]
    k = ps[:, RW_DIM:2 * RW_DIM]
    v = ps[:, 2 * RW_DIM:3 * RW_DIM]
    wd = ps[:, RW_OFF_WD:RW_OFF_WD + LORA_PAD]
    ad = ps[:, RW_OFF_AD:RW_OFF_AD + LORA_PAD]
    gd = ps[:, RW_OFF_GD:RW_OFF_GD + RW_GATE_LORA]

    lw = -math.exp(-0.5) * _sigmoid(w0_ref[...] + _dot(jnp.tanh(wd).astype(BF16), w2_ref[...]))
    a = _sigmoid(a0_ref[...] + _dot(ad.astype(BF16), a2_ref[...]))
    g = _dot(_sigmoid(gd).astype(BF16), g2_ref[...])

    seg = seg_ref[...]

    def head_sum(x):
        w = seg.shape[0]
        return jnp.concatenate(
            [_dot_sel_r(x[:, j * w:(j + 1) * w], seg) for j in range(RW_DIM // w)], axis=1)

    kk = k * kk_ref[...]
    kk = kk * lax.rsqrt(jnp.maximum(head_sum(kk * kk), 1e-24))
    k = k * (1.0 + (a - 1.0) * ka_ref[...])
    b_neg = -(kk * a)

    ci = lax.broadcasted_iota(jnp.int32, (chunk, chunk), 0)
    cj = lax.broadcasted_iota(jnp.int32, (chunk, chunk), 1)
    cum = _dot_sel_l((ci >= cj).astype(BF16), lw)
    cum_end = cum[chunk - 1:chunk, :]
    e_neg = jnp.exp(-cum)
    kt = (kk * jnp.exp(cum - lw)).astype(BF16)
    rt = (r * jnp.exp(cum)).astype(BF16)
    kd = (k * e_neg).astype(BF16)
    bd_neg = (b_neg * e_neg).astype(BF16)
    e_end = jnp.exp(cum_end - cum)
    k_end = (k * e_end).astype(BF16)
    b_end_neg = (b_neg * e_end).astype(BF16)
    p_end = jnp.exp(cum_end)
    vb = v.astype(BF16)

    ri = lax.broadcasted_iota(jnp.int32, (2 * chunk, 2 * chunk), 0)
    rj = lax.broadcasted_iota(jnp.int32, (2 * chunk, 2 * chunk), 1)
    bi = jnp.where(ri >= chunk, ri - chunk + 1, ri)
    bj = jnp.where(rj >= chunk, rj - chunk, rj)
    keep = bi > bj
    ti = lax.broadcasted_iota(jnp.int32, (chunk, 2 * chunk), 0)
    tj = lax.broadcasted_iota(jnp.int32, (chunk, 2 * chunk), 1)
    hi_lane = tj >= chunk
    eye_hi = (tj == ti + chunk).astype(F32)
    zeros_v = jnp.zeros((chunk, HEAD), BF16)

    heads = range(RW_HEADS)
    sls = [slice(h * HEAD, (h + 1) * HEAD) for h in heads]
    s_old = [s_ref[0, h] for h in heads]
    s_b = [s.astype(BF16) for s in s_old]
    amats = [jnp.where(keep, _dot_nt(jnp.concatenate([kt[:, sl], rt[:, sl]], axis=0),
                                     jnp.concatenate([bd_neg[:, sl], kd[:, sl]], axis=0)), 0.0)
             for sl in sls]
    top = [m[:chunk] for m in amats]
    bot = [m[chunk:].astype(BF16) for m in amats]
    top_b = [x.astype(BF16) for x in top]
    tq = [_dot(xb[:, :chunk], jnp.where(hi_lane, eye_hi, x).astype(BF16)) + eye_hi
          for x, xb in zip(top, top_b)]
    n = 2
    while n < chunk:
        tq = [jnp.where(hi_lane, x, 0.0) + _dot(x[:, :chunk].astype(BF16), x.astype(BF16)) for x in tq]
        n *= 2
    t_b = [x[:, chunk:].astype(BF16) for x in tq]
    v_h = [vb[:, sl] for sl in sls]
    rhs = [_dot_nt(kt[:, sl], sb) + _dot(xb, jnp.concatenate([zeros_v, vh], axis=0))
           for sl, sb, xb, vh in zip(sls, s_b, top_b, v_h)]
    ub = [_dot(t, x.astype(BF16)).astype(BF16) for t, x in zip(t_b, rhs)]
    uv = [jnp.concatenate([u, vh], axis=0) for u, vh in zip(ub, v_h)]
    for h in heads:
        sl = sls[h]
        yacc_ref[:, sl] = _dot_nt(rt[:, sl], s_b[h]) + _dot(bot[h], uv[h])
        s_ref[0, h] = s_old[h] * p_end[:, sl] + _dot_tn(
            uv[h], jnp.concatenate([b_end_neg[:, sl], k_end[:, sl]], axis=0))

    y = yacc_ref[...]
    inv_n = 1.0 / HEAD
    mean = head_sum(y) * inv_n
    yc = y - mean
    var = head_sum(yc * yc) * inv_n
    yn = yc * lax.rsqrt(var + RW_GN_EPS) * gnw_ref[...] + gnb_ref[...]
    bonus = head_sum(r * k * rk_ref[...]) * v
    y_ref[...] = ((yn + bonus) * g).astype(y_ref.dtype)


def _seq_rows(prompt):
    if prompt:
        blk = lambda s, c: jnp.where(c == 0, ROW_H0 // CHUNK + s, s * (SEQ // CHUNK) + c - 1)
        return (BATCH, SEQ_CHUNKS), CHUNK, PAD, blk
    return (DEC_BATCH, 1), DEC_SEQ, 0, lambda s, c: ROW_S0 // DEC_SEQ + s


_MIX_PARAMS = dict(dimension_semantics=("parallel", "arbitrary"), vmem_limit_bytes=VMEM_LIMIT)


def _rwkv_mix(p_all, hist, s0, wts, y_other, *, prompt):
    grid, chunk, pad, blk = _seq_rows(prompt)
    const2 = lambda s, c: (0, 0)
    wspecs = [pl.BlockSpec(w.shape, const2) for w in wts]
    n_in = 3 + len(wts)
    return pl.pallas_call(
        functools.partial(_rwkv_kernel, chunk=chunk, pad=pad),
        grid=grid,
        in_specs=[
            pl.BlockSpec((chunk, RW_COLS), lambda s, c: (blk(s, c), PC_RW // RW_COLS)),
            pl.BlockSpec((1, 1, RW_COLS), lambda s, c: (s, 0, 0)),
            pl.BlockSpec((1, RW_HEADS, HEAD, HEAD), lambda s, c: (s, 0, 0, 0)),
        ] + wspecs + [pl.BlockSpec(memory_space=pl.ANY)],
        out_specs=[
            pl.BlockSpec((chunk, RW_DIM), lambda s, c: (blk(s, c), 0)),
            pl.BlockSpec((1, RW_HEADS, HEAD, HEAD), lambda s, c: (s, 0, 0, 0)),
            pl.BlockSpec((1, 1, RW_COLS), lambda s, c: (s, 0, 0)),
        ],
        out_shape=[jax.ShapeDtypeStruct(y_other.shape, y_other.dtype),
                   jax.ShapeDtypeStruct((grid[0], RW_HEADS, HEAD, HEAD), F32),
                   jax.ShapeDtypeStruct((grid[0], 1, RW_COLS), F32)],
        scratch_shapes=[pltpu.VMEM((chunk, RW_DIM), F32)],
        input_output_aliases={n_in: 0},
        compiler_params=pltpu.CompilerParams(**_MIX_PARAMS),
        name="rwkv_mix_c%d" % chunk,
    )(p_all, hist, s0, *wts, y_other)


def _ssd_kernel(xs_ref, bc_ref, z_ref, dt_ref, hist_ref, h0_ref, cw_ref, cb_ref, dtb_ref, alog_ref,
                dskip_ref, nw_ref, exp_ref, y_other_ref,
                y_ref, h_ref, tail_ref, xpad_ref, yacc_ref, *, chunk, pad):
    del y_other_ref
    c = pl.program_id(1)
    rows = lax.broadcasted_iota(jnp.int32, (chunk, 1), 0)

    @pl.when(c == 0)
    def _():
        xpad_ref[0:CONV_HIST, :] = hist_ref[0]
        h_ref[...] = h0_ref[...]

    u = jnp.concatenate([xs_ref[...], bc_ref[...]], axis=1)
    if pad:
        is_pad = jnp.logical_and(c == 0, rows < pad)
        u = jnp.where(is_pad, 0.0, u)
    xpad_ref[CONV_HIST:CONV_HIST + chunk, :] = u
    conv = cb_ref[...] + u * cw_ref[CONV_W - 1:CONV_W, :]
    for i in range(CONV_W - 1):
        back = CONV_W - 1 - i
        conv = conv + xpad_ref[CONV_HIST - back:CONV_HIST - back + chunk, :] * cw_ref[i:i + 1, :]
    hist_next = xpad_ref[chunk:chunk + CONV_HIST, :]
    xpad_ref[0:CONV_HIST, :] = hist_next
    tail_ref[0] = hist_next
    xbc = conv * _sigmoid(conv)
    xs = xbc[:, 0:SSM_DIM]
    bm = xbc[:, SSM_DIM:SSM_DIM + BC_DIM].astype(BF16)
    cm = xbc[:, SSM_DIM + BC_DIM:CONV_DIM].astype(BF16)

    dt = _softplus(dt_ref[...] + dtb_ref[...])
    if pad:
        dt = jnp.where(is_pad, 0.0, dt)
    da = dt * (-jnp.exp(alog_ref[...]))
    ci = lax.broadcasted_iota(jnp.int32, (chunk, chunk), 0)
    cj = lax.broadcasted_iota(jnp.int32, (chunk, chunk), 1)
    causal = ci >= cj
    acs = _dot_sel_l(causal.astype(BF16), da)
    acs_t = acs.T
    acs_end = acs[chunk - 1:chunk, :]

    expand = exp_ref[...]
    xdt = xs * _dot_sel_r(dt, expand)
    xdt_b = xdt.astype(BF16)
    xdt_end = (xdt * _dot_sel_r(jnp.exp(acs_end - acs), expand)).astype(BF16)
    e_acs = _dot_sel_r(jnp.exp(acs), expand)
    decay_end = jnp.exp(acs_t[:, chunk - 1:chunk])

    gw = SSM_HPG * HEAD
    groups = range(SSM_GROUPS)
    heads = range(SSM_HEADS)
    bm_g = [bm[:, g * SSM_STATE:(g + 1) * SSM_STATE] for g in groups]
    cm_g = [cm[:, g * SSM_STATE:(g + 1) * SSM_STATE] for g in groups]
    h_old = [h_ref[0, hd] for hd in heads]
    cb = [_dot_nt(cm_g[g], bm_g[g]) for g in groups]
    y_off = [_dot_nt(cm_g[g], jnp.concatenate(
        [h_old[g * SSM_HPG + hh].astype(BF16) for hh in range(SSM_HPG)], axis=0)) for g in groups]
    st = [_dot_tn(xdt_end[:, g * gw:(g + 1) * gw], bm_g[g]) for g in groups]
    for hd in heads:
        sl = slice(hd * HEAD, (hd + 1) * HEAD)
        seg = acs[:, hd:hd + 1] - acs_t[hd:hd + 1, :]
        lmat = jnp.exp(jnp.where(causal, seg, -jnp.inf))
        yacc_ref[:, sl] = _dot((cb[hd // SSM_HPG] * lmat).astype(BF16), xdt_b[:, sl])
    for hd in heads:
        g, hh = divmod(hd, SSM_HPG)
        h_ref[0, hd] = h_old[hd] * decay_end[hd:hd + 1, :] + st[g][hh * HEAD:(hh + 1) * HEAD, :]

    y = yacc_ref[...] + jnp.concatenate(y_off, axis=1) * e_acs + xs * dskip_ref[...]
    zz = z_ref[...]
    y = y * (zz * _sigmoid(zz))
    parts = []
    for g in groups:
        yg = y[:, g * gw:(g + 1) * gw]
        ms = jnp.mean(yg * yg, axis=-1, keepdims=True)
        parts.append(yg * lax.rsqrt(ms + RMS_EPS))
    y_ref[...] = (jnp.concatenate(parts, axis=1) * nw_ref[...]).astype(y_ref.dtype)


def _ssd_mix(p_all, hist, h0, wts, y_other, *, prompt):
    grid, chunk, pad, blk = _seq_rows(prompt)
    const2 = lambda s, c: (0, 0)
    wspecs = [pl.BlockSpec(w.shape, const2) for w in wts]
    n_in = 6 + len(wts)
    return pl.pallas_call(
        functools.partial(_ssd_kernel, chunk=chunk, pad=pad),
        grid=grid,
        in_specs=[
            pl.BlockSpec((chunk, SSM_DIM), lambda s, c: (blk(s, c), PC_XS // SSM_DIM)),
            pl.BlockSpec((chunk, 2 * BC_DIM), lambda s, c: (blk(s, c), PC_BC // (2 * BC_DIM))),
            pl.BlockSpec((chunk, SSM_DIM), lambda s, c: (blk(s, c), PC_Z // SSM_DIM)),
            pl.BlockSpec((chunk, DT_COLS), lambda s, c: (blk(s, c), PC_DT // DT_COLS)),
            pl.BlockSpec((1, CONV_HIST, CONV_DIM), lambda s, c: (s, 0, 0)),
            pl.BlockSpec((1, SSM_HEADS, HEAD, SSM_STATE), lambda s, c: (s, 0, 0, 0)),
        ] + wspecs + [pl.BlockSpec(memory_space=pl.ANY)],
        out_specs=[
            pl.BlockSpec((chunk, SSM_DIM), lambda s, c: (blk(s, c), 0)),
            pl.BlockSpec((1, SSM_HEADS, HEAD, SSM_STATE), lambda s, c: (s, 0, 0, 0)),
            pl.BlockSpec((1, CONV_HIST, CONV_DIM), lambda s, c: (s, 0, 0)),
        ],
        out_shape=[jax.ShapeDtypeStruct(y_other.shape, y_other.dtype),
                   jax.ShapeDtypeStruct((grid[0], SSM_HEADS, HEAD, SSM_STATE), F32),
                   jax.ShapeDtypeStruct((grid[0], CONV_HIST, CONV_DIM), F32)],
        scratch_shapes=[pltpu.VMEM((CONV_HIST + chunk, CONV_DIM), F32),
                        pltpu.VMEM((chunk, SSM_DIM), F32)],
        input_output_aliases={n_in: 0},
        compiler_params=pltpu.CompilerParams(**_MIX_PARAMS),
        name="ssd_mix_c%d" % chunk,
    )(p_all, p_all, p_all, p_all, hist, h0, *wts, y_other)


def _merge_kernel(x_ref, yrw_ref, yssm_ref, ga_ref, gb_ref, bga_ref, bgb_ref, wrw_ref, wssm_ref,
                  wout_ref, g_ref, b_ref, o_ref):
    merged = (_sigmoid(ga_ref[...] + bga_ref[...]) * _dot(yrw_ref[...], wrw_ref[...])
              + _sigmoid(gb_ref[...] + bgb_ref[...]) * _dot(yssm_ref[...], wssm_ref[...]))
    s = ALPHA * x_ref[...] + _dot(merged.astype(BF16), wout_ref[...])
    o_ref[...] = _layer_norm(s, g_ref[...], b_ref[...])


def _merge(x, y_rw, y_ssm, p_all, b_gate, w_rw_out, w_ssm_out, w_out, ln_g, ln_b):
    m, d = x.shape
    tm = MERGE_TM
    row = lambda i: (i, 0)
    const = lambda i: (0, 0)
    resident = lambda w: pl.BlockSpec(w.shape, const, pipeline_mode=pl.Buffered(1))
    return pl.pallas_call(
        _merge_kernel,
        grid=(m // tm,),
        in_specs=[
            pl.BlockSpec((tm, d), row),
            pl.BlockSpec((tm, RW_DIM), row),
            pl.BlockSpec((tm, SSM_DIM), row),
            pl.BlockSpec((tm, d), lambda i: (i, PC_GATE // d)),
            pl.BlockSpec((tm, d), lambda i: (i, PC_GATE // d + 1)),
            pl.BlockSpec((1, d), lambda i: (0, 0)),
            pl.BlockSpec((1, d), lambda i: (0, 1)),
            resident(w_rw_out), resident(w_ssm_out), resident(w_out),
            pl.BlockSpec((1, d), const),
            pl.BlockSpec((1, d), const),
        ],
        out_specs=pl.BlockSpec((tm, d), row),
        out_shape=jax.ShapeDtypeStruct((m, d), F32),
        compiler_params=pltpu.CompilerParams(
            dimension_semantics=("parallel",), vmem_limit_bytes=VMEM_LIMIT),
        name="merge_ln",
    )(x, y_rw, y_ssm, p_all, p_all, b_gate, b_gate, w_rw_out, w_ssm_out, w_out, ln_g, ln_b)


def _pad_cols(x, width):
    return jnp.pad(x, [(0, 0)] * (x.ndim - 1) + [(0, width - x.shape[-1])])


def _rw_cols(x):
    o_wd, o_ad, o_gd = 3 * RW_DIM, 3 * RW_DIM + RW_DECAY_LORA, 3 * RW_DIM + RW_DECAY_LORA + RW_AAA_LORA
    return jnp.concatenate([
        x[..., :o_wd],
        _pad_cols(x[..., o_wd:o_ad], LORA_PAD),
        _pad_cols(x[..., o_ad:o_gd], LORA_PAD),
        x[..., o_gd:],
    ], axis=-1)


def _rw_cols_inv(x):
    return jnp.concatenate([
        x[..., :RW_OFF_WD],
        x[..., RW_OFF_WD:RW_OFF_WD + RW_DECAY_LORA],
        x[..., RW_OFF_AD:RW_OFF_AD + RW_AAA_LORA],
        x[..., RW_OFF_GD:],
    ], axis=-1)


def _block_ones(n, blk):
    i = jnp.arange(n) // blk
    return (i[:, None] == i[None, :]).astype(BF16)


def kernel(x_prompt, x_sample, state_rwkv_shift, state_wkv, state_conv, state_ssm, meta_tokens, ffn1_gu, ffn1_dn, ln1_g, ln1_b, w_in, b_gate, rw_mu, rw_w0, rw_w2, rw_a0, rw_a2, rw_g2, rw_kk, rw_ka, rw_rk, rw_gn_w, rw_gn_b, conv_w, conv_b, dt_bias, a_log, d_skip, ssm_norm_w, w_rw_out, w_ssm_out, w_out, ln2_g, ln2_b, ffn2_gu, ffn2_dn, ln3_g, ln3_b):
    lyr = 0
    row = lambda t: t[lyr].reshape(1, -1).astype(F32)

    head_rows = jnp.concatenate([jnp.zeros((PAD, D_MODEL), F32), meta_tokens.astype(F32)], axis=0)
    x_extra = jnp.concatenate([x_sample.reshape(S_ROWS, D_MODEL)] + [head_rows] * BATCH, axis=0)
    x1, x1b = _ffn_ln_in(x_prompt.reshape(F_ROWS, D_MODEL), x_extra, ffn1_gu[lyr].astype(BF16),
                         ffn1_dn[lyr].astype(BF16), row(ln1_g), row(ln1_b))

    w = w_in[lyr].astype(BF16)
    o_z = RW_SHIFT_COLS
    o_xbc = o_z + SSM_DIM
    o_dt = o_xbc + CONV_DIM
    o_gate = o_dt + SSM_HEADS
    w_all = jnp.concatenate([
        _rw_cols(w[:, :o_z]), _pad_cols(w[:, o_dt:o_gate], DT_SLOT), w[:, o_gate:],
        w[:, o_z:o_xbc], w[:, o_xbc:o_dt]], axis=1)
    p_all = _project(x1b, w_all)

    pad_rows = lambda t, n: jnp.pad(t, ((0, n - t.shape[0]), (0, 0)))
    rw_wts = [
        _rw_cols(rw_mu[lyr]).reshape(1, RW_COLS), row(rw_w0),
        pad_rows(rw_w2[lyr], LORA_PAD).astype(BF16), row(rw_a0),
        pad_rows(rw_a2[lyr], LORA_PAD).astype(BF16), rw_g2[lyr].astype(BF16),
        row(rw_kk), row(rw_ka), row(rw_rk), row(rw_gn_w), row(rw_gn_b),
        _block_ones(2 * LANES, HEAD),
    ]
    y_rw, wkv_s, shift_s = _rwkv_mix(
        p_all, _rw_cols(state_rwkv_shift[lyr]), state_wkv[lyr], rw_wts,
        jnp.zeros((M_ROWS, RW_DIM), BF16), prompt=False)
    y_rw, wkv_p, shift_p = _rwkv_mix(
        p_all, jnp.zeros((BATCH, 1, RW_COLS), F32), jnp.zeros((BATCH, RW_HEADS, HEAD, HEAD), F32), rw_wts,
        y_rw, prompt=True)

    head_of_lane = jnp.arange(SSM_DIM) // HEAD
    expand = (jnp.arange(DT_COLS)[:, None] == head_of_lane[None, :]).astype(BF16)
    ssd_wts = [
        conv_w[lyr], row(conv_b), _pad_cols(row(dt_bias), DT_COLS), _pad_cols(row(a_log), DT_COLS),
        jnp.repeat(d_skip[lyr], HEAD).reshape(1, SSM_DIM), row(ssm_norm_w),
        expand,
    ]
    hist_rows = lambda t: jnp.pad(t, ((0, 0), (CONV_HIST - (CONV_W - 1), 0), (0, 0)))
    y_ssm, ssm_s, conv_s = _ssd_mix(
        p_all, hist_rows(state_conv[lyr]), state_ssm[lyr], ssd_wts,
        jnp.zeros((M_ROWS, SSM_DIM), BF16), prompt=False)
    y_ssm, ssm_p, conv_p = _ssd_mix(
        p_all, jnp.zeros((BATCH, CONV_HIST, CONV_DIM), F32),
        jnp.zeros((BATCH, SSM_HEADS, HEAD, SSM_STATE), F32), ssd_wts, y_ssm, prompt=True)

    x2 = _merge(x1, y_rw, y_ssm, p_all, row(b_gate), w_rw_out[lyr].astype(BF16),
                w_ssm_out[lyr].astype(BF16), w_out[lyr].astype(BF16), row(ln2_g), row(ln2_b))
    y_frames, y_extra = _ffn_ln_out(x2, ffn2_gu[lyr].astype(BF16), ffn2_dn[lyr].astype(BF16),
                                    row(ln3_g), row(ln3_b))

    y_prompt = y_frames.reshape(BATCH, SEQ, D_MODEL)
    y_sample = y_extra[:S_ROWS].reshape(DEC_BATCH, DEC_SEQ, D_MODEL)
    conv_of = lambda t: t[:, CONV_HIST - (CONV_W - 1):][None]
    return (y_prompt, y_sample,
            _rw_cols_inv(shift_p)[None], wkv_p[None], conv_of(conv_p), ssm_p[None],
            _rw_cols_inv(shift_s)[None], wkv_s[None], conv_of(conv_s), ssm_s[None])
```

```python
import functools
import math

import jax
import jax.numpy as jnp
from jax import lax
from jax.experimental import pallas as pl
from jax.experimental.pallas import tpu as pltpu

F32 = jnp.float32
BF16 = jnp.bfloat16

D_MODEL = 2048
BATCH = 4
SEQ = 2048
DEC_BATCH = 16
DEC_SEQ = 16
CHUNK = 64
N_META = 16
PAD = (-N_META) % CHUNK
HEAD_ROWS = PAD + N_META
SEQ_CHUNKS = (HEAD_ROWS + SEQ) // CHUNK
F_ROWS = BATCH * SEQ
S_ROWS = DEC_BATCH * DEC_SEQ
ROW_S0 = F_ROWS
ROW_H0 = F_ROWS + S_ROWS
X_ROWS = S_ROWS + BATCH * HEAD_ROWS
M_ROWS = F_ROWS + X_ROWS

HEAD = 64
RW_DIM = D_MODEL // 2
RW_HEADS = RW_DIM // HEAD
RW_DECAY_LORA = 96
RW_AAA_LORA = 96
RW_GATE_LORA = 256
RW_SHIFT_COLS = 3 * RW_DIM + RW_DECAY_LORA + RW_AAA_LORA + RW_GATE_LORA
RW_GN_EPS = 64e-5
SSM_DIM = D_MODEL
SSM_HEADS = SSM_DIM // HEAD
SSM_GROUPS = 4
SSM_HPG = SSM_HEADS // SSM_GROUPS
SSM_STATE = 128
CONV_W = 4
BC_DIM = SSM_GROUPS * SSM_STATE
CONV_DIM = SSM_DIM + 2 * BC_DIM
RMS_EPS = 1e-5
D_FF = 5632
LN_EPS = 1e-5
DEPTH = 1
ALPHA = (2 * DEPTH) ** 0.25

LANES = 128
SUBLANES = 8
VMEM_LIMIT = 56 * 1024 * 1024

LORA_PAD = LANES
RW_OFF_WD = 3 * RW_DIM
RW_OFF_AD = RW_OFF_WD + LORA_PAD
RW_OFF_GD = RW_OFF_AD + LORA_PAD
RW_COLS = RW_OFF_GD + RW_GATE_LORA
DT_COLS = LANES
CONV_HIST = SUBLANES

FFN_TM = 512
FFN_TF = 512
MM_TM = 512
MM_TN = 1024
MERGE_TM = 256

DT_SLOT = 512
PC_RW = 0
PC_DT = PC_RW + RW_COLS
PC_GATE = PC_DT + DT_SLOT
PC_Z = PC_GATE + 2 * D_MODEL
PC_XS = PC_Z + SSM_DIM
PC_BC = PC_XS + SSM_DIM
P_COLS = PC_BC + 2 * BC_DIM
assert PC_DT % DT_COLS == 0 and PC_GATE % D_MODEL == 0 and PC_Z % SSM_DIM == 0
assert PC_XS % SSM_DIM == 0 and PC_BC % (2 * BC_DIM) == 0 and P_COLS % MM_TN == 0
assert F_ROWS % FFN_TM == 0 and X_ROWS == FFN_TM and M_ROWS % MM_TM == 0 and ROW_H0 % MERGE_TM == 0


def _dot(a, b):
    return jnp.dot(a, b, preferred_element_type=F32)


def _dot_nt(a, b):
    return lax.dot_general(a, b, (((1,), (1,)), ((), ())), preferred_element_type=F32)


def _dot_tn(a, b):
    return lax.dot_general(a, b, (((0,), (0,)), ((), ())), preferred_element_type=F32)


def _split2(x):
    hi = x.astype(BF16)
    lo = (x - hi.astype(F32)).astype(BF16)
    return hi, lo


def _dot_sel_r(x, sel):
    hi, lo = _split2(x)
    return _dot(hi, sel) + _dot(lo, sel)


def _dot_sel_l(sel, x):
    hi, lo = _split2(x)
    return _dot(sel, hi) + _dot(sel, lo)


def _causal_blocks(n, blk):
    i = lax.broadcasted_iota(jnp.int32, (n, n), 0)
    j = lax.broadcasted_iota(jnp.int32, (n, n), 1)
    keep = jnp.logical_and(i >= j, jnp.bitwise_xor(i, j) < blk)
    return jnp.where(keep, 1.0, 0.0).astype(BF16)


def _sigmoid(x):
    return jax.nn.sigmoid(x)


def _softplus(x):
    return jnp.maximum(x, 0.0) + jnp.log1p(jnp.exp(-jnp.abs(x)))


def _layer_norm(s, g, b):
    mu = jnp.mean(s, axis=-1, keepdims=True)
    d = s - mu
    var = jnp.mean(d * d, axis=-1, keepdims=True)
    return d * lax.rsqrt(var + LN_EPS) * g + b


N_MAIN_TILES = F_ROWS // FFN_TM


def _ffn_step(xb_ref, wg_ref, wu_ref, wd_ref, acc_ref):
    xb = xb_ref[...]
    gate = _dot(xb, wg_ref[...])
    up = _dot(xb, wu_ref[...])
    h = (gate * _sigmoid(gate) * up).astype(BF16)
    acc_ref[...] += _dot(h, wd_ref[...])


def _ffn_in_kernel(xm_ref, xe_ref, wg_ref, wu_ref, wd_ref, g_ref, b_ref, o_ref, ob_ref,
                   xb_ref, acc_ref):
    i, f = pl.program_id(0), pl.program_id(1)
    load_x = lambda: jnp.where(i < N_MAIN_TILES, xm_ref[...], xe_ref[...])

    @pl.when(f == 0)
    def _():
        xb_ref[...] = load_x().astype(BF16)
        acc_ref[...] = jnp.zeros_like(acc_ref)

    _ffn_step(xb_ref, wg_ref, wu_ref, wd_ref, acc_ref)

    @pl.when(f == pl.num_programs(1) - 1)
    def _():
        y = _layer_norm(ALPHA * load_x() + 0.5 * acc_ref[...], g_ref[...], b_ref[...])
        o_ref[...] = y
        ob_ref[...] = y.astype(BF16)


def _ffn_out_kernel(x_ref, wg_ref, wu_ref, wd_ref, g_ref, b_ref, om_ref, oe_ref, xb_ref, acc_ref):
    i, f = pl.program_id(0), pl.program_id(1)

    @pl.when(f == 0)
    def _():
        xb_ref[...] = x_ref[...].astype(BF16)
        acc_ref[...] = jnp.zeros_like(acc_ref)

    _ffn_step(xb_ref, wg_ref, wu_ref, wd_ref, acc_ref)

    @pl.when(f == pl.num_programs(1) - 1)
    def _():
        y = _layer_norm(ALPHA * x_ref[...] + 0.5 * acc_ref[...], g_ref[...], b_ref[...])

        @pl.when(i < N_MAIN_TILES)
        def _():
            om_ref[...] = y

        @pl.when(i == N_MAIN_TILES)
        def _():
            oe_ref[...] = y


def _ffn_specs(d):
    nf = D_FF // FFN_TF
    return nf, [
        pl.BlockSpec((d, FFN_TF), lambda i, f: (0, f)),
        pl.BlockSpec((d, FFN_TF), lambda i, f: (0, f + nf)),
        pl.BlockSpec((FFN_TF, d), lambda i, f: (f, 0)),
        pl.BlockSpec((1, d), lambda i, f: (0, 0)),
        pl.BlockSpec((1, d), lambda i, f: (0, 0)),
    ]


_FFN_PARAMS = dict(dimension_semantics=("arbitrary", "arbitrary"), vmem_limit_bytes=VMEM_LIMIT)
_main_tile = lambda i, f: (jnp.minimum(i, N_MAIN_TILES - 1), 0)
_last_tile = lambda i, f: (0, 0)
_any_tile = lambda i, f: (i, 0)


def _ffn_ln_in(x_main, x_extra, w_gu, w_dn, ln_g, ln_b):
    d = x_main.shape[1]
    nf, wspecs = _ffn_specs(d)
    tile = (FFN_TM, d)
    return pl.pallas_call(
        _ffn_in_kernel,
        grid=(M_ROWS // FFN_TM, nf),
        in_specs=[pl.BlockSpec(tile, _main_tile),
                  pl.BlockSpec(tile, _last_tile, pipeline_mode=pl.Buffered(1))] + wspecs,
        out_specs=[pl.BlockSpec(tile, _any_tile), pl.BlockSpec(tile, _any_tile)],
        out_shape=[jax.ShapeDtypeStruct((M_ROWS, d), F32), jax.ShapeDtypeStruct((M_ROWS, d), BF16)],
        scratch_shapes=[pltpu.VMEM(tile, BF16), pltpu.VMEM(tile, F32)],
        compiler_params=pltpu.CompilerParams(**_FFN_PARAMS),
        name="ffn_ln_in",
    )(x_main, x_extra, w_gu, w_gu, w_dn, ln_g, ln_b)


def _ffn_ln_out(x, w_gu, w_dn, ln_g, ln_b):
    d = x.shape[1]
    nf, wspecs = _ffn_specs(d)
    tile = (FFN_TM, d)
    return pl.pallas_call(
        _ffn_out_kernel,
        grid=(M_ROWS // FFN_TM, nf),
        in_specs=[pl.BlockSpec(tile, _any_tile)] + wspecs,
        out_specs=[pl.BlockSpec(tile, _main_tile), pl.BlockSpec(tile, _last_tile)],
        out_shape=[jax.ShapeDtypeStruct((F_ROWS, d), F32), jax.ShapeDtypeStruct((X_ROWS, d), F32)],
        scratch_shapes=[pltpu.VMEM(tile, BF16), pltpu.VMEM(tile, F32)],
        compiler_params=pltpu.CompilerParams(**_FFN_PARAMS),
        name="ffn_ln_out",
    )(x, w_gu, w_gu, w_dn, ln_g, ln_b)


def _mm_kernel(x_ref, w_ref, o_ref):
    o_ref[...] = _dot(x_ref[...], w_ref[...])


def _project(x, w):
    m, k = x.shape
    n = w.shape[1]
    return pl.pallas_call(
        _mm_kernel,
        grid=(n // MM_TN, m // MM_TM),
        in_specs=[pl.BlockSpec((MM_TM, k), lambda j, i: (i, 0)),
                  pl.BlockSpec((k, MM_TN), lambda j, i: (0, j))],
        out_specs=pl.BlockSpec((MM_TM, MM_TN), lambda j, i: (i, j)),
        out_shape=jax.ShapeDtypeStruct((m, n), F32),
        compiler_params=pltpu.CompilerParams(
            dimension_semantics=("parallel", "parallel"), vmem_limit_bytes=VMEM_LIMIT),
        name="proj_in",
    )(x, w)


def _rwkv_kernel(*refs, chunk, pad, n_seq, n_p):
    p_refs = refs[:n_p]
    (hist_ref, s0_ref, mu_ref, w0_ref, w2_ref, a0_ref, a2_ref, g2_ref, kk_ref, ka_ref, rk_ref,
     gnw_ref, gnb_ref, seg_ref, y_ref, s_ref, shift_ref, yacc_ref) = refs[n_p:]
    c = pl.program_id(1)
    n_rows = n_seq * chunk
    seqs = range(n_seq)
    rsl = [slice(q * chunk, (q + 1) * chunk) for q in seqs]
    rows = lax.broadcasted_iota(jnp.int32, (n_rows, 1), 0)

    @pl.when(c == 0)
    def _():
        shift_ref[...] = hist_ref[...]
        s_ref[...] = s0_ref[...]

    p = jnp.concatenate([r[...] for r in p_refs], axis=0)
    if pad:
        p = jnp.where(jnp.logical_and(c == 0, jnp.bitwise_and(rows, chunk - 1) < pad), 0.0, p)
    prev = pltpu.roll(p, 1, 0)
    for q in seqs:
        prev = jnp.where(rows == q * chunk, shift_ref[q], prev)
    for q in seqs:
        shift_ref[q] = p[(q + 1) * chunk - 1:(q + 1) * chunk, :]
    ps = p + (prev - p) * mu_ref[...]

    r = ps[:, 0:RW_DIM]
    k = ps[:, RW_DIM:2 * RW_DIM]
    v = ps[:, 2 * RW_DIM:3 * RW_DIM]
    wd = ps[:, RW_OFF_WD:RW_OFF_WD + LORA_PAD]
    ad = ps[:, RW_OFF_AD:RW_OFF_AD + LORA_PAD]
    gd = ps[:, RW_OFF_GD:RW_OFF_GD + RW_GATE_LORA]

    lw = -math.exp(-0.5) * _sigmoid(w0_ref[...] + _dot(jnp.tanh(wd).astype(BF16), w2_ref[...]))
    a = _sigmoid(a0_ref[...] + _dot(ad.astype(BF16), a2_ref[...]))
    g = _dot(_sigmoid(gd).astype(BF16), g2_ref[...])

    seg = seg_ref[...]

    def head_sum(x):
        w = seg.shape[0]
        return jnp.concatenate(
            [_dot_sel_r(x[:, j * w:(j + 1) * w], seg) for j in range(RW_DIM // w)], axis=1)

    kk = k * kk_ref[...]
    kk = kk * lax.rsqrt(jnp.maximum(head_sum(kk * kk), 1e-24))
    k = k * (1.0 + (a - 1.0) * ka_ref[...])
    b_neg = -(kk * a)

    cum = _dot_sel_l(_causal_blocks(n_rows, chunk), lw)
    cum_last = [cum[(q + 1) * chunk - 1:(q + 1) * chunk, :] for q in seqs]
    cum_end = jnp.concatenate([jnp.broadcast_to(x, (chunk, RW_DIM)) for x in cum_last], axis=0)
    e_neg = jnp.exp(-cum)
    kt = (kk * jnp.exp(cum - lw)).astype(BF16)
    rt = (r * jnp.exp(cum)).astype(BF16)
    kd = (k * e_neg).astype(BF16)
    bd_neg = (b_neg * e_neg).astype(BF16)
    e_end = jnp.exp(cum_end - cum)
    k_end = (k * e_end).astype(BF16)
    b_end_neg = (b_neg * e_end).astype(BF16)
    p_end = [jnp.exp(x) for x in cum_last]
    vb = v.astype(BF16)

    ri = lax.broadcasted_iota(jnp.int32, (2 * chunk, 2 * chunk), 0)
    rj = lax.broadcasted_iota(jnp.int32, (2 * chunk, 2 * chunk), 1)
    bi = jnp.where(ri >= chunk, ri - chunk + 1, ri)
    bj = jnp.where(rj >= chunk, rj - chunk, rj)
    keep = bi > bj
    ti = lax.broadcasted_iota(jnp.int32, (chunk, 2 * chunk), 0)
    tj = lax.broadcasted_iota(jnp.int32, (chunk, 2 * chunk), 1)
    hi_lane = tj >= chunk
    eye_hi = (tj == ti + chunk).astype(F32)
    zeros_v = jnp.zeros((chunk, HEAD), BF16)

    pairs = [(q, h) for q in seqs for h in range(RW_HEADS)]
    sls = [(rsl[q], slice(h * HEAD, (h + 1) * HEAD)) for q, h in pairs]
    s_old = [s_ref[q, h] for q, h in pairs]
    s_b = [s.astype(BF16) for s in s_old]
    amats = [jnp.where(keep, _dot_nt(jnp.concatenate([kt[sl], rt[sl]], axis=0),
                                     jnp.concatenate([bd_neg[sl], kd[sl]], axis=0)), 0.0)
             for sl in sls]
    top = [m[:chunk] for m in amats]
    bot = [m[chunk:].astype(BF16) for m in amats]
    top_b = [x.astype(BF16) for x in top]
    tq = [_dot(xb[:, :chunk], jnp.where(hi_lane, eye_hi, x).astype(BF16)) + eye_hi
          for x, xb in zip(top, top_b)]
    n = 2
    while n < chunk:
        tq = [jnp.where(hi_lane, x, 0.0) + _dot(x[:, :chunk].astype(BF16), x.astype(BF16)) for x in tq]
        n *= 2
    t_b = [x[:, chunk:].astype(BF16) for x in tq]
    v_h = [vb[sl] for sl in sls]
    rhs = [_dot_nt(kt[sl], sb) + _dot(xb, jnp.concatenate([zeros_v, vh], axis=0))
           for sl, sb, xb, vh in zip(sls, s_b, top_b, v_h)]
    ub = [_dot(t, x.astype(BF16)).astype(BF16) for t, x in zip(t_b, rhs)]
    uv = [jnp.concatenate([u, vh], axis=0) for u, vh in zip(ub, v_h)]
    for i, (q, h) in enumerate(pairs):
        sl = sls[i]
        yacc_ref[sl] = _dot_nt(rt[sl], s_b[i]) + _dot(bot[i], uv[i])
        s_ref[q, h] = s_old[i] * p_end[q][:, sl[1]] + _dot_tn(
            uv[i], jnp.concatenate([b_end_neg[sl], k_end[sl]], axis=0))

    y = yacc_ref[...]
    inv_n = 1.0 / HEAD
    mean = head_sum(y) * inv_n
    yc = y - mean
    var = head_sum(yc * yc) * inv_n
    yn = yc * lax.rsqrt(var + RW_GN_EPS) * gnw_ref[...] + gnb_ref[...]
    bonus = head_sum(r * k * rk_ref[...]) * v
    out = ((yn + bonus) * g).astype(y_ref.dtype)
    for q in seqs:
        y_ref[q] = out[rsl[q]]


RWKV_GROUPS = (2, 8)
SSD_GROUPS = (1, 1)


class _SeqPlan:
    def __init__(self, prompt, groups):
        self.prompt = prompt
        if prompt:
            self.n_seq, self.chunk, self.pad, self.total, self.seq_len = groups[0], CHUNK, PAD, BATCH, SEQ
            self.grid = (BATCH // self.n_seq, SEQ_CHUNKS)
        else:
            self.n_seq, self.chunk, self.pad, self.total, self.seq_len = groups[1], DEC_SEQ, 0, DEC_BATCH, DEC_SEQ
            self.grid = (DEC_BATCH // self.n_seq, 1)

    def p_specs(self, width, col):
        cb = col // width
        if not self.prompt:
            rows = self.n_seq * self.chunk
            return [pl.BlockSpec((rows, width), lambda g, c: (ROW_S0 // rows + g, cb))]

        def spec(q):
            def index(g, c):
                seq = g * self.n_seq + q
                return jnp.where(c == 0, ROW_H0 // CHUNK + seq, seq * (SEQ // CHUNK) + c - 1), cb
            return pl.BlockSpec((CHUNK, width), index)
        return [spec(q) for q in range(self.n_seq)]

    def y_spec(self, dim):
        return pl.BlockSpec((self.n_seq, self.chunk, dim), lambda g, c: (g, jnp.maximum(c - 1, 0), 0))

    def y_shape(self, dim):
        return jax.ShapeDtypeStruct((self.total, self.seq_len, dim), BF16)

    def state_spec(self, shape):
        nd = len(shape)
        return pl.BlockSpec((self.n_seq,) + tuple(shape), lambda g, c: (g,) + (0,) * nd)


_MIX_PARAMS = dict(dimension_semantics=("parallel", "arbitrary"), vmem_limit_bytes=VMEM_LIMIT)


def _rwkv_mix(p_all, hist, s0, wts, *, prompt):
    plan = _SeqPlan(prompt, RWKV_GROUPS)
    p_specs = plan.p_specs(RW_COLS, PC_RW)
    const2 = lambda g, c: (0, 0)
    wspecs = [pl.BlockSpec(w.shape, const2) for w in wts]
    shift_spec = plan.state_spec((1, RW_COLS))
    wkv_spec = plan.state_spec((RW_HEADS, HEAD, HEAD))
    return pl.pallas_call(
        functools.partial(_rwkv_kernel, chunk=plan.chunk, pad=plan.pad, n_seq=plan.n_seq, n_p=len(p_specs)),
        grid=plan.grid,
        in_specs=p_specs + [shift_spec, wkv_spec] + wspecs,
        out_specs=[plan.y_spec(RW_DIM), wkv_spec, shift_spec],
        out_shape=[plan.y_shape(RW_DIM),
                   jax.ShapeDtypeStruct((plan.total, RW_HEADS, HEAD, HEAD), F32),
                   jax.ShapeDtypeStruct((plan.total, 1, RW_COLS), F32)],
        scratch_shapes=[pltpu.VMEM((plan.n_seq * plan.chunk, RW_DIM), F32)],
        compiler_params=pltpu.CompilerParams(**_MIX_PARAMS),
        name="rwkv_mix_c%d" % plan.chunk,
    )(*([p_all] * len(p_specs)), hist, s0, *wts)


def _ssd_kernel(*refs, chunk, pad, n_seq, n_p):
    xs_refs, bc_refs, z_refs, dt_refs = (refs[i * n_p:(i + 1) * n_p] for i in range(4))
    (hist_ref, h0_ref, cw_ref, cb_ref, dtb_ref, alog_ref, dskip_ref, nw_ref, exp_ref,
     y_ref, h_ref, tail_ref, xpad_ref, yacc_ref) = refs[4 * n_p:]
    c = pl.program_id(1)
    n_rows = n_seq * chunk
    seqs = range(n_seq)
    rsl = [slice(q * chunk, (q + 1) * chunk) for q in seqs]
    rows = lax.broadcasted_iota(jnp.int32, (n_rows, 1), 0)
    stack = lambda rs: jnp.concatenate([r[...] for r in rs], axis=0)

    @pl.when(c == 0)
    def _():
        xpad_ref[:, 0:CONV_HIST, :] = hist_ref[...]
        h_ref[...] = h0_ref[...]

    u = jnp.concatenate([stack(xs_refs), stack(bc_refs)], axis=1)
    if pad:
        is_pad = jnp.logical_and(c == 0, jnp.bitwise_and(rows, chunk - 1) < pad)
        u = jnp.where(is_pad, 0.0, u)
    convs = []
    for q in seqs:
        u_q = u[rsl[q]]
        xpad_ref[q, CONV_HIST:CONV_HIST + chunk, :] = u_q
        conv = cb_ref[...] + u_q * cw_ref[CONV_W - 1:CONV_W, :]
        for i in range(CONV_W - 1):
            back = CONV_W - 1 - i
            conv = conv + xpad_ref[q, CONV_HIST - back:CONV_HIST - back + chunk, :] * cw_ref[i:i + 1, :]
        hist_next = xpad_ref[q, chunk:chunk + CONV_HIST, :]
        xpad_ref[q, 0:CONV_HIST, :] = hist_next
        tail_ref[q] = hist_next
        convs.append(conv)
    conv = jnp.concatenate(convs, axis=0)
    xbc = conv * _sigmoid(conv)
    xs = xbc[:, 0:SSM_DIM]
    bm = xbc[:, SSM_DIM:SSM_DIM + BC_DIM].astype(BF16)
    cm = xbc[:, SSM_DIM + BC_DIM:CONV_DIM].astype(BF16)

    dt = _softplus(stack(dt_refs) + dtb_ref[...])
    if pad:
        dt = jnp.where(is_pad, 0.0, dt)
    da = dt * (-jnp.exp(alog_ref[...]))
    ci = lax.broadcasted_iota(jnp.int32, (chunk, chunk), 0)
    cj = lax.broadcasted_iota(jnp.int32, (chunk, chunk), 1)
    causal = ci >= cj
    acs = _dot_sel_l(_causal_blocks(n_rows, chunk), da)
    acs_t = acs.T
    acs_end = jnp.concatenate(
        [jnp.broadcast_to(acs[(q + 1) * chunk - 1:(q + 1) * chunk, :], (chunk, DT_COLS)) for q in seqs], axis=0)

    expand = exp_ref[...]
    xdt = xs * _dot_sel_r(dt, expand)
    xdt_b = xdt.astype(BF16)
    xdt_end = (xdt * _dot_sel_r(jnp.exp(acs_end - acs), expand)).astype(BF16)
    e_acs = _dot_sel_r(jnp.exp(acs), expand)
    decay_end = [jnp.exp(acs_t[:, (q + 1) * chunk - 1:(q + 1) * chunk]) for q in seqs]

    gw = SSM_HPG * HEAD
    groups = range(SSM_GROUPS)
    heads = range(SSM_HEADS)
    qg = [(q, g) for q in seqs for g in groups]
    qh = [(q, hd) for q in seqs for hd in heads]
    bm_g = {(q, g): bm[rsl[q], g * SSM_STATE:(g + 1) * SSM_STATE] for q, g in qg}
    cm_g = {(q, g): cm[rsl[q], g * SSM_STATE:(g + 1) * SSM_STATE] for q, g in qg}
    h_old = {(q, hd): h_ref[q, hd] for q, hd in qh}
    cb = {k: _dot_nt(cm_g[k], bm_g[k]) for k in qg}
    y_off = {(q, g): _dot_nt(cm_g[q, g], jnp.concatenate(
        [h_old[q, g * SSM_HPG + hh].astype(BF16) for hh in range(SSM_HPG)], axis=0)) for q, g in qg}
    st = {(q, g): _dot_tn(xdt_end[rsl[q], g * gw:(g + 1) * gw], bm_g[q, g]) for q, g in qg}
    for q, hd in qh:
        sl = slice(hd * HEAD, (hd + 1) * HEAD)
        seg = acs[rsl[q], hd:hd + 1] - acs_t[hd:hd + 1, rsl[q]]
        lmat = jnp.exp(jnp.where(causal, seg, -jnp.inf))
        yacc_ref[rsl[q], sl] = _dot((cb[q, hd // SSM_HPG] * lmat).astype(BF16), xdt_b[rsl[q], sl])
    for q, hd in qh:
        g, hh = divmod(hd, SSM_HPG)
        h_ref[q, hd] = h_old[q, hd] * decay_end[q][hd:hd + 1, :] + st[q, g][hh * HEAD:(hh + 1) * HEAD, :]

    y_off_all = jnp.concatenate(
        [jnp.concatenate([y_off[q, g] for g in groups], axis=1) for q in seqs], axis=0)
    y = yacc_ref[...] + y_off_all * e_acs + xs * dskip_ref[...]
    zz = stack(z_refs)
    y = y * (zz * _sigmoid(zz))
    parts = []
    for g in groups:
        yg = y[:, g * gw:(g + 1) * gw]
        ms = jnp.mean(yg * yg, axis=-1, keepdims=True)
        parts.append(yg * lax.rsqrt(ms + RMS_EPS))
    out = (jnp.concatenate(parts, axis=1) * nw_ref[...]).astype(y_ref.dtype)
    for q in seqs:
        y_ref[q] = out[rsl[q]]


def _ssd_mix(p_all, hist, h0, wts, *, prompt):
    plan = _SeqPlan(prompt, SSD_GROUPS)
    p_specs = (plan.p_specs(SSM_DIM, PC_XS) + plan.p_specs(2 * BC_DIM, PC_BC)
               + plan.p_specs(SSM_DIM, PC_Z) + plan.p_specs(DT_COLS, PC_DT))
    const2 = lambda g, c: (0, 0)
    wspecs = [pl.BlockSpec(w.shape, const2) for w in wts]
    conv_spec = plan.state_spec((CONV_HIST, CONV_DIM))
    ssm_spec = plan.state_spec((SSM_HEADS, HEAD, SSM_STATE))
    return pl.pallas_call(
        functools.partial(_ssd_kernel, chunk=plan.chunk, pad=plan.pad, n_seq=plan.n_seq, n_p=len(p_specs) // 4),
        grid=plan.grid,
        in_specs=p_specs + [conv_spec, ssm_spec] + wspecs,
        out_specs=[plan.y_spec(SSM_DIM), ssm_spec, conv_spec],
        out_shape=[plan.y_shape(SSM_DIM),
                   jax.ShapeDtypeStruct((plan.total, SSM_HEADS, HEAD, SSM_STATE), F32),
                   jax.ShapeDtypeStruct((plan.total, CONV_HIST, CONV_DIM), F32)],
        scratch_shapes=[pltpu.VMEM((plan.n_seq, CONV_HIST + plan.chunk, CONV_DIM), F32),
                        pltpu.VMEM((plan.n_seq * plan.chunk, SSM_DIM), F32)],
        compiler_params=pltpu.CompilerParams(**_MIX_PARAMS),
        name="ssd_mix_c%d" % plan.chunk,
    )(*([p_all] * len(p_specs)), hist, h0, *wts)


MERGE_F_TILES = F_ROWS // MERGE_TM
assert S_ROWS == MERGE_TM


def _merge_kernel(x_ref, yrw_f_ref, yrw_s_ref, yssm_f_ref, yssm_s_ref, ga_ref, gb_ref, bga_ref, bgb_ref,
                  wrw_ref, wssm_ref, wout_ref, g_ref, b_ref, o_ref):
    is_frame = pl.program_id(0) < MERGE_F_TILES
    y_rw = jnp.where(is_frame, yrw_f_ref[...], yrw_s_ref[...])
    y_ssm = jnp.where(is_frame, yssm_f_ref[...], yssm_s_ref[...])
    merged = (_sigmoid(ga_ref[...] + bga_ref[...]) * _dot(y_rw, wrw_ref[...])
              + _sigmoid(gb_ref[...] + bgb_ref[...]) * _dot(y_ssm, wssm_ref[...]))
    s = ALPHA * x_ref[...] + _dot(merged.astype(BF16), wout_ref[...])
    o_ref[...] = _layer_norm(s, g_ref[...], b_ref[...])


def _merge(x, y_rw_f, y_rw_s, y_ssm_f, y_ssm_s, p_all, b_gate, w_rw_out, w_ssm_out, w_out, ln_g, ln_b):
    m, d = x.shape
    tm = MERGE_TM
    row = lambda i: (i, 0)
    frame_row = lambda i: (jnp.minimum(i, MERGE_F_TILES - 1), 0)
    const = lambda i: (0, 0)
    resident = lambda w: pl.BlockSpec(w.shape, const, pipeline_mode=pl.Buffered(1))
    return pl.pallas_call(
        _merge_kernel,
        grid=(m // tm,),
        in_specs=[
            pl.BlockSpec((tm, d), row),
            pl.BlockSpec((tm, RW_DIM), frame_row),
            pl.BlockSpec((tm, RW_DIM), const),
            pl.BlockSpec((tm, SSM_DIM), frame_row),
            pl.BlockSpec((tm, SSM_DIM), const),
            pl.BlockSpec((tm, d), lambda i: (i, PC_GATE // d)),
            pl.BlockSpec((tm, d), lambda i: (i, PC_GATE // d + 1)),
            pl.BlockSpec((1, d), lambda i: (0, 0)),
            pl.BlockSpec((1, d), lambda i: (0, 1)),
            resident(w_rw_out), resident(w_ssm_out), resident(w_out),
            pl.BlockSpec((1, d), const),
            pl.BlockSpec((1, d), const),
        ],
        out_specs=pl.BlockSpec((tm, d), row),
        out_shape=jax.ShapeDtypeStruct((m, d), F32),
        compiler_params=pltpu.CompilerParams(
            dimension_semantics=("arbitrary",), vmem_limit_bytes=VMEM_LIMIT),
        name="merge_ln",
    )(x, y_rw_f, y_rw_s, y_ssm_f, y_ssm_s, p_all, p_all, b_gate, b_gate, w_rw_out, w_ssm_out, w_out,
      ln_g, ln_b)


def _pad_cols(x, width):
    return jnp.pad(x, [(0, 0)] * (x.ndim - 1) + [(0, width - x.shape[-1])])


def _rw_cols(x):
    o_wd, o_ad, o_gd = 3 * RW_DIM, 3 * RW_DIM + RW_DECAY_LORA, 3 * RW_DIM + RW_DECAY_LORA + RW_AAA_LORA
    return jnp.concatenate([
        x[..., :o_wd],
        _pad_cols(x[..., o_wd:o_ad], LORA_PAD),
        _pad_cols(x[..., o_ad:o_gd], LORA_PAD),
        x[..., o_gd:],
    ], axis=-1)


def _rw_cols_inv(x):
    return jnp.concatenate([
        x[..., :RW_OFF_WD],
        x[..., RW_OFF_WD:RW_OFF_WD + RW_DECAY_LORA],
        x[..., RW_OFF_AD:RW_OFF_AD + RW_AAA_LORA],
        x[..., RW_OFF_GD:],
    ], axis=-1)


def _block_ones(n, blk):
    i = jnp.arange(n) // blk
    return (i[:, None] == i[None, :]).astype(BF16)


def kernel(x_prompt, x_sample, state_rwkv_shift, state_wkv, state_conv, state_ssm, meta_tokens, ffn1_gu, ffn1_dn, ln1_g, ln1_b, w_in, b_gate, rw_mu, rw_w0, rw_w2, rw_a0, rw_a2, rw_g2, rw_kk, rw_ka, rw_rk, rw_gn_w, rw_gn_b, conv_w, conv_b, dt_bias, a_log, d_skip, ssm_norm_w, w_rw_out, w_ssm_out, w_out, ln2_g, ln2_b, ffn2_gu, ffn2_dn, ln3_g, ln3_b):
    lyr = 0
    row = lambda t: t[lyr].reshape(1, -1).astype(F32)

    head_rows = jnp.concatenate([jnp.zeros((PAD, D_MODEL), F32), meta_tokens.astype(F32)], axis=0)
    x_extra = jnp.concatenate([x_sample.reshape(S_ROWS, D_MODEL)] + [head_rows] * BATCH, axis=0)
    x1, x1b = _ffn_ln_in(x_prompt.reshape(F_ROWS, D_MODEL), x_extra, ffn1_gu[lyr].astype(BF16),
                         ffn1_dn[lyr].astype(BF16), row(ln1_g), row(ln1_b))

    w = w_in[lyr].astype(BF16)
    o_z = RW_SHIFT_COLS
    o_xbc = o_z + SSM_DIM
    o_dt = o_xbc + CONV_DIM
    o_gate = o_dt + SSM_HEADS
    w_all = jnp.concatenate([
        _rw_cols(w[:, :o_z]), _pad_cols(w[:, o_dt:o_gate], DT_SLOT), w[:, o_gate:],
        w[:, o_z:o_xbc], w[:, o_xbc:o_dt]], axis=1)
    p_all = _project(x1b, w_all)

    pad_rows = lambda t, n: jnp.pad(t, ((0, n - t.shape[0]), (0, 0)))
    rw_wts = [
        _rw_cols(rw_mu[lyr]).reshape(1, RW_COLS), row(rw_w0),
        pad_rows(rw_w2[lyr], LORA_PAD).astype(BF16), row(rw_a0),
        pad_rows(rw_a2[lyr], LORA_PAD).astype(BF16), rw_g2[lyr].astype(BF16),
        row(rw_kk), row(rw_ka), row(rw_rk), row(rw_gn_w), row(rw_gn_b),
        _block_ones(2 * LANES, HEAD),
    ]
    y_rw_s, wkv_s, shift_s = _rwkv_mix(
        p_all, _rw_cols(state_rwkv_shift[lyr]), state_wkv[lyr], rw_wts, prompt=False)
    y_rw_f, wkv_p, shift_p = _rwkv_mix(
        p_all, jnp.zeros((BATCH, 1, RW_COLS), F32), jnp.zeros((BATCH, RW_HEADS, HEAD, HEAD), F32), rw_wts,
        prompt=True)

    head_of_lane = jnp.arange(SSM_DIM) // HEAD
    expand = (jnp.arange(DT_COLS)[:, None] == head_of_lane[None, :]).astype(BF16)
    ssd_wts = [
        conv_w[lyr], row(conv_b), _pad_cols(row(dt_bias), DT_COLS), _pad_cols(row(a_log), DT_COLS),
        jnp.repeat(d_skip[lyr], HEAD).reshape(1, SSM_DIM), row(ssm_norm_w),
        expand,
    ]
    hist_rows = lambda t: jnp.pad(t, ((0, 0), (CONV_HIST - (CONV_W - 1), 0), (0, 0)))
    y_ssm_s, ssm_s, conv_s = _ssd_mix(
        p_all, hist_rows(state_conv[lyr]), state_ssm[lyr], ssd_wts, prompt=False)
    y_ssm_f, ssm_p, conv_p = _ssd_mix(
        p_all, jnp.zeros((BATCH, CONV_HIST, CONV_DIM), F32),
        jnp.zeros((BATCH, SSM_HEADS, HEAD, SSM_STATE), F32), ssd_wts, prompt=True)

    flat = lambda t: t.reshape(-1, t.shape[-1])
    x2 = _merge(x1, flat(y_rw_f), flat(y_rw_s), flat(y_ssm_f), flat(y_ssm_s), p_all, row(b_gate),
                w_rw_out[lyr].astype(BF16), w_ssm_out[lyr].astype(BF16), w_out[lyr].astype(BF16),
                row(ln2_g), row(ln2_b))
    y_frames, y_extra = _ffn_ln_out(x2, ffn2_gu[lyr].astype(BF16), ffn2_dn[lyr].astype(BF16),
                                    row(ln3_g), row(ln3_b))

    y_prompt = y_frames.reshape(BATCH, SEQ, D_MODEL)
    y_sample = y_extra[:S_ROWS].reshape(DEC_BATCH, DEC_SEQ, D_MODEL)
    conv_of = lambda t: t[:, CONV_HIST - (CONV_W - 1):][None]
    return (y_prompt, y_sample,
            _rw_cols_inv(shift_p)[None], wkv_p[None], conv_of(conv_p), ssm_p[None],
            _rw_cols_inv(shift_s)[None], wkv_s[None], conv_of(conv_s), ssm_s[None])
```

```python
import functools
import math

import jax
import jax.numpy as jnp
from jax import lax
from jax.experimental import pallas as pl
from jax.experimental.pallas import tpu as pltpu

F32 = jnp.float32
BF16 = jnp.bfloat16

D_MODEL = 2048
BATCH = 4
SEQ = 2048
DEC_BATCH = 16
DEC_SEQ = 16
CHUNK = 64
N_META = 16
PAD = (-N_META) % CHUNK
HEAD_ROWS = PAD + N_META
SEQ_CHUNKS = (HEAD_ROWS + SEQ) // CHUNK
F_ROWS = BATCH * SEQ
S_ROWS = DEC_BATCH * DEC_SEQ
ROW_S0 = F_ROWS
ROW_H0 = F_ROWS + S_ROWS
X_ROWS = S_ROWS + BATCH * HEAD_ROWS
M_ROWS = F_ROWS + X_ROWS

HEAD = 64
RW_DIM = D_MODEL // 2
RW_HEADS = RW_DIM // HEAD
RW_DECAY_LORA = 96
RW_AAA_LORA = 96
RW_GATE_LORA = 256
RW_SHIFT_COLS = 3 * RW_DIM + RW_DECAY_LORA + RW_AAA_LORA + RW_GATE_LORA
RW_GN_EPS = 64e-5
SSM_DIM = D_MODEL
SSM_HEADS = SSM_DIM // HEAD
SSM_GROUPS = 4
SSM_HPG = SSM_HEADS // SSM_GROUPS
SSM_STATE = 128
CONV_W = 4
BC_DIM = SSM_GROUPS * SSM_STATE
CONV_DIM = SSM_DIM + 2 * BC_DIM
RMS_EPS = 1e-5
D_FF = 5632
LN_EPS = 1e-5
DEPTH = 1
ALPHA = (2 * DEPTH) ** 0.25

LANES = 128
SUBLANES = 8
VMEM_LIMIT = 56 * 1024 * 1024

LORA_PAD = LANES
RW_OFF_WD = 3 * RW_DIM
RW_OFF_AD = RW_OFF_WD + LORA_PAD
RW_OFF_GD = RW_OFF_AD + LORA_PAD
RW_COLS = RW_OFF_GD + RW_GATE_LORA
DT_COLS = LANES
CONV_HIST = SUBLANES

CAST_ROWS = 2 * SUBLANES
FFN_TM = 512
FFN_TF = 512
MM_TM = 512
MM_TN = 1024
MERGE_TM = 256

DT_SLOT = 512
PC_RW = 0
PC_DT = PC_RW + RW_COLS
PC_GATE = PC_DT + DT_SLOT
PC_Z = PC_GATE + 2 * D_MODEL
PC_XS = PC_Z + SSM_DIM
PC_BC = PC_XS + SSM_DIM
P_COLS = PC_BC + 2 * BC_DIM
assert PC_DT % DT_COLS == 0 and PC_GATE % D_MODEL == 0 and PC_Z % SSM_DIM == 0
assert PC_XS % SSM_DIM == 0 and PC_BC % (2 * BC_DIM) == 0 and P_COLS % MM_TN == 0
assert F_ROWS % FFN_TM == 0 and X_ROWS == FFN_TM and M_ROWS % MM_TM == 0 and ROW_H0 % MERGE_TM == 0


def _dot(a, b):
    return jnp.dot(a, b, preferred_element_type=F32)


def _dot_nt(a, b):
    return lax.dot_general(a, b, (((1,), (1,)), ((), ())), preferred_element_type=F32)


def _dot_tn(a, b):
    return lax.dot_general(a, b, (((0,), (0,)), ((), ())), preferred_element_type=F32)


def _split2(x):
    hi = x.astype(BF16)
    lo = (x - hi.astype(F32)).astype(BF16)
    return hi, lo


def _dot_sel_r(x, sel):
    hi, lo = _split2(x)
    return _dot(hi, sel) + _dot(lo, sel)


def _dot_sel_l(sel, x):
    hi, lo = _split2(x)
    return _dot(sel, hi) + _dot(sel, lo)


def _causal_blocks(n, blk):
    i = lax.broadcasted_iota(jnp.int32, (n, n), 0)
    j = lax.broadcasted_iota(jnp.int32, (n, n), 1)
    keep = jnp.logical_and(i >= j, jnp.bitwise_xor(i, j) < blk)
    return jnp.where(keep, 1.0, 0.0).astype(BF16)


def _sigmoid(x):
    return jax.nn.sigmoid(x)


def _softplus(x):
    return jnp.maximum(x, 0.0) + jnp.log1p(jnp.exp(-jnp.abs(x)))


def _layer_norm(s, g, b):
    mu = jnp.mean(s, axis=-1, keepdims=True)
    d = s - mu
    var = jnp.mean(d * d, axis=-1, keepdims=True)
    return d * lax.rsqrt(var + LN_EPS) * g + b


class _CastSlot:
    def __init__(self, w, rows, t0):
        assert w.ndim == 2 and w.shape[0] % rows == 0
        self.w, self.rows, self.t0, self.n = w, rows, t0, w.shape[0] // rows

    def spec(self, step_of):
        def index(*ids):
            return jnp.clip(step_of(*ids) - self.t0, 0, self.n - 1), 0
        return pl.BlockSpec((self.rows, self.w.shape[1]), index)

    def out_shape(self):
        return jax.ShapeDtypeStruct(self.w.shape, BF16)


def _run_cast_slots(t, slots, in_refs, out_refs):
    for slot, i_ref, o_ref in zip(slots, in_refs, out_refs, strict=True):
        @pl.when(jnp.logical_and(t >= slot.t0, t < slot.t0 + slot.n))
        def _(i_ref=i_ref, o_ref=o_ref):
            o_ref[...] = i_ref[...].astype(BF16)


def _chain_slots(weights_rows, t0=0):
    slots = []
    for w, rows in weights_rows:
        slots.append(_CastSlot(w, rows, t0))
        t0 += slots[-1].n
    return slots, t0


N_MAIN_TILES = F_ROWS // FFN_TM


def _ffn_step(xb_ref, wg_ref, wu_ref, wd_ref, acc_ref):
    xb = xb_ref[...]
    gate = _dot(xb, wg_ref[...])
    up = _dot(xb, wu_ref[...])
    h = (gate * _sigmoid(gate) * up).astype(BF16)
    acc_ref[...] += _dot(h, wd_ref[...])


def _ffn_in_kernel(xm_ref, xe_ref, wg_ref, wu_ref, wd_ref, g_ref, b_ref, *rest, slots):
    n = len(slots)
    cast_in, (o_ref, ob_ref), cast_out, (xb_ref, acc_ref) = rest[:n], rest[n:n + 2], rest[n + 2:2 * n + 2], rest[2 * n + 2:]
    i, f = pl.program_id(0), pl.program_id(1)
    _run_cast_slots(i * pl.num_programs(1) + f, slots, cast_in, cast_out)
    load_x = lambda: jnp.where(i < N_MAIN_TILES, xm_ref[...], xe_ref[...])

    @pl.when(f == 0)
    def _():
        xb_ref[...] = load_x().astype(BF16)
        acc_ref[...] = jnp.zeros_like(acc_ref)

    _ffn_step(xb_ref, wg_ref, wu_ref, wd_ref, acc_ref)

    @pl.when(f == pl.num_programs(1) - 1)
    def _():
        y = _layer_norm(ALPHA * load_x() + 0.5 * acc_ref[...], g_ref[...], b_ref[...])
        o_ref[...] = y
        ob_ref[...] = y.astype(BF16)


def _ffn_out_kernel(x_ref, wg_ref, wu_ref, wd_ref, g_ref, b_ref, om_ref, oe_ref, xb_ref, acc_ref):
    i, f = pl.program_id(0), pl.program_id(1)

    @pl.when(f == 0)
    def _():
        xb_ref[...] = x_ref[...].astype(BF16)
        acc_ref[...] = jnp.zeros_like(acc_ref)

    _ffn_step(xb_ref, wg_ref, wu_ref, wd_ref, acc_ref)

    @pl.when(f == pl.num_programs(1) - 1)
    def _():
        y = _layer_norm(ALPHA * x_ref[...] + 0.5 * acc_ref[...], g_ref[...], b_ref[...])

        @pl.when(i < N_MAIN_TILES)
        def _():
            om_ref[...] = y

        @pl.when(i == N_MAIN_TILES)
        def _():
            oe_ref[...] = y


def _ffn_specs(d):
    nf = D_FF // FFN_TF
    return nf, [
        pl.BlockSpec((d, FFN_TF), lambda i, f: (0, f)),
        pl.BlockSpec((d, FFN_TF), lambda i, f: (0, f + nf)),
        pl.BlockSpec((FFN_TF, d), lambda i, f: (f, 0)),
        pl.BlockSpec((1, d), lambda i, f: (0, 0)),
        pl.BlockSpec((1, d), lambda i, f: (0, 0)),
    ]


_FFN_PARAMS = dict(dimension_semantics=("arbitrary", "arbitrary"), vmem_limit_bytes=VMEM_LIMIT)
_main_tile = lambda i, f: (jnp.minimum(i, N_MAIN_TILES - 1), 0)
_last_tile = lambda i, f: (0, 0)
_any_tile = lambda i, f: (i, 0)


def _ffn_ln_in(x_main, x_extra, w_gu, w_dn, ln_g, ln_b, cast_weights):
    d = x_main.shape[1]
    nf, wspecs = _ffn_specs(d)
    tile = (FFN_TM, d)
    grid = (M_ROWS // FFN_TM, nf)
    slots, t_end = _chain_slots(cast_weights)
    assert t_end <= grid[0] * grid[1]
    cast_specs = [s.spec(lambda i, f: i * nf + f) for s in slots]
    return pl.pallas_call(
        functools.partial(_ffn_in_kernel, slots=slots),
        grid=grid,
        in_specs=[pl.BlockSpec(tile, _main_tile),
                  pl.BlockSpec(tile, _last_tile, pipeline_mode=pl.Buffered(1))] + wspecs + cast_specs,
        out_specs=[pl.BlockSpec(tile, _any_tile), pl.BlockSpec(tile, _any_tile)] + cast_specs,
        out_shape=[jax.ShapeDtypeStruct((M_ROWS, d), F32), jax.ShapeDtypeStruct((M_ROWS, d), BF16)]
        + [s.out_shape() for s in slots],
        scratch_shapes=[pltpu.VMEM(tile, BF16), pltpu.VMEM(tile, F32)],
        compiler_params=pltpu.CompilerParams(**_FFN_PARAMS),
        name="ffn_ln_in",
    )(x_main, x_extra, w_gu, w_gu, w_dn, ln_g, ln_b, *[s.w for s in slots])


def _ffn_ln_out(x, w_gu, w_dn, ln_g, ln_b):
    d = x.shape[1]
    nf, wspecs = _ffn_specs(d)
    tile = (FFN_TM, d)
    return pl.pallas_call(
        _ffn_out_kernel,
        grid=(M_ROWS // FFN_TM, nf),
        in_specs=[pl.BlockSpec(tile, _any_tile)] + wspecs,
        out_specs=[pl.BlockSpec(tile, _main_tile), pl.BlockSpec(tile, _last_tile)],
        out_shape=[jax.ShapeDtypeStruct((F_ROWS, d), F32), jax.ShapeDtypeStruct((X_ROWS, d), F32)],
        scratch_shapes=[pltpu.VMEM(tile, BF16), pltpu.VMEM(tile, F32)],
        compiler_params=pltpu.CompilerParams(**_FFN_PARAMS),
        name="ffn_ln_out",
    )(x, w_gu, w_gu, w_dn, ln_g, ln_b)


def _mm_kernel(x_ref, w_ref, *rest, slots):
    n = len(slots)
    cast_in, o_ref, cast_out = rest[:n], rest[n], rest[n + 1:]
    _run_cast_slots(pl.program_id(0) * pl.num_programs(1) + pl.program_id(1), slots, cast_in, cast_out)
    o_ref[...] = _dot(x_ref[...], w_ref[...])


def _project(x, w, cast_chains):
    m, k = x.shape
    n = w.shape[1]
    grid = (n // MM_TN, m // MM_TM)
    slots = []
    for chain in cast_chains:
        chain_slots, t_end = _chain_slots(chain)
        assert t_end <= grid[0] * grid[1]
        slots += chain_slots
    cast_specs = [s.spec(lambda j, i: j * grid[1] + i) for s in slots]
    return pl.pallas_call(
        functools.partial(_mm_kernel, slots=slots),
        grid=grid,
        in_specs=[pl.BlockSpec((MM_TM, k), lambda j, i: (i, 0)),
                  pl.BlockSpec((k, MM_TN), lambda j, i: (0, j))] + cast_specs,
        out_specs=[pl.BlockSpec((MM_TM, MM_TN), lambda j, i: (i, j))] + cast_specs,
        out_shape=[jax.ShapeDtypeStruct((m, n), F32)] + [s.out_shape() for s in slots],
        compiler_params=pltpu.CompilerParams(
            dimension_semantics=("arbitrary", "arbitrary"), vmem_limit_bytes=VMEM_LIMIT),
        name="proj_in",
    )(x, w, *[s.w for s in slots])


def _rwkv_kernel(*refs, chunk, pad, n_seq, n_p):
    p_refs = refs[:n_p]
    (hist_ref, s0_ref, mu_ref, w0_ref, w2_ref, a0_ref, a2_ref, g2_ref, kk_ref, ka_ref, rk_ref,
     gnw_ref, gnb_ref, seg_ref, y_ref, s_ref, shift_ref, yacc_ref) = refs[n_p:]
    c = pl.program_id(1)
    n_rows = n_seq * chunk
    seqs = range(n_seq)
    rsl = [slice(q * chunk, (q + 1) * chunk) for q in seqs]
    rows = lax.broadcasted_iota(jnp.int32, (n_rows, 1), 0)

    @pl.when(c == 0)
    def _():
        shift_ref[...] = hist_ref[...]
        s_ref[...] = s0_ref[...]

    p = jnp.concatenate([r[...] for r in p_refs], axis=0)
    if pad:
        p = jnp.where(jnp.logical_and(c == 0, jnp.bitwise_and(rows, chunk - 1) < pad), 0.0, p)
    prev = pltpu.roll(p, 1, 0)
    for q in seqs:
        prev = jnp.where(rows == q * chunk, shift_ref[q], prev)
    for q in seqs:
        shift_ref[q] = p[(q + 1) * chunk - 1:(q + 1) * chunk, :]
    ps = p + (prev - p) * mu_ref[...]

    r = ps[:, 0:RW_DIM]
    k = ps[:, RW_DIM:2 * RW_DIM]
    v = ps[:, 2 * RW_DIM:3 * RW_DIM]
    wd = ps[:, RW_OFF_WD:RW_OFF_WD + LORA_PAD]
    ad = ps[:, RW_OFF_AD:RW_OFF_AD + LORA_PAD]
    gd = ps[:, RW_OFF_GD:RW_OFF_GD + RW_GATE_LORA]

    lw = -math.exp(-0.5) * _sigmoid(w0_ref[...] + _dot(jnp.tanh(wd).astype(BF16), w2_ref[...]))
    a = _sigmoid(a0_ref[...] + _dot(ad.astype(BF16), a2_ref[...]))
    g = _dot(_sigmoid(gd).astype(BF16), g2_ref[...])

    seg = seg_ref[...]

    def head_sum(x):
        w = seg.shape[0]
        return jnp.concatenate(
            [_dot_sel_r(x[:, j * w:(j + 1) * w], seg) for j in range(RW_DIM // w)], axis=1)

    kk = k * kk_ref[...]
    kk = kk * lax.rsqrt(jnp.maximum(head_sum(kk * kk), 1e-24))
    k = k * (1.0 + (a - 1.0) * ka_ref[...])
    b_neg = -(kk * a)

    cum = _dot_sel_l(_causal_blocks(n_rows, chunk), lw)
    cum_last = [cum[(q + 1) * chunk - 1:(q + 1) * chunk, :] for q in seqs]
    cum_end = jnp.concatenate([jnp.broadcast_to(x, (chunk, RW_DIM)) for x in cum_last], axis=0)
    e_neg = jnp.exp(-cum)
    kt = (kk * jnp.exp(cum - lw)).astype(BF16)
    rt = (r * jnp.exp(cum)).astype(BF16)
    kd = (k * e_neg).astype(BF16)
    bd_neg = (b_neg * e_neg).astype(BF16)
    e_end = jnp.exp(cum_end - cum)
    k_end = (k * e_end).astype(BF16)
    b_end_neg = (b_neg * e_end).astype(BF16)
    p_end = [jnp.exp(x) for x in cum_last]
    vb = v.astype(BF16)

    ri = lax.broadcasted_iota(jnp.int32, (2 * chunk, 2 * chunk), 0)
    rj = lax.broadcasted_iota(jnp.int32, (2 * chunk, 2 * chunk), 1)
    bi = jnp.where(ri >= chunk, ri - chunk + 1, ri)
    bj = jnp.where(rj >= chunk, rj - chunk, rj)
    keep = bi > bj
    ti = lax.broadcasted_iota(jnp.int32, (chunk, 2 * chunk), 0)
    tj = lax.broadcasted_iota(jnp.int32, (chunk, 2 * chunk), 1)
    hi_lane = tj >= chunk
    eye_hi = (tj == ti + chunk).astype(F32)
    zeros_v = jnp.zeros((chunk, HEAD), BF16)

    pairs = [(q, h) for q in seqs for h in range(RW_HEADS)]
    sls = [(rsl[q], slice(h * HEAD, (h + 1) * HEAD)) for q, h in pairs]
    s_old = [s_ref[q, h] for q, h in pairs]
    s_b = [s.astype(BF16) for s in s_old]
    amats = [jnp.where(keep, _dot_nt(jnp.concatenate([kt[sl], rt[sl]], axis=0),
                                     jnp.concatenate([bd_neg[sl], kd[sl]], axis=0)), 0.0)
             for sl in sls]
    top = [m[:chunk] for m in amats]
    bot = [m[chunk:].astype(BF16) for m in amats]
    top_b = [x.astype(BF16) for x in top]
    tq = [_dot(xb[:, :chunk], jnp.where(hi_lane, eye_hi, x).astype(BF16)) + eye_hi
          for x, xb in zip(top, top_b)]
    n = 2
    while n < chunk:
        tq = [jnp.where(hi_lane, x, 0.0) + _dot(x[:, :chunk].astype(BF16), x.astype(BF16)) for x in tq]
        n *= 2
    t_b = [x[:, chunk:].astype(BF16) for x in tq]
    v_h = [vb[sl] for sl in sls]
    rhs = [_dot_nt(kt[sl], sb) + _dot(xb, jnp.concatenate([zeros_v, vh], axis=0))
           for sl, sb, xb, vh in zip(sls, s_b, top_b, v_h)]
    ub = [_dot(t, x.astype(BF16)).astype(BF16) for t, x in zip(t_b, rhs)]
    uv = [jnp.concatenate([u, vh], axis=0) for u, vh in zip(ub, v_h)]
    for i, (q, h) in enumerate(pairs):
        sl = sls[i]
        yacc_ref[sl] = _dot_nt(rt[sl], s_b[i]) + _dot(bot[i], uv[i])
        s_ref[q, h] = s_old[i] * p_end[q][:, sl[1]] + _dot_tn(
            uv[i], jnp.concatenate([b_end_neg[sl], k_end[sl]], axis=0))

    y = yacc_ref[...]
    inv_n = 1.0 / HEAD
    mean = head_sum(y) * inv_n
    yc = y - mean
    var = head_sum(yc * yc) * inv_n
    yn = yc * lax.rsqrt(var + RW_GN_EPS) * gnw_ref[...] + gnb_ref[...]
    bonus = head_sum(r * k * rk_ref[...]) * v
    out = ((yn + bonus) * g).astype(y_ref.dtype)
    for q in seqs:
        y_ref[q] = out[rsl[q]]


RWKV_GROUPS = (2, 8)
SSD_GROUPS = (1, 1)


class _SeqPlan:
    def __init__(self, prompt, groups):
        self.prompt = prompt
        if prompt:
            self.n_seq, self.chunk, self.pad, self.total, self.seq_len = groups[0], CHUNK, PAD, BATCH, SEQ
            self.grid = (BATCH // self.n_seq, SEQ_CHUNKS)
        else:
            self.n_seq, self.chunk, self.pad, self.total, self.seq_len = groups[1], DEC_SEQ, 0, DEC_BATCH, DEC_SEQ
            self.grid = (DEC_BATCH // self.n_seq, 1)

    def p_specs(self, width, col):
        cb = col // width
        if not self.prompt:
            rows = self.n_seq * self.chunk
            return [pl.BlockSpec((rows, width), lambda g, c: (ROW_S0 // rows + g, cb))]

        def spec(q):
            def index(g, c):
                seq = g * self.n_seq + q
                return jnp.where(c == 0, ROW_H0 // CHUNK + seq, seq * (SEQ // CHUNK) + c - 1), cb
            return pl.BlockSpec((CHUNK, width), index)
        return [spec(q) for q in range(self.n_seq)]

    def y_spec(self, dim):
        return pl.BlockSpec((self.n_seq, self.chunk, dim), lambda g, c: (g, jnp.maximum(c - 1, 0), 0))

    def y_shape(self, dim):
        return jax.ShapeDtypeStruct((self.total, self.seq_len, dim), BF16)

    def state_spec(self, shape):
        nd = len(shape)
        return pl.BlockSpec((self.n_seq,) + tuple(shape), lambda g, c: (g,) + (0,) * nd)


_MIX_PARAMS = dict(dimension_semantics=("parallel", "arbitrary"), vmem_limit_bytes=VMEM_LIMIT)


def _rwkv_mix(p_all, hist, s0, wts, *, prompt):
    plan = _SeqPlan(prompt, RWKV_GROUPS)
    p_specs = plan.p_specs(RW_COLS, PC_RW)
    const2 = lambda g, c: (0, 0)
    wspecs = [pl.BlockSpec(w.shape, const2) for w in wts]
    shift_spec = plan.state_spec((1, RW_COLS))
    wkv_spec = plan.state_spec((RW_HEADS, HEAD, HEAD))
    return pl.pallas_call(
        functools.partial(_rwkv_kernel, chunk=plan.chunk, pad=plan.pad, n_seq=plan.n_seq, n_p=len(p_specs)),
        grid=plan.grid,
        in_specs=p_specs + [shift_spec, wkv_spec] + wspecs,
        out_specs=[plan.y_spec(RW_DIM), wkv_spec, shift_spec],
        out_shape=[plan.y_shape(RW_DIM),
                   jax.ShapeDtypeStruct((plan.total, RW_HEADS, HEAD, HEAD), F32),
                   jax.ShapeDtypeStruct((plan.total, 1, RW_COLS), F32)],
        scratch_shapes=[pltpu.VMEM((plan.n_seq * plan.chunk, RW_DIM), F32)],
        compiler_params=pltpu.CompilerParams(**_MIX_PARAMS),
        name="rwkv_mix_c%d" % plan.chunk,
    )(*([p_all] * len(p_specs)), hist, s0, *wts)


def _ssd_kernel(*refs, chunk, pad, n_seq, n_p):
    xs_refs, bc_refs, z_refs, dt_refs = (refs[i * n_p:(i + 1) * n_p] for i in range(4))
    (hist_ref, h0_ref, cw_ref, cb_ref, dtb_ref, alog_ref, dskip_ref, nw_ref, exp_ref,
     y_ref, h_ref, tail_ref, xpad_ref, yacc_ref) = refs[4 * n_p:]
    c = pl.program_id(1)
    n_rows = n_seq * chunk
    seqs = range(n_seq)
    rsl = [slice(q * chunk, (q + 1) * chunk) for q in seqs]
    rows = lax.broadcasted_iota(jnp.int32, (n_rows, 1), 0)
    stack = lambda rs: jnp.concatenate([r[...] for r in rs], axis=0)

    @pl.when(c == 0)
    def _():
        xpad_ref[:, 0:CONV_HIST, :] = hist_ref[...]
        h_ref[...] = h0_ref[...]

    u = jnp.concatenate([stack(xs_refs), stack(bc_refs)], axis=1)
    if pad:
        is_pad = jnp.logical_and(c == 0, jnp.bitwise_and(rows, chunk - 1) < pad)
        u = jnp.where(is_pad, 0.0, u)
    convs = []
    for q in seqs:
        u_q = u[rsl[q]]
        xpad_ref[q, CONV_HIST:CONV_HIST + chunk, :] = u_q
        conv = cb_ref[...] + u_q * cw_ref[CONV_W - 1:CONV_W, :]
        for i in range(CONV_W - 1):
            back = CONV_W - 1 - i
            conv = conv + xpad_ref[q, CONV_HIST - back:CONV_HIST - back + chunk, :] * cw_ref[i:i + 1, :]
        hist_next = xpad_ref[q, chunk:chunk + CONV_HIST, :]
        xpad_ref[q, 0:CONV_HIST, :] = hist_next
        tail_ref[q] = hist_next
        convs.append(conv)
    conv = jnp.concatenate(convs, axis=0)
    xbc = conv * _sigmoid(conv)
    xs = xbc[:, 0:SSM_DIM]
    bm = xbc[:, SSM_DIM:SSM_DIM + BC_DIM].astype(BF16)
    cm = xbc[:, SSM_DIM + BC_DIM:CONV_DIM].astype(BF16)

    dt = _softplus(stack(dt_refs) + dtb_ref[...])
    if pad:
        dt = jnp.where(is_pad, 0.0, dt)
    da = dt * (-jnp.exp(alog_ref[...]))
    ci = lax.broadcasted_iota(jnp.int32, (chunk, chunk), 0)
    cj = lax.broadcasted_iota(jnp.int32, (chunk, chunk), 1)
    causal = ci >= cj
    acs = _dot_sel_l(_causal_blocks(n_rows, chunk), da)
    acs_t = acs.T
    acs_end = jnp.concatenate(
        [jnp.broadcast_to(acs[(q + 1) * chunk - 1:(q + 1) * chunk, :], (chunk, DT_COLS)) for q in seqs], axis=0)

    expand = exp_ref[...]
    xdt = xs * _dot_sel_r(dt, expand)
    xdt_b = xdt.astype(BF16)
    xdt_end = (xdt * _dot_sel_r(jnp.exp(acs_end - acs), expand)).astype(BF16)
    e_acs = _dot_sel_r(jnp.exp(acs), expand)
    decay_end = [jnp.exp(acs_t[:, (q + 1) * chunk - 1:(q + 1) * chunk]) for q in seqs]

    gw = SSM_HPG * HEAD
    groups = range(SSM_GROUPS)
    heads = range(SSM_HEADS)
    qg = [(q, g) for q in seqs for g in groups]
    qh = [(q, hd) for q in seqs for hd in heads]
    bm_g = {(q, g): bm[rsl[q], g * SSM_STATE:(g + 1) * SSM_STATE] for q, g in qg}
    cm_g = {(q, g): cm[rsl[q], g * SSM_STATE:(g + 1) * SSM_STATE] for q, g in qg}
    h_old = {(q, hd): h_ref[q, hd] for q, hd in qh}
    cb = {k: _dot_nt(cm_g[k], bm_g[k]) for k in qg}
    y_off = {(q, g): _dot_nt(cm_g[q, g], jnp.concatenate(
        [h_old[q, g * SSM_HPG + hh].astype(BF16) for hh in range(SSM_HPG)], axis=0)) for q, g in qg}
    st = {(q, g): _dot_tn(xdt_end[rsl[q], g * gw:(g + 1) * gw], bm_g[q, g]) for q, g in qg}
    for q, hd in qh:
        sl = slice(hd * HEAD, (hd + 1) * HEAD)
        seg = acs[rsl[q], hd:hd + 1] - acs_t[hd:hd + 1, rsl[q]]
        lmat = jnp.exp(jnp.where(causal, seg, -jnp.inf))
        yacc_ref[rsl[q], sl] = _dot((cb[q, hd // SSM_HPG] * lmat).astype(BF16), xdt_b[rsl[q], sl])
    for q, hd in qh:
        g, hh = divmod(hd, SSM_HPG)
        h_ref[q, hd] = h_old[q, hd] * decay_end[q][hd:hd + 1, :] + st[q, g][hh * HEAD:(hh + 1) * HEAD, :]

    y_off_all = jnp.concatenate(
        [jnp.concatenate([y_off[q, g] for g in groups], axis=1) for q in seqs], axis=0)
    y = yacc_ref[...] + y_off_all * e_acs + xs * dskip_ref[...]
    zz = stack(z_refs)
    y = y * (zz * _sigmoid(zz))
    parts = []
    for g in groups:
        yg = y[:, g * gw:(g + 1) * gw]
        ms = jnp.mean(yg * yg, axis=-1, keepdims=True)
        parts.append(yg * lax.rsqrt(ms + RMS_EPS))
    out = (jnp.concatenate(parts, axis=1) * nw_ref[...]).astype(y_ref.dtype)
    for q in seqs:
        y_ref[q] = out[rsl[q]]


def _ssd_mix(p_all, hist, h0, wts, *, prompt):
    plan = _SeqPlan(prompt, SSD_GROUPS)
    p_specs = (plan.p_specs(SSM_DIM, PC_XS) + plan.p_specs(2 * BC_DIM, PC_BC)
               + plan.p_specs(SSM_DIM, PC_Z) + plan.p_specs(DT_COLS, PC_DT))
    const2 = lambda g, c: (0, 0)
    wspecs = [pl.BlockSpec(w.shape, const2) for w in wts]
    conv_spec = plan.state_spec((CONV_HIST, CONV_DIM))
    ssm_spec = plan.state_spec((SSM_HEADS, HEAD, SSM_STATE))
    return pl.pallas_call(
        functools.partial(_ssd_kernel, chunk=plan.chunk, pad=plan.pad, n_seq=plan.n_seq, n_p=len(p_specs) // 4),
        grid=plan.grid,
        in_specs=p_specs + [conv_spec, ssm_spec] + wspecs,
        out_specs=[plan.y_spec(SSM_DIM), ssm_spec, conv_spec],
        out_shape=[plan.y_shape(SSM_DIM),
                   jax.ShapeDtypeStruct((plan.total, SSM_HEADS, HEAD, SSM_STATE), F32),
                   jax.ShapeDtypeStruct((plan.total, CONV_HIST, CONV_DIM), F32)],
        scratch_shapes=[pltpu.VMEM((plan.n_seq, CONV_HIST + plan.chunk, CONV_DIM), F32),
                        pltpu.VMEM((plan.n_seq * plan.chunk, SSM_DIM), F32)],
        compiler_params=pltpu.CompilerParams(**_MIX_PARAMS),
        name="ssd_mix_c%d" % plan.chunk,
    )(*([p_all] * len(p_specs)), hist, h0, *wts)


MERGE_F_TILES = F_ROWS // MERGE_TM
assert S_ROWS == MERGE_TM


def _merge_kernel(x_ref, yrw_f_ref, yrw_s_ref, yssm_f_ref, yssm_s_ref, ga_ref, gb_ref, bga_ref, bgb_ref,
                  wrw_ref, wssm_ref, wout_ref, g_ref, b_ref, o_ref):
    is_frame = pl.program_id(0) < MERGE_F_TILES
    y_rw = jnp.where(is_frame, yrw_f_ref[...], yrw_s_ref[...])
    y_ssm = jnp.where(is_frame, yssm_f_ref[...], yssm_s_ref[...])
    merged = (_sigmoid(ga_ref[...] + bga_ref[...]) * _dot(y_rw, wrw_ref[...])
              + _sigmoid(gb_ref[...] + bgb_ref[...]) * _dot(y_ssm, wssm_ref[...]))
    s = ALPHA * x_ref[...] + _dot(merged.astype(BF16), wout_ref[...])
    o_ref[...] = _layer_norm(s, g_ref[...], b_ref[...])


def _merge(x, y_rw_f, y_rw_s, y_ssm_f, y_ssm_s, p_all, b_gate, w_rw_out, w_ssm_out, w_out, ln_g, ln_b):
    m, d = x.shape
    tm = MERGE_TM
    row = lambda i: (i, 0)
    frame_row = lambda i: (jnp.minimum(i, MERGE_F_TILES - 1), 0)
    const = lambda i: (0, 0)
    resident = lambda w: pl.BlockSpec(w.shape, const, pipeline_mode=pl.Buffered(1))
    return pl.pallas_call(
        _merge_kernel,
        grid=(m // tm,),
        in_specs=[
            pl.BlockSpec((tm, d), row),
            pl.BlockSpec((tm, RW_DIM), frame_row),
            pl.BlockSpec((tm, RW_DIM), const),
            pl.BlockSpec((tm, SSM_DIM), frame_row),
            pl.BlockSpec((tm, SSM_DIM), const),
            pl.BlockSpec((tm, d), lambda i: (i, PC_GATE // d)),
            pl.BlockSpec((tm, d), lambda i: (i, PC_GATE // d + 1)),
            pl.BlockSpec((1, d), lambda i: (0, 0)),
            pl.BlockSpec((1, d), lambda i: (0, 1)),
            resident(w_rw_out), resident(w_ssm_out), resident(w_out),
            pl.BlockSpec((1, d), const),
            pl.BlockSpec((1, d), const),
        ],
        out_specs=pl.BlockSpec((tm, d), row),
        out_shape=jax.ShapeDtypeStruct((m, d), F32),
        compiler_params=pltpu.CompilerParams(
            dimension_semantics=("arbitrary",), vmem_limit_bytes=VMEM_LIMIT),
        name="merge_ln",
    )(x, y_rw_f, y_rw_s, y_ssm_f, y_ssm_s, p_all, p_all, b_gate, b_gate, w_rw_out, w_ssm_out, w_out,
      ln_g, ln_b)


def _pad_cols(x, width):
    return jnp.pad(x, [(0, 0)] * (x.ndim - 1) + [(0, width - x.shape[-1])])


def _rw_cols(x):
    o_wd, o_ad, o_gd = 3 * RW_DIM, 3 * RW_DIM + RW_DECAY_LORA, 3 * RW_DIM + RW_DECAY_LORA + RW_AAA_LORA
    return jnp.concatenate([
        x[..., :o_wd],
        _pad_cols(x[..., o_wd:o_ad], LORA_PAD),
        _pad_cols(x[..., o_ad:o_gd], LORA_PAD),
        x[..., o_gd:],
    ], axis=-1)


def _rw_cols_inv(x):
    return jnp.concatenate([
        x[..., :RW_OFF_WD],
        x[..., RW_OFF_WD:RW_OFF_WD + RW_DECAY_LORA],
        x[..., RW_OFF_AD:RW_OFF_AD + RW_AAA_LORA],
        x[..., RW_OFF_GD:],
    ], axis=-1)


def _block_ones(n, blk):
    i = jnp.arange(n) // blk
    return (i[:, None] == i[None, :]).astype(BF16)


def kernel(x_prompt, x_sample, state_rwkv_shift, state_wkv, state_conv, state_ssm, meta_tokens, ffn1_gu, ffn1_dn, ln1_g, ln1_b, w_in, b_gate, rw_mu, rw_w0, rw_w2, rw_a0, rw_a2, rw_g2, rw_kk, rw_ka, rw_rk, rw_gn_w, rw_gn_b, conv_w, conv_b, dt_bias, a_log, d_skip, ssm_norm_w, w_rw_out, w_ssm_out, w_out, ln2_g, ln2_b, ffn2_gu, ffn2_dn, ln3_g, ln3_b):
    lyr = 0
    row = lambda t: t[lyr].reshape(1, -1).astype(F32)

    head_rows = jnp.concatenate([jnp.zeros((PAD, D_MODEL), F32), meta_tokens.astype(F32)], axis=0)
    x_extra = jnp.concatenate([x_sample.reshape(S_ROWS, D_MODEL)] + [head_rows] * BATCH, axis=0)
    x1, x1b, w = _ffn_ln_in(x_prompt.reshape(F_ROWS, D_MODEL), x_extra, ffn1_gu[lyr].astype(BF16),
                            ffn1_dn[lyr].astype(BF16), row(ln1_g), row(ln1_b), [(w_in[lyr], CAST_ROWS)])

    o_z = RW_SHIFT_COLS
    o_xbc = o_z + SSM_DIM
    o_dt = o_xbc + CONV_DIM
    o_gate = o_dt + SSM_HEADS
    w_all = jnp.concatenate([
        _rw_cols(w[:, :o_z]), _pad_cols(w[:, o_dt:o_gate], DT_SLOT), w[:, o_gate:],
        w[:, o_z:o_xbc], w[:, o_xbc:o_dt]], axis=1)
    p_all, gu2, dn2, w_rw_o, w_ssm_o, w_o = _project(x1b, w_all, [
        [(ffn2_gu[lyr], CAST_ROWS), (ffn2_dn[lyr], 4 * CAST_ROWS)],
        [(w_rw_out[lyr], 4 * CAST_ROWS), (w_ssm_out[lyr], 4 * CAST_ROWS), (w_out[lyr], 4 * CAST_ROWS)]])

    pad_rows = lambda t, n: jnp.pad(t, ((0, n - t.shape[0]), (0, 0)))
    rw_wts = [
        _rw_cols(rw_mu[lyr]).reshape(1, RW_COLS), row(rw_w0),
        pad_rows(rw_w2[lyr], LORA_PAD).astype(BF16), row(rw_a0),
        pad_rows(rw_a2[lyr], LORA_PAD).astype(BF16), rw_g2[lyr].astype(BF16),
        row(rw_kk), row(rw_ka), row(rw_rk), row(rw_gn_w), row(rw_gn_b),
        _block_ones(2 * LANES, HEAD),
    ]
    y_rw_s, wkv_s, shift_s = _rwkv_mix(
        p_all, _rw_cols(state_rwkv_shift[lyr]), state_wkv[lyr], rw_wts, prompt=False)
    y_rw_f, wkv_p, shift_p = _rwkv_mix(
        p_all, jnp.zeros((BATCH, 1, RW_COLS), F32), jnp.zeros((BATCH, RW_HEADS, HEAD, HEAD), F32), rw_wts,
        prompt=True)

    head_of_lane = jnp.arange(SSM_DIM) // HEAD
    expand = (jnp.arange(DT_COLS)[:, None] == head_of_lane[None, :]).astype(BF16)
    ssd_wts = [
        conv_w[lyr], row(conv_b), _pad_cols(row(dt_bias), DT_COLS), _pad_cols(row(a_log), DT_COLS),
        jnp.repeat(d_skip[lyr], HEAD).reshape(1, SSM_DIM), row(ssm_norm_w),
        expand,
    ]
    hist_rows = lambda t: jnp.pad(t, ((0, 0), (CONV_HIST - (CONV_W - 1), 0), (0, 0)))
    y_ssm_s, ssm_s, conv_s = _ssd_mix(
        p_all, hist_rows(state_conv[lyr]), state_ssm[lyr], ssd_wts, prompt=False)
    y_ssm_f, ssm_p, conv_p = _ssd_mix(
        p_all, jnp.zeros((BATCH, CONV_HIST, CONV_DIM), F32),
        jnp.zeros((BATCH, SSM_HEADS, HEAD, SSM_STATE), F32), ssd_wts, prompt=True)

    flat = lambda t: t.reshape(-1, t.shape[-1])
    x2 = _merge(x1, flat(y_rw_f), flat(y_rw_s), flat(y_ssm_f), flat(y_ssm_s), p_all, row(b_gate),
                w_rw_o, w_ssm_o, w_o, row(ln2_g), row(ln2_b))
    y_frames, y_extra = _ffn_ln_out(x2, gu2, dn2, row(ln3_g), row(ln3_b))

    y_prompt = y_frames.reshape(BATCH, SEQ, D_MODEL)
    y_sample = y_extra[:S_ROWS].reshape(DEC_BATCH, DEC_SEQ, D_MODEL)
    conv_of = lambda t: t[:, CONV_HIST - (CONV_W - 1):][None]
    return (y_prompt, y_sample,
            _rw_cols_inv(shift_p)[None], wkv_p[None], conv_of(conv_p), ssm_p[None],
            _rw_cols_inv(shift_s)[None], wkv_s[None], conv_of(conv_s), ssm_s[None])
```

```python
import functools
import math

import jax
import jax.numpy as jnp
from jax import lax
from jax.experimental import pallas as pl
from jax.experimental.pallas import tpu as pltpu

F32 = jnp.float32
BF16 = jnp.bfloat16

D_MODEL = 2048
BATCH = 4
SEQ = 2048
DEC_BATCH = 16
DEC_SEQ = 16
CHUNK = 64
N_META = 16
PAD = (-N_META) % CHUNK
HEAD_ROWS = PAD + N_META
SEQ_CHUNKS = (HEAD_ROWS + SEQ) // CHUNK
F_ROWS = BATCH * SEQ
S_ROWS = DEC_BATCH * DEC_SEQ
ROW_S0 = F_ROWS
ROW_H0 = F_ROWS + S_ROWS
X_ROWS = S_ROWS + BATCH * HEAD_ROWS
M_ROWS = F_ROWS + X_ROWS

HEAD = 64
RW_DIM = D_MODEL // 2
RW_HEADS = RW_DIM // HEAD
RW_DECAY_LORA = 96
RW_AAA_LORA = 96
RW_GATE_LORA = 256
RW_SHIFT_COLS = 3 * RW_DIM + RW_DECAY_LORA + RW_AAA_LORA + RW_GATE_LORA
RW_GN_EPS = 64e-5
SSM_DIM = D_MODEL
SSM_HEADS = SSM_DIM // HEAD
SSM_GROUPS = 4
SSM_HPG = SSM_HEADS // SSM_GROUPS
SSM_STATE = 128
CONV_W = 4
BC_DIM = SSM_GROUPS * SSM_STATE
CONV_DIM = SSM_DIM + 2 * BC_DIM
RMS_EPS = 1e-5
D_FF = 5632
LN_EPS = 1e-5
DEPTH = 1
ALPHA = (2 * DEPTH) ** 0.25

LANES = 128
SUBLANES = 8
VMEM_LIMIT = 56 * 1024 * 1024

LORA_PAD = LANES
RW_OFF_WD = 3 * RW_DIM
RW_OFF_AD = RW_OFF_WD + LORA_PAD
RW_OFF_GD = RW_OFF_AD + LORA_PAD
RW_COLS = RW_OFF_GD + RW_GATE_LORA
DT_COLS = LANES
CONV_HIST = SUBLANES

MIX_CAST_ROWS = 128
FFN_TM = 512
FFN_TF = 512
MM_TM = 512
MM_TN = 1024
MERGE_TM = 256

DT_SLOT = 512
PC_RW = 0
PC_DT = PC_RW + RW_COLS
PC_GATE = PC_DT + DT_SLOT
PC_Z = PC_GATE + 2 * D_MODEL
PC_XS = PC_Z + SSM_DIM
PC_BC = PC_XS + SSM_DIM
P_COLS = PC_BC + 2 * BC_DIM
assert PC_DT % DT_COLS == 0 and PC_GATE % D_MODEL == 0 and PC_Z % SSM_DIM == 0
assert PC_XS % SSM_DIM == 0 and PC_BC % (2 * BC_DIM) == 0 and P_COLS % MM_TN == 0
assert F_ROWS % FFN_TM == 0 and X_ROWS == FFN_TM and M_ROWS % MM_TM == 0 and ROW_H0 % MERGE_TM == 0


def _dot(a, b):
    return jnp.dot(a, b, preferred_element_type=F32)


def _dot_nt(a, b):
    return lax.dot_general(a, b, (((1,), (1,)), ((), ())), preferred_element_type=F32)


def _dot_tn(a, b):
    return lax.dot_general(a, b, (((0,), (0,)), ((), ())), preferred_element_type=F32)


def _split2(x):
    hi = x.astype(BF16)
    lo = (x - hi.astype(F32)).astype(BF16)
    return hi, lo


def _dot_sel_r(x, sel):
    hi, lo = _split2(x)
    return _dot(hi, sel) + _dot(lo, sel)


def _dot_sel_l(sel, x):
    hi, lo = _split2(x)
    return _dot(sel, hi) + _dot(sel, lo)


def _causal_blocks(n, blk):
    i = lax.broadcasted_iota(jnp.int32, (n, n), 0)
    j = lax.broadcasted_iota(jnp.int32, (n, n), 1)
    keep = jnp.logical_and(i >= j, jnp.bitwise_xor(i, j) < blk)
    return jnp.where(keep, 1.0, 0.0).astype(BF16)


def _sigmoid(x):
    return jax.nn.sigmoid(x)


def _softplus(x):
    return jnp.maximum(x, 0.0) + jnp.log1p(jnp.exp(-jnp.abs(x)))


def _layer_norm(s, g, b):
    mu = jnp.mean(s, axis=-1, keepdims=True)
    d = s - mu
    var = jnp.mean(d * d, axis=-1, keepdims=True)
    return d * lax.rsqrt(var + LN_EPS) * g + b


class _CastSlot:
    def __init__(self, w, rows, t0):
        assert w.ndim == 2 and w.shape[0] % rows == 0
        self.w, self.rows, self.t0, self.n = w, rows, t0, w.shape[0] // rows

    def spec(self, step_of):
        def index(*ids):
            return jnp.clip(step_of(*ids) - self.t0, 0, self.n - 1), 0
        return pl.BlockSpec((self.rows, self.w.shape[1]), index)

    def out_shape(self):
        return jax.ShapeDtypeStruct(self.w.shape, BF16)


def _run_cast_slots(t, slots, in_refs, out_refs):
    for slot, i_ref, o_ref in zip(slots, in_refs, out_refs, strict=True):
        @pl.when(jnp.logical_and(t >= slot.t0, t < slot.t0 + slot.n))
        def _(i_ref=i_ref, o_ref=o_ref):
            o_ref[...] = i_ref[...].astype(BF16)


def _split_refs(refs, *counts):
    groups, at = [], 0
    for n in counts:
        groups.append(refs[at:at + n])
        at += n
    return (*groups, refs[at:])


def _chain_slots(weights_rows, t0=0):
    slots = []
    for w, rows in weights_rows:
        slots.append(_CastSlot(w, rows, t0))
        t0 += slots[-1].n
    return slots, t0


N_MAIN_TILES = F_ROWS // FFN_TM


def _ffn_step(xb_ref, wg_ref, wu_ref, wd_ref, acc_ref):
    xb = xb_ref[...]
    gate = _dot(xb, wg_ref[...])
    up = _dot(xb, wu_ref[...])
    h = (gate * _sigmoid(gate) * up).astype(BF16)
    acc_ref[...] += _dot(h, wd_ref[...])


def _ffn_in_kernel(xm_ref, xe_ref, wg_ref, wu_ref, wd_ref, g_ref, b_ref, *rest, slots):
    n = len(slots)
    cast_in, (o_ref, ob_ref), cast_out, (xb_ref, acc_ref) = rest[:n], rest[n:n + 2], rest[n + 2:2 * n + 2], rest[2 * n + 2:]
    i, f = pl.program_id(0), pl.program_id(1)
    _run_cast_slots(i * pl.num_programs(1) + f, slots, cast_in, cast_out)
    load_x = lambda: jnp.where(i < N_MAIN_TILES, xm_ref[...], xe_ref[...])

    @pl.when(f == 0)
    def _():
        xb_ref[...] = load_x().astype(BF16)
        acc_ref[...] = jnp.zeros_like(acc_ref)

    _ffn_step(xb_ref, wg_ref, wu_ref, wd_ref, acc_ref)

    @pl.when(f == pl.num_programs(1) - 1)
    def _():
        y = _layer_norm(ALPHA * load_x() + 0.5 * acc_ref[...], g_ref[...], b_ref[...])
        o_ref[...] = y
        ob_ref[...] = y.astype(BF16)


def _ffn_out_kernel(x_ref, wg_ref, wu_ref, wd_ref, g_ref, b_ref, om_ref, oe_ref, xb_ref, acc_ref):
    i, f = pl.program_id(0), pl.program_id(1)

    @pl.when(f == 0)
    def _():
        xb_ref[...] = x_ref[...].astype(BF16)
        acc_ref[...] = jnp.zeros_like(acc_ref)

    _ffn_step(xb_ref, wg_ref, wu_ref, wd_ref, acc_ref)

    @pl.when(f == pl.num_programs(1) - 1)
    def _():
        y = _layer_norm(ALPHA * x_ref[...] + 0.5 * acc_ref[...], g_ref[...], b_ref[...])

        @pl.when(i < N_MAIN_TILES)
        def _():
            om_ref[...] = y

        @pl.when(i == N_MAIN_TILES)
        def _():
            oe_ref[...] = y


def _ffn_specs(d):
    nf = D_FF // FFN_TF
    return nf, [
        pl.BlockSpec((d, FFN_TF), lambda i, f: (0, f)),
        pl.BlockSpec((d, FFN_TF), lambda i, f: (0, f + nf)),
        pl.BlockSpec((FFN_TF, d), lambda i, f: (f, 0)),
        pl.BlockSpec((1, d), lambda i, f: (0, 0)),
        pl.BlockSpec((1, d), lambda i, f: (0, 0)),
    ]


_FFN_PARAMS = dict(dimension_semantics=("arbitrary", "arbitrary"), vmem_limit_bytes=VMEM_LIMIT)
_main_tile = lambda i, f: (jnp.minimum(i, N_MAIN_TILES - 1), 0)
_last_tile = lambda i, f: (0, 0)
_any_tile = lambda i, f: (i, 0)


def _ffn_ln_in(x_main, x_extra, w_gu, w_dn, ln_g, ln_b, cast_weights):
    d = x_main.shape[1]
    nf, wspecs = _ffn_specs(d)
    tile = (FFN_TM, d)
    grid = (M_ROWS // FFN_TM, nf)
    slots, t_end = _chain_slots(cast_weights)
    assert t_end <= grid[0] * grid[1]
    cast_specs = [s.spec(lambda i, f: i * nf + f) for s in slots]
    return pl.pallas_call(
        functools.partial(_ffn_in_kernel, slots=slots),
        grid=grid,
        in_specs=[pl.BlockSpec(tile, _main_tile),
                  pl.BlockSpec(tile, _last_tile, pipeline_mode=pl.Buffered(1))] + wspecs + cast_specs,
        out_specs=[pl.BlockSpec(tile, _any_tile), pl.BlockSpec(tile, _any_tile)] + cast_specs,
        out_shape=[jax.ShapeDtypeStruct((M_ROWS, d), F32), jax.ShapeDtypeStruct((M_ROWS, d), BF16)]
        + [s.out_shape() for s in slots],
        scratch_shapes=[pltpu.VMEM(tile, BF16), pltpu.VMEM(tile, F32)],
        compiler_params=pltpu.CompilerParams(**_FFN_PARAMS),
        name="ffn_ln_in",
    )(x_main, x_extra, w_gu, w_gu, w_dn, ln_g, ln_b, *[s.w for s in slots])


def _ffn_ln_out(x, w_gu, w_dn, ln_g, ln_b):
    d = x.shape[1]
    nf, wspecs = _ffn_specs(d)
    tile = (FFN_TM, d)
    return pl.pallas_call(
        _ffn_out_kernel,
        grid=(M_ROWS // FFN_TM, nf),
        in_specs=[pl.BlockSpec(tile, _any_tile)] + wspecs,
        out_specs=[pl.BlockSpec(tile, _main_tile), pl.BlockSpec(tile, _last_tile)],
        out_shape=[jax.ShapeDtypeStruct((F_ROWS, d), F32), jax.ShapeDtypeStruct((X_ROWS, d), F32)],
        scratch_shapes=[pltpu.VMEM(tile, BF16), pltpu.VMEM(tile, F32)],
        compiler_params=pltpu.CompilerParams(**_FFN_PARAMS),
        name="ffn_ln_out",
    )(x, w_gu, w_gu, w_dn, ln_g, ln_b)


def _mm_kernel(x_ref, w_ref, *rest, slots):
    n = len(slots)
    cast_in, o_ref, cast_out = rest[:n], rest[n], rest[n + 1:]
    _run_cast_slots(pl.program_id(0) * pl.num_programs(1) + pl.program_id(1), slots, cast_in, cast_out)
    o_ref[...] = _dot(x_ref[...], w_ref[...])


def _project(x, w, cast_chains):
    m, k = x.shape
    n = w.shape[1]
    grid = (n // MM_TN, m // MM_TM)
    slots = []
    for chain in cast_chains:
        chain_slots, t_end = _chain_slots(chain)
        assert t_end <= grid[0] * grid[1]
        slots += chain_slots
    cast_specs = [s.spec(lambda j, i: j * grid[1] + i) for s in slots]
    return pl.pallas_call(
        functools.partial(_mm_kernel, slots=slots),
        grid=grid,
        in_specs=[pl.BlockSpec((MM_TM, k), lambda j, i: (i, 0)),
                  pl.BlockSpec((k, MM_TN), lambda j, i: (0, j))] + cast_specs,
        out_specs=[pl.BlockSpec((MM_TM, MM_TN), lambda j, i: (i, j))] + cast_specs,
        out_shape=[jax.ShapeDtypeStruct((m, n), F32)] + [s.out_shape() for s in slots],
        compiler_params=pltpu.CompilerParams(
            dimension_semantics=("arbitrary", "arbitrary"), vmem_limit_bytes=VMEM_LIMIT),
        name="proj_in",
    )(x, w, *[s.w for s in slots])


def _rwkv_kernel(*refs, chunk, pad, n_seq, n_p, slots):
    p_refs, fixed, cast_in, outs, cast_out, (yacc_ref,) = _split_refs(refs, n_p, 14, len(slots), 3, len(slots))
    (hist_ref, s0_ref, mu_ref, w0_ref, w2_ref, a0_ref, a2_ref, g2_ref, kk_ref, ka_ref, rk_ref,
     gnw_ref, gnb_ref, seg_ref) = fixed
    y_ref, s_ref, shift_ref = outs
    c = pl.program_id(1)
    _run_cast_slots(pl.program_id(0) * pl.num_programs(1) + c, slots, cast_in, cast_out)
    n_rows = n_seq * chunk
    seqs = range(n_seq)
    rsl = [slice(q * chunk, (q + 1) * chunk) for q in seqs]
    rows = lax.broadcasted_iota(jnp.int32, (n_rows, 1), 0)

    @pl.when(c == 0)
    def _():
        shift_ref[...] = hist_ref[...]
        s_ref[...] = s0_ref[...]

    p = jnp.concatenate([r[...] for r in p_refs], axis=0)
    if pad:
        p = jnp.where(jnp.logical_and(c == 0, jnp.bitwise_and(rows, chunk - 1) < pad), 0.0, p)
    prev = pltpu.roll(p, 1, 0)
    for q in seqs:
        prev = jnp.where(rows == q * chunk, shift_ref[q], prev)
    for q in seqs:
        shift_ref[q] = p[(q + 1) * chunk - 1:(q + 1) * chunk, :]
    ps = p + (prev - p) * mu_ref[...]

    r = ps[:, 0:RW_DIM]
    k = ps[:, RW_DIM:2 * RW_DIM]
    v = ps[:, 2 * RW_DIM:3 * RW_DIM]
    wd = ps[:, RW_OFF_WD:RW_OFF_WD + LORA_PAD]
    ad = ps[:, RW_OFF_AD:RW_OFF_AD + LORA_PAD]
    gd = ps[:, RW_OFF_GD:RW_OFF_GD + RW_GATE_LORA]

    lw = -math.exp(-0.5) * _sigmoid(w0_ref[...] + _dot(jnp.tanh(wd).astype(BF16), w2_ref[...]))
    a = _sigmoid(a0_ref[...] + _dot(ad.astype(BF16), a2_ref[...]))
    g = _dot(_sigmoid(gd).astype(BF16), g2_ref[...])

    seg = seg_ref[...]

    def head_sum(x):
        w = seg.shape[0]
        return jnp.concatenate(
            [_dot_sel_r(x[:, j * w:(j + 1) * w], seg) for j in range(RW_DIM // w)], axis=1)

    kk = k * kk_ref[...]
    kk = kk * lax.rsqrt(jnp.maximum(head_sum(kk * kk), 1e-24))
    k = k * (1.0 + (a - 1.0) * ka_ref[...])
    b_neg = -(kk * a)

    cum = _dot_sel_l(_causal_blocks(n_rows, chunk), lw)
    cum_last = [cum[(q + 1) * chunk - 1:(q + 1) * chunk, :] for q in seqs]
    cum_end = jnp.concatenate([jnp.broadcast_to(x, (chunk, RW_DIM)) for x in cum_last], axis=0)
    e_neg = jnp.exp(-cum)
    kt = (kk * jnp.exp(cum - lw)).astype(BF16)
    rt = (r * jnp.exp(cum)).astype(BF16)
    kd = (k * e_neg).astype(BF16)
    bd_neg = (b_neg * e_neg).astype(BF16)
    e_end = jnp.exp(cum_end - cum)
    k_end = (k * e_end).astype(BF16)
    b_end_neg = (b_neg * e_end).astype(BF16)
    p_end = [jnp.exp(x) for x in cum_last]
    vb = v.astype(BF16)

    ri = lax.broadcasted_iota(jnp.int32, (2 * chunk, 2 * chunk), 0)
    rj = lax.broadcasted_iota(jnp.int32, (2 * chunk, 2 * chunk), 1)
    bi = jnp.where(ri >= chunk, ri - chunk + 1, ri)
    bj = jnp.where(rj >= chunk, rj - chunk, rj)
    keep = bi > bj
    ti = lax.broadcasted_iota(jnp.int32, (chunk, 2 * chunk), 0)
    tj = lax.broadcasted_iota(jnp.int32, (chunk, 2 * chunk), 1)
    hi_lane = tj >= chunk
    eye_hi = (tj == ti + chunk).astype(F32)
    zeros_v = jnp.zeros((chunk, HEAD), BF16)

    pairs = [(q, h) for q in seqs for h in range(RW_HEADS)]
    sls = [(rsl[q], slice(h * HEAD, (h + 1) * HEAD)) for q, h in pairs]
    s_old = [s_ref[q, h] for q, h in pairs]
    s_b = [s.astype(BF16) for s in s_old]
    amats = [jnp.where(keep, _dot_nt(jnp.concatenate([kt[sl], rt[sl]], axis=0),
                                     jnp.concatenate([bd_neg[sl], kd[sl]], axis=0)), 0.0)
             for sl in sls]
    top = [m[:chunk] for m in amats]
    bot = [m[chunk:].astype(BF16) for m in amats]
    top_b = [x.astype(BF16) for x in top]
    tq = [_dot(xb[:, :chunk], jnp.where(hi_lane, eye_hi, x).astype(BF16)) + eye_hi
          for x, xb in zip(top, top_b)]
    n = 2
    while n < chunk:
        tq = [jnp.where(hi_lane, x, 0.0) + _dot(x[:, :chunk].astype(BF16), x.astype(BF16)) for x in tq]
        n *= 2
    t_b = [x[:, chunk:].astype(BF16) for x in tq]
    v_h = [vb[sl] for sl in sls]
    rhs = [_dot_nt(kt[sl], sb) + _dot(xb, jnp.concatenate([zeros_v, vh], axis=0))
           for sl, sb, xb, vh in zip(sls, s_b, top_b, v_h)]
    ub = [_dot(t, x.astype(BF16)).astype(BF16) for t, x in zip(t_b, rhs)]
    uv = [jnp.concatenate([u, vh], axis=0) for u, vh in zip(ub, v_h)]
    for i, (q, h) in enumerate(pairs):
        sl = sls[i]
        yacc_ref[sl] = _dot_nt(rt[sl], s_b[i]) + _dot(bot[i], uv[i])
        s_ref[q, h] = s_old[i] * p_end[q][:, sl[1]] + _dot_tn(
            uv[i], jnp.concatenate([b_end_neg[sl], k_end[sl]], axis=0))

    y = yacc_ref[...]
    inv_n = 1.0 / HEAD
    mean = head_sum(y) * inv_n
    yc = y - mean
    var = head_sum(yc * yc) * inv_n
    yn = yc * lax.rsqrt(var + RW_GN_EPS) * gnw_ref[...] + gnb_ref[...]
    bonus = head_sum(r * k * rk_ref[...]) * v
    out = ((yn + bonus) * g).astype(y_ref.dtype)
    for q in seqs:
        y_ref[q] = out[rsl[q]]


RWKV_GROUPS = (2, 8)
SSD_GROUPS = (1, 1)


class _SeqPlan:
    def __init__(self, prompt, groups):
        self.prompt = prompt
        if prompt:
            self.n_seq, self.chunk, self.pad, self.total, self.seq_len = groups[0], CHUNK, PAD, BATCH, SEQ
            self.grid = (BATCH // self.n_seq, SEQ_CHUNKS)
        else:
            self.n_seq, self.chunk, self.pad, self.total, self.seq_len = groups[1], DEC_SEQ, 0, DEC_BATCH, DEC_SEQ
            self.grid = (DEC_BATCH // self.n_seq, 1)

    def p_specs(self, width, col):
        cb = col // width
        if not self.prompt:
            rows = self.n_seq * self.chunk
            return [pl.BlockSpec((rows, width), lambda g, c: (ROW_S0 // rows + g, cb))]

        def spec(q):
            def index(g, c):
                seq = g * self.n_seq + q
                return jnp.where(c == 0, ROW_H0 // CHUNK + seq, seq * (SEQ // CHUNK) + c - 1), cb
            return pl.BlockSpec((CHUNK, width), index)
        return [spec(q) for q in range(self.n_seq)]

    def y_spec(self, dim):
        return pl.BlockSpec((self.n_seq, self.chunk, dim), lambda g, c: (g, jnp.maximum(c - 1, 0), 0))

    def y_shape(self, dim):
        return jax.ShapeDtypeStruct((self.total, self.seq_len, dim), BF16)

    def state_spec(self, shape):
        nd = len(shape)
        return pl.BlockSpec((self.n_seq,) + tuple(shape), lambda g, c: (g,) + (0,) * nd)

    def cast_slots(self, cast_weights):
        slots, t_end = _chain_slots(cast_weights)
        assert t_end <= self.grid[0] * self.grid[1]
        return slots, [s.spec(lambda g, c: g * self.grid[1] + c) for s in slots]


_MIX_PARAMS = dict(dimension_semantics=("arbitrary", "arbitrary"), vmem_limit_bytes=VMEM_LIMIT)


def _rwkv_mix(p_all, hist, s0, wts, *, prompt, cast_weights=()):
    plan = _SeqPlan(prompt, RWKV_GROUPS)
    slots, cast_specs = plan.cast_slots(cast_weights)
    p_specs = plan.p_specs(RW_COLS, PC_RW)
    const2 = lambda g, c: (0, 0)
    wspecs = [pl.BlockSpec(w.shape, const2) for w in wts]
    shift_spec = plan.state_spec((1, RW_COLS))
    wkv_spec = plan.state_spec((RW_HEADS, HEAD, HEAD))
    return pl.pallas_call(
        functools.partial(_rwkv_kernel, chunk=plan.chunk, pad=plan.pad, n_seq=plan.n_seq, n_p=len(p_specs),
                          slots=slots),
        grid=plan.grid,
        in_specs=p_specs + [shift_spec, wkv_spec] + wspecs + cast_specs,
        out_specs=[plan.y_spec(RW_DIM), wkv_spec, shift_spec] + cast_specs,
        out_shape=[plan.y_shape(RW_DIM),
                   jax.ShapeDtypeStruct((plan.total, RW_HEADS, HEAD, HEAD), F32),
                   jax.ShapeDtypeStruct((plan.total, 1, RW_COLS), F32)] + [s.out_shape() for s in slots],
        scratch_shapes=[pltpu.VMEM((plan.n_seq * plan.chunk, RW_DIM), F32)],
        compiler_params=pltpu.CompilerParams(**_MIX_PARAMS),
        name="rwkv_mix_c%d" % plan.chunk,
    )(*([p_all] * len(p_specs)), hist, s0, *wts, *[s.w for s in slots])


def _ssd_kernel(*refs, chunk, pad, n_seq, n_p, slots):
    (xs_refs, bc_refs, z_refs, dt_refs, fixed, cast_in, outs, cast_out, (xpad_ref, yacc_ref)) = _split_refs(
        refs, n_p, n_p, n_p, n_p, 9, len(slots), 3, len(slots))
    hist_ref, h0_ref, cw_ref, cb_ref, dtb_ref, alog_ref, dskip_ref, nw_ref, exp_ref = fixed
    y_ref, h_ref, tail_ref = outs
    c = pl.program_id(1)
    _run_cast_slots(pl.program_id(0) * pl.num_programs(1) + c, slots, cast_in, cast_out)
    n_rows = n_seq * chunk
    seqs = range(n_seq)
    rsl = [slice(q * chunk, (q + 1) * chunk) for q in seqs]
    rows = lax.broadcasted_iota(jnp.int32, (n_rows, 1), 0)
    stack = lambda rs: jnp.concatenate([r[...] for r in rs], axis=0)

    @pl.when(c == 0)
    def _():
        xpad_ref[:, 0:CONV_HIST, :] = hist_ref[...]
        h_ref[...] = h0_ref[...]

    u = jnp.concatenate([stack(xs_refs), stack(bc_refs)], axis=1)
    if pad:
        is_pad = jnp.logical_and(c == 0, jnp.bitwise_and(rows, chunk - 1) < pad)
        u = jnp.where(is_pad, 0.0, u)
    convs = []
    for q in seqs:
        u_q = u[rsl[q]]
        xpad_ref[q, CONV_HIST:CONV_HIST + chunk, :] = u_q
        conv = cb_ref[...] + u_q * cw_ref[CONV_W - 1:CONV_W, :]
        for i in range(CONV_W - 1):
            back = CONV_W - 1 - i
            conv = conv + xpad_ref[q, CONV_HIST - back:CONV_HIST - back + chunk, :] * cw_ref[i:i + 1, :]
        hist_next = xpad_ref[q, chunk:chunk + CONV_HIST, :]
        xpad_ref[q, 0:CONV_HIST, :] = hist_next
        tail_ref[q] = hist_next
        convs.append(conv)
    conv = jnp.concatenate(convs, axis=0)
    xbc = conv * _sigmoid(conv)
    xs = xbc[:, 0:SSM_DIM]
    bm = xbc[:, SSM_DIM:SSM_DIM + BC_DIM].astype(BF16)
    cm = xbc[:, SSM_DIM + BC_DIM:CONV_DIM].astype(BF16)

    dt = _softplus(stack(dt_refs) + dtb_ref[...])
    if pad:
        dt = jnp.where(is_pad, 0.0, dt)
    da = dt * (-jnp.exp(alog_ref[...]))
    ci = lax.broadcasted_iota(jnp.int32, (chunk, chunk), 0)
    cj = lax.broadcasted_iota(jnp.int32, (chunk, chunk), 1)
    causal = ci >= cj
    acs = _dot_sel_l(_causal_blocks(n_rows, chunk), da)
    acs_t = acs.T
    acs_end = jnp.concatenate(
        [jnp.broadcast_to(acs[(q + 1) * chunk - 1:(q + 1) * chunk, :], (chunk, DT_COLS)) for q in seqs], axis=0)

    expand = exp_ref[...]
    xdt = xs * _dot_sel_r(dt, expand)
    xdt_b = xdt.astype(BF16)
    xdt_end = (xdt * _dot_sel_r(jnp.exp(acs_end - acs), expand)).astype(BF16)
    e_acs = _dot_sel_r(jnp.exp(acs), expand)
    decay_end = [jnp.exp(acs_t[:, (q + 1) * chunk - 1:(q + 1) * chunk]) for q in seqs]

    gw = SSM_HPG * HEAD
    groups = range(SSM_GROUPS)
    heads = range(SSM_HEADS)
    qg = [(q, g) for q in seqs for g in groups]
    qh = [(q, hd) for q in seqs for hd in heads]
    bm_g = {(q, g): bm[rsl[q], g * SSM_STATE:(g + 1) * SSM_STATE] for q, g in qg}
    cm_g = {(q, g): cm[rsl[q], g * SSM_STATE:(g + 1) * SSM_STATE] for q, g in qg}
    h_old = {(q, hd): h_ref[q, hd] for q, hd in qh}
    cb = {k: _dot_nt(cm_g[k], bm_g[k]) for k in qg}
    y_off = {(q, g): _dot_nt(cm_g[q, g], jnp.concatenate(
        [h_old[q, g * SSM_HPG + hh].astype(BF16) for hh in range(SSM_HPG)], axis=0)) for q, g in qg}
    st = {(q, g): _dot_tn(xdt_end[rsl[q], g * gw:(g + 1) * gw], bm_g[q, g]) for q, g in qg}
    for q, hd in qh:
        sl = slice(hd * HEAD, (hd + 1) * HEAD)
        seg = acs[rsl[q], hd:hd + 1] - acs_t[hd:hd + 1, rsl[q]]
        lmat = jnp.exp(jnp.where(causal, seg, -jnp.inf))
        yacc_ref[rsl[q], sl] = _dot((cb[q, hd // SSM_HPG] * lmat).astype(BF16), xdt_b[rsl[q], sl])
    for q, hd in qh:
        g, hh = divmod(hd, SSM_HPG)
        h_ref[q, hd] = h_old[q, hd] * decay_end[q][hd:hd + 1, :] + st[q, g][hh * HEAD:(hh + 1) * HEAD, :]

    y_off_all = jnp.concatenate(
        [jnp.concatenate([y_off[q, g] for g in groups], axis=1) for q in seqs], axis=0)
    y = yacc_ref[...] + y_off_all * e_acs + xs * dskip_ref[...]
    zz = stack(z_refs)
    y = y * (zz * _sigmoid(zz))
    parts = []
    for g in groups:
        yg = y[:, g * gw:(g + 1) * gw]
        ms = jnp.mean(yg * yg, axis=-1, keepdims=True)
        parts.append(yg * lax.rsqrt(ms + RMS_EPS))
    out = (jnp.concatenate(parts, axis=1) * nw_ref[...]).astype(y_ref.dtype)
    for q in seqs:
        y_ref[q] = out[rsl[q]]


def _ssd_mix(p_all, hist, h0, wts, *, prompt, cast_weights=()):
    plan = _SeqPlan(prompt, SSD_GROUPS)
    slots, cast_specs = plan.cast_slots(cast_weights)
    p_specs = (plan.p_specs(SSM_DIM, PC_XS) + plan.p_specs(2 * BC_DIM, PC_BC)
               + plan.p_specs(SSM_DIM, PC_Z) + plan.p_specs(DT_COLS, PC_DT))
    const2 = lambda g, c: (0, 0)
    wspecs = [pl.BlockSpec(w.shape, const2) for w in wts]
    conv_spec = plan.state_spec((CONV_HIST, CONV_DIM))
    ssm_spec = plan.state_spec((SSM_HEADS, HEAD, SSM_STATE))
    return pl.pallas_call(
        functools.partial(_ssd_kernel, chunk=plan.chunk, pad=plan.pad, n_seq=plan.n_seq, n_p=len(p_specs) // 4,
                          slots=slots),
        grid=plan.grid,
        in_specs=p_specs + [conv_spec, ssm_spec] + wspecs + cast_specs,
        out_specs=[plan.y_spec(SSM_DIM), ssm_spec, conv_spec] + cast_specs,
        out_shape=[plan.y_shape(SSM_DIM),
                   jax.ShapeDtypeStruct((plan.total, SSM_HEADS, HEAD, SSM_STATE), F32),
                   jax.ShapeDtypeStruct((plan.total, CONV_HIST, CONV_DIM), F32)] + [s.out_shape() for s in slots],
        scratch_shapes=[pltpu.VMEM((plan.n_seq, CONV_HIST + plan.chunk, CONV_DIM), F32),
                        pltpu.VMEM((plan.n_seq * plan.chunk, SSM_DIM), F32)],
        compiler_params=pltpu.CompilerParams(**_MIX_PARAMS),
        name="ssd_mix_c%d" % plan.chunk,
    )(*([p_all] * len(p_specs)), hist, h0, *wts, *[s.w for s in slots])


MERGE_F_TILES = F_ROWS // MERGE_TM
assert S_ROWS == MERGE_TM


def _merge_kernel(x_ref, yrw_f_ref, yrw_s_ref, yssm_f_ref, yssm_s_ref, ga_ref, gb_ref, bga_ref, bgb_ref,
                  wrw_ref, wssm_ref, wout_ref, g_ref, b_ref, o_ref):
    is_frame = pl.program_id(0) < MERGE_F_TILES
    y_rw = jnp.where(is_frame, yrw_f_ref[...], yrw_s_ref[...])
    y_ssm = jnp.where(is_frame, yssm_f_ref[...], yssm_s_ref[...])
    merged = (_sigmoid(ga_ref[...] + bga_ref[...]) * _dot(y_rw, wrw_ref[...])
              + _sigmoid(gb_ref[...] + bgb_ref[...]) * _dot(y_ssm, wssm_ref[...]))
    s = ALPHA * x_ref[...] + _dot(merged.astype(BF16), wout_ref[...])
    o_ref[...] = _layer_norm(s, g_ref[...], b_ref[...])


def _merge(x, y_rw_f, y_rw_s, y_ssm_f, y_ssm_s, p_all, b_gate, w_rw_out, w_ssm_out, w_out, ln_g, ln_b):
    m, d = x.shape
    tm = MERGE_TM
    row = lambda i: (i, 0)
    frame_row = lambda i: (jnp.minimum(i, MERGE_F_TILES - 1), 0)
    const = lambda i: (0, 0)
    resident = lambda w: pl.BlockSpec(w.shape, const, pipeline_mode=pl.Buffered(1))
    return pl.pallas_call(
        _merge_kernel,
        grid=(m // tm,),
        in_specs=[
            pl.BlockSpec((tm, d), row),
            pl.BlockSpec((tm, RW_DIM), frame_row),
            pl.BlockSpec((tm, RW_DIM), const),
            pl.BlockSpec((tm, SSM_DIM), frame_row),
            pl.BlockSpec((tm, SSM_DIM), const),
            pl.BlockSpec((tm, d), lambda i: (i, PC_GATE // d)),
            pl.BlockSpec((tm, d), lambda i: (i, PC_GATE // d + 1)),
            pl.BlockSpec((1, d), lambda i: (0, 0)),
            pl.BlockSpec((1, d), lambda i: (0, 1)),
            resident(w_rw_out), resident(w_ssm_out), resident(w_out),
            pl.BlockSpec((1, d), const),
            pl.BlockSpec((1, d), const),
        ],
        out_specs=pl.BlockSpec((tm, d), row),
        out_shape=jax.ShapeDtypeStruct((m, d), F32),
        compiler_params=pltpu.CompilerParams(
            dimension_semantics=("arbitrary",), vmem_limit_bytes=VMEM_LIMIT),
        name="merge_ln",
    )(x, y_rw_f, y_rw_s, y_ssm_f, y_ssm_s, p_all, p_all, b_gate, b_gate, w_rw_out, w_ssm_out, w_out,
      ln_g, ln_b)


def _pad_cols(x, width):
    return jnp.pad(x, [(0, 0)] * (x.ndim - 1) + [(0, width - x.shape[-1])])


def _rw_cols(x):
    o_wd, o_ad, o_gd = 3 * RW_DIM, 3 * RW_DIM + RW_DECAY_LORA, 3 * RW_DIM + RW_DECAY_LORA + RW_AAA_LORA
    return jnp.concatenate([
        x[..., :o_wd],
        _pad_cols(x[..., o_wd:o_ad], LORA_PAD),
        _pad_cols(x[..., o_ad:o_gd], LORA_PAD),
        x[..., o_gd:],
    ], axis=-1)


def _rw_cols_inv(x):
    return jnp.concatenate([
        x[..., :RW_OFF_WD],
        x[..., RW_OFF_WD:RW_OFF_WD + RW_DECAY_LORA],
        x[..., RW_OFF_AD:RW_OFF_AD + RW_AAA_LORA],
        x[..., RW_OFF_GD:],
    ], axis=-1)


def _block_ones(n, blk):
    i = jnp.arange(n) // blk
    return (i[:, None] == i[None, :]).astype(BF16)


def kernel(x_prompt, x_sample, state_rwkv_shift, state_wkv, state_conv, state_ssm, meta_tokens, ffn1_gu, ffn1_dn, ln1_g, ln1_b, w_in, b_gate, rw_mu, rw_w0, rw_w2, rw_a0, rw_a2, rw_g2, rw_kk, rw_ka, rw_rk, rw_gn_w, rw_gn_b, conv_w, conv_b, dt_bias, a_log, d_skip, ssm_norm_w, w_rw_out, w_ssm_out, w_out, ln2_g, ln2_b, ffn2_gu, ffn2_dn, ln3_g, ln3_b):
    lyr = 0
    row = lambda t: t[lyr].reshape(1, -1).astype(F32)

    head_rows = jnp.concatenate([jnp.zeros((PAD, D_MODEL), F32), meta_tokens.astype(F32)], axis=0)
    x_extra = jnp.concatenate([x_sample.reshape(S_ROWS, D_MODEL)] + [head_rows] * BATCH, axis=0)
    x1, x1b = _ffn_ln_in(x_prompt.reshape(F_ROWS, D_MODEL), x_extra, ffn1_gu[lyr].astype(BF16),
                         ffn1_dn[lyr].astype(BF16), row(ln1_g), row(ln1_b), [])

    w = w_in[lyr].astype(BF16)
    o_z = RW_SHIFT_COLS
    o_xbc = o_z + SSM_DIM
    o_dt = o_xbc + CONV_DIM
    o_gate = o_dt + SSM_HEADS
    w_all = jnp.concatenate([
        _rw_cols(w[:, :o_z]), _pad_cols(w[:, o_dt:o_gate], DT_SLOT), w[:, o_gate:],
        w[:, o_z:o_xbc], w[:, o_xbc:o_dt]], axis=1)
    p_all, = _project(x1b, w_all, [])

    pad_rows = lambda t, n: jnp.pad(t, ((0, n - t.shape[0]), (0, 0)))
    rw_wts = [
        _rw_cols(rw_mu[lyr]).reshape(1, RW_COLS), row(rw_w0),
        pad_rows(rw_w2[lyr], LORA_PAD).astype(BF16), row(rw_a0),
        pad_rows(rw_a2[lyr], LORA_PAD).astype(BF16), rw_g2[lyr].astype(BF16),
        row(rw_kk), row(rw_ka), row(rw_rk), row(rw_gn_w), row(rw_gn_b),
        _block_ones(2 * LANES, HEAD),
    ]
    y_rw_s, wkv_s, shift_s = _rwkv_mix(
        p_all, _rw_cols(state_rwkv_shift[lyr]), state_wkv[lyr], rw_wts, prompt=False)
    y_rw_f, wkv_p, shift_p, w_rw_o, w_ssm_o, w_o = _rwkv_mix(
        p_all, jnp.zeros((BATCH, 1, RW_COLS), F32), jnp.zeros((BATCH, RW_HEADS, HEAD, HEAD), F32), rw_wts,
        prompt=True, cast_weights=[(w_rw_out[lyr], MIX_CAST_ROWS), (w_ssm_out[lyr], MIX_CAST_ROWS),
                                   (w_out[lyr], MIX_CAST_ROWS)])

    head_of_lane = jnp.arange(SSM_DIM) // HEAD
    expand = (jnp.arange(DT_COLS)[:, None] == head_of_lane[None, :]).astype(BF16)
    ssd_wts = [
        conv_w[lyr], row(conv_b), _pad_cols(row(dt_bias), DT_COLS), _pad_cols(row(a_log), DT_COLS),
        jnp.repeat(d_skip[lyr], HEAD).reshape(1, SSM_DIM), row(ssm_norm_w),
        expand,
    ]
    hist_rows = lambda t: jnp.pad(t, ((0, 0), (CONV_HIST - (CONV_W - 1), 0), (0, 0)))
    y_ssm_s, ssm_s, conv_s = _ssd_mix(
        p_all, hist_rows(state_conv[lyr]), state_ssm[lyr], ssd_wts, prompt=False)
    y_ssm_f, ssm_p, conv_p, gu2, dn2 = _ssd_mix(
        p_all, jnp.zeros((BATCH, CONV_HIST, CONV_DIM), F32),
        jnp.zeros((BATCH, SSM_HEADS, HEAD, SSM_STATE), F32), ssd_wts, prompt=True,
        cast_weights=[(ffn2_gu[lyr], MIX_CAST_ROWS // 4), (ffn2_dn[lyr], MIX_CAST_ROWS)])

    flat = lambda t: t.reshape(-1, t.shape[-1])
    x2 = _merge(x1, flat(y_rw_f), flat(y_rw_s), flat(y_ssm_f), flat(y_ssm_s), p_all, row(b_gate),
                w_rw_o, w_ssm_o, w_o, row(ln2_g), row(ln2_b))
    y_frames, y_extra = _ffn_ln_out(x2, gu2, dn2, row(ln3_g), row(ln3_b))

    y_prompt = y_frames.reshape(BATCH, SEQ, D_MODEL)
    y_sample = y_extra[:S_ROWS].reshape(DEC_BATCH, DEC_SEQ, D_MODEL)
    conv_of = lambda t: t[:, CONV_HIST - (CONV_W - 1):][None]
    return (y_prompt, y_sample,
            _rw_cols_inv(shift_p)[None], wkv_p[None], conv_of(conv_p), ssm_p[None],
            _rw_cols_inv(shift_s)[None], wkv_s[None], conv_of(conv_s), ssm_s[None])
```

```python
import functools
import math

import jax
import jax.numpy as jnp
from jax import lax
from jax.experimental import pallas as pl
from jax.experimental.pallas import tpu as pltpu

F32 = jnp.float32
BF16 = jnp.bfloat16

D_MODEL = 2048
BATCH = 4
SEQ = 2048
DEC_BATCH = 16
DEC_SEQ = 16
CHUNK = 64
N_META = 16
PAD = (-N_META) % CHUNK
HEAD_ROWS = PAD + N_META
SEQ_CHUNKS = (HEAD_ROWS + SEQ) // CHUNK
F_ROWS = BATCH * SEQ
S_ROWS = DEC_BATCH * DEC_SEQ
ROW_S0 = F_ROWS
ROW_H0 = F_ROWS + S_ROWS
X_ROWS = S_ROWS + BATCH * HEAD_ROWS
M_ROWS = F_ROWS + X_ROWS

HEAD = 64
RW_DIM = D_MODEL // 2
RW_HEADS = RW_DIM // HEAD
RW_DECAY_LORA = 96
RW_AAA_LORA = 96
RW_GATE_LORA = 256
RW_SHIFT_COLS = 3 * RW_DIM + RW_DECAY_LORA + RW_AAA_LORA + RW_GATE_LORA
RW_GN_EPS = 64e-5
SSM_DIM = D_MODEL
SSM_HEADS = SSM_DIM // HEAD
SSM_GROUPS = 4
SSM_HPG = SSM_HEADS // SSM_GROUPS
SSM_STATE = 128
CONV_W = 4
BC_DIM = SSM_GROUPS * SSM_STATE
CONV_DIM = SSM_DIM + 2 * BC_DIM
RMS_EPS = 1e-5
D_FF = 5632
LN_EPS = 1e-5
DEPTH = 1
ALPHA = (2 * DEPTH) ** 0.25

LANES = 128
SUBLANES = 8
VMEM_LIMIT = 56 * 1024 * 1024

LORA_PAD = LANES
RW_OFF_WD = 3 * RW_DIM
RW_OFF_AD = RW_OFF_WD + LORA_PAD
RW_OFF_GD = RW_OFF_AD + LORA_PAD
RW_COLS = RW_OFF_GD + RW_GATE_LORA
DT_COLS = LANES
CONV_HIST = SUBLANES

MIX_CAST_ROWS = 128
FFN_TM = 512
FFN_TF = 512
MM_TM = 512
MM_TN = 1024
MERGE_TM = 256

DT_SLOT = 512
PC_RW = 0
PC_DT = PC_RW + RW_COLS
PC_GATE = PC_DT + DT_SLOT
PC_Z = PC_GATE + 2 * D_MODEL
PC_XS = PC_Z + SSM_DIM
PC_BC = PC_XS + SSM_DIM
P_COLS = PC_BC + 2 * BC_DIM
assert PC_DT % DT_COLS == 0 and PC_GATE % D_MODEL == 0 and PC_Z % SSM_DIM == 0
assert PC_XS % SSM_DIM == 0 and PC_BC % (2 * BC_DIM) == 0 and P_COLS % MM_TN == 0
assert F_ROWS % FFN_TM == 0 and X_ROWS == FFN_TM and M_ROWS % MM_TM == 0 and ROW_H0 % MERGE_TM == 0


def _dot(a, b):
    return jnp.dot(a, b, preferred_element_type=F32)


def _dot_nt(a, b):
    return lax.dot_general(a, b, (((1,), (1,)), ((), ())), preferred_element_type=F32)


def _dot_tn(a, b):
    return lax.dot_general(a, b, (((0,), (0,)), ((), ())), preferred_element_type=F32)


def _split2(x):
    hi = x.astype(BF16)
    lo = (x - hi.astype(F32)).astype(BF16)
    return hi, lo


def _dot_sel_r(x, sel):
    hi, lo = _split2(x)
    return _dot(hi, sel) + _dot(lo, sel)


def _dot_sel_l(sel, x):
    hi, lo = _split2(x)
    return _dot(sel, hi) + _dot(sel, lo)


def _causal_blocks(n, blk):
    i = lax.broadcasted_iota(jnp.int32, (n, n), 0)
    j = lax.broadcasted_iota(jnp.int32, (n, n), 1)
    keep = jnp.logical_and(i >= j, jnp.bitwise_xor(i, j) < blk)
    return jnp.where(keep, 1.0, 0.0).astype(BF16)


def _sigmoid(x):
    return jax.nn.sigmoid(x)


def _softplus(x):
    return jnp.maximum(x, 0.0) + jnp.log1p(jnp.exp(-jnp.abs(x)))


def _layer_norm(s, g, b):
    mu = jnp.mean(s, axis=-1, keepdims=True)
    d = s - mu
    var = jnp.mean(d * d, axis=-1, keepdims=True)
    return d * lax.rsqrt(var + LN_EPS) * g + b


class _CastSlot:
    def __init__(self, w, rows, t0):
        assert w.ndim == 2 and w.shape[0] % rows == 0
        self.w, self.rows, self.t0, self.n = w, rows, t0, w.shape[0] // rows

    def spec(self, step_of):
        def index(*ids):
            return jnp.clip(step_of(*ids) - self.t0, 0, self.n - 1), 0
        return pl.BlockSpec((self.rows, self.w.shape[1]), index)

    def out_shape(self):
        return jax.ShapeDtypeStruct(self.w.shape, BF16)


def _run_cast_slots(t, slots, in_refs, out_refs):
    for slot, i_ref, o_ref in zip(slots, in_refs, out_refs, strict=True):
        @pl.when(jnp.logical_and(t >= slot.t0, t < slot.t0 + slot.n))
        def _(i_ref=i_ref, o_ref=o_ref):
            o_ref[...] = i_ref[...].astype(BF16)


def _split_refs(refs, *counts):
    groups, at = [], 0
    for n in counts:
        groups.append(refs[at:at + n])
        at += n
    return (*groups, refs[at:])


def _chain_slots(weights_rows, t0=0):
    slots = []
    for w, rows in weights_rows:
        slots.append(_CastSlot(w, rows, t0))
        t0 += slots[-1].n
    return slots, t0


N_MAIN_TILES = F_ROWS // FFN_TM


N_TILES = M_ROWS // FFN_TM


def _ffn_body(load_x, emit_mid, emit_last, wg_ref, wu_ref, wd_ref, g_ref, b_ref, xb_ref, acc_ref, s_ref):
    i, f = pl.program_id(0), pl.program_id(1)
    last = pl.num_programs(1) - 1
    real = i < N_TILES
    both = jnp.logical_and

    def mlp(xb):
        gate = _dot(xb, wg_ref[...])
        up = _dot(xb, wu_ref[...])
        return _dot((gate * _sigmoid(gate) * up).astype(BF16), wd_ref[...])

    def finish_prev(emit):
        emit(_layer_norm(s_ref[...], g_ref[...], b_ref[...]))

    def first_step(with_prev):
        xb = load_x().astype(BF16)
        xb_ref[...] = xb
        acc_ref[...] = mlp(xb)
        if with_prev:
            finish_prev(emit_mid)

    pl.when(both(f == 0, i == 0))(functools.partial(first_step, False))
    pl.when(both(f == 0, both(i > 0, real)))(functools.partial(first_step, True))
    pl.when(both(f == 0, i == N_TILES))(functools.partial(finish_prev, emit_last))

    @pl.when(both(real, both(f > 0, f < last)))
    def _():
        acc_ref[...] += mlp(xb_ref[...])

    @pl.when(both(real, f == last))
    def _():
        s_ref[...] = ALPHA * load_x() + 0.5 * (acc_ref[...] + mlp(xb_ref[...]))


def _ffn_in_kernel(xm_ref, xe_ref, wg_ref, wu_ref, wd_ref, g_ref, b_ref, o_ref, ob_ref, *scratch):
    load_x = lambda: jnp.where(pl.program_id(0) < N_MAIN_TILES, xm_ref[...], xe_ref[...])

    def emit(y):
        o_ref[...] = y
        ob_ref[...] = y.astype(BF16)

    _ffn_body(load_x, emit, emit, wg_ref, wu_ref, wd_ref, g_ref, b_ref, *scratch)


def _ffn_out_kernel(x_ref, wg_ref, wu_ref, wd_ref, g_ref, b_ref, om_ref, oe_ref, *scratch):
    assert N_TILES == N_MAIN_TILES + 1

    def emit_frames(y):
        om_ref[...] = y

    def emit_extra(y):
        oe_ref[...] = y

    _ffn_body(lambda: x_ref[...], emit_frames, emit_extra, wg_ref, wu_ref, wd_ref, g_ref, b_ref, *scratch)


def _ffn_specs(d):
    nf = D_FF // FFN_TF
    step = lambda i, f: jnp.where(i == N_TILES, nf - 1, f)
    return nf, [
        pl.BlockSpec((d, FFN_TF), lambda i, f: (0, step(i, f))),
        pl.BlockSpec((d, FFN_TF), lambda i, f: (0, step(i, f) + nf)),
        pl.BlockSpec((FFN_TF, d), lambda i, f: (step(i, f), 0)),
        pl.BlockSpec((1, d), lambda i, f: (0, 0)),
        pl.BlockSpec((1, d), lambda i, f: (0, 0)),
    ]


_FFN_PARAMS = dict(dimension_semantics=("arbitrary", "arbitrary"), vmem_limit_bytes=VMEM_LIMIT)
_in_frame_tile = lambda i, f: (jnp.minimum(i, N_MAIN_TILES - 1), 0)
_in_any_tile = lambda i, f: (jnp.minimum(i, N_TILES - 1), 0)
_out_frame_tile = lambda i, f: (jnp.clip(i - 1, 0, N_MAIN_TILES - 1), 0)
_out_any_tile = lambda i, f: (jnp.maximum(i - 1, 0), 0)
_only_tile = lambda i, f: (0, 0)


def _ffn_scratch(tile):
    return [pltpu.VMEM(tile, BF16), pltpu.VMEM(tile, F32), pltpu.VMEM(tile, F32)]


def _ffn_ln_in(x_main, x_extra, w_gu, w_dn, ln_g, ln_b):
    d = x_main.shape[1]
    nf, wspecs = _ffn_specs(d)
    tile = (FFN_TM, d)
    return pl.pallas_call(
        _ffn_in_kernel,
        grid=(N_TILES + 1, nf),
        in_specs=[pl.BlockSpec(tile, _in_frame_tile),
                  pl.BlockSpec(tile, _only_tile, pipeline_mode=pl.Buffered(1))] + wspecs,
        out_specs=[pl.BlockSpec(tile, _out_any_tile), pl.BlockSpec(tile, _out_any_tile)],
        out_shape=[jax.ShapeDtypeStruct((M_ROWS, d), F32), jax.ShapeDtypeStruct((M_ROWS, d), BF16)],
        scratch_shapes=_ffn_scratch(tile),
        compiler_params=pltpu.CompilerParams(**_FFN_PARAMS),
        name="ffn_ln_in",
    )(x_main, x_extra, w_gu, w_gu, w_dn, ln_g, ln_b)


def _ffn_ln_out(x, w_gu, w_dn, ln_g, ln_b):
    d = x.shape[1]
    nf, wspecs = _ffn_specs(d)
    tile = (FFN_TM, d)
    return pl.pallas_call(
        _ffn_out_kernel,
        grid=(N_TILES + 1, nf),
        in_specs=[pl.BlockSpec(tile, _in_any_tile)] + wspecs,
        out_specs=[pl.BlockSpec(tile, _out_frame_tile), pl.BlockSpec(tile, _only_tile)],
        out_shape=[jax.ShapeDtypeStruct((F_ROWS, d), F32), jax.ShapeDtypeStruct((X_ROWS, d), F32)],
        scratch_shapes=_ffn_scratch(tile),
        compiler_params=pltpu.CompilerParams(**_FFN_PARAMS),
        name="ffn_ln_out",
    )(x, w_gu, w_gu, w_dn, ln_g, ln_b)


def _mm_kernel(x_ref, w_ref, *rest, slots):
    n = len(slots)
    cast_in, o_ref, cast_out = rest[:n], rest[n], rest[n + 1:]
    _run_cast_slots(pl.program_id(0) * pl.num_programs(1) + pl.program_id(1), slots, cast_in, cast_out)
    o_ref[...] = _dot(x_ref[...], w_ref[...])


def _project(x, w, cast_chains):
    m, k = x.shape
    n = w.shape[1]
    grid = (n // MM_TN, m // MM_TM)
    slots = []
    for chain in cast_chains:
        chain_slots, t_end = _chain_slots(chain)
        assert t_end <= grid[0] * grid[1]
        slots += chain_slots
    cast_specs = [s.spec(lambda j, i: j * grid[1] + i) for s in slots]
    return pl.pallas_call(
        functools.partial(_mm_kernel, slots=slots),
        grid=grid,
        in_specs=[pl.BlockSpec((MM_TM, k), lambda j, i: (i, 0)),
                  pl.BlockSpec((k, MM_TN), lambda j, i: (0, j))] + cast_specs,
        out_specs=[pl.BlockSpec((MM_TM, MM_TN), lambda j, i: (i, j))] + cast_specs,
        out_shape=[jax.ShapeDtypeStruct((m, n), F32)] + [s.out_shape() for s in slots],
        compiler_params=pltpu.CompilerParams(
            dimension_semantics=("arbitrary", "arbitrary"), vmem_limit_bytes=VMEM_LIMIT),
        name="proj_in",
    )(x, w, *[s.w for s in slots])


def _rwkv_kernel(*refs, chunk, pad, n_seq, n_p, slots):
    p_refs, fixed, cast_in, outs, cast_out, (yacc_ref,) = _split_refs(refs, n_p, 14, len(slots), 3, len(slots))
    (hist_ref, s0_ref, mu_ref, w0_ref, w2_ref, a0_ref, a2_ref, g2_ref, kk_ref, ka_ref, rk_ref,
     gnw_ref, gnb_ref, seg_ref) = fixed
    y_ref, s_ref, shift_ref = outs
    c = pl.program_id(1)
    _run_cast_slots(pl.program_id(0) * pl.num_programs(1) + c, slots, cast_in, cast_out)
    n_rows = n_seq * chunk
    seqs = range(n_seq)
    rsl = [slice(q * chunk, (q + 1) * chunk) for q in seqs]
    rows = lax.broadcasted_iota(jnp.int32, (n_rows, 1), 0)

    @pl.when(c == 0)
    def _():
        shift_ref[...] = hist_ref[...]
        s_ref[...] = s0_ref[...]

    p = jnp.concatenate([r[...] for r in p_refs], axis=0)
    if pad:
        p = jnp.where(jnp.logical_and(c == 0, jnp.bitwise_and(rows, chunk - 1) < pad), 0.0, p)
    prev = pltpu.roll(p, 1, 0)
    for q in seqs:
        prev = jnp.where(rows == q * chunk, shift_ref[q], prev)
    for q in seqs:
        shift_ref[q] = p[(q + 1) * chunk - 1:(q + 1) * chunk, :]
    ps = p + (prev - p) * mu_ref[...]

    r = ps[:, 0:RW_DIM]
    k = ps[:, RW_DIM:2 * RW_DIM]
    v = ps[:, 2 * RW_DIM:3 * RW_DIM]
    wd = ps[:, RW_OFF_WD:RW_OFF_WD + LORA_PAD]
    ad = ps[:, RW_OFF_AD:RW_OFF_AD + LORA_PAD]
    gd = ps[:, RW_OFF_GD:RW_OFF_GD + RW_GATE_LORA]

    lw = -math.exp(-0.5) * _sigmoid(w0_ref[...] + _dot(jnp.tanh(wd).astype(BF16), w2_ref[...]))
    a = _sigmoid(a0_ref[...] + _dot(ad.astype(BF16), a2_ref[...]))
    g = _dot(_sigmoid(gd).astype(BF16), g2_ref[...])

    seg = seg_ref[...]

    def head_sum(x):
        w = seg.shape[0]
        return jnp.concatenate(
            [_dot_sel_r(x[:, j * w:(j + 1) * w], seg) for j in range(RW_DIM // w)], axis=1)

    kk = k * kk_ref[...]
    kk = kk * lax.rsqrt(jnp.maximum(head_sum(kk * kk), 1e-24))
    k = k * (1.0 + (a - 1.0) * ka_ref[...])
    b_neg = -(kk * a)

    cum = _dot_sel_l(_causal_blocks(n_rows, chunk), lw)
    cum_last = [cum[(q + 1) * chunk - 1:(q + 1) * chunk, :] for q in seqs]
    cum_end = jnp.concatenate([jnp.broadcast_to(x, (chunk, RW_DIM)) for x in cum_last], axis=0)
    e_neg = jnp.exp(-cum)
    kt = (kk * jnp.exp(cum - lw)).astype(BF16)
    rt = (r * jnp.exp(cum)).astype(BF16)
    kd = (k * e_neg).astype(BF16)
    bd_neg = (b_neg * e_neg).astype(BF16)
    e_end = jnp.exp(cum_end - cum)
    k_end = (k * e_end).astype(BF16)
    b_end_neg = (b_neg * e_end).astype(BF16)
    p_end = [jnp.exp(x) for x in cum_last]
    vb = v.astype(BF16)

    ri = lax.broadcasted_iota(jnp.int32, (2 * chunk, 2 * chunk), 0)
    rj = lax.broadcasted_iota(jnp.int32, (2 * chunk, 2 * chunk), 1)
    bi = jnp.where(ri >= chunk, ri - chunk + 1, ri)
    bj = jnp.where(rj >= chunk, rj - chunk, rj)
    keep = bi > bj
    ti = lax.broadcasted_iota(jnp.int32, (chunk, 2 * chunk), 0)
    tj = lax.broadcasted_iota(jnp.int32, (chunk, 2 * chunk), 1)
    hi_lane = tj >= chunk
    eye_hi = (tj == ti + chunk).astype(F32)
    zeros_v = jnp.zeros((chunk, HEAD), BF16)

    pairs = [(q, h) for q in seqs for h in range(RW_HEADS)]
    sls = [(rsl[q], slice(h * HEAD, (h + 1) * HEAD)) for q, h in pairs]
    s_old = [s_ref[q, h] for q, h in pairs]
    s_b = [s.astype(BF16) for s in s_old]
    amats = [jnp.where(keep, _dot_nt(jnp.concatenate([kt[sl], rt[sl]], axis=0),
                                     jnp.concatenate([bd_neg[sl], kd[sl]], axis=0)), 0.0)
             for sl in sls]
    top = [m[:chunk] for m in amats]
    bot = [m[chunk:].astype(BF16) for m in amats]
    top_b = [x.astype(BF16) for x in top]
    tq = [_dot(xb[:, :chunk], jnp.where(hi_lane, eye_hi, x).astype(BF16)) + eye_hi
          for x, xb in zip(top, top_b)]
    n = 2
    while n < chunk:
        tq = [jnp.where(hi_lane, x, 0.0) + _dot(x[:, :chunk].astype(BF16), x.astype(BF16)) for x in tq]
        n *= 2
    t_b = [x[:, chunk:].astype(BF16) for x in tq]
    v_h = [vb[sl] for sl in sls]
    rhs = [_dot_nt(kt[sl], sb) + _dot(xb, jnp.concatenate([zeros_v, vh], axis=0))
           for sl, sb, xb, vh in zip(sls, s_b, top_b, v_h)]
    ub = [_dot(t, x.astype(BF16)).astype(BF16) for t, x in zip(t_b, rhs)]
    uv = [jnp.concatenate([u, vh], axis=0) for u, vh in zip(ub, v_h)]
    for i, (q, h) in enumerate(pairs):
        sl = sls[i]
        yacc_ref[sl] = _dot_nt(rt[sl], s_b[i]) + _dot(bot[i], uv[i])
        s_ref[q, h] = s_old[i] * p_end[q][:, sl[1]] + _dot_tn(
            uv[i], jnp.concatenate([b_end_neg[sl], k_end[sl]], axis=0))

    y = yacc_ref[...]
    inv_n = 1.0 / HEAD
    mean = head_sum(y) * inv_n
    yc = y - mean
    var = head_sum(yc * yc) * inv_n
    yn = yc * lax.rsqrt(var + RW_GN_EPS) * gnw_ref[...] + gnb_ref[...]
    bonus = head_sum(r * k * rk_ref[...]) * v
    out = ((yn + bonus) * g).astype(y_ref.dtype)
    for q in seqs:
        y_ref[q] = out[rsl[q]]


RWKV_GROUPS = (2, 8)
SSD_GROUPS = (1, 1)


class _SeqPlan:
    def __init__(self, prompt, groups):
        self.prompt = prompt
        if prompt:
            self.n_seq, self.chunk, self.pad, self.total, self.seq_len = groups[0], CHUNK, PAD, BATCH, SEQ
            self.grid = (BATCH // self.n_seq, SEQ_CHUNKS)
        else:
            self.n_seq, self.chunk, self.pad, self.total, self.seq_len = groups[1], DEC_SEQ, 0, DEC_BATCH, DEC_SEQ
            self.grid = (DEC_BATCH // self.n_seq, 1)

    def p_specs(self, width, col):
        cb = col // width
        if not self.prompt:
            rows = self.n_seq * self.chunk
            return [pl.BlockSpec((rows, width), lambda g, c: (ROW_S0 // rows + g, cb))]

        def spec(q):
            def index(g, c):
                seq = g * self.n_seq + q
                return jnp.where(c == 0, ROW_H0 // CHUNK + seq, seq * (SEQ // CHUNK) + c - 1), cb
            return pl.BlockSpec((CHUNK, width), index)
        return [spec(q) for q in range(self.n_seq)]

    def y_spec(self, dim):
        return pl.BlockSpec((self.n_seq, self.chunk, dim), lambda g, c: (g, jnp.maximum(c - 1, 0), 0))

    def y_shape(self, dim):
        return jax.ShapeDtypeStruct((self.total, self.seq_len, dim), BF16)

    def state_spec(self, shape):
        nd = len(shape)
        return pl.BlockSpec((self.n_seq,) + tuple(shape), lambda g, c: (g,) + (0,) * nd)

    def cast_slots(self, cast_weights):
        slots, t_end = _chain_slots(cast_weights)
        assert t_end <= self.grid[0] * self.grid[1]
        return slots, [s.spec(lambda g, c: g * self.grid[1] + c) for s in slots]


_MIX_PARAMS = dict(dimension_semantics=("arbitrary", "arbitrary"), vmem_limit_bytes=VMEM_LIMIT)


def _rwkv_mix(p_all, hist, s0, wts, *, prompt, cast_weights=()):
    plan = _SeqPlan(prompt, RWKV_GROUPS)
    slots, cast_specs = plan.cast_slots(cast_weights)
    p_specs = plan.p_specs(RW_COLS, PC_RW)
    const2 = lambda g, c: (0, 0)
    wspecs = [pl.BlockSpec(w.shape, const2) for w in wts]
    shift_spec = plan.state_spec((1, RW_COLS))
    wkv_spec = plan.state_spec((RW_HEADS, HEAD, HEAD))
    return pl.pallas_call(
        functools.partial(_rwkv_kernel, chunk=plan.chunk, pad=plan.pad, n_seq=plan.n_seq, n_p=len(p_specs),
                          slots=slots),
        grid=plan.grid,
        in_specs=p_specs + [shift_spec, wkv_spec] + wspecs + cast_specs,
        out_specs=[plan.y_spec(RW_DIM), wkv_spec, shift_spec] + cast_specs,
        out_shape=[plan.y_shape(RW_DIM),
                   jax.ShapeDtypeStruct((plan.total, RW_HEADS, HEAD, HEAD), F32),
                   jax.ShapeDtypeStruct((plan.total, 1, RW_COLS), F32)] + [s.out_shape() for s in slots],
        scratch_shapes=[pltpu.VMEM((plan.n_seq * plan.chunk, RW_DIM), F32)],
        compiler_params=pltpu.CompilerParams(**_MIX_PARAMS),
        name="rwkv_mix_c%d" % plan.chunk,
    )(*([p_all] * len(p_specs)), hist, s0, *wts, *[s.w for s in slots])


def _ssd_kernel(*refs, chunk, pad, n_seq, n_p, slots):
    (xs_refs, bc_refs, z_refs, dt_refs, fixed, cast_in, outs, cast_out, (xpad_ref, yacc_ref)) = _split_refs(
        refs, n_p, n_p, n_p, n_p, 9, len(slots), 3, len(slots))
    hist_ref, h0_ref, cw_ref, cb_ref, dtb_ref, alog_ref, dskip_ref, nw_ref, exp_ref = fixed
    y_ref, h_ref, tail_ref = outs
    c = pl.program_id(1)
    _run_cast_slots(pl.program_id(0) * pl.num_programs(1) + c, slots, cast_in, cast_out)
    n_rows = n_seq * chunk
    seqs = range(n_seq)
    rsl = [slice(q * chunk, (q + 1) * chunk) for q in seqs]
    rows = lax.broadcasted_iota(jnp.int32, (n_rows, 1), 0)
    stack = lambda rs: jnp.concatenate([r[...] for r in rs], axis=0)

    @pl.when(c == 0)
    def _():
        xpad_ref[:, 0:CONV_HIST, :] = hist_ref[...]
        h_ref[...] = h0_ref[...]

    u = jnp.concatenate([stack(xs_refs), stack(bc_refs)], axis=1)
    if pad:
        is_pad = jnp.logical_and(c == 0, jnp.bitwise_and(rows, chunk - 1) < pad)
        u = jnp.where(is_pad, 0.0, u)
    convs = []
    for q in seqs:
        u_q = u[rsl[q]]
        xpad_ref[q, CONV_HIST:CONV_HIST + chunk, :] = u_q
        conv = cb_ref[...] + u_q * cw_ref[CONV_W - 1:CONV_W, :]
        for i in range(CONV_W - 1):
            back = CONV_W - 1 - i
            conv = conv + xpad_ref[q, CONV_HIST - back:CONV_HIST - back + chunk, :] * cw_ref[i:i + 1, :]
        hist_next = xpad_ref[q, chunk:chunk + CONV_HIST, :]
        xpad_ref[q, 0:CONV_HIST, :] = hist_next
        tail_ref[q] = hist_next
        convs.append(conv)
    conv = jnp.concatenate(convs, axis=0)
    xbc = conv * _sigmoid(conv)
    xs = xbc[:, 0:SSM_DIM]
    bm = xbc[:, SSM_DIM:SSM_DIM + BC_DIM].astype(BF16)
    cm = xbc[:, SSM_DIM + BC_DIM:CONV_DIM].astype(BF16)

    dt = _softplus(stack(dt_refs) + dtb_ref[...])
    if pad:
        dt = jnp.where(is_pad, 0.0, dt)
    da = dt * (-jnp.exp(alog_ref[...]))
    ci = lax.broadcasted_iota(jnp.int32, (chunk, chunk), 0)
    cj = lax.broadcasted_iota(jnp.int32, (chunk, chunk), 1)
    causal = ci >= cj
    acs = _dot_sel_l(_causal_blocks(n_rows, chunk), da)
    acs_t = acs.T
    acs_end = jnp.concatenate(
        [jnp.broadcast_to(acs[(q + 1) * chunk - 1:(q + 1) * chunk, :], (chunk, DT_COLS)) for q in seqs], axis=0)

    expand = exp_ref[...]
    xdt = xs * _dot_sel_r(dt, expand)
    xdt_b = xdt.astype(BF16)
    xdt_end = (xdt * _dot_sel_r(jnp.exp(acs_end - acs), expand)).astype(BF16)
    e_acs = _dot_sel_r(jnp.exp(acs), expand)
    decay_end = [jnp.exp(acs_t[:, (q + 1) * chunk - 1:(q + 1) * chunk]) for q in seqs]

    gw = SSM_HPG * HEAD
    groups = range(SSM_GROUPS)
    heads = range(SSM_HEADS)
    qg = [(q, g) for q in seqs for g in groups]
    qh = [(q, hd) for q in seqs for hd in heads]
    bm_g = {(q, g): bm[rsl[q], g * SSM_STATE:(g + 1) * SSM_STATE] for q, g in qg}
    cm_g = {(q, g): cm[rsl[q], g * SSM_STATE:(g + 1) * SSM_STATE] for q, g in qg}
    h_old = {(q, hd): h_ref[q, hd] for q, hd in qh}
    cb = {k: _dot_nt(cm_g[k], bm_g[k]) for k in qg}
    y_off = {(q, g): _dot_nt(cm_g[q, g], jnp.concatenate(
        [h_old[q, g * SSM_HPG + hh].astype(BF16) for hh in range(SSM_HPG)], axis=0)) for q, g in qg}
    st = {(q, g): _dot_tn(xdt_end[rsl[q], g * gw:(g + 1) * gw], bm_g[q, g]) for q, g in qg}
    for q, hd in qh:
        sl = slice(hd * HEAD, (hd + 1) * HEAD)
        seg = acs[rsl[q], hd:hd + 1] - acs_t[hd:hd + 1, rsl[q]]
        lmat = jnp.exp(jnp.where(causal, seg, -jnp.inf))
        yacc_ref[rsl[q], sl] = _dot((cb[q, hd // SSM_HPG] * lmat).astype(BF16), xdt_b[rsl[q], sl])
    for q, hd in qh:
        g, hh = divmod(hd, SSM_HPG)
        h_ref[q, hd] = h_old[q, hd] * decay_end[q][hd:hd + 1, :] + st[q, g][hh * HEAD:(hh + 1) * HEAD, :]

    y_off_all = jnp.concatenate(
        [jnp.concatenate([y_off[q, g] for g in groups], axis=1) for q in seqs], axis=0)
    y = yacc_ref[...] + y_off_all * e_acs + xs * dskip_ref[...]
    zz = stack(z_refs)
    y = y * (zz * _sigmoid(zz))
    parts = []
    for g in groups:
        yg = y[:, g * gw:(g + 1) * gw]
        ms = jnp.mean(yg * yg, axis=-1, keepdims=True)
        parts.append(yg * lax.rsqrt(ms + RMS_EPS))
    out = (jnp.concatenate(parts, axis=1) * nw_ref[...]).astype(y_ref.dtype)
    for q in seqs:
        y_ref[q] = out[rsl[q]]


def _ssd_mix(p_all, hist, h0, wts, *, prompt, cast_weights=()):
    plan = _SeqPlan(prompt, SSD_GROUPS)
    slots, cast_specs = plan.cast_slots(cast_weights)
    p_specs = (plan.p_specs(SSM_DIM, PC_XS) + plan.p_specs(2 * BC_DIM, PC_BC)
               + plan.p_specs(SSM_DIM, PC_Z) + plan.p_specs(DT_COLS, PC_DT))
    const2 = lambda g, c: (0, 0)
    wspecs = [pl.BlockSpec(w.shape, const2) for w in wts]
    conv_spec = plan.state_spec((CONV_HIST, CONV_DIM))
    ssm_spec = plan.state_spec((SSM_HEADS, HEAD, SSM_STATE))
    return pl.pallas_call(
        functools.partial(_ssd_kernel, chunk=plan.chunk, pad=plan.pad, n_seq=plan.n_seq, n_p=len(p_specs) // 4,
                          slots=slots),
        grid=plan.grid,
        in_specs=p_specs + [conv_spec, ssm_spec] + wspecs + cast_specs,
        out_specs=[plan.y_spec(SSM_DIM), ssm_spec, conv_spec] + cast_specs,
        out_shape=[plan.y_shape(SSM_DIM),
                   jax.ShapeDtypeStruct((plan.total, SSM_HEADS, HEAD, SSM_STATE), F32),
                   jax.ShapeDtypeStruct((plan.total, CONV_HIST, CONV_DIM), F32)] + [s.out_shape() for s in slots],
        scratch_shapes=[pltpu.VMEM((plan.n_seq, CONV_HIST + plan.chunk, CONV_DIM), F32),
                        pltpu.VMEM((plan.n_seq * plan.chunk, SSM_DIM), F32)],
        compiler_params=pltpu.CompilerParams(**_MIX_PARAMS),
        name="ssd_mix_c%d" % plan.chunk,
    )(*([p_all] * len(p_specs)), hist, h0, *wts, *[s.w for s in slots])


MERGE_F_TILES = F_ROWS // MERGE_TM
assert S_ROWS == MERGE_TM


def _merge_kernel(x_ref, yrw_f_ref, yrw_s_ref, yssm_f_ref, yssm_s_ref, ga_ref, gb_ref, bga_ref, bgb_ref,
                  wrw_ref, wssm_ref, wout_ref, g_ref, b_ref, o_ref):
    is_frame = pl.program_id(0) < MERGE_F_TILES
    y_rw = jnp.where(is_frame, yrw_f_ref[...], yrw_s_ref[...])
    y_ssm = jnp.where(is_frame, yssm_f_ref[...], yssm_s_ref[...])
    merged = (_sigmoid(ga_ref[...] + bga_ref[...]) * _dot(y_rw, wrw_ref[...])
              + _sigmoid(gb_ref[...] + bgb_ref[...]) * _dot(y_ssm, wssm_ref[...]))
    s = ALPHA * x_ref[...] + _dot(merged.astype(BF16), wout_ref[...])
    o_ref[...] = _layer_norm(s, g_ref[...], b_ref[...])


def _merge(x, y_rw_f, y_rw_s, y_ssm_f, y_ssm_s, p_all, b_gate, w_rw_out, w_ssm_out, w_out, ln_g, ln_b):
    m, d = x.shape
    tm = MERGE_TM
    row = lambda i: (i, 0)
    frame_row = lambda i: (jnp.minimum(i, MERGE_F_TILES - 1), 0)
    const = lambda i: (0, 0)
    resident = lambda w: pl.BlockSpec(w.shape, const, pipeline_mode=pl.Buffered(1))
    return pl.pallas_call(
        _merge_kernel,
        grid=(m // tm,),
        in_specs=[
            pl.BlockSpec((tm, d), row),
            pl.BlockSpec((tm, RW_DIM), frame_row),
            pl.BlockSpec((tm, RW_DIM), const),
            pl.BlockSpec((tm, SSM_DIM), frame_row),
            pl.BlockSpec((tm, SSM_DIM), const),
            pl.BlockSpec((tm, d), lambda i: (i, PC_GATE // d)),
            pl.BlockSpec((tm, d), lambda i: (i, PC_GATE // d + 1)),
            pl.BlockSpec((1, d), lambda i: (0, 0)),
            pl.BlockSpec((1, d), lambda i: (0, 1)),
            resident(w_rw_out), resident(w_ssm_out), resident(w_out),
            pl.BlockSpec((1, d), const),
            pl.BlockSpec((1, d), const),
        ],
        out_specs=pl.BlockSpec((tm, d), row),
        out_shape=jax.ShapeDtypeStruct((m, d), F32),
        compiler_params=pltpu.CompilerParams(
            dimension_semantics=("arbitrary",), vmem_limit_bytes=VMEM_LIMIT),
        name="merge_ln",
    )(x, y_rw_f, y_rw_s, y_ssm_f, y_ssm_s, p_all, p_all, b_gate, b_gate, w_rw_out, w_ssm_out, w_out,
      ln_g, ln_b)


def _pad_cols(x, width):
    return jnp.pad(x, [(0, 0)] * (x.ndim - 1) + [(0, width - x.shape[-1])])


def _rw_cols(x):
    o_wd, o_ad, o_gd = 3 * RW_DIM, 3 * RW_DIM + RW_DECAY_LORA, 3 * RW_DIM + RW_DECAY_LORA + RW_AAA_LORA
    return jnp.concatenate([
        x[..., :o_wd],
        _pad_cols(x[..., o_wd:o_ad], LORA_PAD),
        _pad_cols(x[..., o_ad:o_gd], LORA_PAD),
        x[..., o_gd:],
    ], axis=-1)


def _rw_cols_inv(x):
    return jnp.concatenate([
        x[..., :RW_OFF_WD],
        x[..., RW_OFF_WD:RW_OFF_WD + RW_DECAY_LORA],
        x[..., RW_OFF_AD:RW_OFF_AD + RW_AAA_LORA],
        x[..., RW_OFF_GD:],
    ], axis=-1)


def _block_ones(n, blk):
    i = jnp.arange(n) // blk
    return (i[:, None] == i[None, :]).astype(BF16)


def kernel(x_prompt, x_sample, state_rwkv_shift, state_wkv, state_conv, state_ssm, meta_tokens, ffn1_gu, ffn1_dn, ln1_g, ln1_b, w_in, b_gate, rw_mu, rw_w0, rw_w2, rw_a0, rw_a2, rw_g2, rw_kk, rw_ka, rw_rk, rw_gn_w, rw_gn_b, conv_w, conv_b, dt_bias, a_log, d_skip, ssm_norm_w, w_rw_out, w_ssm_out, w_out, ln2_g, ln2_b, ffn2_gu, ffn2_dn, ln3_g, ln3_b):
    lyr = 0
    row = lambda t: t[lyr].reshape(1, -1).astype(F32)

    head_rows = jnp.concatenate([jnp.zeros((PAD, D_MODEL), F32), meta_tokens.astype(F32)], axis=0)
    x_extra = jnp.concatenate([x_sample.reshape(S_ROWS, D_MODEL)] + [head_rows] * BATCH, axis=0)
    x1, x1b = _ffn_ln_in(x_prompt.reshape(F_ROWS, D_MODEL), x_extra, ffn1_gu[lyr].astype(BF16),
                         ffn1_dn[lyr].astype(BF16), row(ln1_g), row(ln1_b))

    w = w_in[lyr].astype(BF16)
    o_z = RW_SHIFT_COLS
    o_xbc = o_z + SSM_DIM
    o_dt = o_xbc + CONV_DIM
    o_gate = o_dt + SSM_HEADS
    w_all = jnp.concatenate([
        _rw_cols(w[:, :o_z]), _pad_cols(w[:, o_dt:o_gate], DT_SLOT), w[:, o_gate:],
        w[:, o_z:o_xbc], w[:, o_xbc:o_dt]], axis=1)
    p_all, = _project(x1b, w_all, [])

    pad_rows = lambda t, n: jnp.pad(t, ((0, n - t.shape[0]), (0, 0)))
    rw_wts = [
        _rw_cols(rw_mu[lyr]).reshape(1, RW_COLS), row(rw_w0),
        pad_rows(rw_w2[lyr], LORA_PAD).astype(BF16), row(rw_a0),
        pad_rows(rw_a2[lyr], LORA_PAD).astype(BF16), rw_g2[lyr].astype(BF16),
        row(rw_kk), row(rw_ka), row(rw_rk), row(rw_gn_w), row(rw_gn_b),
        _block_ones(2 * LANES, HEAD),
    ]
    y_rw_s, wkv_s, shift_s = _rwkv_mix(
        p_all, _rw_cols(state_rwkv_shift[lyr]), state_wkv[lyr], rw_wts, prompt=False)
    y_rw_f, wkv_p, shift_p, w_rw_o, w_ssm_o, w_o = _rwkv_mix(
        p_all, jnp.zeros((BATCH, 1, RW_COLS), F32), jnp.zeros((BATCH, RW_HEADS, HEAD, HEAD), F32), rw_wts,
        prompt=True, cast_weights=[(w_rw_out[lyr], MIX_CAST_ROWS), (w_ssm_out[lyr], MIX_CAST_ROWS),
                                   (w_out[lyr], MIX_CAST_ROWS)])

    head_of_lane = jnp.arange(SSM_DIM) // HEAD
    expand = (jnp.arange(DT_COLS)[:, None] == head_of_lane[None, :]).astype(BF16)
    ssd_wts = [
        conv_w[lyr], row(conv_b), _pad_cols(row(dt_bias), DT_COLS), _pad_cols(row(a_log), DT_COLS),
        jnp.repeat(d_skip[lyr], HEAD).reshape(1, SSM_DIM), row(ssm_norm_w),
        expand,
    ]
    hist_rows = lambda t: jnp.pad(t, ((0, 0), (CONV_HIST - (CONV_W - 1), 0), (0, 0)))
    y_ssm_s, ssm_s, conv_s = _ssd_mix(
        p_all, hist_rows(state_conv[lyr]), state_ssm[lyr], ssd_wts, prompt=False)
    y_ssm_f, ssm_p, conv_p, gu2, dn2 = _ssd_mix(
        p_all, jnp.zeros((BATCH, CONV_HIST, CONV_DIM), F32),
        jnp.zeros((BATCH, SSM_HEADS, HEAD, SSM_STATE), F32), ssd_wts, prompt=True,
        cast_weights=[(ffn2_gu[lyr], MIX_CAST_ROWS // 4), (ffn2_dn[lyr], MIX_CAST_ROWS)])

    flat = lambda t: t.reshape(-1, t.shape[-1])
    x2 = _merge(x1, flat(y_rw_f), flat(y_rw_s), flat(y_ssm_f), flat(y_ssm_s), p_all, row(b_gate),
                w_rw_o, w_ssm_o, w_o, row(ln2_g), row(ln2_b))
    y_frames, y_extra = _ffn_ln_out(x2, gu2, dn2, row(ln3_g), row(ln3_b))

    y_prompt = y_frames.reshape(BATCH, SEQ, D_MODEL)
    y_sample = y_extra[:S_ROWS].reshape(DEC_BATCH, DEC_SEQ, D_MODEL)
    conv_of = lambda t: t[:, CONV_HIST - (CONV_W - 1):][None]
    return (y_prompt, y_sample,
            _rw_cols_inv(shift_p)[None], wkv_p[None], conv_of(conv_p), ssm_p[None],
            _rw_cols_inv(shift_s)[None], wkv_s[None], conv_of(conv_s), ssm_s[None])
```

```python
import functools
import math

import jax
import jax.numpy as jnp
from jax import lax
from jax.experimental import pallas as pl
from jax.experimental.pallas import tpu as pltpu

F32 = jnp.float32
BF16 = jnp.bfloat16

D_MODEL = 2048
BATCH = 4
SEQ = 2048
DEC_BATCH = 16
DEC_SEQ = 16
CHUNK = 64
N_META = 16
PAD = (-N_META) % CHUNK
HEAD_ROWS = PAD + N_META
SEQ_CHUNKS = (HEAD_ROWS + SEQ) // CHUNK
F_ROWS = BATCH * SEQ
S_ROWS = DEC_BATCH * DEC_SEQ
ROW_S0 = F_ROWS
ROW_H0 = F_ROWS + S_ROWS
X_ROWS = S_ROWS + BATCH * HEAD_ROWS
M_ROWS = F_ROWS + X_ROWS

HEAD = 64
RW_DIM = D_MODEL // 2
RW_HEADS = RW_DIM // HEAD
RW_DECAY_LORA = 96
RW_AAA_LORA = 96
RW_GATE_LORA = 256
RW_SHIFT_COLS = 3 * RW_DIM + RW_DECAY_LORA + RW_AAA_LORA + RW_GATE_LORA
RW_GN_EPS = 64e-5
SSM_DIM = D_MODEL
SSM_HEADS = SSM_DIM // HEAD
SSM_GROUPS = 4
SSM_HPG = SSM_HEADS // SSM_GROUPS
SSM_STATE = 128
CONV_W = 4
BC_DIM = SSM_GROUPS * SSM_STATE
CONV_DIM = SSM_DIM + 2 * BC_DIM
RMS_EPS = 1e-5
D_FF = 5632
LN_EPS = 1e-5
DEPTH = 1
ALPHA = (2 * DEPTH) ** 0.25

LANES = 128
SUBLANES = 8
VMEM_LIMIT = 56 * 1024 * 1024

LORA_PAD = LANES
RW_OFF_WD = 3 * RW_DIM
RW_OFF_AD = RW_OFF_WD + LORA_PAD
RW_OFF_GD = RW_OFF_AD + LORA_PAD
RW_COLS = RW_OFF_GD + RW_GATE_LORA
DT_COLS = LANES
CONV_HIST = SUBLANES

MIX_CAST_ROWS = 128
FFN_TM = 512
FFN_TF = 512
MM_TM = 512
MM_TN = 1024
MERGE_TM = 256

DT_SLOT = 512
PC_RW = 0
PC_DT = PC_RW + RW_COLS
PC_GATE = PC_DT + DT_SLOT
PC_Z = PC_GATE + 2 * D_MODEL
PC_XS = PC_Z + SSM_DIM
PC_BC = PC_XS + SSM_DIM
P_COLS = PC_BC + 2 * BC_DIM
assert PC_DT % DT_COLS == 0 and PC_GATE % D_MODEL == 0 and PC_Z % SSM_DIM == 0
assert PC_XS % SSM_DIM == 0 and PC_BC % (2 * BC_DIM) == 0 and P_COLS % MM_TN == 0
assert F_ROWS % FFN_TM == 0 and X_ROWS == FFN_TM and M_ROWS % MM_TM == 0 and ROW_H0 % MERGE_TM == 0


def _dot(a, b):
    return jnp.dot(a, b, preferred_element_type=F32)


def _dot_nt(a, b):
    return lax.dot_general(a, b, (((1,), (1,)), ((), ())), preferred_element_type=F32)


def _dot_tn(a, b):
    return lax.dot_general(a, b, (((0,), (0,)), ((), ())), preferred_element_type=F32)


def _split2(x):
    hi = x.astype(BF16)
    lo = (x - hi.astype(F32)).astype(BF16)
    return hi, lo


def _dot_sel_r(x, sel):
    hi, lo = _split2(x)
    return _dot(hi, sel) + _dot(lo, sel)


def _dot_sel_l(sel, x):
    hi, lo = _split2(x)
    return _dot(sel, hi) + _dot(sel, lo)


def _causal_blocks(n, blk):
    i = lax.broadcasted_iota(jnp.int32, (n, n), 0)
    j = lax.broadcasted_iota(jnp.int32, (n, n), 1)
    keep = jnp.logical_and(i >= j, jnp.bitwise_xor(i, j) < blk)
    return jnp.where(keep, 1.0, 0.0).astype(BF16)


def _sigmoid(x):
    return jax.nn.sigmoid(x)


def _softplus(x):
    return jnp.maximum(x, 0.0) + jnp.log1p(jnp.exp(-jnp.abs(x)))


def _layer_norm(s, g, b):
    mu = jnp.mean(s, axis=-1, keepdims=True)
    d = s - mu
    var = jnp.mean(d * d, axis=-1, keepdims=True)
    return d * lax.rsqrt(var + LN_EPS) * g + b


class _CastSlot:
    def __init__(self, w, rows, t0, col_block=None):
        assert w.ndim == 2 and w.shape[0] % rows == 0 and (col_block is None or w.shape[1] % col_block == 0)
        self.w, self.rows, self.t0, self.n, self.col_block = w, rows, t0, w.shape[0] // rows, col_block

    def _block_of(self, step_of):
        return lambda *ids: jnp.clip(step_of(*ids) - self.t0, 0, self.n - 1)

    def in_spec(self, step_of):
        k = self._block_of(step_of)
        return pl.BlockSpec((self.rows, self.w.shape[1]), lambda *ids: (k(*ids), 0))

    def out_spec(self, step_of):
        k = self._block_of(step_of)
        if self.col_block is None:
            return pl.BlockSpec((self.rows, self.w.shape[1]), lambda *ids: (k(*ids), 0))
        n_cb = self.w.shape[1] // self.col_block
        return pl.BlockSpec((n_cb, self.rows, self.col_block), lambda *ids: (0, k(*ids), 0))

    def out_shape(self):
        if self.col_block is None:
            return jax.ShapeDtypeStruct(self.w.shape, BF16)
        return jax.ShapeDtypeStruct((self.w.shape[1] // self.col_block, self.w.shape[0], self.col_block), BF16)


def _run_cast_slots(t, slots, in_refs, out_refs):
    for slot, i_ref, o_ref in zip(slots, in_refs, out_refs, strict=True):
        @pl.when(jnp.logical_and(t >= slot.t0, t < slot.t0 + slot.n))
        def _(slot=slot, i_ref=i_ref, o_ref=o_ref):
            if slot.col_block is None:
                o_ref[...] = i_ref[...].astype(BF16)
            else:
                cb = slot.col_block
                for k in range(slot.w.shape[1] // cb):
                    o_ref[k] = i_ref[:, k * cb:(k + 1) * cb].astype(BF16)


def _split_refs(refs, *counts):
    groups, at = [], 0
    for n in counts:
        groups.append(refs[at:at + n])
        at += n
    return (*groups, refs[at:])


def _chain_slots(weights, t0=0):
    slots = []
    for w, rows, *col_block in weights:
        slots.append(_CastSlot(w, rows, t0, *col_block))
        t0 += slots[-1].n
    return slots, t0


N_MAIN_TILES = F_ROWS // FFN_TM


N_TILES = M_ROWS // FFN_TM


def _ffn_body(load_x, emit_mid, emit_last, wg_ref, wu_ref, wd_ref, g_ref, b_ref, xb_ref, acc_ref, s_ref):
    i, f = pl.program_id(0), pl.program_id(1)
    last = pl.num_programs(1) - 1
    real = i < N_TILES
    both = jnp.logical_and

    def mlp(xb):
        gate = _dot(xb, wg_ref[...])
        up = _dot(xb, wu_ref[...])
        return _dot((gate * _sigmoid(gate) * up).astype(BF16), wd_ref[...])

    def finish_prev(emit):
        emit(_layer_norm(s_ref[...], g_ref[...], b_ref[...]))

    def first_step(with_prev):
        xb = load_x().astype(BF16)
        xb_ref[...] = xb
        acc_ref[...] = mlp(xb)
        if with_prev:
            finish_prev(emit_mid)

    pl.when(both(f == 0, i == 0))(functools.partial(first_step, False))
    pl.when(both(f == 0, both(i > 0, real)))(functools.partial(first_step, True))
    pl.when(both(f == 0, i == N_TILES))(functools.partial(finish_prev, emit_last))

    @pl.when(both(real, both(f > 0, f < last)))
    def _():
        acc_ref[...] += mlp(xb_ref[...])

    @pl.when(both(real, f == last))
    def _():
        s_ref[...] = ALPHA * load_x() + 0.5 * (acc_ref[...] + mlp(xb_ref[...]))


def _ffn_in_kernel(xm_ref, xe_ref, wg_ref, wu_ref, wd_ref, g_ref, b_ref, o_ref, ob_ref, *scratch):
    load_x = lambda: jnp.where(pl.program_id(0) < N_MAIN_TILES, xm_ref[...], xe_ref[...])

    def emit(y):
        o_ref[...] = y
        ob_ref[...] = y.astype(BF16)

    _ffn_body(load_x, emit, emit, wg_ref, wu_ref, wd_ref, g_ref, b_ref, *scratch)


def _ffn_out_kernel(x_ref, wg_ref, wu_ref, wd_ref, g_ref, b_ref, om_ref, oe_ref, *scratch):
    assert N_TILES == N_MAIN_TILES + 1

    def emit_frames(y):
        om_ref[...] = y

    def emit_extra(y):
        oe_ref[...] = y

    _ffn_body(lambda: x_ref[...], emit_frames, emit_extra, wg_ref, wu_ref, wd_ref, g_ref, b_ref, *scratch)


def _ffn_specs(d):
    nf = D_FF // FFN_TF
    step = lambda i, f: jnp.where(i == N_TILES, nf - 1, f)
    return nf, [
        pl.BlockSpec((None, d, FFN_TF), lambda i, f: (step(i, f), 0, 0)),
        pl.BlockSpec((None, d, FFN_TF), lambda i, f: (step(i, f) + nf, 0, 0)),
        pl.BlockSpec((FFN_TF, d), lambda i, f: (step(i, f), 0)),
        pl.BlockSpec((1, d), lambda i, f: (0, 0)),
        pl.BlockSpec((1, d), lambda i, f: (0, 0)),
    ]


_FFN_PARAMS = dict(dimension_semantics=("arbitrary", "arbitrary"), vmem_limit_bytes=VMEM_LIMIT)
_in_frame_tile = lambda i, f: (jnp.minimum(i, N_MAIN_TILES - 1), 0)
_in_any_tile = lambda i, f: (jnp.minimum(i, N_TILES - 1), 0)
_out_frame_tile = lambda i, f: (jnp.clip(i - 1, 0, N_MAIN_TILES - 1), 0)
_out_any_tile = lambda i, f: (jnp.maximum(i - 1, 0), 0)
_only_tile = lambda i, f: (0, 0)


def _ffn_scratch(tile):
    return [pltpu.VMEM(tile, BF16), pltpu.VMEM(tile, F32), pltpu.VMEM(tile, F32)]


def _ffn_ln_in(x_main, x_extra, w_gu, w_dn, ln_g, ln_b):
    d = x_main.shape[1]
    nf, wspecs = _ffn_specs(d)
    tile = (FFN_TM, d)
    return pl.pallas_call(
        _ffn_in_kernel,
        grid=(N_TILES + 1, nf),
        in_specs=[pl.BlockSpec(tile, _in_frame_tile),
                  pl.BlockSpec(tile, _only_tile, pipeline_mode=pl.Buffered(1))] + wspecs,
        out_specs=[pl.BlockSpec(tile, _out_any_tile), pl.BlockSpec(tile, _out_any_tile)],
        out_shape=[jax.ShapeDtypeStruct((M_ROWS, d), F32), jax.ShapeDtypeStruct((M_ROWS, d), BF16)],
        scratch_shapes=_ffn_scratch(tile),
        compiler_params=pltpu.CompilerParams(**_FFN_PARAMS),
        name="ffn_ln_in",
    )(x_main, x_extra, w_gu, w_gu, w_dn, ln_g, ln_b)


def _ffn_ln_out(x, w_gu, w_dn, ln_g, ln_b):
    d = x.shape[1]
    nf, wspecs = _ffn_specs(d)
    tile = (FFN_TM, d)
    return pl.pallas_call(
        _ffn_out_kernel,
        grid=(N_TILES + 1, nf),
        in_specs=[pl.BlockSpec(tile, _in_any_tile)] + wspecs,
        out_specs=[pl.BlockSpec(tile, _out_frame_tile), pl.BlockSpec(tile, _only_tile)],
        out_shape=[jax.ShapeDtypeStruct((F_ROWS, d), F32), jax.ShapeDtypeStruct((X_ROWS, d), F32)],
        scratch_shapes=_ffn_scratch(tile),
        compiler_params=pltpu.CompilerParams(**_FFN_PARAMS),
        name="ffn_ln_out",
    )(x, w_gu, w_gu, w_dn, ln_g, ln_b)


def _mm_kernel(x_ref, w_ref, o_ref):
    o_ref[...] = _dot(x_ref[...], w_ref[...])


def _project(x, w):
    m, k = x.shape
    n = w.shape[1]
    return pl.pallas_call(
        _mm_kernel,
        grid=(n // MM_TN, m // MM_TM),
        in_specs=[pl.BlockSpec((MM_TM, k), lambda j, i: (i, 0)),
                  pl.BlockSpec((k, MM_TN), lambda j, i: (0, j))],
        out_specs=pl.BlockSpec((MM_TM, MM_TN), lambda j, i: (i, j)),
        out_shape=jax.ShapeDtypeStruct((m, n), F32),
        compiler_params=pltpu.CompilerParams(
            dimension_semantics=("parallel", "parallel"), vmem_limit_bytes=VMEM_LIMIT),
        name="proj_in",
    )(x, w)


def _rwkv_kernel(*refs, chunk, pad, n_seq, n_p, slots):
    p_refs, fixed, cast_in, outs, cast_out, (yacc_ref,) = _split_refs(refs, n_p, 14, len(slots), 3, len(slots))
    (hist_ref, s0_ref, mu_ref, w0_ref, w2_ref, a0_ref, a2_ref, g2_ref, kk_ref, ka_ref, rk_ref,
     gnw_ref, gnb_ref, seg_ref) = fixed
    y_ref, s_ref, shift_ref = outs
    c = pl.program_id(1)
    _run_cast_slots(pl.program_id(0) * pl.num_programs(1) + c, slots, cast_in, cast_out)
    n_rows = n_seq * chunk
    seqs = range(n_seq)
    rsl = [slice(q * chunk, (q + 1) * chunk) for q in seqs]
    rows = lax.broadcasted_iota(jnp.int32, (n_rows, 1), 0)

    @pl.when(c == 0)
    def _():
        shift_ref[...] = hist_ref[...]
        s_ref[...] = s0_ref[...]

    p = jnp.concatenate([r[...] for r in p_refs], axis=0)
    if pad:
        p = jnp.where(jnp.logical_and(c == 0, jnp.bitwise_and(rows, chunk - 1) < pad), 0.0, p)
    prev = pltpu.roll(p, 1, 0)
    for q in seqs:
        prev = jnp.where(rows == q * chunk, shift_ref[q], prev)
    for q in seqs:
        shift_ref[q] = p[(q + 1) * chunk - 1:(q + 1) * chunk, :]
    ps = p + (prev - p) * mu_ref[...]

    r = ps[:, 0:RW_DIM]
    k = ps[:, RW_DIM:2 * RW_DIM]
    v = ps[:, 2 * RW_DIM:3 * RW_DIM]
    wd = ps[:, RW_OFF_WD:RW_OFF_WD + LORA_PAD]
    ad = ps[:, RW_OFF_AD:RW_OFF_AD + LORA_PAD]
    gd = ps[:, RW_OFF_GD:RW_OFF_GD + RW_GATE_LORA]

    lw = -math.exp(-0.5) * _sigmoid(w0_ref[...] + _dot(jnp.tanh(wd).astype(BF16), w2_ref[...]))
    a = _sigmoid(a0_ref[...] + _dot(ad.astype(BF16), a2_ref[...]))
    g = _dot(_sigmoid(gd).astype(BF16), g2_ref[...])

    seg = seg_ref[...]

    def head_sum(x):
        w = seg.shape[0]
        return jnp.concatenate(
            [_dot_sel_r(x[:, j * w:(j + 1) * w], seg) for j in range(RW_DIM // w)], axis=1)

    kk = k * kk_ref[...]
    kk = kk * lax.rsqrt(jnp.maximum(head_sum(kk * kk), 1e-24))
    k = k * (1.0 + (a - 1.0) * ka_ref[...])
    b_neg = -(kk * a)

    cum = _dot_sel_l(_causal_blocks(n_rows, chunk), lw)
    cum_last = [cum[(q + 1) * chunk - 1:(q + 1) * chunk, :] for q in seqs]
    cum_end = jnp.concatenate([jnp.broadcast_to(x, (chunk, RW_DIM)) for x in cum_last], axis=0)
    e_neg = jnp.exp(-cum)
    kt = (kk * jnp.exp(cum - lw)).astype(BF16)
    rt = (r * jnp.exp(cum)).astype(BF16)
    kd = (k * e_neg).astype(BF16)
    bd_neg = (b_neg * e_neg).astype(BF16)
    e_end = jnp.exp(cum_end - cum)
    k_end = (k * e_end).astype(BF16)
    b_end_neg = (b_neg * e_end).astype(BF16)
    p_end = [jnp.exp(x) for x in cum_last]
    vb = v.astype(BF16)

    ri = lax.broadcasted_iota(jnp.int32, (2 * chunk, 2 * chunk), 0)
    rj = lax.broadcasted_iota(jnp.int32, (2 * chunk, 2 * chunk), 1)
    bi = jnp.where(ri >= chunk, ri - chunk + 1, ri)
    bj = jnp.where(rj >= chunk, rj - chunk, rj)
    keep = bi > bj
    ti = lax.broadcasted_iota(jnp.int32, (chunk, 2 * chunk), 0)
    tj = lax.broadcasted_iota(jnp.int32, (chunk, 2 * chunk), 1)
    hi_lane = tj >= chunk
    eye_hi = (tj == ti + chunk).astype(F32)
    zeros_v = jnp.zeros((chunk, HEAD), BF16)

    pairs = [(q, h) for q in seqs for h in range(RW_HEADS)]
    sls = [(rsl[q], slice(h * HEAD, (h + 1) * HEAD)) for q, h in pairs]
    s_old = [s_ref[q, h] for q, h in pairs]
    s_b = [s.astype(BF16) for s in s_old]
    amats = [jnp.where(keep, _dot_nt(jnp.concatenate([kt[sl], rt[sl]], axis=0),
                                     jnp.concatenate([bd_neg[sl], kd[sl]], axis=0)), 0.0)
             for sl in sls]
    top = [m[:chunk] for m in amats]
    bot = [m[chunk:].astype(BF16) for m in amats]
    top_b = [x.astype(BF16) for x in top]
    tq = [_dot(xb[:, :chunk], jnp.where(hi_lane, eye_hi, x).astype(BF16)) + eye_hi
          for x, xb in zip(top, top_b)]
    n = 2
    while n < chunk:
        tq = [jnp.where(hi_lane, x, 0.0) + _dot(x[:, :chunk].astype(BF16), x.astype(BF16)) for x in tq]
        n *= 2
    t_b = [x[:, chunk:].astype(BF16) for x in tq]
    v_h = [vb[sl] for sl in sls]
    rhs = [_dot_nt(kt[sl], sb) + _dot(xb, jnp.concatenate([zeros_v, vh], axis=0))
           for sl, sb, xb, vh in zip(sls, s_b, top_b, v_h)]
    ub = [_dot(t, x.astype(BF16)).astype(BF16) for t, x in zip(t_b, rhs)]
    uv = [jnp.concatenate([u, vh], axis=0) for u, vh in zip(ub, v_h)]
    for i, (q, h) in enumerate(pairs):
        sl = sls[i]
        yacc_ref[sl] = _dot_nt(rt[sl], s_b[i]) + _dot(bot[i], uv[i])
        s_ref[q, h] = s_old[i] * p_end[q][:, sl[1]] + _dot_tn(
            uv[i], jnp.concatenate([b_end_neg[sl], k_end[sl]], axis=0))

    y = yacc_ref[...]
    inv_n = 1.0 / HEAD
    mean = head_sum(y) * inv_n
    yc = y - mean
    var = head_sum(yc * yc) * inv_n
    yn = yc * lax.rsqrt(var + RW_GN_EPS) * gnw_ref[...] + gnb_ref[...]
    bonus = head_sum(r * k * rk_ref[...]) * v
    out = ((yn + bonus) * g).astype(y_ref.dtype)
    for q in seqs:
        y_ref[q] = out[rsl[q]]


RWKV_GROUPS = (2, 8)
SSD_GROUPS = (1, 1)


class _SeqPlan:
    def __init__(self, prompt, groups):
        self.prompt = prompt
        if prompt:
            self.n_seq, self.chunk, self.pad, self.total, self.seq_len = groups[0], CHUNK, PAD, BATCH, SEQ
            self.grid = (BATCH // self.n_seq, SEQ_CHUNKS)
        else:
            self.n_seq, self.chunk, self.pad, self.total, self.seq_len = groups[1], DEC_SEQ, 0, DEC_BATCH, DEC_SEQ
            self.grid = (DEC_BATCH // self.n_seq, 1)

    def p_specs(self, width, col):
        cb = col // width
        if not self.prompt:
            rows = self.n_seq * self.chunk
            return [pl.BlockSpec((rows, width), lambda g, c: (ROW_S0 // rows + g, cb))]

        def spec(q):
            def index(g, c):
                seq = g * self.n_seq + q
                return jnp.where(c == 0, ROW_H0 // CHUNK + seq, seq * (SEQ // CHUNK) + c - 1), cb
            return pl.BlockSpec((CHUNK, width), index)
        return [spec(q) for q in range(self.n_seq)]

    def y_spec(self, dim):
        return pl.BlockSpec((self.n_seq, self.chunk, dim), lambda g, c: (g, jnp.maximum(c - 1, 0), 0))

    def y_shape(self, dim):
        return jax.ShapeDtypeStruct((self.total, self.seq_len, dim), BF16)

    def state_spec(self, shape):
        nd = len(shape)
        return pl.BlockSpec((self.n_seq,) + tuple(shape), lambda g, c: (g,) + (0,) * nd)

    def cast_slots(self, cast_weights):
        slots, t_end = _chain_slots(cast_weights)
        assert t_end <= self.grid[0] * self.grid[1]
        step_of = lambda g, c: g * self.grid[1] + c
        return slots, [s.in_spec(step_of) for s in slots], [s.out_spec(step_of) for s in slots]


_MIX_PARAMS = dict(dimension_semantics=("arbitrary", "arbitrary"), vmem_limit_bytes=VMEM_LIMIT)


def _rwkv_mix(p_all, hist, s0, wts, *, prompt, cast_weights=()):
    plan = _SeqPlan(prompt, RWKV_GROUPS)
    slots, cast_in_specs, cast_out_specs = plan.cast_slots(cast_weights)
    p_specs = plan.p_specs(RW_COLS, PC_RW)
    const2 = lambda g, c: (0, 0)
    wspecs = [pl.BlockSpec(w.shape, const2) for w in wts]
    shift_spec = plan.state_spec((1, RW_COLS))
    wkv_spec = plan.state_spec((RW_HEADS, HEAD, HEAD))
    return pl.pallas_call(
        functools.partial(_rwkv_kernel, chunk=plan.chunk, pad=plan.pad, n_seq=plan.n_seq, n_p=len(p_specs),
                          slots=slots),
        grid=plan.grid,
        in_specs=p_specs + [shift_spec, wkv_spec] + wspecs + cast_in_specs,
        out_specs=[plan.y_spec(RW_DIM), wkv_spec, shift_spec] + cast_out_specs,
        out_shape=[plan.y_shape(RW_DIM),
                   jax.ShapeDtypeStruct((plan.total, RW_HEADS, HEAD, HEAD), F32),
                   jax.ShapeDtypeStruct((plan.total, 1, RW_COLS), F32)] + [s.out_shape() for s in slots],
        scratch_shapes=[pltpu.VMEM((plan.n_seq * plan.chunk, RW_DIM), F32)],
        compiler_params=pltpu.CompilerParams(**_MIX_PARAMS),
        name="rwkv_mix_c%d" % plan.chunk,
    )(*([p_all] * len(p_specs)), hist, s0, *wts, *[s.w for s in slots])


def _ssd_kernel(*refs, chunk, pad, n_seq, n_p, slots):
    (xs_refs, bc_refs, z_refs, dt_refs, fixed, cast_in, outs, cast_out, (xpad_ref, yacc_ref)) = _split_refs(
        refs, n_p, n_p, n_p, n_p, 9, len(slots), 3, len(slots))
    hist_ref, h0_ref, cw_ref, cb_ref, dtb_ref, alog_ref, dskip_ref, nw_ref, exp_ref = fixed
    y_ref, h_ref, tail_ref = outs
    c = pl.program_id(1)
    _run_cast_slots(pl.program_id(0) * pl.num_programs(1) + c, slots, cast_in, cast_out)
    n_rows = n_seq * chunk
    seqs = range(n_seq)
    rsl = [slice(q * chunk, (q + 1) * chunk) for q in seqs]
    rows = lax.broadcasted_iota(jnp.int32, (n_rows, 1), 0)
    stack = lambda rs: jnp.concatenate([r[...] for r in rs], axis=0)

    @pl.when(c == 0)
    def _():
        xpad_ref[:, 0:CONV_HIST, :] = hist_ref[...]
        h_ref[...] = h0_ref[...]

    u = jnp.concatenate([stack(xs_refs), stack(bc_refs)], axis=1)
    if pad:
        is_pad = jnp.logical_and(c == 0, jnp.bitwise_and(rows, chunk - 1) < pad)
        u = jnp.where(is_pad, 0.0, u)
    convs = []
    for q in seqs:
        u_q = u[rsl[q]]
        xpad_ref[q, CONV_HIST:CONV_HIST + chunk, :] = u_q
        conv = cb_ref[...] + u_q * cw_ref[CONV_W - 1:CONV_W, :]
        for i in range(CONV_W - 1):
            back = CONV_W - 1 - i
            conv = conv + xpad_ref[q, CONV_HIST - back:CONV_HIST - back + chunk, :] * cw_ref[i:i + 1, :]
        hist_next = xpad_ref[q, chunk:chunk + CONV_HIST, :]
        xpad_ref[q, 0:CONV_HIST, :] = hist_next
        tail_ref[q] = hist_next
        convs.append(conv)
    conv = jnp.concatenate(convs, axis=0)
    xbc = conv * _sigmoid(conv)
    xs = xbc[:, 0:SSM_DIM]
    bm = xbc[:, SSM_DIM:SSM_DIM + BC_DIM].astype(BF16)
    cm = xbc[:, SSM_DIM + BC_DIM:CONV_DIM].astype(BF16)

    dt = _softplus(stack(dt_refs) + dtb_ref[...])
    if pad:
        dt = jnp.where(is_pad, 0.0, dt)
    da = dt * (-jnp.exp(alog_ref[...]))
    ci = lax.broadcasted_iota(jnp.int32, (chunk, chunk), 0)
    cj = lax.broadcasted_iota(jnp.int32, (chunk, chunk), 1)
    causal = ci >= cj
    acs = _dot_sel_l(_causal_blocks(n_rows, chunk), da)
    acs_t = acs.T
    acs_end = jnp.concatenate(
        [jnp.broadcast_to(acs[(q + 1) * chunk - 1:(q + 1) * chunk, :], (chunk, DT_COLS)) for q in seqs], axis=0)

    expand = exp_ref[...]
    xdt = xs * _dot_sel_r(dt, expand)
    xdt_b = xdt.astype(BF16)
    xdt_end = (xdt * _dot_sel_r(jnp.exp(acs_end - acs), expand)).astype(BF16)
    e_acs = _dot_sel_r(jnp.exp(acs), expand)
    decay_end = [jnp.exp(acs_t[:, (q + 1) * chunk - 1:(q + 1) * chunk]) for q in seqs]

    gw = SSM_HPG * HEAD
    groups = range(SSM_GROUPS)
    heads = range(SSM_HEADS)
    qg = [(q, g) for q in seqs for g in groups]
    qh = [(q, hd) for q in seqs for hd in heads]
    bm_g = {(q, g): bm[rsl[q], g * SSM_STATE:(g + 1) * SSM_STATE] for q, g in qg}
    cm_g = {(q, g): cm[rsl[q], g * SSM_STATE:(g + 1) * SSM_STATE] for q, g in qg}
    h_old = {(q, hd): h_ref[q, hd] for q, hd in qh}
    cb = {k: _dot_nt(cm_g[k], bm_g[k]) for k in qg}
    y_off = {(q, g): _dot_nt(cm_g[q, g], jnp.concatenate(
        [h_old[q, g * SSM_HPG + hh].astype(BF16) for hh in range(SSM_HPG)], axis=0)) for q, g in qg}
    st = {(q, g): _dot_tn(xdt_end[rsl[q], g * gw:(g + 1) * gw], bm_g[q, g]) for q, g in qg}
    for q, hd in qh:
        sl = slice(hd * HEAD, (hd + 1) * HEAD)
        seg = acs[rsl[q], hd:hd + 1] - acs_t[hd:hd + 1, rsl[q]]
        lmat = jnp.exp(jnp.where(causal, seg, -jnp.inf))
        yacc_ref[rsl[q], sl] = _dot((cb[q, hd // SSM_HPG] * lmat).astype(BF16), xdt_b[rsl[q], sl])
    for q, hd in qh:
        g, hh = divmod(hd, SSM_HPG)
        h_ref[q, hd] = h_old[q, hd] * decay_end[q][hd:hd + 1, :] + st[q, g][hh * HEAD:(hh + 1) * HEAD, :]

    y_off_all = jnp.concatenate(
        [jnp.concatenate([y_off[q, g] for g in groups], axis=1) for q in seqs], axis=0)
    y = yacc_ref[...] + y_off_all * e_acs + xs * dskip_ref[...]
    zz = stack(z_refs)
    y = y * (zz * _sigmoid(zz))
    parts = []
    for g in groups:
        yg = y[:, g * gw:(g + 1) * gw]
        ms = jnp.mean(yg * yg, axis=-1, keepdims=True)
        parts.append(yg * lax.rsqrt(ms + RMS_EPS))
    out = (jnp.concatenate(parts, axis=1) * nw_ref[...]).astype(y_ref.dtype)
    for q in seqs:
        y_ref[q] = out[rsl[q]]


def _ssd_mix(p_all, hist, h0, wts, *, prompt, cast_weights=()):
    plan = _SeqPlan(prompt, SSD_GROUPS)
    slots, cast_in_specs, cast_out_specs = plan.cast_slots(cast_weights)
    p_specs = (plan.p_specs(SSM_DIM, PC_XS) + plan.p_specs(2 * BC_DIM, PC_BC)
               + plan.p_specs(SSM_DIM, PC_Z) + plan.p_specs(DT_COLS, PC_DT))
    const2 = lambda g, c: (0, 0)
    wspecs = [pl.BlockSpec(w.shape, const2) for w in wts]
    conv_spec = plan.state_spec((CONV_HIST, CONV_DIM))
    ssm_spec = plan.state_spec((SSM_HEADS, HEAD, SSM_STATE))
    return pl.pallas_call(
        functools.partial(_ssd_kernel, chunk=plan.chunk, pad=plan.pad, n_seq=plan.n_seq, n_p=len(p_specs) // 4,
                          slots=slots),
        grid=plan.grid,
        in_specs=p_specs + [conv_spec, ssm_spec] + wspecs + cast_in_specs,
        out_specs=[plan.y_spec(SSM_DIM), ssm_spec, conv_spec] + cast_out_specs,
        out_shape=[plan.y_shape(SSM_DIM),
                   jax.ShapeDtypeStruct((plan.total, SSM_HEADS, HEAD, SSM_STATE), F32),
                   jax.ShapeDtypeStruct((plan.total, CONV_HIST, CONV_DIM), F32)] + [s.out_shape() for s in slots],
        scratch_shapes=[pltpu.VMEM((plan.n_seq, CONV_HIST + plan.chunk, CONV_DIM), F32),
                        pltpu.VMEM((plan.n_seq * plan.chunk, SSM_DIM), F32)],
        compiler_params=pltpu.CompilerParams(**_MIX_PARAMS),
        name="ssd_mix_c%d" % plan.chunk,
    )(*([p_all] * len(p_specs)), hist, h0, *wts, *[s.w for s in slots])


MERGE_F_TILES = F_ROWS // MERGE_TM
assert S_ROWS == MERGE_TM


def _merge_kernel(x_ref, yrw_f_ref, yrw_s_ref, yssm_f_ref, yssm_s_ref, ga_ref, gb_ref, bga_ref, bgb_ref,
                  wrw_ref, wssm_ref, wout_ref, g_ref, b_ref, o_ref):
    is_frame = pl.program_id(0) < MERGE_F_TILES
    y_rw = jnp.where(is_frame, yrw_f_ref[...], yrw_s_ref[...])
    y_ssm = jnp.where(is_frame, yssm_f_ref[...], yssm_s_ref[...])
    merged = (_sigmoid(ga_ref[...] + bga_ref[...]) * _dot(y_rw, wrw_ref[...])
              + _sigmoid(gb_ref[...] + bgb_ref[...]) * _dot(y_ssm, wssm_ref[...]))
    s = ALPHA * x_ref[...] + _dot(merged.astype(BF16), wout_ref[...])
    o_ref[...] = _layer_norm(s, g_ref[...], b_ref[...])


def _merge(x, y_rw_f, y_rw_s, y_ssm_f, y_ssm_s, p_all, b_gate, w_rw_out, w_ssm_out, w_out, ln_g, ln_b):
    m, d = x.shape
    tm = MERGE_TM
    row = lambda i: (i, 0)
    frame_row = lambda i: (jnp.minimum(i, MERGE_F_TILES - 1), 0)
    const = lambda i: (0, 0)
    resident = lambda w: pl.BlockSpec(w.shape, const, pipeline_mode=pl.Buffered(1))
    return pl.pallas_call(
        _merge_kernel,
        grid=(m // tm,),
        in_specs=[
            pl.BlockSpec((tm, d), row),
            pl.BlockSpec((tm, RW_DIM), frame_row),
            pl.BlockSpec((tm, RW_DIM), const),
            pl.BlockSpec((tm, SSM_DIM), frame_row),
            pl.BlockSpec((tm, SSM_DIM), const),
            pl.BlockSpec((tm, d), lambda i: (i, PC_GATE // d)),
            pl.BlockSpec((tm, d), lambda i: (i, PC_GATE // d + 1)),
            pl.BlockSpec((1, d), lambda i: (0, 0)),
            pl.BlockSpec((1, d), lambda i: (0, 1)),
            resident(w_rw_out), resident(w_ssm_out), resident(w_out),
            pl.BlockSpec((1, d), const),
            pl.BlockSpec((1, d), const),
        ],
        out_specs=pl.BlockSpec((tm, d), row),
        out_shape=jax.ShapeDtypeStruct((m, d), F32),
        compiler_params=pltpu.CompilerParams(
            dimension_semantics=("arbitrary",), vmem_limit_bytes=VMEM_LIMIT),
        name="merge_ln",
    )(x, y_rw_f, y_rw_s, y_ssm_f, y_ssm_s, p_all, p_all, b_gate, b_gate, w_rw_out, w_ssm_out, w_out,
      ln_g, ln_b)


def _pad_cols(x, width):
    return jnp.pad(x, [(0, 0)] * (x.ndim - 1) + [(0, width - x.shape[-1])])


def _rw_cols(x):
    o_wd, o_ad, o_gd = 3 * RW_DIM, 3 * RW_DIM + RW_DECAY_LORA, 3 * RW_DIM + RW_DECAY_LORA + RW_AAA_LORA
    return jnp.concatenate([
        x[..., :o_wd],
        _pad_cols(x[..., o_wd:o_ad], LORA_PAD),
        _pad_cols(x[..., o_ad:o_gd], LORA_PAD),
        x[..., o_gd:],
    ], axis=-1)


def _rw_cols_inv(x):
    return jnp.concatenate([
        x[..., :RW_OFF_WD],
        x[..., RW_OFF_WD:RW_OFF_WD + RW_DECAY_LORA],
        x[..., RW_OFF_AD:RW_OFF_AD + RW_AAA_LORA],
        x[..., RW_OFF_GD:],
    ], axis=-1)


def _block_ones(n, blk):
    i = jnp.arange(n) // blk
    return (i[:, None] == i[None, :]).astype(BF16)


def kernel(x_prompt, x_sample, state_rwkv_shift, state_wkv, state_conv, state_ssm, meta_tokens, ffn1_gu, ffn1_dn, ln1_g, ln1_b, w_in, b_gate, rw_mu, rw_w0, rw_w2, rw_a0, rw_a2, rw_g2, rw_kk, rw_ka, rw_rk, rw_gn_w, rw_gn_b, conv_w, conv_b, dt_bias, a_log, d_skip, ssm_norm_w, w_rw_out, w_ssm_out, w_out, ln2_g, ln2_b, ffn2_gu, ffn2_dn, ln3_g, ln3_b):
    lyr = 0
    row = lambda t: t[lyr].reshape(1, -1).astype(F32)

    head_rows = jnp.concatenate([jnp.zeros((PAD, D_MODEL), F32), meta_tokens.astype(F32)], axis=0)
    x_extra = jnp.concatenate([x_sample.reshape(S_ROWS, D_MODEL)] + [head_rows] * BATCH, axis=0)
    gu1 = ffn1_gu[lyr].astype(BF16).reshape(D_MODEL, 2 * D_FF // FFN_TF, FFN_TF).transpose(1, 0, 2)
    x1, x1b = _ffn_ln_in(x_prompt.reshape(F_ROWS, D_MODEL), x_extra, gu1,
                         ffn1_dn[lyr].astype(BF16), row(ln1_g), row(ln1_b))

    w = w_in[lyr].astype(BF16)
    o_z = RW_SHIFT_COLS
    o_xbc = o_z + SSM_DIM
    o_dt = o_xbc + CONV_DIM
    o_gate = o_dt + SSM_HEADS
    w_all = jnp.concatenate([
        _rw_cols(w[:, :o_z]), _pad_cols(w[:, o_dt:o_gate], DT_SLOT), w[:, o_gate:],
        w[:, o_z:o_xbc], w[:, o_xbc:o_dt]], axis=1)
    p_all = _project(x1b, w_all)

    pad_rows = lambda t, n: jnp.pad(t, ((0, n - t.shape[0]), (0, 0)))
    rw_wts = [
        _rw_cols(rw_mu[lyr]).reshape(1, RW_COLS), row(rw_w0),
        pad_rows(rw_w2[lyr], LORA_PAD).astype(BF16), row(rw_a0),
        pad_rows(rw_a2[lyr], LORA_PAD).astype(BF16), rw_g2[lyr].astype(BF16),
        row(rw_kk), row(rw_ka), row(rw_rk), row(rw_gn_w), row(rw_gn_b),
        _block_ones(2 * LANES, HEAD),
    ]
    y_rw_s, wkv_s, shift_s = _rwkv_mix(
        p_all, _rw_cols(state_rwkv_shift[lyr]), state_wkv[lyr], rw_wts, prompt=False)
    y_rw_f, wkv_p, shift_p, w_rw_o, w_ssm_o, w_o = _rwkv_mix(
        p_all, jnp.zeros((BATCH, 1, RW_COLS), F32), jnp.zeros((BATCH, RW_HEADS, HEAD, HEAD), F32), rw_wts,
        prompt=True, cast_weights=[(w_rw_out[lyr], MIX_CAST_ROWS), (w_ssm_out[lyr], MIX_CAST_ROWS),
                                   (w_out[lyr], MIX_CAST_ROWS)])

    head_of_lane = jnp.arange(SSM_DIM) // HEAD
    expand = (jnp.arange(DT_COLS)[:, None] == head_of_lane[None, :]).astype(BF16)
    ssd_wts = [
        conv_w[lyr], row(conv_b), _pad_cols(row(dt_bias), DT_COLS), _pad_cols(row(a_log), DT_COLS),
        jnp.repeat(d_skip[lyr], HEAD).reshape(1, SSM_DIM), row(ssm_norm_w),
        expand,
    ]
    hist_rows = lambda t: jnp.pad(t, ((0, 0), (CONV_HIST - (CONV_W - 1), 0), (0, 0)))
    y_ssm_s, ssm_s, conv_s = _ssd_mix(
        p_all, hist_rows(state_conv[lyr]), state_ssm[lyr], ssd_wts, prompt=False)
    y_ssm_f, ssm_p, conv_p, gu2, dn2 = _ssd_mix(
        p_all, jnp.zeros((BATCH, CONV_HIST, CONV_DIM), F32),
        jnp.zeros((BATCH, SSM_HEADS, HEAD, SSM_STATE), F32), ssd_wts, prompt=True,
        cast_weights=[(ffn2_gu[lyr], MIX_CAST_ROWS // 4, FFN_TF), (ffn2_dn[lyr], MIX_CAST_ROWS)])

    flat = lambda t: t.reshape(-1, t.shape[-1])
    x2 = _merge(x1, flat(y_rw_f), flat(y_rw_s), flat(y_ssm_f), flat(y_ssm_s), p_all, row(b_gate),
                w_rw_o, w_ssm_o, w_o, row(ln2_g), row(ln2_b))
    y_frames, y_extra = _ffn_ln_out(x2, gu2, dn2, row(ln3_g), row(ln3_b))

    y_prompt = y_frames.reshape(BATCH, SEQ, D_MODEL)
    y_sample = y_extra[:S_ROWS].reshape(DEC_BATCH, DEC_SEQ, D_MODEL)
    conv_of = lambda t: t[:, CONV_HIST - (CONV_W - 1):][None]
    return (y_prompt, y_sample,
            _rw_cols_inv(shift_p)[None], wkv_p[None], conv_of(conv_p), ssm_p[None],
            _rw_cols_inv(shift_s)[None], wkv_s[None], conv_of(conv_s), ssm_s[None])
```

```python
import functools
import math

import jax
import jax.numpy as jnp
from jax import lax
from jax.experimental import pallas as pl
from jax.experimental.pallas import tpu as pltpu

F32 = jnp.float32
BF16 = jnp.bfloat16

D_MODEL = 2048
BATCH = 4
SEQ = 2048
DEC_BATCH = 16
DEC_SEQ = 16
CHUNK = 64
N_META = 16
PAD = (-N_META) % CHUNK
HEAD_ROWS = PAD + N_META
SEQ_CHUNKS = (HEAD_ROWS + SEQ) // CHUNK
F_ROWS = BATCH * SEQ
S_ROWS = DEC_BATCH * DEC_SEQ
ROW_S0 = F_ROWS
ROW_H0 = F_ROWS + S_ROWS
X_ROWS = S_ROWS + BATCH * HEAD_ROWS
M_ROWS = F_ROWS + X_ROWS

HEAD = 64
RW_DIM = D_MODEL // 2
RW_HEADS = RW_DIM // HEAD
RW_DECAY_LORA = 96
RW_AAA_LORA = 96
RW_GATE_LORA = 256
RW_SHIFT_COLS = 3 * RW_DIM + RW_DECAY_LORA + RW_AAA_LORA + RW_GATE_LORA
RW_GN_EPS = 64e-5
SSM_DIM = D_MODEL
SSM_HEADS = SSM_DIM // HEAD
SSM_GROUPS = 4
SSM_HPG = SSM_HEADS // SSM_GROUPS
SSM_STATE = 128
CONV_W = 4
BC_DIM = SSM_GROUPS * SSM_STATE
CONV_DIM = SSM_DIM + 2 * BC_DIM
RMS_EPS = 1e-5
D_FF = 5632
LN_EPS = 1e-5
DEPTH = 1
ALPHA = (2 * DEPTH) ** 0.25

LANES = 128
SUBLANES = 8
VMEM_LIMIT = 56 * 1024 * 1024

LORA_PAD = LANES
RW_OFF_WD = 3 * RW_DIM
RW_OFF_AD = RW_OFF_WD + LORA_PAD
RW_OFF_GD = RW_OFF_AD + LORA_PAD
RW_COLS = RW_OFF_GD + RW_GATE_LORA
DT_COLS = LANES
CONV_HIST = SUBLANES

MIX_CAST_ROWS = 128
FFN_TM = 512
FFN_TF = 512
MM_TM = 512
MM_TN = 1024
MERGE_TM = 256

DT_SLOT = 512
PC_RW = 0
PC_DT = PC_RW + RW_COLS
PC_GATE = PC_DT + DT_SLOT
PC_Z = PC_GATE + 2 * D_MODEL
PC_XS = PC_Z + SSM_DIM
PC_BC = PC_XS + SSM_DIM
P_COLS = PC_BC + 2 * BC_DIM
assert PC_DT % DT_COLS == 0 and PC_GATE % D_MODEL == 0 and PC_Z % SSM_DIM == 0
assert PC_XS % SSM_DIM == 0 and PC_BC % (2 * BC_DIM) == 0 and P_COLS % MM_TN == 0
assert F_ROWS % FFN_TM == 0 and X_ROWS == FFN_TM and M_ROWS % MM_TM == 0 and ROW_H0 % MERGE_TM == 0


def _dot(a, b):
    return jnp.dot(a, b, preferred_element_type=F32)


def _dot_nt(a, b):
    return lax.dot_general(a, b, (((1,), (1,)), ((), ())), preferred_element_type=F32)


def _dot_tn(a, b):
    return lax.dot_general(a, b, (((0,), (0,)), ((), ())), preferred_element_type=F32)


def _split2(x):
    hi = x.astype(BF16)
    lo = (x - hi.astype(F32)).astype(BF16)
    return hi, lo


def _dot_sel_r(x, sel):
    hi, lo = _split2(x)
    return _dot(hi, sel) + _dot(lo, sel)


def _dot_sel_l(sel, x):
    hi, lo = _split2(x)
    return _dot(sel, hi) + _dot(sel, lo)


def _causal_blocks(n, blk):
    i = lax.broadcasted_iota(jnp.int32, (n, n), 0)
    j = lax.broadcasted_iota(jnp.int32, (n, n), 1)
    keep = jnp.logical_and(i >= j, jnp.bitwise_xor(i, j) < blk)
    return jnp.where(keep, 1.0, 0.0).astype(BF16)


def _sigmoid(x):
    return jax.nn.sigmoid(x)


def _softplus(x):
    return jnp.maximum(x, 0.0) + jnp.log1p(jnp.exp(-jnp.abs(x)))


def _layer_norm(s, g, b):
    mu = jnp.mean(s, axis=-1, keepdims=True)
    d = s - mu
    var = jnp.mean(d * d, axis=-1, keepdims=True)
    return d * lax.rsqrt(var + LN_EPS) * g + b


class _CastSlot:
    def __init__(self, w, rows, t0):
        assert w.ndim == 2 and w.shape[0] % rows == 0
        self.w, self.rows, self.t0, self.n = w, rows, t0, w.shape[0] // rows

    def spec(self, step_of):
        def index(*ids):
            return jnp.clip(step_of(*ids) - self.t0, 0, self.n - 1), 0
        return pl.BlockSpec((self.rows, self.w.shape[1]), index)

    def out_shape(self):
        return jax.ShapeDtypeStruct(self.w.shape, BF16)


def _run_cast_slots(t, slots, in_refs, out_refs):
    for slot, i_ref, o_ref in zip(slots, in_refs, out_refs, strict=True):
        @pl.when(jnp.logical_and(t >= slot.t0, t < slot.t0 + slot.n))
        def _(i_ref=i_ref, o_ref=o_ref):
            o_ref[...] = i_ref[...].astype(BF16)


def _split_refs(refs, *counts):
    groups, at = [], 0
    for n in counts:
        groups.append(refs[at:at + n])
        at += n
    return (*groups, refs[at:])


def _chain_slots(weights_rows, t0=0):
    slots = []
    for w, rows in weights_rows:
        slots.append(_CastSlot(w, rows, t0))
        t0 += slots[-1].n
    return slots, t0


N_MAIN_TILES = F_ROWS // FFN_TM


N_TILES = M_ROWS // FFN_TM


def _ffn_body(load_x, emit_mid, emit_last, wg_ref, wu_ref, wd_ref, g_ref, b_ref, xb_ref, acc_ref, s_ref):
    i, f = pl.program_id(0), pl.program_id(1)
    last = pl.num_programs(1) - 1
    real = i < N_TILES
    both = jnp.logical_and

    def mlp(xb):
        gate = _dot(xb, wg_ref[...])
        up = _dot(xb, wu_ref[...])
        return _dot((gate * _sigmoid(gate) * up).astype(BF16), wd_ref[...])

    def finish_prev(emit):
        emit(_layer_norm(s_ref[...], g_ref[...], b_ref[...]))

    def first_step(with_prev):
        xb = load_x().astype(BF16)
        xb_ref[...] = xb
        acc_ref[...] = mlp(xb)
        if with_prev:
            finish_prev(emit_mid)

    pl.when(both(f == 0, i == 0))(functools.partial(first_step, False))
    pl.when(both(f == 0, both(i > 0, real)))(functools.partial(first_step, True))
    pl.when(both(f == 0, i == N_TILES))(functools.partial(finish_prev, emit_last))

    @pl.when(both(real, both(f > 0, f < last)))
    def _():
        acc_ref[...] += mlp(xb_ref[...])

    @pl.when(both(real, f == last))
    def _():
        s_ref[...] = ALPHA * load_x() + 0.5 * (acc_ref[...] + mlp(xb_ref[...]))


def _ffn_in_kernel(xm_ref, xe_ref, wg_ref, wu_ref, wd_ref, g_ref, b_ref, o_ref, ob_ref, *scratch):
    load_x = lambda: jnp.where(pl.program_id(0) < N_MAIN_TILES, xm_ref[...], xe_ref[...])

    def emit(y):
        o_ref[...] = y
        ob_ref[...] = y.astype(BF16)

    _ffn_body(load_x, emit, emit, wg_ref, wu_ref, wd_ref, g_ref, b_ref, *scratch)


def _ffn_out_kernel(x_ref, wg_ref, wu_ref, wd_ref, g_ref, b_ref, om_ref, oe_ref, *scratch):
    assert N_TILES == N_MAIN_TILES + 1

    def emit_frames(y):
        om_ref[...] = y

    def emit_extra(y):
        oe_ref[...] = y

    _ffn_body(lambda: x_ref[...], emit_frames, emit_extra, wg_ref, wu_ref, wd_ref, g_ref, b_ref, *scratch)


def _ffn_specs(d):
    nf = D_FF // FFN_TF
    step = lambda i, f: jnp.where(i == N_TILES, nf - 1, f)
    return nf, [
        pl.BlockSpec((d, FFN_TF), lambda i, f: (0, step(i, f))),
        pl.BlockSpec((d, FFN_TF), lambda i, f: (0, step(i, f) + nf)),
        pl.BlockSpec((FFN_TF, d), lambda i, f: (step(i, f), 0)),
        pl.BlockSpec((1, d), lambda i, f: (0, 0)),
        pl.BlockSpec((1, d), lambda i, f: (0, 0)),
    ]


_FFN_PARAMS = dict(dimension_semantics=("arbitrary", "arbitrary"), vmem_limit_bytes=VMEM_LIMIT)
_in_frame_tile = lambda i, f: (jnp.minimum(i, N_MAIN_TILES - 1), 0)
_in_any_tile = lambda i, f: (jnp.minimum(i, N_TILES - 1), 0)
_out_frame_tile = lambda i, f: (jnp.clip(i - 1, 0, N_MAIN_TILES - 1), 0)
_out_any_tile = lambda i, f: (jnp.maximum(i - 1, 0), 0)
_only_tile = lambda i, f: (0, 0)


def _ffn_scratch(tile):
    return [pltpu.VMEM(tile, BF16), pltpu.VMEM(tile, F32), pltpu.VMEM(tile, F32)]


def _ffn_ln_in(x_main, x_extra, w_gu, w_dn, ln_g, ln_b):
    d = x_main.shape[1]
    nf, wspecs = _ffn_specs(d)
    tile = (FFN_TM, d)
    return pl.pallas_call(
        _ffn_in_kernel,
        grid=(N_TILES + 1, nf),
        in_specs=[pl.BlockSpec(tile, _in_frame_tile),
                  pl.BlockSpec(tile, _only_tile, pipeline_mode=pl.Buffered(1))] + wspecs,
        out_specs=[pl.BlockSpec(tile, _out_any_tile), pl.BlockSpec(tile, _out_any_tile)],
        out_shape=[jax.ShapeDtypeStruct((M_ROWS, d), F32), jax.ShapeDtypeStruct((M_ROWS, d), BF16)],
        scratch_shapes=_ffn_scratch(tile),
        compiler_params=pltpu.CompilerParams(**_FFN_PARAMS),
        name="ffn_ln_in",
    )(x_main, x_extra, w_gu, w_gu, w_dn, ln_g, ln_b)


def _ffn_ln_out(x, w_gu, w_dn, ln_g, ln_b):
    d = x.shape[1]
    nf, wspecs = _ffn_specs(d)
    tile = (FFN_TM, d)
    return pl.pallas_call(
        _ffn_out_kernel,
        grid=(N_TILES + 1, nf),
        in_specs=[pl.BlockSpec(tile, _in_any_tile)] + wspecs,
        out_specs=[pl.BlockSpec(tile, _out_frame_tile), pl.BlockSpec(tile, _only_tile)],
        out_shape=[jax.ShapeDtypeStruct((F_ROWS, d), F32), jax.ShapeDtypeStruct((X_ROWS, d), F32)],
        scratch_shapes=_ffn_scratch(tile),
        compiler_params=pltpu.CompilerParams(**_FFN_PARAMS),
        name="ffn_ln_out",
    )(x, w_gu, w_gu, w_dn, ln_g, ln_b)


def _mm_kernel(x_ref, wt_ref, o_ref):
    o_ref[...] = _dot_nt(x_ref[...], wt_ref[...])


def _project(x, wt):
    m, k = x.shape
    n = wt.shape[0]
    return pl.pallas_call(
        _mm_kernel,
        grid=(n // MM_TN, m // MM_TM),
        in_specs=[pl.BlockSpec((MM_TM, k), lambda j, i: (i, 0)),
                  pl.BlockSpec((MM_TN, k), lambda j, i: (j, 0))],
        out_specs=pl.BlockSpec((MM_TM, MM_TN), lambda j, i: (i, j)),
        out_shape=jax.ShapeDtypeStruct((m, n), F32),
        compiler_params=pltpu.CompilerParams(
            dimension_semantics=("parallel", "parallel"), vmem_limit_bytes=VMEM_LIMIT),
        name="proj_in",
    )(x, wt)


def _rwkv_kernel(*refs, chunk, pad, n_seq, n_p, slots):
    p_refs, fixed, cast_in, outs, cast_out, (yacc_ref,) = _split_refs(refs, n_p, 14, len(slots), 3, len(slots))
    (hist_ref, s0_ref, mu_ref, w0_ref, w2_ref, a0_ref, a2_ref, g2_ref, kk_ref, ka_ref, rk_ref,
     gnw_ref, gnb_ref, seg_ref) = fixed
    y_ref, s_ref, shift_ref = outs
    c = pl.program_id(1)
    _run_cast_slots(pl.program_id(0) * pl.num_programs(1) + c, slots, cast_in, cast_out)
    n_rows = n_seq * chunk
    seqs = range(n_seq)
    rsl = [slice(q * chunk, (q + 1) * chunk) for q in seqs]
    rows = lax.broadcasted_iota(jnp.int32, (n_rows, 1), 0)

    @pl.when(c == 0)
    def _():
        shift_ref[...] = hist_ref[...]
        s_ref[...] = s0_ref[...]

    p = jnp.concatenate([r[...] for r in p_refs], axis=0)
    if pad:
        p = jnp.where(jnp.logical_and(c == 0, jnp.bitwise_and(rows, chunk - 1) < pad), 0.0, p)
    prev = pltpu.roll(p, 1, 0)
    for q in seqs:
        prev = jnp.where(rows == q * chunk, shift_ref[q], prev)
    for q in seqs:
        shift_ref[q] = p[(q + 1) * chunk - 1:(q + 1) * chunk, :]
    ps = p + (prev - p) * mu_ref[...]

    r = ps[:, 0:RW_DIM]
    k = ps[:, RW_DIM:2 * RW_DIM]
    v = ps[:, 2 * RW_DIM:3 * RW_DIM]
    wd = ps[:, RW_OFF_WD:RW_OFF_WD + LORA_PAD]
    ad = ps[:, RW_OFF_AD:RW_OFF_AD + LORA_PAD]
    gd = ps[:, RW_OFF_GD:RW_OFF_GD + RW_GATE_LORA]

    lw = -math.exp(-0.5) * _sigmoid(w0_ref[...] + _dot(jnp.tanh(wd).astype(BF16), w2_ref[...]))
    a = _sigmoid(a0_ref[...] + _dot(ad.astype(BF16), a2_ref[...]))
    g = _dot(_sigmoid(gd).astype(BF16), g2_ref[...])

    seg = seg_ref[...]

    def head_sum(x):
        w = seg.shape[0]
        return jnp.concatenate(
            [_dot_sel_r(x[:, j * w:(j + 1) * w], seg) for j in range(RW_DIM // w)], axis=1)

    kk = k * kk_ref[...]
    kk = kk * lax.rsqrt(jnp.maximum(head_sum(kk * kk), 1e-24))
    k = k * (1.0 + (a - 1.0) * ka_ref[...])
    b_neg = -(kk * a)

    cum = _dot_sel_l(_causal_blocks(n_rows, chunk), lw)
    cum_last = [cum[(q + 1) * chunk - 1:(q + 1) * chunk, :] for q in seqs]
    cum_end = jnp.concatenate([jnp.broadcast_to(x, (chunk, RW_DIM)) for x in cum_last], axis=0)
    e_neg = jnp.exp(-cum)
    kt = (kk * jnp.exp(cum - lw)).astype(BF16)
    rt = (r * jnp.exp(cum)).astype(BF16)
    kd = (k * e_neg).astype(BF16)
    bd_neg = (b_neg * e_neg).astype(BF16)
    e_end = jnp.exp(cum_end - cum)
    k_end = (k * e_end).astype(BF16)
    b_end_neg = (b_neg * e_end).astype(BF16)
    p_end = [jnp.exp(x) for x in cum_last]
    vb = v.astype(BF16)

    ri = lax.broadcasted_iota(jnp.int32, (2 * chunk, 2 * chunk), 0)
    rj = lax.broadcasted_iota(jnp.int32, (2 * chunk, 2 * chunk), 1)
    bi = jnp.where(ri >= chunk, ri - chunk + 1, ri)
    bj = jnp.where(rj >= chunk, rj - chunk, rj)
    keep = bi > bj
    ti = lax.broadcasted_iota(jnp.int32, (chunk, 2 * chunk), 0)
    tj = lax.broadcasted_iota(jnp.int32, (chunk, 2 * chunk), 1)
    hi_lane = tj >= chunk
    eye_hi = (tj == ti + chunk).astype(F32)
    zeros_v = jnp.zeros((chunk, HEAD), BF16)

    pairs = [(q, h) for q in seqs for h in range(RW_HEADS)]
    sls = [(rsl[q], slice(h * HEAD, (h + 1) * HEAD)) for q, h in pairs]
    s_old = [s_ref[q, h] for q, h in pairs]
    s_b = [s.astype(BF16) for s in s_old]
    amats = [jnp.where(keep, _dot_nt(jnp.concatenate([kt[sl], rt[sl]], axis=0),
                                     jnp.concatenate([bd_neg[sl], kd[sl]], axis=0)), 0.0)
             for sl in sls]
    top = [m[:chunk] for m in amats]
    bot = [m[chunk:].astype(BF16) for m in amats]
    top_b = [x.astype(BF16) for x in top]
    tq = [_dot(xb[:, :chunk], jnp.where(hi_lane, eye_hi, x).astype(BF16)) + eye_hi
          for x, xb in zip(top, top_b)]
    n = 2
    while n < chunk:
        tq = [jnp.where(hi_lane, x, 0.0) + _dot(x[:, :chunk].astype(BF16), x.astype(BF16)) for x in tq]
        n *= 2
    t_b = [x[:, chunk:].astype(BF16) for x in tq]
    v_h = [vb[sl] for sl in sls]
    rhs = [_dot_nt(kt[sl], sb) + _dot(xb, jnp.concatenate([zeros_v, vh], axis=0))
           for sl, sb, xb, vh in zip(sls, s_b, top_b, v_h)]
    ub = [_dot(t, x.astype(BF16)).astype(BF16) for t, x in zip(t_b, rhs)]
    uv = [jnp.concatenate([u, vh], axis=0) for u, vh in zip(ub, v_h)]
    for i, (q, h) in enumerate(pairs):
        sl = sls[i]
        yacc_ref[sl] = _dot_nt(rt[sl], s_b[i]) + _dot(bot[i], uv[i])
        s_ref[q, h] = s_old[i] * p_end[q][:, sl[1]] + _dot_tn(
            uv[i], jnp.concatenate([b_end_neg[sl], k_end[sl]], axis=0))

    y = yacc_ref[...]
    inv_n = 1.0 / HEAD
    mean = head_sum(y) * inv_n
    yc = y - mean
    var = head_sum(yc * yc) * inv_n
    yn = yc * lax.rsqrt(var + RW_GN_EPS) * gnw_ref[...] + gnb_ref[...]
    bonus = head_sum(r * k * rk_ref[...]) * v
    out = ((yn + bonus) * g).astype(y_ref.dtype)
    for q in seqs:
        y_ref[q] = out[rsl[q]]


RWKV_GROUPS = (2, 8)
SSD_GROUPS = (1, 1)


class _SeqPlan:
    def __init__(self, prompt, groups):
        self.prompt = prompt
        if prompt:
            self.n_seq, self.chunk, self.pad, self.total, self.seq_len = groups[0], CHUNK, PAD, BATCH, SEQ
            self.grid = (BATCH // self.n_seq, SEQ_CHUNKS)
        else:
            self.n_seq, self.chunk, self.pad, self.total, self.seq_len = groups[1], DEC_SEQ, 0, DEC_BATCH, DEC_SEQ
            self.grid = (DEC_BATCH // self.n_seq, 1)

    def p_specs(self, width, col):
        cb = col // width
        if not self.prompt:
            rows = self.n_seq * self.chunk
            return [pl.BlockSpec((rows, width), lambda g, c: (ROW_S0 // rows + g, cb))]

        def spec(q):
            def index(g, c):
                seq = g * self.n_seq + q
                return jnp.where(c == 0, ROW_H0 // CHUNK + seq, seq * (SEQ // CHUNK) + c - 1), cb
            return pl.BlockSpec((CHUNK, width), index)
        return [spec(q) for q in range(self.n_seq)]

    def y_spec(self, dim):
        return pl.BlockSpec((self.n_seq, self.chunk, dim), lambda g, c: (g, jnp.maximum(c - 1, 0), 0))

    def y_shape(self, dim):
        return jax.ShapeDtypeStruct((self.total, self.seq_len, dim), BF16)

    def state_spec(self, shape):
        nd = len(shape)
        return pl.BlockSpec((self.n_seq,) + tuple(shape), lambda g, c: (g,) + (0,) * nd)

    def cast_slots(self, cast_weights):
        slots, t_end = _chain_slots(cast_weights)
        assert t_end <= self.grid[0] * self.grid[1]
        specs = [s.spec(lambda g, c: g * self.grid[1] + c) for s in slots]
        return slots, specs, specs


_MIX_PARAMS = dict(dimension_semantics=("arbitrary", "arbitrary"), vmem_limit_bytes=VMEM_LIMIT)


def _rwkv_mix(p_all, hist, s0, wts, *, prompt, cast_weights=()):
    plan = _SeqPlan(prompt, RWKV_GROUPS)
    slots, cast_in_specs, cast_out_specs = plan.cast_slots(cast_weights)
    p_specs = plan.p_specs(RW_COLS, PC_RW)
    const2 = lambda g, c: (0, 0)
    wspecs = [pl.BlockSpec(w.shape, const2) for w in wts]
    shift_spec = plan.state_spec((1, RW_COLS))
    wkv_spec = plan.state_spec((RW_HEADS, HEAD, HEAD))
    return pl.pallas_call(
        functools.partial(_rwkv_kernel, chunk=plan.chunk, pad=plan.pad, n_seq=plan.n_seq, n_p=len(p_specs),
                          slots=slots),
        grid=plan.grid,
        in_specs=p_specs + [shift_spec, wkv_spec] + wspecs + cast_in_specs,
        out_specs=[plan.y_spec(RW_DIM), wkv_spec, shift_spec] + cast_out_specs,
        out_shape=[plan.y_shape(RW_DIM),
                   jax.ShapeDtypeStruct((plan.total, RW_HEADS, HEAD, HEAD), F32),
                   jax.ShapeDtypeStruct((plan.total, 1, RW_COLS), F32)] + [s.out_shape() for s in slots],
        scratch_shapes=[pltpu.VMEM((plan.n_seq * plan.chunk, RW_DIM), F32)],
        compiler_params=pltpu.CompilerParams(**_MIX_PARAMS),
        name="rwkv_mix_c%d" % plan.chunk,
    )(*([p_all] * len(p_specs)), hist, s0, *wts, *[s.w for s in slots])


def _ssd_kernel(*refs, chunk, pad, n_seq, n_p, slots):
    (xs_refs, bc_refs, z_refs, dt_refs, fixed, cast_in, outs, cast_out, (xpad_ref, yacc_ref)) = _split_refs(
        refs, n_p, n_p, n_p, n_p, 9, len(slots), 3, len(slots))
    hist_ref, h0_ref, cw_ref, cb_ref, dtb_ref, alog_ref, dskip_ref, nw_ref, exp_ref = fixed
    y_ref, h_ref, tail_ref = outs
    c = pl.program_id(1)
    _run_cast_slots(pl.program_id(0) * pl.num_programs(1) + c, slots, cast_in, cast_out)
    n_rows = n_seq * chunk
    seqs = range(n_seq)
    rsl = [slice(q * chunk, (q + 1) * chunk) for q in seqs]
    rows = lax.broadcasted_iota(jnp.int32, (n_rows, 1), 0)
    stack = lambda rs: jnp.concatenate([r[...] for r in rs], axis=0)

    @pl.when(c == 0)
    def _():
        xpad_ref[:, 0:CONV_HIST, :] = hist_ref[...]
        h_ref[...] = h0_ref[...]

    u = jnp.concatenate([stack(xs_refs), stack(bc_refs)], axis=1)
    if pad:
        is_pad = jnp.logical_and(c == 0, jnp.bitwise_and(rows, chunk - 1) < pad)
        u = jnp.where(is_pad, 0.0, u)
    convs = []
    for q in seqs:
        u_q = u[rsl[q]]
        xpad_ref[q, CONV_HIST:CONV_HIST + chunk, :] = u_q
        conv = cb_ref[...] + u_q * cw_ref[CONV_W - 1:CONV_W, :]
        for i in range(CONV_W - 1):
            back = CONV_W - 1 - i
            conv = conv + xpad_ref[q, CONV_HIST - back:CONV_HIST - back + chunk, :] * cw_ref[i:i + 1, :]
        hist_next = xpad_ref[q, chunk:chunk + CONV_HIST, :]
        xpad_ref[q, 0:CONV_HIST, :] = hist_next
        tail_ref[q] = hist_next
        convs.append(conv)
    conv = jnp.concatenate(convs, axis=0)
    xbc = conv * _sigmoid(conv)
    xs = xbc[:, 0:SSM_DIM]
    bm = xbc[:, SSM_DIM:SSM_DIM + BC_DIM].astype(BF16)
    cm = xbc[:, SSM_DIM + BC_DIM:CONV_DIM].astype(BF16)

    dt = _softplus(stack(dt_refs) + dtb_ref[...])
    if pad:
        dt = jnp.where(is_pad, 0.0, dt)
    da = dt * (-jnp.exp(alog_ref[...]))
    ci = lax.broadcasted_iota(jnp.int32, (chunk, chunk), 0)
    cj = lax.broadcasted_iota(jnp.int32, (chunk, chunk), 1)
    causal = ci >= cj
    acs = _dot_sel_l(_causal_blocks(n_rows, chunk), da)
    acs_t = acs.T
    acs_end = jnp.concatenate(
        [jnp.broadcast_to(acs[(q + 1) * chunk - 1:(q + 1) * chunk, :], (chunk, DT_COLS)) for q in seqs], axis=0)

    expand = exp_ref[...]
    xdt = xs * _dot_sel_r(dt, expand)
    xdt_b = xdt.astype(BF16)
    xdt_end = (xdt * _dot_sel_r(jnp.exp(acs_end - acs), expand)).astype(BF16)
    e_acs = _dot_sel_r(jnp.exp(acs), expand)
    decay_end = [jnp.exp(acs_t[:, (q + 1) * chunk - 1:(q + 1) * chunk]) for q in seqs]

    gw = SSM_HPG * HEAD
    groups = range(SSM_GROUPS)
    heads = range(SSM_HEADS)
    qg = [(q, g) for q in seqs for g in groups]
    qh = [(q, hd) for q in seqs for hd in heads]
    bm_g = {(q, g): bm[rsl[q], g * SSM_STATE:(g + 1) * SSM_STATE] for q, g in qg}
    cm_g = {(q, g): cm[rsl[q], g * SSM_STATE:(g + 1) * SSM_STATE] for q, g in qg}
    h_old = {(q, hd): h_ref[q, hd] for q, hd in qh}
    cb = {k: _dot_nt(cm_g[k], bm_g[k]) for k in qg}
    y_off = {(q, g): _dot_nt(cm_g[q, g], jnp.concatenate(
        [h_old[q, g * SSM_HPG + hh].astype(BF16) for hh in range(SSM_HPG)], axis=0)) for q, g in qg}
    st = {(q, g): _dot_tn(xdt_end[rsl[q], g * gw:(g + 1) * gw], bm_g[q, g]) for q, g in qg}
    for q, hd in qh:
        sl = slice(hd * HEAD, (hd + 1) * HEAD)
        seg = acs[rsl[q], hd:hd + 1] - acs_t[hd:hd + 1, rsl[q]]
        lmat = jnp.exp(jnp.where(causal, seg, -jnp.inf))
        yacc_ref[rsl[q], sl] = _dot((cb[q, hd // SSM_HPG] * lmat).astype(BF16), xdt_b[rsl[q], sl])
    for q, hd in qh:
        g, hh = divmod(hd, SSM_HPG)
        h_ref[q, hd] = h_old[q, hd] * decay_end[q][hd:hd + 1, :] + st[q, g][hh * HEAD:(hh + 1) * HEAD, :]

    y_off_all = jnp.concatenate(
        [jnp.concatenate([y_off[q, g] for g in groups], axis=1) for q in seqs], axis=0)
    y = yacc_ref[...] + y_off_all * e_acs + xs * dskip_ref[...]
    zz = stack(z_refs)
    y = y * (zz * _sigmoid(zz))
    parts = []
    for g in groups:
        yg = y[:, g * gw:(g + 1) * gw]
        ms = jnp.mean(yg * yg, axis=-1, keepdims=True)
        parts.append(yg * lax.rsqrt(ms + RMS_EPS))
    out = (jnp.concatenate(parts, axis=1) * nw_ref[...]).astype(y_ref.dtype)
    for q in seqs:
        y_ref[q] = out[rsl[q]]


def _ssd_mix(p_all, hist, h0, wts, *, prompt, cast_weights=()):
    plan = _SeqPlan(prompt, SSD_GROUPS)
    slots, cast_in_specs, cast_out_specs = plan.cast_slots(cast_weights)
    p_specs = (plan.p_specs(SSM_DIM, PC_XS) + plan.p_specs(2 * BC_DIM, PC_BC)
               + plan.p_specs(SSM_DIM, PC_Z) + plan.p_specs(DT_COLS, PC_DT))
    const2 = lambda g, c: (0, 0)
    wspecs = [pl.BlockSpec(w.shape, const2) for w in wts]
    conv_spec = plan.state_spec((CONV_HIST, CONV_DIM))
    ssm_spec = plan.state_spec((SSM_HEADS, HEAD, SSM_STATE))
    return pl.pallas_call(
        functools.partial(_ssd_kernel, chunk=plan.chunk, pad=plan.pad, n_seq=plan.n_seq, n_p=len(p_specs) // 4,
                          slots=slots),
        grid=plan.grid,
        in_specs=p_specs + [conv_spec, ssm_spec] + wspecs + cast_in_specs,
        out_specs=[plan.y_spec(SSM_DIM), ssm_spec, conv_spec] + cast_out_specs,
        out_shape=[plan.y_shape(SSM_DIM),
                   jax.ShapeDtypeStruct((plan.total, SSM_HEADS, HEAD, SSM_STATE), F32),
                   jax.ShapeDtypeStruct((plan.total, CONV_HIST, CONV_DIM), F32)] + [s.out_shape() for s in slots],
        scratch_shapes=[pltpu.VMEM((plan.n_seq, CONV_HIST + plan.chunk, CONV_DIM), F32),
                        pltpu.VMEM((plan.n_seq * plan.chunk, SSM_DIM), F32)],
        compiler_params=pltpu.CompilerParams(**_MIX_PARAMS),
        name="ssd_mix_c%d" % plan.chunk,
    )(*([p_all] * len(p_specs)), hist, h0, *wts, *[s.w for s in slots])


MERGE_F_TILES = F_ROWS // MERGE_TM
assert S_ROWS == MERGE_TM


def _merge_kernel(x_ref, yrw_f_ref, yrw_s_ref, yssm_f_ref, yssm_s_ref, ga_ref, gb_ref, bga_ref, bgb_ref,
                  wrw_ref, wssm_ref, wout_ref, g_ref, b_ref, o_ref):
    is_frame = pl.program_id(0) < MERGE_F_TILES
    y_rw = jnp.where(is_frame, yrw_f_ref[...], yrw_s_ref[...])
    y_ssm = jnp.where(is_frame, yssm_f_ref[...], yssm_s_ref[...])
    merged = (_sigmoid(ga_ref[...] + bga_ref[...]) * _dot(y_rw, wrw_ref[...])
              + _sigmoid(gb_ref[...] + bgb_ref[...]) * _dot(y_ssm, wssm_ref[...]))
    s = ALPHA * x_ref[...] + _dot(merged.astype(BF16), wout_ref[...])
    o_ref[...] = _layer_norm(s, g_ref[...], b_ref[...])


def _merge(x, y_rw_f, y_rw_s, y_ssm_f, y_ssm_s, p_all, b_gate, w_rw_out, w_ssm_out, w_out, ln_g, ln_b):
    m, d = x.shape
    tm = MERGE_TM
    row = lambda i: (i, 0)
    frame_row = lambda i: (jnp.minimum(i, MERGE_F_TILES - 1), 0)
    const = lambda i: (0, 0)
    resident = lambda w: pl.BlockSpec(w.shape, const, pipeline_mode=pl.Buffered(1))
    return pl.pallas_call(
        _merge_kernel,
        grid=(m // tm,),
        in_specs=[
            pl.BlockSpec((tm, d), row),
            pl.BlockSpec((tm, RW_DIM), frame_row),
            pl.BlockSpec((tm, RW_DIM), const),
            pl.BlockSpec((tm, SSM_DIM), frame_row),
            pl.BlockSpec((tm, SSM_DIM), const),
            pl.BlockSpec((tm, d), lambda i: (i, PC_GATE // d)),
            pl.BlockSpec((tm, d), lambda i: (i, PC_GATE // d + 1)),
            pl.BlockSpec((1, d), lambda i: (0, 0)),
            pl.BlockSpec((1, d), lambda i: (0, 1)),
            resident(w_rw_out), resident(w_ssm_out), resident(w_out),
            pl.BlockSpec((1, d), const),
            pl.BlockSpec((1, d), const),
        ],
        out_specs=pl.BlockSpec((tm, d), row),
        out_shape=jax.ShapeDtypeStruct((m, d), F32),
        compiler_params=pltpu.CompilerParams(
            dimension_semantics=("arbitrary",), vmem_limit_bytes=VMEM_LIMIT),
        name="merge_ln",
    )(x, y_rw_f, y_rw_s, y_ssm_f, y_ssm_s, p_all, p_all, b_gate, b_gate, w_rw_out, w_ssm_out, w_out,
      ln_g, ln_b)


def _pad_cols(x, width):
    return jnp.concatenate([x, jnp.zeros(x.shape[:-1] + (width - x.shape[-1],), x.dtype)], axis=-1)


def _rw_cols(x):
    o_wd, o_ad, o_gd = 3 * RW_DIM, 3 * RW_DIM + RW_DECAY_LORA, 3 * RW_DIM + RW_DECAY_LORA + RW_AAA_LORA
    return jnp.concatenate([
        x[..., :o_wd],
        _pad_cols(x[..., o_wd:o_ad], LORA_PAD),
        _pad_cols(x[..., o_ad:o_gd], LORA_PAD),
        x[..., o_gd:],
    ], axis=-1)


def _pad_rows(x, height):
    return jnp.concatenate([x, jnp.zeros((height - x.shape[0],) + x.shape[1:], x.dtype)], axis=0)


def _rw_rows(x):
    o_wd, o_ad, o_gd = 3 * RW_DIM, 3 * RW_DIM + RW_DECAY_LORA, 3 * RW_DIM + RW_DECAY_LORA + RW_AAA_LORA
    return jnp.concatenate([
        x[:o_wd], _pad_rows(x[o_wd:o_ad], LORA_PAD), _pad_rows(x[o_ad:o_gd], LORA_PAD), x[o_gd:]], axis=0)


def _rw_cols_inv(x):
    return jnp.concatenate([
        x[..., :RW_OFF_WD],
        x[..., RW_OFF_WD:RW_OFF_WD + RW_DECAY_LORA],
        x[..., RW_OFF_AD:RW_OFF_AD + RW_AAA_LORA],
        x[..., RW_OFF_GD:],
    ], axis=-1)


def _block_ones(n, blk):
    i = jnp.arange(n) // blk
    return (i[:, None] == i[None, :]).astype(BF16)


def kernel(x_prompt, x_sample, state_rwkv_shift, state_wkv, state_conv, state_ssm, meta_tokens, ffn1_gu, ffn1_dn, ln1_g, ln1_b, w_in, b_gate, rw_mu, rw_w0, rw_w2, rw_a0, rw_a2, rw_g2, rw_kk, rw_ka, rw_rk, rw_gn_w, rw_gn_b, conv_w, conv_b, dt_bias, a_log, d_skip, ssm_norm_w, w_rw_out, w_ssm_out, w_out, ln2_g, ln2_b, ffn2_gu, ffn2_dn, ln3_g, ln3_b):
    lyr = 0
    row = lambda t: t[lyr].reshape(1, -1).astype(F32)

    head_rows = jnp.concatenate([jnp.zeros((PAD, D_MODEL), F32), meta_tokens.astype(F32)], axis=0)
    x_extra = jnp.concatenate([x_sample.reshape(S_ROWS, D_MODEL)] + [head_rows] * BATCH, axis=0)
    x1, x1b = _ffn_ln_in(x_prompt.reshape(F_ROWS, D_MODEL), x_extra, ffn1_gu[lyr].astype(BF16),
                         ffn1_dn[lyr].astype(BF16), row(ln1_g), row(ln1_b))

    wt = jnp.swapaxes(w_in[lyr], 0, 1).astype(BF16)
    o_z = RW_SHIFT_COLS
    o_xbc = o_z + SSM_DIM
    o_dt = o_xbc + CONV_DIM
    o_gate = o_dt + SSM_HEADS
    wt_all = jnp.concatenate([
        _rw_rows(wt[:o_z]), _pad_rows(wt[o_dt:o_gate], DT_SLOT), wt[o_gate:], wt[o_z:o_xbc], wt[o_xbc:o_dt]], axis=0)
    p_all = _project(x1b, wt_all)

    rw_wts = [
        _rw_cols(rw_mu[lyr]).reshape(1, RW_COLS), row(rw_w0),
        _pad_rows(rw_w2[lyr], LORA_PAD).astype(BF16), row(rw_a0),
        _pad_rows(rw_a2[lyr], LORA_PAD).astype(BF16), rw_g2[lyr].astype(BF16),
        row(rw_kk), row(rw_ka), row(rw_rk), row(rw_gn_w), row(rw_gn_b),
        _block_ones(2 * LANES, HEAD),
    ]
    y_rw_s, wkv_s, shift_s = _rwkv_mix(
        p_all, _rw_cols(state_rwkv_shift[lyr]), state_wkv[lyr], rw_wts, prompt=False)
    y_rw_f, wkv_p, shift_p, w_rw_o, w_ssm_o, w_o = _rwkv_mix(
        p_all, jnp.zeros((BATCH, 1, RW_COLS), F32), jnp.zeros((BATCH, RW_HEADS, HEAD, HEAD), F32), rw_wts,
        prompt=True, cast_weights=[(w_rw_out[lyr], MIX_CAST_ROWS), (w_ssm_out[lyr], MIX_CAST_ROWS),
                                   (w_out[lyr], MIX_CAST_ROWS)])

    head_of_lane = jnp.arange(SSM_DIM) // HEAD
    expand = (jnp.arange(DT_COLS)[:, None] == head_of_lane[None, :]).astype(BF16)
    ssd_wts = [
        conv_w[lyr], row(conv_b), _pad_cols(row(dt_bias), DT_COLS), _pad_cols(row(a_log), DT_COLS),
        jnp.repeat(d_skip[lyr], HEAD).reshape(1, SSM_DIM), row(ssm_norm_w),
        expand,
    ]
    hist_rows = lambda t: jnp.pad(t, ((0, 0), (CONV_HIST - (CONV_W - 1), 0), (0, 0)))
    y_ssm_s, ssm_s, conv_s = _ssd_mix(
        p_all, hist_rows(state_conv[lyr]), state_ssm[lyr], ssd_wts, prompt=False)
    y_ssm_f, ssm_p, conv_p, gu2, dn2 = _ssd_mix(
        p_all, jnp.zeros((BATCH, CONV_HIST, CONV_DIM), F32),
        jnp.zeros((BATCH, SSM_HEADS, HEAD, SSM_STATE), F32), ssd_wts, prompt=True,
        cast_weights=[(ffn2_gu[lyr], MIX_CAST_ROWS // 4), (ffn2_dn[lyr], MIX_CAST_ROWS)])

    flat = lambda t: t.reshape(-1, t.shape[-1])
    x2 = _merge(x1, flat(y_rw_f), flat(y_rw_s), flat(y_ssm_f), flat(y_ssm_s), p_all, row(b_gate),
                w_rw_o, w_ssm_o, w_o, row(ln2_g), row(ln2_b))
    y_frames, y_extra = _ffn_ln_out(x2, gu2, dn2, row(ln3_g), row(ln3_b))

    y_prompt = y_frames.reshape(BATCH, SEQ, D_MODEL)
    y_sample = y_extra[:S_ROWS].reshape(DEC_BATCH, DEC_SEQ, D_MODEL)
    conv_of = lambda t: t[:, CONV_HIST - (CONV_W - 1):][None]
    return (y_prompt, y_sample,
            _rw_cols_inv(shift_p)[None], wkv_p[None], conv_of(conv_p), ssm_p[None],
            _rw_cols_inv(shift_s)[None], wkv_s[None], conv_of(conv_s), ssm_s[None])
```

```python
import functools
import math

import jax
import jax.numpy as jnp
from jax import lax
from jax.experimental import pallas as pl
from jax.experimental.pallas import tpu as pltpu

F32 = jnp.float32
BF16 = jnp.bfloat16

D_MODEL = 2048
BATCH = 4
SEQ = 2048
DEC_BATCH = 16
DEC_SEQ = 16
CHUNK = 64
N_META = 16
PAD = (-N_META) % CHUNK
HEAD_ROWS = PAD + N_META
SEQ_CHUNKS = (HEAD_ROWS + SEQ) // CHUNK
F_ROWS = BATCH * SEQ
S_ROWS = DEC_BATCH * DEC_SEQ
ROW_S0 = F_ROWS
ROW_H0 = F_ROWS + S_ROWS
X_ROWS = S_ROWS + BATCH * HEAD_ROWS
M_ROWS = F_ROWS + X_ROWS

HEAD = 64
RW_DIM = D_MODEL // 2
RW_HEADS = RW_DIM // HEAD
RW_DECAY_LORA = 96
RW_AAA_LORA = 96
RW_GATE_LORA = 256
RW_SHIFT_COLS = 3 * RW_DIM + RW_DECAY_LORA + RW_AAA_LORA + RW_GATE_LORA
RW_GN_EPS = 64e-5
SSM_DIM = D_MODEL
SSM_HEADS = SSM_DIM // HEAD
SSM_GROUPS = 4
SSM_HPG = SSM_HEADS // SSM_GROUPS
SSM_STATE = 128
CONV_W = 4
BC_DIM = SSM_GROUPS * SSM_STATE
CONV_DIM = SSM_DIM + 2 * BC_DIM
RMS_EPS = 1e-5
D_FF = 5632
LN_EPS = 1e-5
DEPTH = 1
ALPHA = (2 * DEPTH) ** 0.25

LANES = 128
SUBLANES = 8
VMEM_LIMIT = 56 * 1024 * 1024

LORA_PAD = LANES
RW_OFF_WD = 3 * RW_DIM
RW_OFF_AD = RW_OFF_WD + LORA_PAD
RW_OFF_GD = RW_OFF_AD + LORA_PAD
RW_COLS = RW_OFF_GD + RW_GATE_LORA
DT_COLS = LANES
CONV_HIST = SUBLANES

MIX_CAST_ROWS = 128
FFN_TM = 512
FFN_TF = 512
MM_TM = 512
MM_TN = 1024
MERGE_TM = 256

DT_SLOT = 512
PC_RW = 0
PC_DT = PC_RW + RW_COLS
PC_GATE = PC_DT + DT_SLOT
PC_Z = PC_GATE + 2 * D_MODEL
PC_XS = PC_Z + SSM_DIM
PC_BC = PC_XS + SSM_DIM
P_COLS = PC_BC + 2 * BC_DIM
assert PC_DT % DT_COLS == 0 and PC_GATE % D_MODEL == 0 and PC_Z % SSM_DIM == 0
assert PC_XS % SSM_DIM == 0 and PC_BC % (2 * BC_DIM) == 0 and P_COLS % MM_TN == 0
assert F_ROWS % FFN_TM == 0 and X_ROWS == FFN_TM and M_ROWS % MM_TM == 0 and ROW_H0 % MERGE_TM == 0


def _dot(a, b):
    return jnp.dot(a, b, preferred_element_type=F32)


def _dot_nt(a, b):
    return lax.dot_general(a, b, (((1,), (1,)), ((), ())), preferred_element_type=F32)


def _dot_tn(a, b):
    return lax.dot_general(a, b, (((0,), (0,)), ((), ())), preferred_element_type=F32)


def _split2(x):
    hi = x.astype(BF16)
    lo = (x - hi.astype(F32)).astype(BF16)
    return hi, lo


def _dot_sel_r(x, sel):
    hi, lo = _split2(x)
    return _dot(hi, sel) + _dot(lo, sel)


def _dot_sel_l(sel, x):
    hi, lo = _split2(x)
    return _dot(sel, hi) + _dot(sel, lo)


def _causal_blocks(n, blk):
    i = lax.broadcasted_iota(jnp.int32, (n, n), 0)
    j = lax.broadcasted_iota(jnp.int32, (n, n), 1)
    keep = jnp.logical_and(i >= j, jnp.bitwise_xor(i, j) < blk)
    return jnp.where(keep, 1.0, 0.0).astype(BF16)


def _sigmoid(x):
    return jax.nn.sigmoid(x)


def _softplus(x):
    return jnp.maximum(x, 0.0) + jnp.log1p(jnp.exp(-jnp.abs(x)))


def _layer_norm(s, g, b):
    mu = jnp.mean(s, axis=-1, keepdims=True)
    d = s - mu
    var = jnp.mean(d * d, axis=-1, keepdims=True)
    return d * lax.rsqrt(var + LN_EPS) * g + b


class _CastSlot:
    def __init__(self, w, rows, t0):
        assert w.ndim == 2 and w.shape[0] % rows == 0
        self.w, self.rows, self.t0, self.n = w, rows, t0, w.shape[0] // rows

    def spec(self, step_of):
        def index(*ids):
            return jnp.clip(step_of(*ids) - self.t0, 0, self.n - 1), 0
        return pl.BlockSpec((self.rows, self.w.shape[1]), index)

    def out_shape(self):
        return jax.ShapeDtypeStruct(self.w.shape, BF16)


def _run_cast_slots(t, slots, in_refs, out_refs):
    for slot, i_ref, o_ref in zip(slots, in_refs, out_refs, strict=True):
        @pl.when(jnp.logical_and(t >= slot.t0, t < slot.t0 + slot.n))
        def _(i_ref=i_ref, o_ref=o_ref):
            o_ref[...] = i_ref[...].astype(BF16)


def _split_refs(refs, *counts):
    groups, at = [], 0
    for n in counts:
        groups.append(refs[at:at + n])
        at += n
    return (*groups, refs[at:])


def _chain_slots(weights_rows, t0=0):
    slots = []
    for w, rows in weights_rows:
        slots.append(_CastSlot(w, rows, t0))
        t0 += slots[-1].n
    return slots, t0


N_MAIN_TILES = F_ROWS // FFN_TM


N_TILES = M_ROWS // FFN_TM


def _ffn_body(load_x, emit_mid, emit_last, wg_ref, wu_ref, wd_ref, g_ref, b_ref, xb_ref, acc_ref, s_ref):
    i, f = pl.program_id(0), pl.program_id(1)
    last = pl.num_programs(1) - 1
    real = i < N_TILES
    both = jnp.logical_and

    def mlp(xb):
        gate = _dot(xb, wg_ref[...])
        up = _dot(xb, wu_ref[...])
        return _dot((gate * _sigmoid(gate) * up).astype(BF16), wd_ref[...])

    def finish_prev(emit):
        emit(_layer_norm(s_ref[...], g_ref[...], b_ref[...]))

    def first_step(with_prev):
        xb = load_x().astype(BF16)
        xb_ref[...] = xb
        acc_ref[...] = mlp(xb)
        if with_prev:
            finish_prev(emit_mid)

    pl.when(both(f == 0, i == 0))(functools.partial(first_step, False))
    pl.when(both(f == 0, both(i > 0, real)))(functools.partial(first_step, True))
    pl.when(both(f == 0, i == N_TILES))(functools.partial(finish_prev, emit_last))

    @pl.when(both(real, both(f > 0, f < last)))
    def _():
        acc_ref[...] += mlp(xb_ref[...])

    @pl.when(both(real, f == last))
    def _():
        s_ref[...] = ALPHA * load_x() + 0.5 * (acc_ref[...] + mlp(xb_ref[...]))


def _ffn_in_kernel(xm_ref, xe_ref, wg_ref, wu_ref, wd_ref, g_ref, b_ref, o_ref, ob_ref, *scratch):
    load_x = lambda: jnp.where(pl.program_id(0) < N_MAIN_TILES, xm_ref[...], xe_ref[...])

    def emit(y):
        o_ref[...] = y
        ob_ref[...] = y.astype(BF16)

    _ffn_body(load_x, emit, emit, wg_ref, wu_ref, wd_ref, g_ref, b_ref, *scratch)


def _ffn_out_kernel(x_ref, wg_ref, wu_ref, wd_ref, g_ref, b_ref, om_ref, oe_ref, *scratch):
    assert N_TILES == N_MAIN_TILES + 1

    def emit_frames(y):
        om_ref[...] = y

    def emit_extra(y):
        oe_ref[...] = y

    _ffn_body(lambda: x_ref[...], emit_frames, emit_extra, wg_ref, wu_ref, wd_ref, g_ref, b_ref, *scratch)


def _ffn_specs(d):
    nf = D_FF // FFN_TF
    step = lambda i, f: jnp.where(i == N_TILES, nf - 1, f)
    return nf, [
        pl.BlockSpec((d, FFN_TF), lambda i, f: (0, step(i, f))),
        pl.BlockSpec((d, FFN_TF), lambda i, f: (0, step(i, f) + nf)),
        pl.BlockSpec((FFN_TF, d), lambda i, f: (step(i, f), 0)),
        pl.BlockSpec((1, d), lambda i, f: (0, 0)),
        pl.BlockSpec((1, d), lambda i, f: (0, 0)),
    ]


_FFN_PARAMS = dict(dimension_semantics=("arbitrary", "arbitrary"), vmem_limit_bytes=VMEM_LIMIT)
_in_frame_tile = lambda i, f: (jnp.minimum(i, N_MAIN_TILES - 1), 0)
_in_any_tile = lambda i, f: (jnp.minimum(i, N_TILES - 1), 0)
_out_frame_tile = lambda i, f: (jnp.clip(i - 1, 0, N_MAIN_TILES - 1), 0)
_out_any_tile = lambda i, f: (jnp.maximum(i - 1, 0), 0)
_only_tile = lambda i, f: (0, 0)


def _ffn_scratch(tile):
    return [pltpu.VMEM(tile, BF16), pltpu.VMEM(tile, F32), pltpu.VMEM(tile, F32)]


def _ffn_ln_in(x_main, x_extra, w_gu, w_dn, ln_g, ln_b):
    d = x_main.shape[1]
    nf, wspecs = _ffn_specs(d)
    tile = (FFN_TM, d)
    return pl.pallas_call(
        _ffn_in_kernel,
        grid=(N_TILES + 1, nf),
        in_specs=[pl.BlockSpec(tile, _in_frame_tile),
                  pl.BlockSpec(tile, _only_tile, pipeline_mode=pl.Buffered(1))] + wspecs,
        out_specs=[pl.BlockSpec(tile, _out_any_tile), pl.BlockSpec(tile, _out_any_tile)],
        out_shape=[jax.ShapeDtypeStruct((M_ROWS, d), F32), jax.ShapeDtypeStruct((M_ROWS, d), BF16)],
        scratch_shapes=_ffn_scratch(tile),
        compiler_params=pltpu.CompilerParams(**_FFN_PARAMS),
        name="ffn_ln_in",
    )(x_main, x_extra, w_gu, w_gu, w_dn, ln_g, ln_b)


def _ffn_ln_out(x, w_gu, w_dn, ln_g, ln_b):
    d = x.shape[1]
    nf, wspecs = _ffn_specs(d)
    tile = (FFN_TM, d)
    return pl.pallas_call(
        _ffn_out_kernel,
        grid=(N_TILES + 1, nf),
        in_specs=[pl.BlockSpec(tile, _in_any_tile)] + wspecs,
        out_specs=[pl.BlockSpec(tile, _out_frame_tile), pl.BlockSpec(tile, _only_tile)],
        out_shape=[jax.ShapeDtypeStruct((F_ROWS, d), F32), jax.ShapeDtypeStruct((X_ROWS, d), F32)],
        scratch_shapes=_ffn_scratch(tile),
        compiler_params=pltpu.CompilerParams(**_FFN_PARAMS),
        name="ffn_ln_out",
    )(x, w_gu, w_gu, w_dn, ln_g, ln_b)


def _proj_tiles():
    o_z = RW_SHIFT_COLS
    o_xbc = o_z + SSM_DIM
    o_dt = o_xbc + CONV_DIM
    o_gate = o_dt + SSM_HEADS
    runs = [(PC_RW, 0, 3 * RW_DIM), (PC_GATE, o_gate, 2 * D_MODEL), (PC_Z, o_z, SSM_DIM), (PC_XS, o_xbc, CONV_DIM)]
    src = [None] * (P_COLS // MM_TN)
    for dst0, src0, width in runs:
        assert dst0 % MM_TN == 0 and width % MM_TN == 0 and src0 % (2 * SUBLANES) == 0
        for t in range(width // MM_TN):
            src[dst0 // MM_TN + t] = src0 + t * MM_TN
    assert src.count(None) == 1
    return src


def _mm_kernel(x_ref, wt_ref, wmix_ref, o_ref, *, mixed_tile):
    j = pl.program_id(0)

    @pl.when(j != mixed_tile)
    def _():
        o_ref[...] = _dot_nt(x_ref[...], wt_ref[...])

    @pl.when(j == mixed_tile)
    def _():
        o_ref[...] = _dot_nt(x_ref[...], wmix_ref[...])


def _project(x, wt, wt_mixed):
    m, k = x.shape
    src = _proj_tiles()
    mixed_tile = src.index(None)

    unit = 2 * SUBLANES

    def wt_row(j, i):
        start = 0
        for t, s in enumerate(src):
            if s is not None:
                start = jnp.where(j == t, s // unit, start)
        return start * unit, 0

    return pl.pallas_call(
        functools.partial(_mm_kernel, mixed_tile=mixed_tile),
        grid=(len(src), m // MM_TM),
        in_specs=[pl.BlockSpec((MM_TM, k), lambda j, i: (i, 0)),
                  pl.BlockSpec((pl.Element(MM_TN), pl.Element(k)), wt_row),
                  pl.BlockSpec((MM_TN, k), lambda j, i: (0, 0))],
        out_specs=pl.BlockSpec((MM_TM, MM_TN), lambda j, i: (i, j)),
        out_shape=jax.ShapeDtypeStruct((m, P_COLS), F32),
        compiler_params=pltpu.CompilerParams(
            dimension_semantics=("parallel", "parallel"), vmem_limit_bytes=VMEM_LIMIT),
        name="proj_in",
    )(x, wt, wt_mixed)


def _rwkv_kernel(*refs, chunk, pad, n_seq, n_p, slots):
    p_refs, fixed, cast_in, outs, cast_out, (yacc_ref,) = _split_refs(refs, n_p, 14, len(slots), 3, len(slots))
    (hist_ref, s0_ref, mu_ref, w0_ref, w2_ref, a0_ref, a2_ref, g2_ref, kk_ref, ka_ref, rk_ref,
     gnw_ref, gnb_ref, seg_ref) = fixed
    y_ref, s_ref, shift_ref = outs
    c = pl.program_id(1)
    _run_cast_slots(pl.program_id(0) * pl.num_programs(1) + c, slots, cast_in, cast_out)
    n_rows = n_seq * chunk
    seqs = range(n_seq)
    rsl = [slice(q * chunk, (q + 1) * chunk) for q in seqs]
    rows = lax.broadcasted_iota(jnp.int32, (n_rows, 1), 0)

    @pl.when(c == 0)
    def _():
        shift_ref[...] = hist_ref[...]
        s_ref[...] = s0_ref[...]

    p = jnp.concatenate([r[...] for r in p_refs], axis=0)
    if pad:
        p = jnp.where(jnp.logical_and(c == 0, jnp.bitwise_and(rows, chunk - 1) < pad), 0.0, p)
    prev = pltpu.roll(p, 1, 0)
    for q in seqs:
        prev = jnp.where(rows == q * chunk, shift_ref[q], prev)
    for q in seqs:
        shift_ref[q] = p[(q + 1) * chunk - 1:(q + 1) * chunk, :]
    ps = p + (prev - p) * mu_ref[...]

    r = ps[:, 0:RW_DIM]
    k = ps[:, RW_DIM:2 * RW_DIM]
    v = ps[:, 2 * RW_DIM:3 * RW_DIM]
    wd = ps[:, RW_OFF_WD:RW_OFF_WD + LORA_PAD]
    ad = ps[:, RW_OFF_AD:RW_OFF_AD + LORA_PAD]
    gd = ps[:, RW_OFF_GD:RW_OFF_GD + RW_GATE_LORA]

    lw = -math.exp(-0.5) * _sigmoid(w0_ref[...] + _dot(jnp.tanh(wd).astype(BF16), w2_ref[...]))
    a = _sigmoid(a0_ref[...] + _dot(ad.astype(BF16), a2_ref[...]))
    g = _dot(_sigmoid(gd).astype(BF16), g2_ref[...])

    seg = seg_ref[...]

    def head_sum(x):
        w = seg.shape[0]
        return jnp.concatenate(
            [_dot_sel_r(x[:, j * w:(j + 1) * w], seg) for j in range(RW_DIM // w)], axis=1)

    kk = k * kk_ref[...]
    kk = kk * lax.rsqrt(jnp.maximum(head_sum(kk * kk), 1e-24))
    k = k * (1.0 + (a - 1.0) * ka_ref[...])
    b_neg = -(kk * a)

    cum = _dot_sel_l(_causal_blocks(n_rows, chunk), lw)
    cum_last = [cum[(q + 1) * chunk - 1:(q + 1) * chunk, :] for q in seqs]
    cum_end = jnp.concatenate([jnp.broadcast_to(x, (chunk, RW_DIM)) for x in cum_last], axis=0)
    e_neg = jnp.exp(-cum)
    kt = (kk * jnp.exp(cum - lw)).astype(BF16)
    rt = (r * jnp.exp(cum)).astype(BF16)
    kd = (k * e_neg).astype(BF16)
    bd_neg = (b_neg * e_neg).astype(BF16)
    e_end = jnp.exp(cum_end - cum)
    k_end = (k * e_end).astype(BF16)
    b_end_neg = (b_neg * e_end).astype(BF16)
    p_end = [jnp.exp(x) for x in cum_last]
    vb = v.astype(BF16)

    ri = lax.broadcasted_iota(jnp.int32, (2 * chunk, 2 * chunk), 0)
    rj = lax.broadcasted_iota(jnp.int32, (2 * chunk, 2 * chunk), 1)
    bi = jnp.where(ri >= chunk, ri - chunk + 1, ri)
    bj = jnp.where(rj >= chunk, rj - chunk, rj)
    keep = bi > bj
    ti = lax.broadcasted_iota(jnp.int32, (chunk, 2 * chunk), 0)
    tj = lax.broadcasted_iota(jnp.int32, (chunk, 2 * chunk), 1)
    hi_lane = tj >= chunk
    eye_hi = (tj == ti + chunk).astype(F32)
    zeros_v = jnp.zeros((chunk, HEAD), BF16)

    pairs = [(q, h) for q in seqs for h in range(RW_HEADS)]
    sls = [(rsl[q], slice(h * HEAD, (h + 1) * HEAD)) for q, h in pairs]
    s_old = [s_ref[q, h] for q, h in pairs]
    s_b = [s.astype(BF16) for s in s_old]
    amats = [jnp.where(keep, _dot_nt(jnp.concatenate([kt[sl], rt[sl]], axis=0),
                                     jnp.concatenate([bd_neg[sl], kd[sl]], axis=0)), 0.0)
             for sl in sls]
    top = [m[:chunk] for m in amats]
    bot = [m[chunk:].astype(BF16) for m in amats]
    top_b = [x.astype(BF16) for x in top]
    tq = [_dot(xb[:, :chunk], jnp.where(hi_lane, eye_hi, x).astype(BF16)) + eye_hi
          for x, xb in zip(top, top_b)]
    n = 2
    while n < chunk:
        tq = [jnp.where(hi_lane, x, 0.0) + _dot(x[:, :chunk].astype(BF16), x.astype(BF16)) for x in tq]
        n *= 2
    t_b = [x[:, chunk:].astype(BF16) for x in tq]
    v_h = [vb[sl] for sl in sls]
    rhs = [_dot_nt(kt[sl], sb) + _dot(xb, jnp.concatenate([zeros_v, vh], axis=0))
           for sl, sb, xb, vh in zip(sls, s_b, top_b, v_h)]
    ub = [_dot(t, x.astype(BF16)).astype(BF16) for t, x in zip(t_b, rhs)]
    uv = [jnp.concatenate([u, vh], axis=0) for u, vh in zip(ub, v_h)]
    for i, (q, h) in enumerate(pairs):
        sl = sls[i]
        yacc_ref[sl] = _dot_nt(rt[sl], s_b[i]) + _dot(bot[i], uv[i])
        s_ref[q, h] = s_old[i] * p_end[q][:, sl[1]] + _dot_tn(
            uv[i], jnp.concatenate([b_end_neg[sl], k_end[sl]], axis=0))

    y = yacc_ref[...]
    inv_n = 1.0 / HEAD
    mean = head_sum(y) * inv_n
    yc = y - mean
    var = head_sum(yc * yc) * inv_n
    yn = yc * lax.rsqrt(var + RW_GN_EPS) * gnw_ref[...] + gnb_ref[...]
    bonus = head_sum(r * k * rk_ref[...]) * v
    out = ((yn + bonus) * g).astype(y_ref.dtype)
    for q in seqs:
        y_ref[q] = out[rsl[q]]


RWKV_GROUPS = (2, 8)
SSD_GROUPS = (1, 1)


class _SeqPlan:
    def __init__(self, prompt, groups):
        self.prompt = prompt
        if prompt:
            self.n_seq, self.chunk, self.pad, self.total, self.seq_len = groups[0], CHUNK, PAD, BATCH, SEQ
            self.grid = (BATCH // self.n_seq, SEQ_CHUNKS)
        else:
            self.n_seq, self.chunk, self.pad, self.total, self.seq_len = groups[1], DEC_SEQ, 0, DEC_BATCH, DEC_SEQ
            self.grid = (DEC_BATCH // self.n_seq, 1)

    def p_specs(self, width, col):
        cb = col // width
        if not self.prompt:
            rows = self.n_seq * self.chunk
            return [pl.BlockSpec((rows, width), lambda g, c: (ROW_S0 // rows + g, cb))]

        def spec(q):
            def index(g, c):
                seq = g * self.n_seq + q
                return jnp.where(c == 0, ROW_H0 // CHUNK + seq, seq * (SEQ // CHUNK) + c - 1), cb
            return pl.BlockSpec((CHUNK, width), index)
        return [spec(q) for q in range(self.n_seq)]

    def y_spec(self, dim):
        return pl.BlockSpec((self.n_seq, self.chunk, dim), lambda g, c: (g, jnp.maximum(c - 1, 0), 0))

    def y_shape(self, dim):
        return jax.ShapeDtypeStruct((self.total, self.seq_len, dim), BF16)

    def state_spec(self, shape):
        nd = len(shape)
        return pl.BlockSpec((self.n_seq,) + tuple(shape), lambda g, c: (g,) + (0,) * nd)

    def cast_slots(self, cast_weights):
        slots, t_end = _chain_slots(cast_weights)
        assert t_end <= self.grid[0] * self.grid[1]
        specs = [s.spec(lambda g, c: g * self.grid[1] + c) for s in slots]
        return slots, specs, specs


_MIX_PARAMS = dict(dimension_semantics=("arbitrary", "arbitrary"), vmem_limit_bytes=VMEM_LIMIT)


def _rwkv_mix(p_all, hist, s0, wts, *, prompt, cast_weights=()):
    plan = _SeqPlan(prompt, RWKV_GROUPS)
    slots, cast_in_specs, cast_out_specs = plan.cast_slots(cast_weights)
    p_specs = plan.p_specs(RW_COLS, PC_RW)
    const2 = lambda g, c: (0, 0)
    wspecs = [pl.BlockSpec(w.shape, const2) for w in wts]
    shift_spec = plan.state_spec((1, RW_COLS))
    wkv_spec = plan.state_spec((RW_HEADS, HEAD, HEAD))
    return pl.pallas_call(
        functools.partial(_rwkv_kernel, chunk=plan.chunk, pad=plan.pad, n_seq=plan.n_seq, n_p=len(p_specs),
                          slots=slots),
        grid=plan.grid,
        in_specs=p_specs + [shift_spec, wkv_spec] + wspecs + cast_in_specs,
        out_specs=[plan.y_spec(RW_DIM), wkv_spec, shift_spec] + cast_out_specs,
        out_shape=[plan.y_shape(RW_DIM),
                   jax.ShapeDtypeStruct((plan.total, RW_HEADS, HEAD, HEAD), F32),
                   jax.ShapeDtypeStruct((plan.total, 1, RW_COLS), F32)] + [s.out_shape() for s in slots],
        scratch_shapes=[pltpu.VMEM((plan.n_seq * plan.chunk, RW_DIM), F32)],
        compiler_params=pltpu.CompilerParams(**_MIX_PARAMS),
        name="rwkv_mix_c%d" % plan.chunk,
    )(*([p_all] * len(p_specs)), hist, s0, *wts, *[s.w for s in slots])


def _ssd_kernel(*refs, chunk, pad, n_seq, n_p, slots):
    (xs_refs, bc_refs, z_refs, dt_refs, fixed, cast_in, outs, cast_out, (xpad_ref, yacc_ref)) = _split_refs(
        refs, n_p, n_p, n_p, n_p, 9, len(slots), 3, len(slots))
    hist_ref, h0_ref, cw_ref, cb_ref, dtb_ref, alog_ref, dskip_ref, nw_ref, exp_ref = fixed
    y_ref, h_ref, tail_ref = outs
    c = pl.program_id(1)
    _run_cast_slots(pl.program_id(0) * pl.num_programs(1) + c, slots, cast_in, cast_out)
    n_rows = n_seq * chunk
    seqs = range(n_seq)
    rsl = [slice(q * chunk, (q + 1) * chunk) for q in seqs]
    rows = lax.broadcasted_iota(jnp.int32, (n_rows, 1), 0)
    stack = lambda rs: jnp.concatenate([r[...] for r in rs], axis=0)

    @pl.when(c == 0)
    def _():
        xpad_ref[:, 0:CONV_HIST, :] = hist_ref[...]
        h_ref[...] = h0_ref[...]

    u = jnp.concatenate([stack(xs_refs), stack(bc_refs)], axis=1)
    if pad:
        is_pad = jnp.logical_and(c == 0, jnp.bitwise_and(rows, chunk - 1) < pad)
        u = jnp.where(is_pad, 0.0, u)
    convs = []
    for q in seqs:
        u_q = u[rsl[q]]
        xpad_ref[q, CONV_HIST:CONV_HIST + chunk, :] = u_q
        conv = cb_ref[...] + u_q * cw_ref[CONV_W - 1:CONV_W, :]
        for i in range(CONV_W - 1):
            back = CONV_W - 1 - i
            conv = conv + xpad_ref[q, CONV_HIST - back:CONV_HIST - back + chunk, :] * cw_ref[i:i + 1, :]
        hist_next = xpad_ref[q, chunk:chunk + CONV_HIST, :]
        xpad_ref[q, 0:CONV_HIST, :] = hist_next
        tail_ref[q] = hist_next
        convs.append(conv)
    conv = jnp.concatenate(convs, axis=0)
    xbc = conv * _sigmoid(conv)
    xs = xbc[:, 0:SSM_DIM]
    bm = xbc[:, SSM_DIM:SSM_DIM + BC_DIM].astype(BF16)
    cm = xbc[:, SSM_DIM + BC_DIM:CONV_DIM].astype(BF16)

    dt = _softplus(stack(dt_refs) + dtb_ref[...])
    if pad:
        dt = jnp.where(is_pad, 0.0, dt)
    da = dt * (-jnp.exp(alog_ref[...]))
    ci = lax.broadcasted_iota(jnp.int32, (chunk, chunk), 0)
    cj = lax.broadcasted_iota(jnp.int32, (chunk, chunk), 1)
    causal = ci >= cj
    acs = _dot_sel_l(_causal_blocks(n_rows, chunk), da)
    acs_t = acs.T
    acs_end = jnp.concatenate(
        [jnp.broadcast_to(acs[(q + 1) * chunk - 1:(q + 1) * chunk, :], (chunk, DT_COLS)) for q in seqs], axis=0)

    expand = exp_ref[...]
    xdt = xs * _dot_sel_r(dt, expand)
    xdt_b = xdt.astype(BF16)
    xdt_end = (xdt * _dot_sel_r(jnp.exp(acs_end - acs), expand)).astype(BF16)
    e_acs = _dot_sel_r(jnp.exp(acs), expand)
    decay_end = [jnp.exp(acs_t[:, (q + 1) * chunk - 1:(q + 1) * chunk]) for q in seqs]

    gw = SSM_HPG * HEAD
    groups = range(SSM_GROUPS)
    heads = range(SSM_HEADS)
    qg = [(q, g) for q in seqs for g in groups]
    qh = [(q, hd) for q in seqs for hd in heads]
    bm_g = {(q, g): bm[rsl[q], g * SSM_STATE:(g + 1) * SSM_STATE] for q, g in qg}
    cm_g = {(q, g): cm[rsl[q], g * SSM_STATE:(g + 1) * SSM_STATE] for q, g in qg}
    h_old = {(q, hd): h_ref[q, hd] for q, hd in qh}
    cb = {k: _dot_nt(cm_g[k], bm_g[k]) for k in qg}
    y_off = {(q, g): _dot_nt(cm_g[q, g], jnp.concatenate(
        [h_old[q, g * SSM_HPG + hh].astype(BF16) for hh in range(SSM_HPG)], axis=0)) for q, g in qg}
    st = {(q, g): _dot_tn(xdt_end[rsl[q], g * gw:(g + 1) * gw], bm_g[q, g]) for q, g in qg}
    for q, hd in qh:
        sl = slice(hd * HEAD, (hd + 1) * HEAD)
        seg = acs[rsl[q], hd:hd + 1] - acs_t[hd:hd + 1, rsl[q]]
        lmat = jnp.exp(jnp.where(causal, seg, -jnp.inf))
        yacc_ref[rsl[q], sl] = _dot((cb[q, hd // SSM_HPG] * lmat).astype(BF16), xdt_b[rsl[q], sl])
    for q, hd in qh:
        g, hh = divmod(hd, SSM_HPG)
        h_ref[q, hd] = h_old[q, hd] * decay_end[q][hd:hd + 1, :] + st[q, g][hh * HEAD:(hh + 1) * HEAD, :]

    y_off_all = jnp.concatenate(
        [jnp.concatenate([y_off[q, g] for g in groups], axis=1) for q in seqs], axis=0)
    y = yacc_ref[...] + y_off_all * e_acs + xs * dskip_ref[...]
    zz = stack(z_refs)
    y = y * (zz * _sigmoid(zz))
    parts = []
    for g in groups:
        yg = y[:, g * gw:(g + 1) * gw]
        ms = jnp.mean(yg * yg, axis=-1, keepdims=True)
        parts.append(yg * lax.rsqrt(ms + RMS_EPS))
    out = (jnp.concatenate(parts, axis=1) * nw_ref[...]).astype(y_ref.dtype)
    for q in seqs:
        y_ref[q] = out[rsl[q]]


def _ssd_mix(p_all, hist, h0, wts, *, prompt, cast_weights=()):
    plan = _SeqPlan(prompt, SSD_GROUPS)
    slots, cast_in_specs, cast_out_specs = plan.cast_slots(cast_weights)
    p_specs = (plan.p_specs(SSM_DIM, PC_XS) + plan.p_specs(2 * BC_DIM, PC_BC)
               + plan.p_specs(SSM_DIM, PC_Z) + plan.p_specs(DT_COLS, PC_DT))
    const2 = lambda g, c: (0, 0)
    wspecs = [pl.BlockSpec(w.shape, const2) for w in wts]
    conv_spec = plan.state_spec((CONV_HIST, CONV_DIM))
    ssm_spec = plan.state_spec((SSM_HEADS, HEAD, SSM_STATE))
    return pl.pallas_call(
        functools.partial(_ssd_kernel, chunk=plan.chunk, pad=plan.pad, n_seq=plan.n_seq, n_p=len(p_specs) // 4,
                          slots=slots),
        grid=plan.grid,
        in_specs=p_specs + [conv_spec, ssm_spec] + wspecs + cast_in_specs,
        out_specs=[plan.y_spec(SSM_DIM), ssm_spec, conv_spec] + cast_out_specs,
        out_shape=[plan.y_shape(SSM_DIM),
                   jax.ShapeDtypeStruct((plan.total, SSM_HEADS, HEAD, SSM_STATE), F32),
                   jax.ShapeDtypeStruct((plan.total, CONV_HIST, CONV_DIM), F32)] + [s.out_shape() for s in slots],
        scratch_shapes=[pltpu.VMEM((plan.n_seq, CONV_HIST + plan.chunk, CONV_DIM), F32),
                        pltpu.VMEM((plan.n_seq * plan.chunk, SSM_DIM), F32)],
        compiler_params=pltpu.CompilerParams(**_MIX_PARAMS),
        name="ssd_mix_c%d" % plan.chunk,
    )(*([p_all] * len(p_specs)), hist, h0, *wts, *[s.w for s in slots])


MERGE_F_TILES = F_ROWS // MERGE_TM
assert S_ROWS == MERGE_TM


def _merge_kernel(x_ref, yrw_f_ref, yrw_s_ref, yssm_f_ref, yssm_s_ref, ga_ref, gb_ref, bga_ref, bgb_ref,
                  wrw_ref, wssm_ref, wout_ref, g_ref, b_ref, o_ref):
    is_frame = pl.program_id(0) < MERGE_F_TILES
    y_rw = jnp.where(is_frame, yrw_f_ref[...], yrw_s_ref[...])
    y_ssm = jnp.where(is_frame, yssm_f_ref[...], yssm_s_ref[...])
    merged = (_sigmoid(ga_ref[...] + bga_ref[...]) * _dot(y_rw, wrw_ref[...])
              + _sigmoid(gb_ref[...] + bgb_ref[...]) * _dot(y_ssm, wssm_ref[...]))
    s = ALPHA * x_ref[...] + _dot(merged.astype(BF16), wout_ref[...])
    o_ref[...] = _layer_norm(s, g_ref[...], b_ref[...])


def _merge(x, y_rw_f, y_rw_s, y_ssm_f, y_ssm_s, p_all, b_gate, w_rw_out, w_ssm_out, w_out, ln_g, ln_b):
    m, d = x.shape
    tm = MERGE_TM
    row = lambda i: (i, 0)
    frame_row = lambda i: (jnp.minimum(i, MERGE_F_TILES - 1), 0)
    const = lambda i: (0, 0)
    resident = lambda w: pl.BlockSpec(w.shape, const, pipeline_mode=pl.Buffered(1))
    return pl.pallas_call(
        _merge_kernel,
        grid=(m // tm,),
        in_specs=[
            pl.BlockSpec((tm, d), row),
            pl.BlockSpec((tm, RW_DIM), frame_row),
            pl.BlockSpec((tm, RW_DIM), const),
            pl.BlockSpec((tm, SSM_DIM), frame_row),
            pl.BlockSpec((tm, SSM_DIM), const),
            pl.BlockSpec((tm, d), lambda i: (i, PC_GATE // d)),
            pl.BlockSpec((tm, d), lambda i: (i, PC_GATE // d + 1)),
            pl.BlockSpec((1, d), lambda i: (0, 0)),
            pl.BlockSpec((1, d), lambda i: (0, 1)),
            resident(w_rw_out), resident(w_ssm_out), resident(w_out),
            pl.BlockSpec((1, d), const),
            pl.BlockSpec((1, d), const),
        ],
        out_specs=pl.BlockSpec((tm, d), row),
        out_shape=jax.ShapeDtypeStruct((m, d), F32),
        compiler_params=pltpu.CompilerParams(
            dimension_semantics=("arbitrary",), vmem_limit_bytes=VMEM_LIMIT),
        name="merge_ln",
    )(x, y_rw_f, y_rw_s, y_ssm_f, y_ssm_s, p_all, p_all, b_gate, b_gate, w_rw_out, w_ssm_out, w_out,
      ln_g, ln_b)


def _pad_cols(x, width):
    return jnp.concatenate([x, jnp.zeros(x.shape[:-1] + (width - x.shape[-1],), x.dtype)], axis=-1)


def _rw_cols(x):
    o_wd, o_ad, o_gd = 3 * RW_DIM, 3 * RW_DIM + RW_DECAY_LORA, 3 * RW_DIM + RW_DECAY_LORA + RW_AAA_LORA
    return jnp.concatenate([
        x[..., :o_wd],
        _pad_cols(x[..., o_wd:o_ad], LORA_PAD),
        _pad_cols(x[..., o_ad:o_gd], LORA_PAD),
        x[..., o_gd:],
    ], axis=-1)


def _pad_rows(x, height):
    return jnp.concatenate([x, jnp.zeros((height - x.shape[0],) + x.shape[1:], x.dtype)], axis=0)


def _rw_rows(x):
    o_wd, o_ad, o_gd = 3 * RW_DIM, 3 * RW_DIM + RW_DECAY_LORA, 3 * RW_DIM + RW_DECAY_LORA + RW_AAA_LORA
    return jnp.concatenate([
        x[:o_wd], _pad_rows(x[o_wd:o_ad], LORA_PAD), _pad_rows(x[o_ad:o_gd], LORA_PAD), x[o_gd:]], axis=0)


def _rw_cols_inv(x):
    return jnp.concatenate([
        x[..., :RW_OFF_WD],
        x[..., RW_OFF_WD:RW_OFF_WD + RW_DECAY_LORA],
        x[..., RW_OFF_AD:RW_OFF_AD + RW_AAA_LORA],
        x[..., RW_OFF_GD:],
    ], axis=-1)


def _block_ones(n, blk):
    i = jnp.arange(n) // blk
    return (i[:, None] == i[None, :]).astype(BF16)


def kernel(x_prompt, x_sample, state_rwkv_shift, state_wkv, state_conv, state_ssm, meta_tokens, ffn1_gu, ffn1_dn, ln1_g, ln1_b, w_in, b_gate, rw_mu, rw_w0, rw_w2, rw_a0, rw_a2, rw_g2, rw_kk, rw_ka, rw_rk, rw_gn_w, rw_gn_b, conv_w, conv_b, dt_bias, a_log, d_skip, ssm_norm_w, w_rw_out, w_ssm_out, w_out, ln2_g, ln2_b, ffn2_gu, ffn2_dn, ln3_g, ln3_b):
    lyr = 0
    row = lambda t: t[lyr].reshape(1, -1).astype(F32)

    head_rows = jnp.concatenate([jnp.zeros((PAD, D_MODEL), F32), meta_tokens.astype(F32)], axis=0)
    x_extra = jnp.concatenate([x_sample.reshape(S_ROWS, D_MODEL)] + [head_rows] * BATCH, axis=0)
    x1, x1b = _ffn_ln_in(x_prompt.reshape(F_ROWS, D_MODEL), x_extra, ffn1_gu[lyr].astype(BF16),
                         ffn1_dn[lyr].astype(BF16), row(ln1_g), row(ln1_b))

    wt = jnp.swapaxes(w_in[lyr], 0, 1).astype(BF16)
    o_dt = RW_SHIFT_COLS + SSM_DIM + CONV_DIM
    wt_mixed = jnp.concatenate([
        _rw_rows(wt[:RW_SHIFT_COLS])[3 * RW_DIM:], _pad_rows(wt[o_dt:o_dt + SSM_HEADS], DT_SLOT)], axis=0)
    p_all = _project(x1b, wt, wt_mixed)

    rw_wts = [
        _rw_cols(rw_mu[lyr]).reshape(1, RW_COLS), row(rw_w0),
        _pad_rows(rw_w2[lyr], LORA_PAD).astype(BF16), row(rw_a0),
        _pad_rows(rw_a2[lyr], LORA_PAD).astype(BF16), rw_g2[lyr].astype(BF16),
        row(rw_kk), row(rw_ka), row(rw_rk), row(rw_gn_w), row(rw_gn_b),
        _block_ones(2 * LANES, HEAD),
    ]
    y_rw_s, wkv_s, shift_s = _rwkv_mix(
        p_all, _rw_cols(state_rwkv_shift[lyr]), state_wkv[lyr], rw_wts, prompt=False)
    y_rw_f, wkv_p, shift_p, w_rw_o, w_ssm_o, w_o = _rwkv_mix(
        p_all, jnp.zeros((BATCH, 1, RW_COLS), F32), jnp.zeros((BATCH, RW_HEADS, HEAD, HEAD), F32), rw_wts,
        prompt=True, cast_weights=[(w_rw_out[lyr], MIX_CAST_ROWS), (w_ssm_out[lyr], MIX_CAST_ROWS),
                                   (w_out[lyr], MIX_CAST_ROWS)])

    head_of_lane = jnp.arange(SSM_DIM) // HEAD
    expand = (jnp.arange(DT_COLS)[:, None] == head_of_lane[None, :]).astype(BF16)
    ssd_wts = [
        conv_w[lyr], row(conv_b), _pad_cols(row(dt_bias), DT_COLS), _pad_cols(row(a_log), DT_COLS),
        jnp.repeat(d_skip[lyr], HEAD).reshape(1, SSM_DIM), row(ssm_norm_w),
        expand,
    ]
    hist_rows = lambda t: jnp.pad(t, ((0, 0), (CONV_HIST - (CONV_W - 1), 0), (0, 0)))
    y_ssm_s, ssm_s, conv_s = _ssd_mix(
        p_all, hist_rows(state_conv[lyr]), state_ssm[lyr], ssd_wts, prompt=False)
    y_ssm_f, ssm_p, conv_p, gu2, dn2 = _ssd_mix(
        p_all, jnp.zeros((BATCH, CONV_HIST, CONV_DIM), F32),
        jnp.zeros((BATCH, SSM_HEADS, HEAD, SSM_STATE), F32), ssd_wts, prompt=True,
        cast_weights=[(ffn2_gu[lyr], MIX_CAST_ROWS // 4), (ffn2_dn[lyr], MIX_CAST_ROWS)])

    flat = lambda t: t.reshape(-1, t.shape[-1])
    x2 = _merge(x1, flat(y_rw_f), flat(y_rw_s), flat(y_ssm_f), flat(y_ssm_s), p_all, row(b_gate),
                w_rw_o, w_ssm_o, w_o, row(ln2_g), row(ln2_b))
    y_frames, y_extra = _ffn_ln_out(x2, gu2, dn2, row(ln3_g), row(ln3_b))

    y_prompt = y_frames.reshape(BATCH, SEQ, D_MODEL)
    y_sample = y_extra[:S_ROWS].reshape(DEC_BATCH, DEC_SEQ, D_MODEL)
    conv_of = lambda t: t[:, CONV_HIST - (CONV_W - 1):][None]
    return (y_prompt, y_sample,
            _rw_cols_inv(shift_p)[None], wkv_p[None], conv_of(conv_p), ssm_p[None],
            _rw_cols_inv(shift_s)[None], wkv_s[None], conv_of(conv_s), ssm_s[None])
```

```python
import functools
import math

import jax
import jax.numpy as jnp
from jax import lax
from jax.experimental import pallas as pl
from jax.experimental.pallas import tpu as pltpu

F32 = jnp.float32
BF16 = jnp.bfloat16

D_MODEL = 2048
BATCH = 4
SEQ = 2048
DEC_BATCH = 16
DEC_SEQ = 16
CHUNK = 64
N_META = 16
PAD = (-N_META) % CHUNK
HEAD_ROWS = PAD + N_META
SEQ_CHUNKS = (HEAD_ROWS + SEQ) // CHUNK
F_ROWS = BATCH * SEQ
S_ROWS = DEC_BATCH * DEC_SEQ
ROW_S0 = F_ROWS
ROW_H0 = F_ROWS + S_ROWS
X_ROWS = S_ROWS + BATCH * HEAD_ROWS
M_ROWS = F_ROWS + X_ROWS

HEAD = 64
RW_DIM = D_MODEL // 2
RW_HEADS = RW_DIM // HEAD
RW_DECAY_LORA = 96
RW_AAA_LORA = 96
RW_GATE_LORA = 256
RW_SHIFT_COLS = 3 * RW_DIM + RW_DECAY_LORA + RW_AAA_LORA + RW_GATE_LORA
RW_GN_EPS = 64e-5
SSM_DIM = D_MODEL
SSM_HEADS = SSM_DIM // HEAD
SSM_GROUPS = 4
SSM_HPG = SSM_HEADS // SSM_GROUPS
SSM_STATE = 128
CONV_W = 4
BC_DIM = SSM_GROUPS * SSM_STATE
CONV_DIM = SSM_DIM + 2 * BC_DIM
RMS_EPS = 1e-5
D_FF = 5632
LN_EPS = 1e-5
DEPTH = 1
ALPHA = (2 * DEPTH) ** 0.25

LANES = 128
SUBLANES = 8
VMEM_LIMIT = 56 * 1024 * 1024

LORA_PAD = LANES
RW_OFF_WD = 3 * RW_DIM
RW_OFF_AD = RW_OFF_WD + LORA_PAD
RW_OFF_GD = RW_OFF_AD + LORA_PAD
RW_COLS = RW_OFF_GD + RW_GATE_LORA
DT_COLS = LANES
CONV_HIST = SUBLANES

MIX_CAST_ROWS = 128
FFN_TM = 512
FFN_TF = 512
MM_TM = M_ROWS // 8
MM_TN = 1024
MERGE_TM = 256

DT_SLOT = 512
PC_RW = 0
PC_DT = PC_RW + RW_COLS
PC_GATE = PC_DT + DT_SLOT
PC_Z = PC_GATE + 2 * D_MODEL
PC_XS = PC_Z + SSM_DIM
PC_BC = PC_XS + SSM_DIM
P_COLS = PC_BC + 2 * BC_DIM
assert PC_DT % DT_COLS == 0 and PC_GATE % D_MODEL == 0 and PC_Z % SSM_DIM == 0
assert PC_XS % SSM_DIM == 0 and PC_BC % (2 * BC_DIM) == 0 and P_COLS % MM_TN == 0
assert F_ROWS % FFN_TM == 0 and X_ROWS == FFN_TM and M_ROWS % MM_TM == 0 and ROW_H0 % MERGE_TM == 0


def _dot(a, b):
    return jnp.dot(a, b, preferred_element_type=F32)


def _dot_nt(a, b):
    return lax.dot_general(a, b, (((1,), (1,)), ((), ())), preferred_element_type=F32)


def _dot_tn(a, b):
    return lax.dot_general(a, b, (((0,), (0,)), ((), ())), preferred_element_type=F32)


def _split2(x):
    hi = x.astype(BF16)
    lo = (x - hi.astype(F32)).astype(BF16)
    return hi, lo


def _dot_sel_r(x, sel):
    hi, lo = _split2(x)
    return _dot(hi, sel) + _dot(lo, sel)


def _dot_sel_l(sel, x):
    hi, lo = _split2(x)
    return _dot(sel, hi) + _dot(sel, lo)


def _causal_blocks(n, blk):
    i = lax.broadcasted_iota(jnp.int32, (n, n), 0)
    j = lax.broadcasted_iota(jnp.int32, (n, n), 1)
    keep = jnp.logical_and(i >= j, jnp.bitwise_xor(i, j) < blk)
    return jnp.where(keep, 1.0, 0.0).astype(BF16)


def _sigmoid(x):
    return jax.nn.sigmoid(x)


def _softplus(x):
    return jnp.maximum(x, 0.0) + jnp.log1p(jnp.exp(-jnp.abs(x)))


def _layer_norm(s, g, b):
    mu = jnp.mean(s, axis=-1, keepdims=True)
    d = s - mu
    var = jnp.mean(d * d, axis=-1, keepdims=True)
    return d * lax.rsqrt(var + LN_EPS) * g + b


class _CastSlot:
    def __init__(self, w, rows, t0):
        assert w.ndim == 2 and w.shape[0] % rows == 0
        self.w, self.rows, self.t0, self.n = w, rows, t0, w.shape[0] // rows

    def spec(self, step_of):
        def index(*ids):
            return jnp.clip(step_of(*ids) - self.t0, 0, self.n - 1), 0
        return pl.BlockSpec((self.rows, self.w.shape[1]), index)

    def out_shape(self):
        return jax.ShapeDtypeStruct(self.w.shape, BF16)


def _run_cast_slots(t, slots, in_refs, out_refs):
    for slot, i_ref, o_ref in zip(slots, in_refs, out_refs, strict=True):
        @pl.when(jnp.logical_and(t >= slot.t0, t < slot.t0 + slot.n))
        def _(i_ref=i_ref, o_ref=o_ref):
            o_ref[...] = i_ref[...].astype(BF16)


def _split_refs(refs, *counts):
    groups, at = [], 0
    for n in counts:
        groups.append(refs[at:at + n])
        at += n
    return (*groups, refs[at:])


def _chain_slots(weights_rows, t0=0):
    slots = []
    for w, rows in weights_rows:
        slots.append(_CastSlot(w, rows, t0))
        t0 += slots[-1].n
    return slots, t0


N_MAIN_TILES = F_ROWS // FFN_TM


N_TILES = M_ROWS // FFN_TM


def _ffn_body(load_x, emit_mid, emit_last, wg_ref, wu_ref, wd_ref, g_ref, b_ref, xb_ref, acc_ref, s_ref):
    i, f = pl.program_id(0), pl.program_id(1)
    last = pl.num_programs(1) - 1
    real = i < N_TILES
    both = jnp.logical_and

    def mlp(xb):
        gate = _dot(xb, wg_ref[...])
        up = _dot(xb, wu_ref[...])
        return _dot((gate * _sigmoid(gate) * up).astype(BF16), wd_ref[...])

    def finish_prev(emit):
        emit(_layer_norm(s_ref[...], g_ref[...], b_ref[...]))

    def first_step(with_prev):
        xb = load_x().astype(BF16)
        xb_ref[...] = xb
        acc_ref[...] = mlp(xb)
        if with_prev:
            finish_prev(emit_mid)

    pl.when(both(f == 0, i == 0))(functools.partial(first_step, False))
    pl.when(both(f == 0, both(i > 0, real)))(functools.partial(first_step, True))
    pl.when(both(f == 0, i == N_TILES))(functools.partial(finish_prev, emit_last))

    @pl.when(both(real, both(f > 0, f < last)))
    def _():
        acc_ref[...] += mlp(xb_ref[...])

    @pl.when(both(real, f == last))
    def _():
        s_ref[...] = ALPHA * load_x() + 0.5 * (acc_ref[...] + mlp(xb_ref[...]))


def _ffn_in_kernel(xm_ref, xe_ref, wg_ref, wu_ref, wd_ref, g_ref, b_ref, o_ref, ob_ref, *scratch):
    load_x = lambda: jnp.where(pl.program_id(0) < N_MAIN_TILES, xm_ref[...], xe_ref[...])

    def emit(y):
        o_ref[...] = y
        ob_ref[...] = y.astype(BF16)

    _ffn_body(load_x, emit, emit, wg_ref, wu_ref, wd_ref, g_ref, b_ref, *scratch)


def _ffn_out_kernel(x_ref, wg_ref, wu_ref, wd_ref, g_ref, b_ref, om_ref, oe_ref, *scratch):
    assert N_TILES == N_MAIN_TILES + 1

    def emit_frames(y):
        om_ref[...] = y

    def emit_extra(y):
        oe_ref[...] = y

    _ffn_body(lambda: x_ref[...], emit_frames, emit_extra, wg_ref, wu_ref, wd_ref, g_ref, b_ref, *scratch)


def _ffn_specs(d):
    nf = D_FF // FFN_TF
    step = lambda i, f: jnp.where(i == N_TILES, nf - 1, f)
    return nf, [
        pl.BlockSpec((d, FFN_TF), lambda i, f: (0, step(i, f))),
        pl.BlockSpec((d, FFN_TF), lambda i, f: (0, step(i, f) + nf)),
        pl.BlockSpec((FFN_TF, d), lambda i, f: (step(i, f), 0)),
        pl.BlockSpec((1, d), lambda i, f: (0, 0)),
        pl.BlockSpec((1, d), lambda i, f: (0, 0)),
    ]


_FFN_PARAMS = dict(dimension_semantics=("arbitrary", "arbitrary"), vmem_limit_bytes=VMEM_LIMIT)
_in_frame_tile = lambda i, f: (jnp.minimum(i, N_MAIN_TILES - 1), 0)
_in_any_tile = lambda i, f: (jnp.minimum(i, N_TILES - 1), 0)
_out_frame_tile = lambda i, f: (jnp.clip(i - 1, 0, N_MAIN_TILES - 1), 0)
_out_any_tile = lambda i, f: (jnp.maximum(i - 1, 0), 0)
_only_tile = lambda i, f: (0, 0)


def _ffn_scratch(tile):
    return [pltpu.VMEM(tile, BF16), pltpu.VMEM(tile, F32), pltpu.VMEM(tile, F32)]


def _ffn_ln_in(x_main, x_extra, w_gu, w_dn, ln_g, ln_b):
    d = x_main.shape[1]
    nf, wspecs = _ffn_specs(d)
    tile = (FFN_TM, d)
    return pl.pallas_call(
        _ffn_in_kernel,
        grid=(N_TILES + 1, nf),
        in_specs=[pl.BlockSpec(tile, _in_frame_tile),
                  pl.BlockSpec(tile, _only_tile, pipeline_mode=pl.Buffered(1))] + wspecs,
        out_specs=[pl.BlockSpec(tile, _out_any_tile), pl.BlockSpec(tile, _out_any_tile)],
        out_shape=[jax.ShapeDtypeStruct((M_ROWS, d), F32), jax.ShapeDtypeStruct((M_ROWS, d), BF16)],
        scratch_shapes=_ffn_scratch(tile),
        compiler_params=pltpu.CompilerParams(**_FFN_PARAMS),
        name="ffn_ln_in",
    )(x_main, x_extra, w_gu, w_gu, w_dn, ln_g, ln_b)


def _ffn_ln_out(x, w_gu, w_dn, ln_g, ln_b):
    d = x.shape[1]
    nf, wspecs = _ffn_specs(d)
    tile = (FFN_TM, d)
    return pl.pallas_call(
        _ffn_out_kernel,
        grid=(N_TILES + 1, nf),
        in_specs=[pl.BlockSpec(tile, _in_any_tile)] + wspecs,
        out_specs=[pl.BlockSpec(tile, _out_frame_tile), pl.BlockSpec(tile, _only_tile)],
        out_shape=[jax.ShapeDtypeStruct((F_ROWS, d), F32), jax.ShapeDtypeStruct((X_ROWS, d), F32)],
        scratch_shapes=_ffn_scratch(tile),
        compiler_params=pltpu.CompilerParams(**_FFN_PARAMS),
        name="ffn_ln_out",
    )(x, w_gu, w_gu, w_dn, ln_g, ln_b)


def _proj_tiles():
    o_z = RW_SHIFT_COLS
    o_xbc = o_z + SSM_DIM
    o_dt = o_xbc + CONV_DIM
    o_gate = o_dt + SSM_HEADS
    runs = [(PC_RW, 0, 3 * RW_DIM), (PC_GATE, o_gate, 2 * D_MODEL), (PC_Z, o_z, SSM_DIM), (PC_XS, o_xbc, CONV_DIM)]
    src = [None] * (P_COLS // MM_TN)
    for dst0, src0, width in runs:
        assert dst0 % MM_TN == 0 and width % MM_TN == 0 and src0 % (2 * SUBLANES) == 0
        for t in range(width // MM_TN):
            src[dst0 // MM_TN + t] = src0 + t * MM_TN
    assert src.count(None) == 1
    return src


def _mm_kernel(x_ref, wt_ref, wmix_ref, o_ref, *, mixed_tile):
    j = pl.program_id(0)

    @pl.when(j != mixed_tile)
    def _():
        o_ref[...] = _dot_nt(x_ref[...], wt_ref[...])

    @pl.when(j == mixed_tile)
    def _():
        o_ref[...] = _dot_nt(x_ref[...], wmix_ref[...])


def _project(x, wt, wt_mixed):
    m, k = x.shape
    src = _proj_tiles()
    mixed_tile = src.index(None)

    unit = 2 * SUBLANES

    def wt_row(j, i):
        start = 0
        for t, s in enumerate(src):
            if s is not None:
                start = jnp.where(j == t, s // unit, start)
        return start * unit, 0

    return pl.pallas_call(
        functools.partial(_mm_kernel, mixed_tile=mixed_tile),
        grid=(len(src), m // MM_TM),
        in_specs=[pl.BlockSpec((MM_TM, k), lambda j, i: (i, 0)),
                  pl.BlockSpec((pl.Element(MM_TN), pl.Element(k)), wt_row),
                  pl.BlockSpec((MM_TN, k), lambda j, i: (0, 0), pipeline_mode=pl.Buffered(1))],
        out_specs=pl.BlockSpec((MM_TM, MM_TN), lambda j, i: (i, j)),
        out_shape=jax.ShapeDtypeStruct((m, P_COLS), F32),
        compiler_params=pltpu.CompilerParams(
            dimension_semantics=("parallel", "parallel"), vmem_limit_bytes=VMEM_LIMIT),
        name="proj_in",
    )(x, wt, wt_mixed)


def _rwkv_kernel(*refs, chunk, pad, n_seq, n_p, slots):
    p_refs, fixed, cast_in, outs, cast_out, (yacc_ref,) = _split_refs(refs, n_p, 14, len(slots), 3, len(slots))
    (hist_ref, s0_ref, mu_ref, w0_ref, w2_ref, a0_ref, a2_ref, g2_ref, kk_ref, ka_ref, rk_ref,
     gnw_ref, gnb_ref, seg_ref) = fixed
    y_ref, s_ref, shift_ref = outs
    c = pl.program_id(1)
    _run_cast_slots(pl.program_id(0) * pl.num_programs(1) + c, slots, cast_in, cast_out)
    n_rows = n_seq * chunk
    seqs = range(n_seq)
    rsl = [slice(q * chunk, (q + 1) * chunk) for q in seqs]
    rows = lax.broadcasted_iota(jnp.int32, (n_rows, 1), 0)

    @pl.when(c == 0)
    def _():
        shift_ref[...] = hist_ref[...]
        s_ref[...] = s0_ref[...]

    p = jnp.concatenate([r[...] for r in p_refs], axis=0)
    if pad:
        p = jnp.where(jnp.logical_and(c == 0, jnp.bitwise_and(rows, chunk - 1) < pad), 0.0, p)
    prev = pltpu.roll(p, 1, 0)
    for q in seqs:
        prev = jnp.where(rows == q * chunk, shift_ref[q], prev)
    for q in seqs:
        shift_ref[q] = p[(q + 1) * chunk - 1:(q + 1) * chunk, :]
    ps = p + (prev - p) * mu_ref[...]

    r = ps[:, 0:RW_DIM]
    k = ps[:, RW_DIM:2 * RW_DIM]
    v = ps[:, 2 * RW_DIM:3 * RW_DIM]
    wd = ps[:, RW_OFF_WD:RW_OFF_WD + LORA_PAD]
    ad = ps[:, RW_OFF_AD:RW_OFF_AD + LORA_PAD]
    gd = ps[:, RW_OFF_GD:RW_OFF_GD + RW_GATE_LORA]

    lw = -math.exp(-0.5) * _sigmoid(w0_ref[...] + _dot(jnp.tanh(wd).astype(BF16), w2_ref[...]))
    a = _sigmoid(a0_ref[...] + _dot(ad.astype(BF16), a2_ref[...]))
    g = _dot(_sigmoid(gd).astype(BF16), g2_ref[...])

    seg = seg_ref[...]

    def head_sum(x):
        w = seg.shape[0]
        return jnp.concatenate(
            [_dot_sel_r(x[:, j * w:(j + 1) * w], seg) for j in range(RW_DIM // w)], axis=1)

    kk = k * kk_ref[...]
    kk = kk * lax.rsqrt(jnp.maximum(head_sum(kk * kk), 1e-24))
    k = k * (1.0 + (a - 1.0) * ka_ref[...])
    b_neg = -(kk * a)

    cum = _dot_sel_l(_causal_blocks(n_rows, chunk), lw)
    cum_last = [cum[(q + 1) * chunk - 1:(q + 1) * chunk, :] for q in seqs]
    cum_end = jnp.concatenate([jnp.broadcast_to(x, (chunk, RW_DIM)) for x in cum_last], axis=0)
    e_neg = jnp.exp(-cum)
    kt = (kk * jnp.exp(cum - lw)).astype(BF16)
    rt = (r * jnp.exp(cum)).astype(BF16)
    kd = (k * e_neg).astype(BF16)
    bd_neg = (b_neg * e_neg).astype(BF16)
    e_end = jnp.exp(cum_end - cum)
    k_end = (k * e_end).astype(BF16)
    b_end_neg = (b_neg * e_end).astype(BF16)
    p_end = [jnp.exp(x) for x in cum_last]
    vb = v.astype(BF16)

    ri = lax.broadcasted_iota(jnp.int32, (2 * chunk, 2 * chunk), 0)
    rj = lax.broadcasted_iota(jnp.int32, (2 * chunk, 2 * chunk), 1)
    bi = jnp.where(ri >= chunk, ri - chunk + 1, ri)
    bj = jnp.where(rj >= chunk, rj - chunk, rj)
    keep = bi > bj
    ti = lax.broadcasted_iota(jnp.int32, (chunk, 2 * chunk), 0)
    tj = lax.broadcasted_iota(jnp.int32, (chunk, 2 * chunk), 1)
    hi_lane = tj >= chunk
    eye_hi = (tj == ti + chunk).astype(F32)
    zeros_v = jnp.zeros((chunk, HEAD), BF16)

    pairs = [(q, h) for q in seqs for h in range(RW_HEADS)]
    sls = [(rsl[q], slice(h * HEAD, (h + 1) * HEAD)) for q, h in pairs]
    s_old = [s_ref[q, h] for q, h in pairs]
    s_b = [s.astype(BF16) for s in s_old]
    amats = [jnp.where(keep, _dot_nt(jnp.concatenate([kt[sl], rt[sl]], axis=0),
                                     jnp.concatenate([bd_neg[sl], kd[sl]], axis=0)), 0.0)
             for sl in sls]
    top = [m[:chunk] for m in amats]
    bot = [m[chunk:].astype(BF16) for m in amats]
    top_b = [x.astype(BF16) for x in top]
    tq = [_dot(xb[:, :chunk], jnp.where(hi_lane, eye_hi, x).astype(BF16)) + eye_hi
          for x, xb in zip(top, top_b)]
    n = 2
    while n < chunk:
        tq = [jnp.where(hi_lane, x, 0.0) + _dot(x[:, :chunk].astype(BF16), x.astype(BF16)) for x in tq]
        n *= 2
    t_b = [x[:, chunk:].astype(BF16) for x in tq]
    v_h = [vb[sl] for sl in sls]
    rhs = [_dot_nt(kt[sl], sb) + _dot(xb, jnp.concatenate([zeros_v, vh], axis=0))
           for sl, sb, xb, vh in zip(sls, s_b, top_b, v_h)]
    ub = [_dot(t, x.astype(BF16)).astype(BF16) for t, x in zip(t_b, rhs)]
    uv = [jnp.concatenate([u, vh], axis=0) for u, vh in zip(ub, v_h)]
    for i, (q, h) in enumerate(pairs):
        sl = sls[i]
        yacc_ref[sl] = _dot_nt(rt[sl], s_b[i]) + _dot(bot[i], uv[i])
        s_ref[q, h] = s_old[i] * p_end[q][:, sl[1]] + _dot_tn(
            uv[i], jnp.concatenate([b_end_neg[sl], k_end[sl]], axis=0))

    y = yacc_ref[...]
    inv_n = 1.0 / HEAD
    mean = head_sum(y) * inv_n
    yc = y - mean
    var = head_sum(yc * yc) * inv_n
    yn = yc * lax.rsqrt(var + RW_GN_EPS) * gnw_ref[...] + gnb_ref[...]
    bonus = head_sum(r * k * rk_ref[...]) * v
    out = ((yn + bonus) * g).astype(y_ref.dtype)
    for q in seqs:
        y_ref[q] = out[rsl[q]]


RWKV_GROUPS = (2, 8)
SSD_GROUPS = (1, 1)


class _SeqPlan:
    def __init__(self, prompt, groups):
        self.prompt = prompt
        if prompt:
            self.n_seq, self.chunk, self.pad, self.total, self.seq_len = groups[0], CHUNK, PAD, BATCH, SEQ
            self.grid = (BATCH // self.n_seq, SEQ_CHUNKS)
        else:
            self.n_seq, self.chunk, self.pad, self.total, self.seq_len = groups[1], DEC_SEQ, 0, DEC_BATCH, DEC_SEQ
            self.grid = (DEC_BATCH // self.n_seq, 1)

    def p_specs(self, width, col):
        cb = col // width
        if not self.prompt:
            rows = self.n_seq * self.chunk
            return [pl.BlockSpec((rows, width), lambda g, c: (ROW_S0 // rows + g, cb))]

        def spec(q):
            def index(g, c):
                seq = g * self.n_seq + q
                return jnp.where(c == 0, ROW_H0 // CHUNK + seq, seq * (SEQ // CHUNK) + c - 1), cb
            return pl.BlockSpec((CHUNK, width), index)
        return [spec(q) for q in range(self.n_seq)]

    def y_spec(self, dim):
        return pl.BlockSpec((self.n_seq, self.chunk, dim), lambda g, c: (g, jnp.maximum(c - 1, 0), 0))

    def y_shape(self, dim):
        return jax.ShapeDtypeStruct((self.total, self.seq_len, dim), BF16)

    def state_spec(self, shape):
        nd = len(shape)
        return pl.BlockSpec((self.n_seq,) + tuple(shape), lambda g, c: (g,) + (0,) * nd)

    def cast_slots(self, cast_weights):
        slots, t_end = _chain_slots(cast_weights)
        assert t_end <= self.grid[0] * self.grid[1]
        specs = [s.spec(lambda g, c: g * self.grid[1] + c) for s in slots]
        return slots, specs, specs


_MIX_PARAMS = dict(dimension_semantics=("arbitrary", "arbitrary"), vmem_limit_bytes=VMEM_LIMIT)


def _rwkv_mix(p_all, hist, s0, wts, *, prompt, cast_weights=()):
    plan = _SeqPlan(prompt, RWKV_GROUPS)
    slots, cast_in_specs, cast_out_specs = plan.cast_slots(cast_weights)
    p_specs = plan.p_specs(RW_COLS, PC_RW)
    const2 = lambda g, c: (0, 0)
    wspecs = [pl.BlockSpec(w.shape, const2) for w in wts]
    shift_spec = plan.state_spec((1, RW_COLS))
    wkv_spec = plan.state_spec((RW_HEADS, HEAD, HEAD))
    return pl.pallas_call(
        functools.partial(_rwkv_kernel, chunk=plan.chunk, pad=plan.pad, n_seq=plan.n_seq, n_p=len(p_specs),
                          slots=slots),
        grid=plan.grid,
        in_specs=p_specs + [shift_spec, wkv_spec] + wspecs + cast_in_specs,
        out_specs=[plan.y_spec(RW_DIM), wkv_spec, shift_spec] + cast_out_specs,
        out_shape=[plan.y_shape(RW_DIM),
                   jax.ShapeDtypeStruct((plan.total, RW_HEADS, HEAD, HEAD), F32),
                   jax.ShapeDtypeStruct((plan.total, 1, RW_COLS), F32)] + [s.out_shape() for s in slots],
        scratch_shapes=[pltpu.VMEM((plan.n_seq * plan.chunk, RW_DIM), F32)],
        compiler_params=pltpu.CompilerParams(**_MIX_PARAMS),
        name="rwkv_mix_c%d" % plan.chunk,
    )(*([p_all] * len(p_specs)), hist, s0, *wts, *[s.w for s in slots])


def _ssd_kernel(*refs, chunk, pad, n_seq, n_p, slots):
    (xs_refs, bc_refs, z_refs, dt_refs, fixed, cast_in, outs, cast_out, (xpad_ref, yacc_ref)) = _split_refs(
        refs, n_p, n_p, n_p, n_p, 9, len(slots), 3, len(slots))
    hist_ref, h0_ref, cw_ref, cb_ref, dtb_ref, alog_ref, dskip_ref, nw_ref, exp_ref = fixed
    y_ref, h_ref, tail_ref = outs
    c = pl.program_id(1)
    _run_cast_slots(pl.program_id(0) * pl.num_programs(1) + c, slots, cast_in, cast_out)
    n_rows = n_seq * chunk
    seqs = range(n_seq)
    rsl = [slice(q * chunk, (q + 1) * chunk) for q in seqs]
    rows = lax.broadcasted_iota(jnp.int32, (n_rows, 1), 0)
    stack = lambda rs: jnp.concatenate([r[...] for r in rs], axis=0)

    @pl.when(c == 0)
    def _():
        xpad_ref[:, 0:CONV_HIST, :] = hist_ref[...]
        h_ref[...] = h0_ref[...]

    u = jnp.concatenate([stack(xs_refs), stack(bc_refs)], axis=1)
    if pad:
        is_pad = jnp.logical_and(c == 0, jnp.bitwise_and(rows, chunk - 1) < pad)
        u = jnp.where(is_pad, 0.0, u)
    convs = []
    for q in seqs:
        u_q = u[rsl[q]]
        xpad_ref[q, CONV_HIST:CONV_HIST + chunk, :] = u_q
        conv = cb_ref[...] + u_q * cw_ref[CONV_W - 1:CONV_W, :]
        for i in range(CONV_W - 1):
            back = CONV_W - 1 - i
            conv = conv + xpad_ref[q, CONV_HIST - back:CONV_HIST - back + chunk, :] * cw_ref[i:i + 1, :]
        hist_next = xpad_ref[q, chunk:chunk + CONV_HIST, :]
        xpad_ref[q, 0:CONV_HIST, :] = hist_next
        tail_ref[q] = hist_next
        convs.append(conv)
    conv = jnp.concatenate(convs, axis=0)
    xbc = conv * _sigmoid(conv)
    xs = xbc[:, 0:SSM_DIM]
    bm = xbc[:, SSM_DIM:SSM_DIM + BC_DIM].astype(BF16)
    cm = xbc[:, SSM_DIM + BC_DIM:CONV_DIM].astype(BF16)

    dt = _softplus(stack(dt_refs) + dtb_ref[...])
    if pad:
        dt = jnp.where(is_pad, 0.0, dt)
    da = dt * (-jnp.exp(alog_ref[...]))
    ci = lax.broadcasted_iota(jnp.int32, (chunk, chunk), 0)
    cj = lax.broadcasted_iota(jnp.int32, (chunk, chunk), 1)
    causal = ci >= cj
    acs = _dot_sel_l(_causal_blocks(n_rows, chunk), da)
    acs_t = acs.T
    acs_end = jnp.concatenate(
        [jnp.broadcast_to(acs[(q + 1) * chunk - 1:(q + 1) * chunk, :], (chunk, DT_COLS)) for q in seqs], axis=0)

    expand = exp_ref[...]
    xdt = xs * _dot_sel_r(dt, expand)
    xdt_b = xdt.astype(BF16)
    xdt_end = (xdt * _dot_sel_r(jnp.exp(acs_end - acs), expand)).astype(BF16)
    e_acs = _dot_sel_r(jnp.exp(acs), expand)
    decay_end = [jnp.exp(acs_t[:, (q + 1) * chunk - 1:(q + 1) * chunk]) for q in seqs]

    gw = SSM_HPG * HEAD
    groups = range(SSM_GROUPS)
    heads = range(SSM_HEADS)
    qg = [(q, g) for q in seqs for g in groups]
    qh = [(q, hd) for q in seqs for hd in heads]
    bm_g = {(q, g): bm[rsl[q], g * SSM_STATE:(g + 1) * SSM_STATE] for q, g in qg}
    cm_g = {(q, g): cm[rsl[q], g * SSM_STATE:(g + 1) * SSM_STATE] for q, g in qg}
    h_old = {(q, hd): h_ref[q, hd] for q, hd in qh}
    cb = {k: _dot_nt(cm_g[k], bm_g[k]) for k in qg}
    y_off = {(q, g): _dot_nt(cm_g[q, g], jnp.concatenate(
        [h_old[q, g * SSM_HPG + hh].astype(BF16) for hh in range(SSM_HPG)], axis=0)) for q, g in qg}
    st = {(q, g): _dot_tn(xdt_end[rsl[q], g * gw:(g + 1) * gw], bm_g[q, g]) for q, g in qg}
    for q, hd in qh:
        sl = slice(hd * HEAD, (hd + 1) * HEAD)
        seg = acs[rsl[q], hd:hd + 1] - acs_t[hd:hd + 1, rsl[q]]
        lmat = jnp.exp(jnp.where(causal, seg, -jnp.inf))
        yacc_ref[rsl[q], sl] = _dot((cb[q, hd // SSM_HPG] * lmat).astype(BF16), xdt_b[rsl[q], sl])
    for q, hd in qh:
        g, hh = divmod(hd, SSM_HPG)
        h_ref[q, hd] = h_old[q, hd] * decay_end[q][hd:hd + 1, :] + st[q, g][hh * HEAD:(hh + 1) * HEAD, :]

    y_off_all = jnp.concatenate(
        [jnp.concatenate([y_off[q, g] for g in groups], axis=1) for q in seqs], axis=0)
    y = yacc_ref[...] + y_off_all * e_acs + xs * dskip_ref[...]
    zz = stack(z_refs)
    y = y * (zz * _sigmoid(zz))
    parts = []
    for g in groups:
        yg = y[:, g * gw:(g + 1) * gw]
        ms = jnp.mean(yg * yg, axis=-1, keepdims=True)
        parts.append(yg * lax.rsqrt(ms + RMS_EPS))
    out = (jnp.concatenate(parts, axis=1) * nw_ref[...]).astype(y_ref.dtype)
    for q in seqs:
        y_ref[q] = out[rsl[q]]


def _ssd_mix(p_all, hist, h0, wts, *, prompt, cast_weights=()):
    plan = _SeqPlan(prompt, SSD_GROUPS)
    slots, cast_in_specs, cast_out_specs = plan.cast_slots(cast_weights)
    p_specs = (plan.p_specs(SSM_DIM, PC_XS) + plan.p_specs(2 * BC_DIM, PC_BC)
               + plan.p_specs(SSM_DIM, PC_Z) + plan.p_specs(DT_COLS, PC_DT))
    const2 = lambda g, c: (0, 0)
    wspecs = [pl.BlockSpec(w.shape, const2) for w in wts]
    conv_spec = plan.state_spec((CONV_HIST, CONV_DIM))
    ssm_spec = plan.state_spec((SSM_HEADS, HEAD, SSM_STATE))
    return pl.pallas_call(
        functools.partial(_ssd_kernel, chunk=plan.chunk, pad=plan.pad, n_seq=plan.n_seq, n_p=len(p_specs) // 4,
                          slots=slots),
        grid=plan.grid,
        in_specs=p_specs + [conv_spec, ssm_spec] + wspecs + cast_in_specs,
        out_specs=[plan.y_spec(SSM_DIM), ssm_spec, conv_spec] + cast_out_specs,
        out_shape=[plan.y_shape(SSM_DIM),
                   jax.ShapeDtypeStruct((plan.total, SSM_HEADS, HEAD, SSM_STATE), F32),
                   jax.ShapeDtypeStruct((plan.total, CONV_HIST, CONV_DIM), F32)] + [s.out_shape() for s in slots],
        scratch_shapes=[pltpu.VMEM((plan.n_seq, CONV_HIST + plan.chunk, CONV_DIM), F32),
                        pltpu.VMEM((plan.n_seq * plan.chunk, SSM_DIM), F32)],
        compiler_params=pltpu.CompilerParams(**_MIX_PARAMS),
        name="ssd_mix_c%d" % plan.chunk,
    )(*([p_all] * len(p_specs)), hist, h0, *wts, *[s.w for s in slots])


MERGE_F_TILES = F_ROWS // MERGE_TM
assert S_ROWS == MERGE_TM


def _merge_kernel(x_ref, yrw_f_ref, yrw_s_ref, yssm_f_ref, yssm_s_ref, ga_ref, gb_ref, bga_ref, bgb_ref,
                  wrw_ref, wssm_ref, wout_ref, g_ref, b_ref, o_ref):
    is_frame = pl.program_id(0) < MERGE_F_TILES
    y_rw = jnp.where(is_frame, yrw_f_ref[...], yrw_s_ref[...])
    y_ssm = jnp.where(is_frame, yssm_f_ref[...], yssm_s_ref[...])
    merged = (_sigmoid(ga_ref[...] + bga_ref[...]) * _dot(y_rw, wrw_ref[...])
              + _sigmoid(gb_ref[...] + bgb_ref[...]) * _dot(y_ssm, wssm_ref[...]))
    s = ALPHA * x_ref[...] + _dot(merged.astype(BF16), wout_ref[...])
    o_ref[...] = _layer_norm(s, g_ref[...], b_ref[...])


def _merge(x, y_rw_f, y_rw_s, y_ssm_f, y_ssm_s, p_all, b_gate, w_rw_out, w_ssm_out, w_out, ln_g, ln_b):
    m, d = x.shape
    tm = MERGE_TM
    row = lambda i: (i, 0)
    frame_row = lambda i: (jnp.minimum(i, MERGE_F_TILES - 1), 0)
    const = lambda i: (0, 0)
    resident = lambda w: pl.BlockSpec(w.shape, const, pipeline_mode=pl.Buffered(1))
    return pl.pallas_call(
        _merge_kernel,
        grid=(m // tm,),
        in_specs=[
            pl.BlockSpec((tm, d), row),
            pl.BlockSpec((tm, RW_DIM), frame_row),
            pl.BlockSpec((tm, RW_DIM), const),
            pl.BlockSpec((tm, SSM_DIM), frame_row),
            pl.BlockSpec((tm, SSM_DIM), const),
            pl.BlockSpec((tm, d), lambda i: (i, PC_GATE // d)),
            pl.BlockSpec((tm, d), lambda i: (i, PC_GATE // d + 1)),
            pl.BlockSpec((1, d), lambda i: (0, 0)),
            pl.BlockSpec((1, d), lambda i: (0, 1)),
            resident(w_rw_out), resident(w_ssm_out), resident(w_out),
            pl.BlockSpec((1, d), const),
            pl.BlockSpec((1, d), const),
        ],
        out_specs=pl.BlockSpec((tm, d), row),
        out_shape=jax.ShapeDtypeStruct((m, d), F32),
        compiler_params=pltpu.CompilerParams(
            dimension_semantics=("arbitrary",), vmem_limit_bytes=VMEM_LIMIT),
        name="merge_ln",
    )(x, y_rw_f, y_rw_s, y_ssm_f, y_ssm_s, p_all, p_all, b_gate, b_gate, w_rw_out, w_ssm_out, w_out,
      ln_g, ln_b)


def _pad_cols(x, width):
    return jnp.concatenate([x, jnp.zeros(x.shape[:-1] + (width - x.shape[-1],), x.dtype)], axis=-1)


def _rw_cols(x):
    o_wd, o_ad, o_gd = 3 * RW_DIM, 3 * RW_DIM + RW_DECAY_LORA, 3 * RW_DIM + RW_DECAY_LORA + RW_AAA_LORA
    return jnp.concatenate([
        x[..., :o_wd],
        _pad_cols(x[..., o_wd:o_ad], LORA_PAD),
        _pad_cols(x[..., o_ad:o_gd], LORA_PAD),
        x[..., o_gd:],
    ], axis=-1)


def _pad_rows(x, height):
    return jnp.concatenate([x, jnp.zeros((height - x.shape[0],) + x.shape[1:], x.dtype)], axis=0)


def _rw_rows(x):
    o_wd, o_ad, o_gd = 3 * RW_DIM, 3 * RW_DIM + RW_DECAY_LORA, 3 * RW_DIM + RW_DECAY_LORA + RW_AAA_LORA
    return jnp.concatenate([
        x[:o_wd], _pad_rows(x[o_wd:o_ad], LORA_PAD), _pad_rows(x[o_ad:o_gd], LORA_PAD), x[o_gd:]], axis=0)


def _rw_cols_inv(x):
    return jnp.concatenate([
        x[..., :RW_OFF_WD],
        x[..., RW_OFF_WD:RW_OFF_WD + RW_DECAY_LORA],
        x[..., RW_OFF_AD:RW_OFF_AD + RW_AAA_LORA],
        x[..., RW_OFF_GD:],
    ], axis=-1)


def _block_ones(n, blk):
    i = jnp.arange(n) // blk
    return (i[:, None] == i[None, :]).astype(BF16)


def kernel(x_prompt, x_sample, state_rwkv_shift, state_wkv, state_conv, state_ssm, meta_tokens, ffn1_gu, ffn1_dn, ln1_g, ln1_b, w_in, b_gate, rw_mu, rw_w0, rw_w2, rw_a0, rw_a2, rw_g2, rw_kk, rw_ka, rw_rk, rw_gn_w, rw_gn_b, conv_w, conv_b, dt_bias, a_log, d_skip, ssm_norm_w, w_rw_out, w_ssm_out, w_out, ln2_g, ln2_b, ffn2_gu, ffn2_dn, ln3_g, ln3_b):
    lyr = 0
    row = lambda t: t[lyr].reshape(1, -1).astype(F32)

    head_rows = jnp.concatenate([jnp.zeros((PAD, D_MODEL), F32), meta_tokens.astype(F32)], axis=0)
    x_extra = jnp.concatenate([x_sample.reshape(S_ROWS, D_MODEL)] + [head_rows] * BATCH, axis=0)
    x1, x1b = _ffn_ln_in(x_prompt.reshape(F_ROWS, D_MODEL), x_extra, ffn1_gu[lyr].astype(BF16),
                         ffn1_dn[lyr].astype(BF16), row(ln1_g), row(ln1_b))

    wt = jnp.swapaxes(w_in[lyr], 0, 1).astype(BF16)
    o_dt = RW_SHIFT_COLS + SSM_DIM + CONV_DIM
    wt_mixed = jnp.concatenate([
        _rw_rows(wt[:RW_SHIFT_COLS])[3 * RW_DIM:], _pad_rows(wt[o_dt:o_dt + SSM_HEADS], DT_SLOT)], axis=0)
    p_all = _project(x1b, wt, wt_mixed)

    rw_wts = [
        _rw_cols(rw_mu[lyr]).reshape(1, RW_COLS), row(rw_w0),
        _pad_rows(rw_w2[lyr], LORA_PAD).astype(BF16), row(rw_a0),
        _pad_rows(rw_a2[lyr], LORA_PAD).astype(BF16), rw_g2[lyr].astype(BF16),
        row(rw_kk), row(rw_ka), row(rw_rk), row(rw_gn_w), row(rw_gn_b),
        _block_ones(2 * LANES, HEAD),
    ]
    y_rw_s, wkv_s, shift_s = _rwkv_mix(
        p_all, _rw_cols(state_rwkv_shift[lyr]), state_wkv[lyr], rw_wts, prompt=False)
    y_rw_f, wkv_p, shift_p, w_rw_o, w_ssm_o, w_o = _rwkv_mix(
        p_all, jnp.zeros((BATCH, 1, RW_COLS), F32), jnp.zeros((BATCH, RW_HEADS, HEAD, HEAD), F32), rw_wts,
        prompt=True, cast_weights=[(w_rw_out[lyr], MIX_CAST_ROWS), (w_ssm_out[lyr], MIX_CAST_ROWS),
                                   (w_out[lyr], MIX_CAST_ROWS)])

    head_of_lane = jnp.arange(SSM_DIM) // HEAD
    expand = (jnp.arange(DT_COLS)[:, None] == head_of_lane[None, :]).astype(BF16)
    ssd_wts = [
        conv_w[lyr], row(conv_b), _pad_cols(row(dt_bias), DT_COLS), _pad_cols(row(a_log), DT_COLS),
        jnp.repeat(d_skip[lyr], HEAD).reshape(1, SSM_DIM), row(ssm_norm_w),
        expand,
    ]
    hist_rows = lambda t: jnp.pad(t, ((0, 0), (CONV_HIST - (CONV_W - 1), 0), (0, 0)))
    y_ssm_s, ssm_s, conv_s = _ssd_mix(
        p_all, hist_rows(state_conv[lyr]), state_ssm[lyr], ssd_wts, prompt=False)
    y_ssm_f, ssm_p, conv_p, gu2, dn2 = _ssd_mix(
        p_all, jnp.zeros((BATCH, CONV_HIST, CONV_DIM), F32),
        jnp.zeros((BATCH, SSM_HEADS, HEAD, SSM_STATE), F32), ssd_wts, prompt=True,
        cast_weights=[(ffn2_gu[lyr], MIX_CAST_ROWS // 4), (ffn2_dn[lyr], MIX_CAST_ROWS)])

    flat = lambda t: t.reshape(-1, t.shape[-1])
    x2 = _merge(x1, flat(y_rw_f), flat(y_rw_s), flat(y_ssm_f), flat(y_ssm_s), p_all, row(b_gate),
                w_rw_o, w_ssm_o, w_o, row(ln2_g), row(ln2_b))
    y_frames, y_extra = _ffn_ln_out(x2, gu2, dn2, row(ln3_g), row(ln3_b))

    y_prompt = y_frames.reshape(BATCH, SEQ, D_MODEL)
    y_sample = y_extra[:S_ROWS].reshape(DEC_BATCH, DEC_SEQ, D_MODEL)
    conv_of = lambda t: t[:, CONV_HIST - (CONV_W - 1):][None]
    return (y_prompt, y_sample,
            _rw_cols_inv(shift_p)[None], wkv_p[None], conv_of(conv_p), ssm_p[None],
            _rw_cols_inv(shift_s)[None], wkv_s[None], conv_of(conv_s), ssm_s[None])
```

```python
import functools
import math

import jax
import jax.numpy as jnp
from jax import lax
from jax.experimental import pallas as pl
from jax.experimental.pallas import tpu as pltpu

F32 = jnp.float32
BF16 = jnp.bfloat16

D_MODEL = 2048
BATCH = 4
SEQ = 2048
DEC_BATCH = 16
DEC_SEQ = 16
CHUNK = 64
N_META = 16
PAD = (-N_META) % CHUNK
HEAD_ROWS = PAD + N_META
SEQ_CHUNKS = (HEAD_ROWS + SEQ) // CHUNK
F_ROWS = BATCH * SEQ
S_ROWS = DEC_BATCH * DEC_SEQ
ROW_S0 = F_ROWS
ROW_H0 = F_ROWS + S_ROWS
X_ROWS = S_ROWS + BATCH * HEAD_ROWS
M_ROWS = F_ROWS + X_ROWS

HEAD = 64
RW_DIM = D_MODEL // 2
RW_HEADS = RW_DIM // HEAD
RW_DECAY_LORA = 96
RW_AAA_LORA = 96
RW_GATE_LORA = 256
RW_SHIFT_COLS = 3 * RW_DIM + RW_DECAY_LORA + RW_AAA_LORA + RW_GATE_LORA
RW_GN_EPS = 64e-5
SSM_DIM = D_MODEL
SSM_HEADS = SSM_DIM // HEAD
SSM_GROUPS = 4
SSM_HPG = SSM_HEADS // SSM_GROUPS
SSM_STATE = 128
CONV_W = 4
BC_DIM = SSM_GROUPS * SSM_STATE
CONV_DIM = SSM_DIM + 2 * BC_DIM
RMS_EPS = 1e-5
D_FF = 5632
LN_EPS = 1e-5
DEPTH = 1
ALPHA = (2 * DEPTH) ** 0.25

LANES = 128
SUBLANES = 8
VMEM_LIMIT = 56 * 1024 * 1024

LORA_PAD = LANES
RW_OFF_WD = 3 * RW_DIM
RW_OFF_AD = RW_OFF_WD + LORA_PAD
RW_OFF_GD = RW_OFF_AD + LORA_PAD
RW_COLS = RW_OFF_GD + RW_GATE_LORA
DT_COLS = LANES
CONV_HIST = SUBLANES

MIX_CAST_ROWS = 128
FFN_TM = 512
FFN_TF = 512
MM_TM = M_ROWS // 8
MM_TN = 1024
MERGE_TM = 256

DT_SLOT = 512
PC_RW = 0
PC_DT = PC_RW + RW_COLS
PC_GATE = PC_DT + DT_SLOT
PC_Z = PC_GATE + 2 * D_MODEL
PC_XS = PC_Z + SSM_DIM
PC_BC = PC_XS + SSM_DIM
P_COLS = PC_BC + 2 * BC_DIM
assert PC_DT % DT_COLS == 0 and PC_GATE % D_MODEL == 0 and PC_Z % SSM_DIM == 0
assert PC_XS % SSM_DIM == 0 and PC_BC % (2 * BC_DIM) == 0 and P_COLS % MM_TN == 0
assert F_ROWS % FFN_TM == 0 and X_ROWS == FFN_TM and M_ROWS % MM_TM == 0 and ROW_H0 % MERGE_TM == 0


def _dot(a, b):
    return jnp.dot(a, b, preferred_element_type=F32)


def _dot_nt(a, b):
    return lax.dot_general(a, b, (((1,), (1,)), ((), ())), preferred_element_type=F32)


def _dot_tn(a, b):
    return lax.dot_general(a, b, (((0,), (0,)), ((), ())), preferred_element_type=F32)


def _split2(x):
    hi = x.astype(BF16)
    lo = (x - hi.astype(F32)).astype(BF16)
    return hi, lo


def _dot_sel_r(x, sel):
    hi, lo = _split2(x)
    return _dot(hi, sel) + _dot(lo, sel)


def _dot_sel_l(sel, x):
    hi, lo = _split2(x)
    return _dot(sel, hi) + _dot(sel, lo)


def _causal_blocks(n, blk):
    i = lax.broadcasted_iota(jnp.int32, (n, n), 0)
    j = lax.broadcasted_iota(jnp.int32, (n, n), 1)
    keep = jnp.logical_and(i >= j, jnp.bitwise_xor(i, j) < blk)
    return jnp.where(keep, 1.0, 0.0).astype(BF16)


def _sigmoid(x):
    return jax.nn.sigmoid(x)


def _softplus(x):
    return jnp.maximum(x, 0.0) + jnp.log1p(jnp.exp(-jnp.abs(x)))


def _layer_norm(s, g, b):
    mu = jnp.mean(s, axis=-1, keepdims=True)
    d = s - mu
    var = jnp.mean(d * d, axis=-1, keepdims=True)
    return d * lax.rsqrt(var + LN_EPS) * g + b


class _CastSlot:
    def __init__(self, w, rows, t0):
        assert w.ndim == 2 and w.shape[0] % rows == 0
        self.w, self.rows, self.t0, self.n = w, rows, t0, w.shape[0] // rows

    def spec(self, step_of):
        def index(*ids):
            return jnp.clip(step_of(*ids) - self.t0, 0, self.n - 1), 0
        return pl.BlockSpec((self.rows, self.w.shape[1]), index)

    def out_shape(self):
        return jax.ShapeDtypeStruct(self.w.shape, BF16)


def _run_cast_slots(t, slots, in_refs, out_refs):
    for slot, i_ref, o_ref in zip(slots, in_refs, out_refs, strict=True):
        @pl.when(jnp.logical_and(t >= slot.t0, t < slot.t0 + slot.n))
        def _(i_ref=i_ref, o_ref=o_ref):
            o_ref[...] = i_ref[...].astype(BF16)


def _split_refs(refs, *counts):
    groups, at = [], 0
    for n in counts:
        groups.append(refs[at:at + n])
        at += n
    return (*groups, refs[at:])


def _chain_slots(weights_rows, t0=0):
    slots = []
    for w, rows in weights_rows:
        slots.append(_CastSlot(w, rows, t0))
        t0 += slots[-1].n
    return slots, t0


N_MAIN_TILES = F_ROWS // FFN_TM
N_TILES = M_ROWS // FFN_TM


def _ffn_body(load_x, emit_mid, emit_last, wg_ref, wu_ref, wd_ref, g_ref, b_ref, xb_ref, acc_ref, s_ref):
    i, f = pl.program_id(0), pl.program_id(1)
    last = pl.num_programs(1) - 1
    real = i < N_TILES
    both = jnp.logical_and

    def mlp(xb):
        gate = _dot(xb, wg_ref[...])
        up = _dot(xb, wu_ref[...])
        return _dot((gate * _sigmoid(gate) * up).astype(BF16), wd_ref[...])

    def finish_prev(emit):
        emit(_layer_norm(s_ref[...], g_ref[...], b_ref[...]))

    def first_step(with_prev):
        xb = load_x().astype(BF16)
        xb_ref[...] = xb
        acc_ref[...] = mlp(xb)
        if with_prev:
            finish_prev(emit_mid)

    pl.when(both(f == 0, i == 0))(functools.partial(first_step, False))
    pl.when(both(f == 0, both(i > 0, real)))(functools.partial(first_step, True))
    pl.when(both(f == 0, i == N_TILES))(functools.partial(finish_prev, emit_last))

    @pl.when(both(real, both(f > 0, f < last)))
    def _():
        acc_ref[...] += mlp(xb_ref[...])

    @pl.when(both(real, f == last))
    def _():
        s_ref[...] = ALPHA * load_x() + 0.5 * (acc_ref[...] + mlp(xb_ref[...]))


def _ffn_in_kernel(xm_ref, xe_ref, wg_ref, wu_ref, wd_ref, g_ref, b_ref, o_ref, ob_ref, *scratch):
    load_x = lambda: jnp.where(pl.program_id(0) < N_MAIN_TILES, xm_ref[...], xe_ref[...])

    def emit(y):
        o_ref[...] = y
        ob_ref[...] = y.astype(BF16)

    _ffn_body(load_x, emit, emit, wg_ref, wu_ref, wd_ref, g_ref, b_ref, *scratch)


def _ffn_out_kernel(x_ref, wg_ref, wu_ref, wd_ref, g_ref, b_ref, om_ref, oe_ref, *scratch):
    assert N_TILES == N_MAIN_TILES + 1

    def emit_frames(y):
        om_ref[...] = y

    def emit_extra(y):
        oe_ref[...] = y

    _ffn_body(lambda: x_ref[...], emit_frames, emit_extra, wg_ref, wu_ref, wd_ref, g_ref, b_ref, *scratch)


def _ffn_specs(d):
    nf = D_FF // FFN_TF
    step = lambda i, f: jnp.where(i == N_TILES, nf - 1, f)
    return nf, [
        pl.BlockSpec((d, FFN_TF), lambda i, f: (0, step(i, f))),
        pl.BlockSpec((d, FFN_TF), lambda i, f: (0, step(i, f) + nf)),
        pl.BlockSpec((FFN_TF, d), lambda i, f: (step(i, f), 0)),
        pl.BlockSpec((1, d), lambda i, f: (0, 0)),
        pl.BlockSpec((1, d), lambda i, f: (0, 0)),
    ]


_FFN_PARAMS = dict(dimension_semantics=("arbitrary", "arbitrary"), vmem_limit_bytes=VMEM_LIMIT)
_in_frame_tile = lambda i, f: (jnp.minimum(i, N_MAIN_TILES - 1), 0)
_in_any_tile = lambda i, f: (jnp.minimum(i, N_TILES - 1), 0)
_out_frame_tile = lambda i, f: (jnp.clip(i - 1, 0, N_MAIN_TILES - 1), 0)
_out_any_tile = lambda i, f: (jnp.maximum(i - 1, 0), 0)
_only_tile = lambda i, f: (0, 0)


def _ffn_scratch(tile):
    return [pltpu.VMEM(tile, BF16), pltpu.VMEM(tile, F32), pltpu.VMEM(tile, F32)]


def _ffn_ln_in(x_main, x_extra, w_gu, w_dn, ln_g, ln_b):
    d = x_main.shape[1]
    nf, wspecs = _ffn_specs(d)
    tile = (FFN_TM, d)
    return pl.pallas_call(
        _ffn_in_kernel,
        grid=(N_TILES + 1, nf),
        in_specs=[pl.BlockSpec(tile, _in_frame_tile),
                  pl.BlockSpec(tile, _only_tile, pipeline_mode=pl.Buffered(1))] + wspecs,
        out_specs=[pl.BlockSpec(tile, _out_any_tile), pl.BlockSpec(tile, _out_any_tile)],
        out_shape=[jax.ShapeDtypeStruct((M_ROWS, d), F32), jax.ShapeDtypeStruct((M_ROWS, d), BF16)],
        scratch_shapes=_ffn_scratch(tile),
        compiler_params=pltpu.CompilerParams(**_FFN_PARAMS),
        name="ffn_ln_in",
    )(x_main, x_extra, w_gu, w_gu, w_dn, ln_g, ln_b)


def _ffn_ln_out(x, w_gu, w_dn, ln_g, ln_b):
    d = x.shape[1]
    nf, wspecs = _ffn_specs(d)
    tile = (FFN_TM, d)
    return pl.pallas_call(
        _ffn_out_kernel,
        grid=(N_TILES + 1, nf),
        in_specs=[pl.BlockSpec(tile, _in_any_tile)] + wspecs,
        out_specs=[pl.BlockSpec(tile, _out_frame_tile), pl.BlockSpec(tile, _only_tile)],
        out_shape=[jax.ShapeDtypeStruct((F_ROWS, d), F32), jax.ShapeDtypeStruct((X_ROWS, d), F32)],
        scratch_shapes=_ffn_scratch(tile),
        compiler_params=pltpu.CompilerParams(**_FFN_PARAMS),
        name="ffn_ln_out",
    )(x, w_gu, w_gu, w_dn, ln_g, ln_b)


def _proj_tiles():
    o_z = RW_SHIFT_COLS
    o_xbc = o_z + SSM_DIM
    o_dt = o_xbc + CONV_DIM
    o_gate = o_dt + SSM_HEADS
    runs = [(PC_RW, 0, 3 * RW_DIM), (PC_GATE, o_gate, 2 * D_MODEL), (PC_Z, o_z, SSM_DIM), (PC_XS, o_xbc, CONV_DIM)]
    src = [None] * (P_COLS // MM_TN)
    for dst0, src0, width in runs:
        assert dst0 % MM_TN == 0 and width % MM_TN == 0 and src0 % (2 * SUBLANES) == 0
        for t in range(width // MM_TN):
            src[dst0 // MM_TN + t] = src0 + t * MM_TN
    assert src.count(None) == 1
    return src


def _mm_kernel(x_ref, wt_ref, wmix_ref, o_ref, *, mixed_tile):
    j = pl.program_id(0)

    @pl.when(j != mixed_tile)
    def _():
        o_ref[...] = _dot_nt(x_ref[...], wt_ref[...])

    @pl.when(j == mixed_tile)
    def _():
        o_ref[...] = _dot_nt(x_ref[...], wmix_ref[...])


def _project(x, wt, wt_mixed):
    m, k = x.shape
    src = _proj_tiles()
    mixed_tile = src.index(None)

    unit = 2 * SUBLANES

    def wt_row(j, i):
        start = 0
        for t, s in enumerate(src):
            if s is not None:
                start = jnp.where(j == t, s // unit, start)
        return start * unit, 0

    return pl.pallas_call(
        functools.partial(_mm_kernel, mixed_tile=mixed_tile),
        grid=(len(src), m // MM_TM),
        in_specs=[pl.BlockSpec((MM_TM, k), lambda j, i: (i, 0)),
                  pl.BlockSpec((pl.Element(MM_TN), pl.Element(k)), wt_row),
                  pl.BlockSpec((MM_TN, k), lambda j, i: (0, 0), pipeline_mode=pl.Buffered(1))],
        out_specs=pl.BlockSpec((MM_TM, MM_TN), lambda j, i: (i, j)),
        out_shape=jax.ShapeDtypeStruct((m, P_COLS), F32),
        compiler_params=pltpu.CompilerParams(
            dimension_semantics=("parallel", "parallel"), vmem_limit_bytes=VMEM_LIMIT),
        name="proj_in",
    )(x, wt, wt_mixed)


def _rwkv_kernel(*refs, chunk, pad, n_seq, n_p, slots):
    p_refs, fixed, cast_in, outs, cast_out, (yacc_ref,) = _split_refs(refs, n_p, 14, len(slots), 3, len(slots))
    (hist_ref, s0_ref, mu_ref, w0_ref, w2_ref, a0_ref, a2_ref, g2_ref, kk_ref, ka_ref, rk_ref,
     gnw_ref, gnb_ref, seg_ref) = fixed
    y_ref, s_ref, shift_ref = outs
    c = pl.program_id(1)
    _run_cast_slots(pl.program_id(0) * pl.num_programs(1) + c, slots, cast_in, cast_out)
    n_rows = n_seq * chunk
    seqs = range(n_seq)
    rsl = [slice(q * chunk, (q + 1) * chunk) for q in seqs]
    rows = lax.broadcasted_iota(jnp.int32, (n_rows, 1), 0)

    @pl.when(c == 0)
    def _():
        shift_ref[...] = hist_ref[...]
        s_ref[...] = s0_ref[...]

    p = jnp.concatenate([r[...] for r in p_refs], axis=0)
    if pad:
        p = jnp.where(jnp.logical_and(c == 0, jnp.bitwise_and(rows, chunk - 1) < pad), 0.0, p)
    prev = pltpu.roll(p, 1, 0)
    for q in seqs:
        prev = jnp.where(rows == q * chunk, shift_ref[q], prev)
    for q in seqs:
        shift_ref[q] = p[(q + 1) * chunk - 1:(q + 1) * chunk, :]
    ps = p + (prev - p) * mu_ref[...]

    r = ps[:, 0:RW_DIM]
    k = ps[:, RW_DIM:2 * RW_DIM]
    v = ps[:, 2 * RW_DIM:3 * RW_DIM]
    wd = ps[:, RW_OFF_WD:RW_OFF_WD + LORA_PAD]
    ad = ps[:, RW_OFF_AD:RW_OFF_AD + LORA_PAD]
    gd = ps[:, RW_OFF_GD:RW_OFF_GD + RW_GATE_LORA]

    lw = -math.exp(-0.5) * _sigmoid(w0_ref[...] + _dot(jnp.tanh(wd).astype(BF16), w2_ref[...]))
    a = _sigmoid(a0_ref[...] + _dot(ad.astype(BF16), a2_ref[...]))
    g = _dot(_sigmoid(gd).astype(BF16), g2_ref[...])

    seg = seg_ref[...]

    def head_sum(x):
        w = seg.shape[0]
        return jnp.concatenate(
            [_dot_sel_r(x[:, j * w:(j + 1) * w], seg) for j in range(RW_DIM // w)], axis=1)

    kk = k * kk_ref[...]
    kk = kk * lax.rsqrt(jnp.maximum(head_sum(kk * kk), 1e-24))
    k = k * (1.0 + (a - 1.0) * ka_ref[...])
    b_neg = -(kk * a)

    cum = _dot_sel_l(_causal_blocks(n_rows, chunk), lw)
    cum_last = [cum[(q + 1) * chunk - 1:(q + 1) * chunk, :] for q in seqs]
    cum_end = jnp.concatenate([jnp.broadcast_to(x, (chunk, RW_DIM)) for x in cum_last], axis=0)
    e_neg = jnp.exp(-cum)
    kt = (kk * jnp.exp(cum - lw)).astype(BF16)
    rt = (r * jnp.exp(cum)).astype(BF16)
    kd = (k * e_neg).astype(BF16)
    bd_neg = (b_neg * e_neg).astype(BF16)
    e_end = jnp.exp(cum_end - cum)
    k_end = (k * e_end).astype(BF16)
    b_end_neg = (b_neg * e_end).astype(BF16)
    p_end = [jnp.exp(x) for x in cum_last]
    vb = v.astype(BF16)

    ri = lax.broadcasted_iota(jnp.int32, (2 * chunk, 2 * chunk), 0)
    rj = lax.broadcasted_iota(jnp.int32, (2 * chunk, 2 * chunk), 1)
    bi = jnp.where(ri >= chunk, ri - chunk + 1, ri)
    bj = jnp.where(rj >= chunk, rj - chunk, rj)
    keep = bi > bj
    ti = lax.broadcasted_iota(jnp.int32, (chunk, 2 * chunk), 0)
    tj = lax.broadcasted_iota(jnp.int32, (chunk, 2 * chunk), 1)
    hi_lane = tj >= chunk
    eye_hi = (tj == ti + chunk).astype(F32)
    zeros_v = jnp.zeros((chunk, HEAD), BF16)

    pairs = [(q, h) for q in seqs for h in range(RW_HEADS)]
    sls = [(rsl[q], slice(h * HEAD, (h + 1) * HEAD)) for q, h in pairs]
    s_old = [s_ref[q, h] for q, h in pairs]
    s_b = [s.astype(BF16) for s in s_old]
    amats = [jnp.where(keep, _dot_nt(jnp.concatenate([kt[sl], rt[sl]], axis=0),
                                     jnp.concatenate([bd_neg[sl], kd[sl]], axis=0)), 0.0)
             for sl in sls]
    top = [m[:chunk] for m in amats]
    bot = [m[chunk:].astype(BF16) for m in amats]
    top_b = [x.astype(BF16) for x in top]
    tq = [_dot(xb[:, :chunk], jnp.where(hi_lane, eye_hi, x).astype(BF16)) + eye_hi
          for x, xb in zip(top, top_b)]
    n = 2
    while n < chunk:
        tq = [jnp.where(hi_lane, x, 0.0) + _dot(x[:, :chunk].astype(BF16), x.astype(BF16)) for x in tq]
        n *= 2
    t_b = [x[:, chunk:].astype(BF16) for x in tq]
    v_h = [vb[sl] for sl in sls]
    rhs = [_dot_nt(kt[sl], sb) + _dot(xb, jnp.concatenate([zeros_v, vh], axis=0))
           for sl, sb, xb, vh in zip(sls, s_b, top_b, v_h)]
    ub = [_dot(t, x.astype(BF16)).astype(BF16) for t, x in zip(t_b, rhs)]
    uv = [jnp.concatenate([u, vh], axis=0) for u, vh in zip(ub, v_h)]
    for i, (q, h) in enumerate(pairs):
        sl = sls[i]
        yacc_ref[sl] = _dot_nt(rt[sl], s_b[i]) + _dot(bot[i], uv[i])
        s_ref[q, h] = s_old[i] * p_end[q][:, sl[1]] + _dot_tn(
            uv[i], jnp.concatenate([b_end_neg[sl], k_end[sl]], axis=0))

    y = yacc_ref[...]
    inv_n = 1.0 / HEAD
    mean = head_sum(y) * inv_n
    yc = y - mean
    var = head_sum(yc * yc) * inv_n
    yn = yc * lax.rsqrt(var + RW_GN_EPS) * gnw_ref[...] + gnb_ref[...]
    bonus = head_sum(r * k * rk_ref[...]) * v
    out = ((yn + bonus) * g).astype(y_ref.dtype)
    for q in seqs:
        y_ref[q] = out[rsl[q]]


RWKV_GROUPS = (2, 8)
SSD_GROUPS = (1, 1)


class _SeqPlan:
    def __init__(self, prompt, groups):
        self.prompt = prompt
        if prompt:
            self.n_seq, self.chunk, self.pad, self.total, self.seq_len = groups[0], CHUNK, PAD, BATCH, SEQ
            self.grid = (BATCH // self.n_seq, SEQ_CHUNKS)
        else:
            self.n_seq, self.chunk, self.pad, self.total, self.seq_len = groups[1], DEC_SEQ, 0, DEC_BATCH, DEC_SEQ
            self.grid = (DEC_BATCH // self.n_seq, 1)

    def p_specs(self, width, col):
        cb = col // width
        if not self.prompt:
            rows = self.n_seq * self.chunk
            return [pl.BlockSpec((rows, width), lambda g, c: (ROW_S0 // rows + g, cb))]

        def spec(q):
            def index(g, c):
                seq = g * self.n_seq + q
                return jnp.where(c == 0, ROW_H0 // CHUNK + seq, seq * (SEQ // CHUNK) + c - 1), cb
            return pl.BlockSpec((CHUNK, width), index)
        return [spec(q) for q in range(self.n_seq)]

    def y_spec(self, dim):
        return pl.BlockSpec((self.n_seq, self.chunk, dim), lambda g, c: (g, jnp.maximum(c - 1, 0), 0))

    def y_shape(self, dim):
        return jax.ShapeDtypeStruct((self.total, self.seq_len, dim), BF16)

    def state_spec(self, shape):
        nd = len(shape)
        return pl.BlockSpec((self.n_seq,) + tuple(shape), lambda g, c: (g,) + (0,) * nd)

    def cast_slots(self, cast_weights):
        slots, t_end = _chain_slots(cast_weights)
        assert t_end <= self.grid[0] * self.grid[1]
        specs = [s.spec(lambda g, c: g * self.grid[1] + c) for s in slots]
        return slots, specs, specs


_MIX_PARAMS = dict(dimension_semantics=("arbitrary", "arbitrary"), vmem_limit_bytes=VMEM_LIMIT)


def _rwkv_mix(p_all, hist, s0, wts, *, prompt, cast_weights=()):
    plan = _SeqPlan(prompt, RWKV_GROUPS)
    slots, cast_in_specs, cast_out_specs = plan.cast_slots(cast_weights)
    p_specs = plan.p_specs(RW_COLS, PC_RW)
    const2 = lambda g, c: (0, 0)
    wspecs = [pl.BlockSpec(w.shape, const2) for w in wts]
    shift_spec = plan.state_spec((1, RW_COLS))
    wkv_spec = plan.state_spec((RW_HEADS, HEAD, HEAD))
    return pl.pallas_call(
        functools.partial(_rwkv_kernel, chunk=plan.chunk, pad=plan.pad, n_seq=plan.n_seq, n_p=len(p_specs),
                          slots=slots),
        grid=plan.grid,
        in_specs=p_specs + [shift_spec, wkv_spec] + wspecs + cast_in_specs,
        out_specs=[plan.y_spec(RW_DIM), wkv_spec, shift_spec] + cast_out_specs,
        out_shape=[plan.y_shape(RW_DIM),
                   jax.ShapeDtypeStruct((plan.total, RW_HEADS, HEAD, HEAD), F32),
                   jax.ShapeDtypeStruct((plan.total, 1, RW_COLS), F32)] + [s.out_shape() for s in slots],
        scratch_shapes=[pltpu.VMEM((plan.n_seq * plan.chunk, RW_DIM), F32)],
        compiler_params=pltpu.CompilerParams(**_MIX_PARAMS),
        name="rwkv_mix_c%d" % plan.chunk,
    )(*([p_all] * len(p_specs)), hist, s0, *wts, *[s.w for s in slots])


def _ssd_kernel(*refs, chunk, pad, n_seq, n_p, slots):
    (xs_refs, bc_refs, z_refs, dt_refs, fixed, cast_in, outs, cast_out, (xpad_ref, yacc_ref)) = _split_refs(
        refs, n_p, n_p, n_p, n_p, 9, len(slots), 3, len(slots))
    hist_ref, h0_ref, cw_ref, cb_ref, dtb_ref, alog_ref, dskip_ref, nw_ref, exp_ref = fixed
    y_ref, h_ref, tail_ref = outs
    c = pl.program_id(1)
    _run_cast_slots(pl.program_id(0) * pl.num_programs(1) + c, slots, cast_in, cast_out)
    n_rows = n_seq * chunk
    seqs = range(n_seq)
    rsl = [slice(q * chunk, (q + 1) * chunk) for q in seqs]
    rows = lax.broadcasted_iota(jnp.int32, (n_rows, 1), 0)
    stack = lambda rs: jnp.concatenate([r[...] for r in rs], axis=0)

    @pl.when(c == 0)
    def _():
        xpad_ref[:, 0:CONV_HIST, :] = hist_ref[...]
        h_ref[...] = h0_ref[...]

    u = jnp.concatenate([stack(xs_refs), stack(bc_refs)], axis=1)
    if pad:
        is_pad = jnp.logical_and(c == 0, jnp.bitwise_and(rows, chunk - 1) < pad)
        u = jnp.where(is_pad, 0.0, u)
    convs = []
    for q in seqs:
        u_q = u[rsl[q]]
        xpad_ref[q, CONV_HIST:CONV_HIST + chunk, :] = u_q
        conv = cb_ref[...] + u_q * cw_ref[CONV_W - 1:CONV_W, :]
        for i in range(CONV_W - 1):
            back = CONV_W - 1 - i
            conv = conv + xpad_ref[q, CONV_HIST - back:CONV_HIST - back + chunk, :] * cw_ref[i:i + 1, :]
        hist_next = xpad_ref[q, chunk:chunk + CONV_HIST, :]
        xpad_ref[q, 0:CONV_HIST, :] = hist_next
        tail_ref[q] = hist_next
        convs.append(conv)
    conv = jnp.concatenate(convs, axis=0)
    xbc = conv * _sigmoid(conv)
    xs = xbc[:, 0:SSM_DIM]
    bm = xbc[:, SSM_DIM:SSM_DIM + BC_DIM].astype(BF16)
    cm = xbc[:, SSM_DIM + BC_DIM:CONV_DIM].astype(BF16)

    dt = _softplus(stack(dt_refs) + dtb_ref[...])
    if pad:
        dt = jnp.where(is_pad, 0.0, dt)
    da = dt * (-jnp.exp(alog_ref[...]))
    ci = lax.broadcasted_iota(jnp.int32, (chunk, chunk), 0)
    cj = lax.broadcasted_iota(jnp.int32, (chunk, chunk), 1)
    causal = ci >= cj
    acs = _dot_sel_l(_causal_blocks(n_rows, chunk), da)
    acs_t = acs.T
    acs_end = jnp.concatenate(
        [jnp.broadcast_to(acs[(q + 1) * chunk - 1:(q + 1) * chunk, :], (chunk, DT_COLS)) for q in seqs], axis=0)

    expand = exp_ref[...]
    xdt = xs * _dot_sel_r(dt, expand)
    xdt_b = xdt.astype(BF16)
    xdt_end = (xdt * _dot_sel_r(jnp.exp(acs_end - acs), expand)).astype(BF16)
    e_acs = _dot_sel_r(jnp.exp(acs), expand)
    decay_end = [jnp.exp(acs_t[:, (q + 1) * chunk - 1:(q + 1) * chunk]) for q in seqs]

    gw = SSM_HPG * HEAD
    groups = range(SSM_GROUPS)
    heads = range(SSM_HEADS)
    qg = [(q, g) for q in seqs for g in groups]
    qh = [(q, hd) for q in seqs for hd in heads]
    bm_g = {(q, g): bm[rsl[q], g * SSM_STATE:(g + 1) * SSM_STATE] for q, g in qg}
    cm_g = {(q, g): cm[rsl[q], g * SSM_STATE:(g + 1) * SSM_STATE] for q, g in qg}
    h_old = {(q, hd): h_ref[q, hd] for q, hd in qh}
    cb = {k: _dot_nt(cm_g[k], bm_g[k]) for k in qg}
    y_off = {(q, g): _dot_nt(cm_g[q, g], jnp.concatenate(
        [h_old[q, g * SSM_HPG + hh].astype(BF16) for hh in range(SSM_HPG)], axis=0)) for q, g in qg}
    st = {(q, g): _dot_tn(xdt_end[rsl[q], g * gw:(g + 1) * gw], bm_g[q, g]) for q, g in qg}
    for q, hd in qh:
        sl = slice(hd * HEAD, (hd + 1) * HEAD)
        seg = acs[rsl[q], hd:hd + 1] - acs_t[hd:hd + 1, rsl[q]]
        lmat = jnp.exp(jnp.where(causal, seg, -jnp.inf))
        yacc_ref[rsl[q], sl] = _dot((cb[q, hd // SSM_HPG] * lmat).astype(BF16), xdt_b[rsl[q], sl])
    for q, hd in qh:
        g, hh = divmod(hd, SSM_HPG)
        h_ref[q, hd] = h_old[q, hd] * decay_end[q][hd:hd + 1, :] + st[q, g][hh * HEAD:(hh + 1) * HEAD, :]

    y_off_all = jnp.concatenate(
        [jnp.concatenate([y_off[q, g] for g in groups], axis=1) for q in seqs], axis=0)
    y = yacc_ref[...] + y_off_all * e_acs + xs * dskip_ref[...]
    zz = stack(z_refs)
    y = y * (zz * _sigmoid(zz))
    parts = []
    for g in groups:
        yg = y[:, g * gw:(g + 1) * gw]
        ms = jnp.mean(yg * yg, axis=-1, keepdims=True)
        parts.append(yg * lax.rsqrt(ms + RMS_EPS))
    out = (jnp.concatenate(parts, axis=1) * nw_ref[...]).astype(y_ref.dtype)
    for q in seqs:
        y_ref[q] = out[rsl[q]]


def _ssd_mix(p_all, hist, h0, wts, *, prompt, cast_weights=()):
    plan = _SeqPlan(prompt, SSD_GROUPS)
    slots, cast_in_specs, cast_out_specs = plan.cast_slots(cast_weights)
    p_specs = (plan.p_specs(SSM_DIM, PC_XS) + plan.p_specs(2 * BC_DIM, PC_BC)
               + plan.p_specs(SSM_DIM, PC_Z) + plan.p_specs(DT_COLS, PC_DT))
    const2 = lambda g, c: (0, 0)
    wspecs = [pl.BlockSpec(w.shape, const2) for w in wts]
    conv_spec = plan.state_spec((CONV_HIST, CONV_DIM))
    ssm_spec = plan.state_spec((SSM_HEADS, HEAD, SSM_STATE))
    return pl.pallas_call(
        functools.partial(_ssd_kernel, chunk=plan.chunk, pad=plan.pad, n_seq=plan.n_seq, n_p=len(p_specs) // 4,
                          slots=slots),
        grid=plan.grid,
        in_specs=p_specs + [conv_spec, ssm_spec] + wspecs + cast_in_specs,
        out_specs=[plan.y_spec(SSM_DIM), ssm_spec, conv_spec] + cast_out_specs,
        out_shape=[plan.y_shape(SSM_DIM),
                   jax.ShapeDtypeStruct((plan.total, SSM_HEADS, HEAD, SSM_STATE), F32),
                   jax.ShapeDtypeStruct((plan.total, CONV_HIST, CONV_DIM), F32)] + [s.out_shape() for s in slots],
        scratch_shapes=[pltpu.VMEM((plan.n_seq, CONV_HIST + plan.chunk, CONV_DIM), F32),
                        pltpu.VMEM((plan.n_seq * plan.chunk, SSM_DIM), F32)],
        compiler_params=pltpu.CompilerParams(**_MIX_PARAMS),
        name="ssd_mix_c%d" % plan.chunk,
    )(*([p_all] * len(p_specs)), hist, h0, *wts, *[s.w for s in slots])


MERGE_F_TILES = F_ROWS // MERGE_TM
assert S_ROWS == MERGE_TM


MERGE_TILES = M_ROWS // MERGE_TM


def _merge_kernel(x_ref, yrw_f_ref, yrw_s_ref, yssm_f_ref, yssm_s_ref, ga_ref, gb_ref, bga_ref, bgb_ref,
                  wrw_ref, wssm_ref, wout_ref, g_ref, b_ref, o_ref, s0_ref, s1_ref):
    i = pl.program_id(0)
    both = jnp.logical_and

    def pre_norm():
        is_frame = i < MERGE_F_TILES
        y_rw = jnp.where(is_frame, yrw_f_ref[...], yrw_s_ref[...])
        y_ssm = jnp.where(is_frame, yssm_f_ref[...], yssm_s_ref[...])
        merged = (_sigmoid(ga_ref[...] + bga_ref[...]) * _dot(y_rw, wrw_ref[...])
                  + _sigmoid(gb_ref[...] + bgb_ref[...]) * _dot(y_ssm, wssm_ref[...]))
        return ALPHA * x_ref[...] + _dot(merged.astype(BF16), wout_ref[...])

    def step(prev_ref, next_ref):
        s = pre_norm()
        o_ref[...] = _layer_norm(prev_ref[...], g_ref[...], b_ref[...])
        next_ref[...] = s

    @pl.when(i == 0)
    def _():
        s0_ref[...] = pre_norm()

    odd = jnp.bitwise_and(i, 1) == 1
    inner = both(i > 0, i < MERGE_TILES)
    pl.when(both(inner, odd))(functools.partial(step, s0_ref, s1_ref))
    pl.when(both(inner, jnp.logical_not(odd)))(functools.partial(step, s1_ref, s0_ref))

    @pl.when(i == MERGE_TILES)
    def _():
        last_ref = (s0_ref, s1_ref)[(MERGE_TILES - 1) % 2]
        o_ref[...] = _layer_norm(last_ref[...], g_ref[...], b_ref[...])


def _merge(x, y_rw_f, y_rw_s, y_ssm_f, y_ssm_s, p_all, b_gate, w_rw_out, w_ssm_out, w_out, ln_g, ln_b):
    m, d = x.shape
    tm = MERGE_TM
    tile = lambda i: jnp.minimum(i, MERGE_TILES - 1)
    row = lambda i: (tile(i), 0)
    frame_row = lambda i: (jnp.minimum(i, MERGE_F_TILES - 1), 0)
    const = lambda i: (0, 0)
    resident = lambda w: pl.BlockSpec(w.shape, const, pipeline_mode=pl.Buffered(1))
    return pl.pallas_call(
        _merge_kernel,
        grid=(MERGE_TILES + 1,),
        in_specs=[
            pl.BlockSpec((tm, d), row),
            pl.BlockSpec((tm, RW_DIM), frame_row),
            pl.BlockSpec((tm, RW_DIM), const),
            pl.BlockSpec((tm, SSM_DIM), frame_row),
            pl.BlockSpec((tm, SSM_DIM), const),
            pl.BlockSpec((tm, d), lambda i: (tile(i), PC_GATE // d)),
            pl.BlockSpec((tm, d), lambda i: (tile(i), PC_GATE // d + 1)),
            pl.BlockSpec((1, d), lambda i: (0, 0)),
            pl.BlockSpec((1, d), lambda i: (0, 1)),
            resident(w_rw_out), resident(w_ssm_out), resident(w_out),
            pl.BlockSpec((1, d), const),
            pl.BlockSpec((1, d), const),
        ],
        out_specs=pl.BlockSpec((tm, d), lambda i: (jnp.maximum(i - 1, 0), 0)),
        out_shape=jax.ShapeDtypeStruct((m, d), F32),
        scratch_shapes=[pltpu.VMEM((tm, d), F32), pltpu.VMEM((tm, d), F32)],
        compiler_params=pltpu.CompilerParams(
            dimension_semantics=("arbitrary",), vmem_limit_bytes=VMEM_LIMIT),
        name="merge_ln",
    )(x, y_rw_f, y_rw_s, y_ssm_f, y_ssm_s, p_all, p_all, b_gate, b_gate, w_rw_out, w_ssm_out, w_out,
      ln_g, ln_b)


def _pad_cols(x, width):
    return jnp.concatenate([x, jnp.zeros(x.shape[:-1] + (width - x.shape[-1],), x.dtype)], axis=-1)


def _rw_cols(x):
    o_wd, o_ad, o_gd = 3 * RW_DIM, 3 * RW_DIM + RW_DECAY_LORA, 3 * RW_DIM + RW_DECAY_LORA + RW_AAA_LORA
    return jnp.concatenate([
        x[..., :o_wd],
        _pad_cols(x[..., o_wd:o_ad], LORA_PAD),
        _pad_cols(x[..., o_ad:o_gd], LORA_PAD),
        x[..., o_gd:],
    ], axis=-1)


def _pad_rows(x, height):
    return jnp.concatenate([x, jnp.zeros((height - x.shape[0],) + x.shape[1:], x.dtype)], axis=0)


def _rw_rows(x):
    o_wd, o_ad, o_gd = 3 * RW_DIM, 3 * RW_DIM + RW_DECAY_LORA, 3 * RW_DIM + RW_DECAY_LORA + RW_AAA_LORA
    return jnp.concatenate([
        x[:o_wd], _pad_rows(x[o_wd:o_ad], LORA_PAD), _pad_rows(x[o_ad:o_gd], LORA_PAD), x[o_gd:]], axis=0)


def _rw_cols_inv(x):
    return jnp.concatenate([
        x[..., :RW_OFF_WD],
        x[..., RW_OFF_WD:RW_OFF_WD + RW_DECAY_LORA],
        x[..., RW_OFF_AD:RW_OFF_AD + RW_AAA_LORA],
        x[..., RW_OFF_GD:],
    ], axis=-1)


def _block_ones(n, blk):
    i = jnp.arange(n) // blk
    return (i[:, None] == i[None, :]).astype(BF16)


def kernel(x_prompt, x_sample, state_rwkv_shift, state_wkv, state_conv, state_ssm, meta_tokens, ffn1_gu, ffn1_dn, ln1_g, ln1_b, w_in, b_gate, rw_mu, rw_w0, rw_w2, rw_a0, rw_a2, rw_g2, rw_kk, rw_ka, rw_rk, rw_gn_w, rw_gn_b, conv_w, conv_b, dt_bias, a_log, d_skip, ssm_norm_w, w_rw_out, w_ssm_out, w_out, ln2_g, ln2_b, ffn2_gu, ffn2_dn, ln3_g, ln3_b):
    lyr = 0
    row = lambda t: t[lyr].reshape(1, -1).astype(F32)

    head_rows = jnp.concatenate([jnp.zeros((PAD, D_MODEL), F32), meta_tokens.astype(F32)], axis=0)
    x_extra = jnp.concatenate([x_sample.reshape(S_ROWS, D_MODEL)] + [head_rows] * BATCH, axis=0)
    x1, x1b = _ffn_ln_in(x_prompt.reshape(F_ROWS, D_MODEL), x_extra, ffn1_gu[lyr].astype(BF16),
                         ffn1_dn[lyr].astype(BF16), row(ln1_g), row(ln1_b))

    wt = jnp.swapaxes(w_in[lyr], 0, 1).astype(BF16)
    o_dt = RW_SHIFT_COLS + SSM_DIM + CONV_DIM
    wt_mixed = jnp.concatenate([
        _rw_rows(wt[:RW_SHIFT_COLS])[3 * RW_DIM:], _pad_rows(wt[o_dt:o_dt + SSM_HEADS], DT_SLOT)], axis=0)
    p_all = _project(x1b, wt, wt_mixed)

    rw_wts = [
        _rw_cols(rw_mu[lyr]).reshape(1, RW_COLS), row(rw_w0),
        _pad_rows(rw_w2[lyr], LORA_PAD).astype(BF16), row(rw_a0),
        _pad_rows(rw_a2[lyr], LORA_PAD).astype(BF16), rw_g2[lyr].astype(BF16),
        row(rw_kk), row(rw_ka), row(rw_rk), row(rw_gn_w), row(rw_gn_b),
        _block_ones(2 * LANES, HEAD),
    ]
    y_rw_s, wkv_s, shift_s = _rwkv_mix(
        p_all, _rw_cols(state_rwkv_shift[lyr]), state_wkv[lyr], rw_wts, prompt=False)
    y_rw_f, wkv_p, shift_p, w_rw_o, w_ssm_o, w_o = _rwkv_mix(
        p_all, jnp.zeros((BATCH, 1, RW_COLS), F32), jnp.zeros((BATCH, RW_HEADS, HEAD, HEAD), F32), rw_wts,
        prompt=True, cast_weights=[(w_rw_out[lyr], MIX_CAST_ROWS), (w_ssm_out[lyr], MIX_CAST_ROWS),
                                   (w_out[lyr], MIX_CAST_ROWS)])

    head_of_lane = jnp.arange(SSM_DIM) // HEAD
    expand = (jnp.arange(DT_COLS)[:, None] == head_of_lane[None, :]).astype(BF16)
    ssd_wts = [
        conv_w[lyr], row(conv_b), _pad_cols(row(dt_bias), DT_COLS), _pad_cols(row(a_log), DT_COLS),
        jnp.repeat(d_skip[lyr], HEAD).reshape(1, SSM_DIM), row(ssm_norm_w),
        expand,
    ]
    hist_rows = lambda t: jnp.pad(t, ((0, 0), (CONV_HIST - (CONV_W - 1), 0), (0, 0)))
    y_ssm_s, ssm_s, conv_s = _ssd_mix(
        p_all, hist_rows(state_conv[lyr]), state_ssm[lyr], ssd_wts, prompt=False)
    y_ssm_f, ssm_p, conv_p, gu2, dn2 = _ssd_mix(
        p_all, jnp.zeros((BATCH, CONV_HIST, CONV_DIM), F32),
        jnp.zeros((BATCH, SSM_HEADS, HEAD, SSM_STATE), F32), ssd_wts, prompt=True,
        cast_weights=[(ffn2_gu[lyr], MIX_CAST_ROWS // 4), (ffn2_dn[lyr], MIX_CAST_ROWS)])

    flat = lambda t: t.reshape(-1, t.shape[-1])
    x2 = _merge(x1, flat(y_rw_f), flat(y_rw_s), flat(y_ssm_f), flat(y_ssm_s), p_all, row(b_gate),
                w_rw_o, w_ssm_o, w_o, row(ln2_g), row(ln2_b))
    y_frames, y_extra = _ffn_ln_out(x2, gu2, dn2, row(ln3_g), row(ln3_b))

    y_prompt = y_frames.reshape(BATCH, SEQ, D_MODEL)
    y_sample = y_extra[:S_ROWS].reshape(DEC_BATCH, DEC_SEQ, D_MODEL)
    conv_of = lambda t: t[:, CONV_HIST - (CONV_W - 1):][None]
    return (y_prompt, y_sample,
            _rw_cols_inv(shift_p)[None], wkv_p[None], conv_of(conv_p), ssm_p[None],
            _rw_cols_inv(shift_s)[None], wkv_s[None], conv_of(conv_s), ssm_s[None])
```

```python
import functools
import math

import jax
import jax.numpy as jnp
from jax import lax
from jax.experimental import pallas as pl
from jax.experimental.pallas import tpu as pltpu

F32 = jnp.float32
BF16 = jnp.bfloat16

D_MODEL = 2048
BATCH = 4
SEQ = 2048
DEC_BATCH = 16
DEC_SEQ = 16
CHUNK = 64
N_META = 16
PAD = (-N_META) % CHUNK
HEAD_ROWS = PAD + N_META
SEQ_CHUNKS = (HEAD_ROWS + SEQ) // CHUNK
F_ROWS = BATCH * SEQ
S_ROWS = DEC_BATCH * DEC_SEQ
ROW_S0 = F_ROWS
ROW_H0 = F_ROWS + S_ROWS
X_ROWS = S_ROWS + BATCH * HEAD_ROWS
M_ROWS = F_ROWS + X_ROWS

HEAD = 64
RW_DIM = D_MODEL // 2
RW_HEADS = RW_DIM // HEAD
RW_DECAY_LORA = 96
RW_AAA_LORA = 96
RW_GATE_LORA = 256
RW_SHIFT_COLS = 3 * RW_DIM + RW_DECAY_LORA + RW_AAA_LORA + RW_GATE_LORA
RW_GN_EPS = 64e-5
SSM_DIM = D_MODEL
SSM_HEADS = SSM_DIM // HEAD
SSM_GROUPS = 4
SSM_HPG = SSM_HEADS // SSM_GROUPS
SSM_STATE = 128
CONV_W = 4
BC_DIM = SSM_GROUPS * SSM_STATE
CONV_DIM = SSM_DIM + 2 * BC_DIM
RMS_EPS = 1e-5
D_FF = 5632
LN_EPS = 1e-5
DEPTH = 1
ALPHA = (2 * DEPTH) ** 0.25

LANES = 128
SUBLANES = 8
VMEM_LIMIT = 56 * 1024 * 1024

LORA_PAD = LANES
RW_OFF_WD = 3 * RW_DIM
RW_OFF_AD = RW_OFF_WD + LORA_PAD
RW_OFF_GD = RW_OFF_AD + LORA_PAD
RW_COLS = RW_OFF_GD + RW_GATE_LORA
DT_COLS = LANES
CONV_HIST = SUBLANES

MIX_CAST_ROWS = 128
FFN_CAST_ROWS = 96
FFN_TM = 512
FFN_TF = 512
MM_TM = M_ROWS // 8
MM_TN = 1024
MERGE_TM = 256

DT_SLOT = 512
PC_RW = 0
PC_DT = PC_RW + RW_COLS
PC_GATE = PC_DT + DT_SLOT
PC_Z = PC_GATE + 2 * D_MODEL
PC_XS = PC_Z + SSM_DIM
PC_BC = PC_XS + SSM_DIM
P_COLS = PC_BC + 2 * BC_DIM
assert PC_DT % DT_COLS == 0 and PC_GATE % D_MODEL == 0 and PC_Z % SSM_DIM == 0
assert PC_XS % SSM_DIM == 0 and PC_BC % (2 * BC_DIM) == 0 and P_COLS % MM_TN == 0
assert F_ROWS % FFN_TM == 0 and X_ROWS == FFN_TM and M_ROWS % MM_TM == 0 and ROW_H0 % MERGE_TM == 0


def _dot(a, b):
    return jnp.dot(a, b, preferred_element_type=F32)


def _dot_nt(a, b):
    return lax.dot_general(a, b, (((1,), (1,)), ((), ())), preferred_element_type=F32)


def _dot_tn(a, b):
    return lax.dot_general(a, b, (((0,), (0,)), ((), ())), preferred_element_type=F32)


def _split2(x):
    hi = x.astype(BF16)
    lo = (x - hi.astype(F32)).astype(BF16)
    return hi, lo


def _dot_sel_r(x, sel):
    hi, lo = _split2(x)
    return _dot(hi, sel) + _dot(lo, sel)


def _dot_sel_l(sel, x):
    hi, lo = _split2(x)
    return _dot(sel, hi) + _dot(sel, lo)


def _causal_blocks(n, blk):
    i = lax.broadcasted_iota(jnp.int32, (n, n), 0)
    j = lax.broadcasted_iota(jnp.int32, (n, n), 1)
    keep = jnp.logical_and(i >= j, jnp.bitwise_xor(i, j) < blk)
    return jnp.where(keep, 1.0, 0.0).astype(BF16)


def _sigmoid(x):
    return jax.nn.sigmoid(x)


def _softplus(x):
    return jnp.maximum(x, 0.0) + jnp.log1p(jnp.exp(-jnp.abs(x)))


def _layer_norm(s, g, b):
    mu = jnp.mean(s, axis=-1, keepdims=True)
    d = s - mu
    var = jnp.mean(d * d, axis=-1, keepdims=True)
    return d * lax.rsqrt(var + LN_EPS) * g + b


class _CastSlot:
    def __init__(self, w, rows, t0):
        assert w.ndim == 2 and w.shape[0] % rows == 0
        self.w, self.rows, self.t0, self.n = w, rows, t0, w.shape[0] // rows

    def spec(self, step_of):
        def index(*ids):
            return jnp.clip(step_of(*ids) - self.t0, 0, self.n - 1), 0
        return pl.BlockSpec((self.rows, self.w.shape[1]), index)

    def out_shape(self):
        return jax.ShapeDtypeStruct(self.w.shape, BF16)


def _run_cast_slots(t, slots, in_refs, out_refs):
    for slot, i_ref, o_ref in zip(slots, in_refs, out_refs, strict=True):
        @pl.when(jnp.logical_and(t >= slot.t0, t < slot.t0 + slot.n))
        def _(i_ref=i_ref, o_ref=o_ref):
            o_ref[...] = i_ref[...].astype(BF16)


def _split_refs(refs, *counts):
    groups, at = [], 0
    for n in counts:
        groups.append(refs[at:at + n])
        at += n
    return (*groups, refs[at:])


def _chain_slots(weights_rows, t0=0):
    slots = []
    for w, rows in weights_rows:
        slots.append(_CastSlot(w, rows, t0))
        t0 += slots[-1].n
    return slots, t0


N_MAIN_TILES = F_ROWS // FFN_TM
N_TILES = M_ROWS // FFN_TM


def _ffn_body(load_x, emit_mid, emit_last, wg_ref, wu_ref, wd_ref, g_ref, b_ref, xb_ref, acc_ref, s_ref):
    i, f = pl.program_id(0), pl.program_id(1)
    last = pl.num_programs(1) - 1
    real = i < N_TILES
    both = jnp.logical_and

    def mlp(xb):
        gate = _dot(xb, wg_ref[...])
        up = _dot(xb, wu_ref[...])
        return _dot((gate * _sigmoid(gate) * up).astype(BF16), wd_ref[...])

    def finish_prev(emit):
        emit(_layer_norm(s_ref[...], g_ref[...], b_ref[...]))

    def first_step(with_prev):
        xb = load_x().astype(BF16)
        xb_ref[...] = xb
        acc_ref[...] = mlp(xb)
        if with_prev:
            finish_prev(emit_mid)

    pl.when(both(f == 0, i == 0))(functools.partial(first_step, False))
    pl.when(both(f == 0, both(i > 0, real)))(functools.partial(first_step, True))
    pl.when(both(f == 0, i == N_TILES))(functools.partial(finish_prev, emit_last))

    @pl.when(both(real, both(f > 0, f < last)))
    def _():
        acc_ref[...] += mlp(xb_ref[...])

    @pl.when(both(real, f == last))
    def _():
        s_ref[...] = ALPHA * load_x() + 0.5 * (acc_ref[...] + mlp(xb_ref[...]))


def _ffn_in_kernel(xm_ref, xe_ref, wg_ref, wu_ref, wd_ref, g_ref, b_ref, *rest, slots):
    cast_in, (o_ref, ob_ref), cast_out, scratch = _split_refs(rest, len(slots), 2, len(slots))
    _run_cast_slots(pl.program_id(0) * pl.num_programs(1) + pl.program_id(1), slots, cast_in, cast_out)
    load_x = lambda: jnp.where(pl.program_id(0) < N_MAIN_TILES, xm_ref[...], xe_ref[...])

    def emit(y):
        o_ref[...] = y
        ob_ref[...] = y.astype(BF16)

    _ffn_body(load_x, emit, emit, wg_ref, wu_ref, wd_ref, g_ref, b_ref, *scratch)


def _ffn_out_kernel(x_ref, wg_ref, wu_ref, wd_ref, g_ref, b_ref, om_ref, oe_ref, *scratch):
    assert N_TILES == N_MAIN_TILES + 1

    def emit_frames(y):
        om_ref[...] = y

    def emit_extra(y):
        oe_ref[...] = y

    _ffn_body(lambda: x_ref[...], emit_frames, emit_extra, wg_ref, wu_ref, wd_ref, g_ref, b_ref, *scratch)


def _ffn_specs(d):
    nf = D_FF // FFN_TF
    step = lambda i, f: jnp.where(i == N_TILES, nf - 1, f)
    return nf, [
        pl.BlockSpec((d, FFN_TF), lambda i, f: (0, step(i, f))),
        pl.BlockSpec((d, FFN_TF), lambda i, f: (0, step(i, f) + nf)),
        pl.BlockSpec((FFN_TF, d), lambda i, f: (step(i, f), 0)),
        pl.BlockSpec((1, d), lambda i, f: (0, 0)),
        pl.BlockSpec((1, d), lambda i, f: (0, 0)),
    ]


_FFN_PARAMS = dict(dimension_semantics=("arbitrary", "arbitrary"), vmem_limit_bytes=VMEM_LIMIT)
_in_frame_tile = lambda i, f: (jnp.minimum(i, N_MAIN_TILES - 1), 0)
_in_any_tile = lambda i, f: (jnp.minimum(i, N_TILES - 1), 0)
_out_frame_tile = lambda i, f: (jnp.clip(i - 1, 0, N_MAIN_TILES - 1), 0)
_out_any_tile = lambda i, f: (jnp.maximum(i - 1, 0), 0)
_only_tile = lambda i, f: (0, 0)


def _ffn_scratch(tile):
    return [pltpu.VMEM(tile, BF16), pltpu.VMEM(tile, F32), pltpu.VMEM(tile, F32)]


def _ffn_ln_in(x_main, x_extra, w_gu, w_dn, ln_g, ln_b, cast_weights):
    d = x_main.shape[1]
    nf, wspecs = _ffn_specs(d)
    tile = (FFN_TM, d)
    grid = (N_TILES + 1, nf)
    slots, t_end = _chain_slots(cast_weights)
    assert t_end <= grid[0] * grid[1]
    cast_specs = [s.spec(lambda i, f: i * nf + f) for s in slots]
    return pl.pallas_call(
        functools.partial(_ffn_in_kernel, slots=slots),
        grid=grid,
        in_specs=[pl.BlockSpec(tile, _in_frame_tile),
                  pl.BlockSpec(tile, _only_tile, pipeline_mode=pl.Buffered(1))] + wspecs + cast_specs,
        out_specs=[pl.BlockSpec(tile, _out_any_tile), pl.BlockSpec(tile, _out_any_tile)] + cast_specs,
        out_shape=[jax.ShapeDtypeStruct((M_ROWS, d), F32), jax.ShapeDtypeStruct((M_ROWS, d), BF16)]
        + [s.out_shape() for s in slots],
        scratch_shapes=_ffn_scratch(tile),
        compiler_params=pltpu.CompilerParams(**_FFN_PARAMS),
        name="ffn_ln_in",
    )(x_main, x_extra, w_gu, w_gu, w_dn, ln_g, ln_b, *[s.w for s in slots])


def _ffn_ln_out(x, w_gu, w_dn, ln_g, ln_b):
    d = x.shape[1]
    nf, wspecs = _ffn_specs(d)
    tile = (FFN_TM, d)
    return pl.pallas_call(
        _ffn_out_kernel,
        grid=(N_TILES + 1, nf),
        in_specs=[pl.BlockSpec(tile, _in_any_tile)] + wspecs,
        out_specs=[pl.BlockSpec(tile, _out_frame_tile), pl.BlockSpec(tile, _only_tile)],
        out_shape=[jax.ShapeDtypeStruct((F_ROWS, d), F32), jax.ShapeDtypeStruct((X_ROWS, d), F32)],
        scratch_shapes=_ffn_scratch(tile),
        compiler_params=pltpu.CompilerParams(**_FFN_PARAMS),
        name="ffn_ln_out",
    )(x, w_gu, w_gu, w_dn, ln_g, ln_b)


def _proj_tiles():
    o_z = RW_SHIFT_COLS
    o_xbc = o_z + SSM_DIM
    o_dt = o_xbc + CONV_DIM
    o_gate = o_dt + SSM_HEADS
    runs = [(PC_RW, 0, 3 * RW_DIM), (PC_GATE, o_gate, 2 * D_MODEL), (PC_Z, o_z, SSM_DIM), (PC_XS, o_xbc, CONV_DIM)]
    src = [None] * (P_COLS // MM_TN)
    for dst0, src0, width in runs:
        assert dst0 % MM_TN == 0 and width % MM_TN == 0 and src0 % (2 * SUBLANES) == 0
        for t in range(width // MM_TN):
            src[dst0 // MM_TN + t] = src0 + t * MM_TN
    assert src.count(None) == 1
    return src


def _mm_kernel(x_ref, wt_ref, wmix_ref, o_ref, *, mixed_tile):
    j = pl.program_id(0)

    @pl.when(j != mixed_tile)
    def _():
        o_ref[...] = _dot_nt(x_ref[...], wt_ref[...])

    @pl.when(j == mixed_tile)
    def _():
        o_ref[...] = _dot_nt(x_ref[...], wmix_ref[...])


def _project(x, wt, wt_mixed):
    m, k = x.shape
    src = _proj_tiles()
    mixed_tile = src.index(None)

    unit = 2 * SUBLANES

    def wt_row(j, i):
        start = 0
        for t, s in enumerate(src):
            if s is not None:
                start = jnp.where(j == t, s // unit, start)
        return start * unit, 0

    return pl.pallas_call(
        functools.partial(_mm_kernel, mixed_tile=mixed_tile),
        grid=(len(src), m // MM_TM),
        in_specs=[pl.BlockSpec((MM_TM, k), lambda j, i: (i, 0)),
                  pl.BlockSpec((pl.Element(MM_TN), pl.Element(k)), wt_row),
                  pl.BlockSpec((MM_TN, k), lambda j, i: (0, 0), pipeline_mode=pl.Buffered(1))],
        out_specs=pl.BlockSpec((MM_TM, MM_TN), lambda j, i: (i, j)),
        out_shape=jax.ShapeDtypeStruct((m, P_COLS), F32),
        compiler_params=pltpu.CompilerParams(
            dimension_semantics=("parallel", "parallel"), vmem_limit_bytes=VMEM_LIMIT),
        name="proj_in",
    )(x, wt, wt_mixed)


def _rwkv_kernel(*refs, chunk, pad, n_seq, n_p, slots):
    p_refs, fixed, cast_in, outs, cast_out, (yacc_ref,) = _split_refs(refs, n_p, 14, len(slots), 3, len(slots))
    (hist_ref, s0_ref, mu_ref, w0_ref, w2_ref, a0_ref, a2_ref, g2_ref, kk_ref, ka_ref, rk_ref,
     gnw_ref, gnb_ref, seg_ref) = fixed
    y_ref, s_ref, shift_ref = outs
    c = pl.program_id(1)
    _run_cast_slots(pl.program_id(0) * pl.num_programs(1) + c, slots, cast_in, cast_out)
    n_rows = n_seq * chunk
    seqs = range(n_seq)
    rsl = [slice(q * chunk, (q + 1) * chunk) for q in seqs]
    rows = lax.broadcasted_iota(jnp.int32, (n_rows, 1), 0)

    @pl.when(c == 0)
    def _():
        shift_ref[...] = hist_ref[...]
        s_ref[...] = s0_ref[...]

    p = jnp.concatenate([r[...] for r in p_refs], axis=0)
    if pad:
        p = jnp.where(jnp.logical_and(c == 0, jnp.bitwise_and(rows, chunk - 1) < pad), 0.0, p)
    prev = pltpu.roll(p, 1, 0)
    for q in seqs:
        prev = jnp.where(rows == q * chunk, shift_ref[q], prev)
    for q in seqs:
        shift_ref[q] = p[(q + 1) * chunk - 1:(q + 1) * chunk, :]
    ps = p + (prev - p) * mu_ref[...]

    r = ps[:, 0:RW_DIM]
    k = ps[:, RW_DIM:2 * RW_DIM]
    v = ps[:, 2 * RW_DIM:3 * RW_DIM]
    wd = ps[:, RW_OFF_WD:RW_OFF_WD + LORA_PAD]
    ad = ps[:, RW_OFF_AD:RW_OFF_AD + LORA_PAD]
    gd = ps[:, RW_OFF_GD:RW_OFF_GD + RW_GATE_LORA]

    lw = -math.exp(-0.5) * _sigmoid(w0_ref[...] + _dot(jnp.tanh(wd).astype(BF16), w2_ref[...]))
    a = _sigmoid(a0_ref[...] + _dot(ad.astype(BF16), a2_ref[...]))
    g = _dot(_sigmoid(gd).astype(BF16), g2_ref[...])

    seg = seg_ref[...]

    def head_sum(x):
        w = seg.shape[0]
        return jnp.concatenate(
            [_dot_sel_r(x[:, j * w:(j + 1) * w], seg) for j in range(RW_DIM // w)], axis=1)

    kk = k * kk_ref[...]
    kk = kk * lax.rsqrt(jnp.maximum(head_sum(kk * kk), 1e-24))
    k = k * (1.0 + (a - 1.0) * ka_ref[...])
    b_neg = -(kk * a)

    cum = _dot_sel_l(_causal_blocks(n_rows, chunk), lw)
    cum_last = [cum[(q + 1) * chunk - 1:(q + 1) * chunk, :] for q in seqs]
    cum_end = jnp.concatenate([jnp.broadcast_to(x, (chunk, RW_DIM)) for x in cum_last], axis=0)
    e_neg = jnp.exp(-cum)
    kt = (kk * jnp.exp(cum - lw)).astype(BF16)
    rt = (r * jnp.exp(cum)).astype(BF16)
    kd = (k * e_neg).astype(BF16)
    bd_neg = (b_neg * e_neg).astype(BF16)
    e_end = jnp.exp(cum_end - cum)
    k_end = (k * e_end).astype(BF16)
    b_end_neg = (b_neg * e_end).astype(BF16)
    p_end = [jnp.exp(x) for x in cum_last]
    vb = v.astype(BF16)

    ri = lax.broadcasted_iota(jnp.int32, (2 * chunk, 2 * chunk), 0)
    rj = lax.broadcasted_iota(jnp.int32, (2 * chunk, 2 * chunk), 1)
    bi = jnp.where(ri >= chunk, ri - chunk + 1, ri)
    bj = jnp.where(rj >= chunk, rj - chunk, rj)
    keep = bi > bj
    ti = lax.broadcasted_iota(jnp.int32, (chunk, 2 * chunk), 0)
    tj = lax.broadcasted_iota(jnp.int32, (chunk, 2 * chunk), 1)
    hi_lane = tj >= chunk
    eye_hi = (tj == ti + chunk).astype(F32)
    zeros_v = jnp.zeros((chunk, HEAD), BF16)

    pairs = [(q, h) for q in seqs for h in range(RW_HEADS)]
    sls = [(rsl[q], slice(h * HEAD, (h + 1) * HEAD)) for q, h in pairs]
    s_old = [s_ref[q, h] for q, h in pairs]
    s_b = [s.astype(BF16) for s in s_old]
    amats = [jnp.where(keep, _dot_nt(jnp.concatenate([kt[sl], rt[sl]], axis=0),
                                     jnp.concatenate([bd_neg[sl], kd[sl]], axis=0)), 0.0)
             for sl in sls]
    top = [m[:chunk] for m in amats]
    bot = [m[chunk:].astype(BF16) for m in amats]
    top_b = [x.astype(BF16) for x in top]
    tq = [_dot(xb[:, :chunk], jnp.where(hi_lane, eye_hi, x).astype(BF16)) + eye_hi
          for x, xb in zip(top, top_b)]
    n = 2
    while n < chunk:
        tq = [jnp.where(hi_lane, x, 0.0) + _dot(x[:, :chunk].astype(BF16), x.astype(BF16)) for x in tq]
        n *= 2
    t_b = [x[:, chunk:].astype(BF16) for x in tq]
    v_h = [vb[sl] for sl in sls]
    rhs = [_dot_nt(kt[sl], sb) + _dot(xb, jnp.concatenate([zeros_v, vh], axis=0))
           for sl, sb, xb, vh in zip(sls, s_b, top_b, v_h)]
    ub = [_dot(t, x.astype(BF16)).astype(BF16) for t, x in zip(t_b, rhs)]
    uv = [jnp.concatenate([u, vh], axis=0) for u, vh in zip(ub, v_h)]
    for i, (q, h) in enumerate(pairs):
        sl = sls[i]
        yacc_ref[sl] = _dot_nt(rt[sl], s_b[i]) + _dot(bot[i], uv[i])
        s_ref[q, h] = s_old[i] * p_end[q][:, sl[1]] + _dot_tn(
            uv[i], jnp.concatenate([b_end_neg[sl], k_end[sl]], axis=0))

    y = yacc_ref[...]
    inv_n = 1.0 / HEAD
    mean = head_sum(y) * inv_n
    yc = y - mean
    var = head_sum(yc * yc) * inv_n
    yn = yc * lax.rsqrt(var + RW_GN_EPS) * gnw_ref[...] + gnb_ref[...]
    bonus = head_sum(r * k * rk_ref[...]) * v
    out = ((yn + bonus) * g).astype(y_ref.dtype)
    for q in seqs:
        y_ref[q] = out[rsl[q]]


RWKV_GROUPS = (2, 8)
SSD_GROUPS = (1, 1)


class _SeqPlan:
    def __init__(self, prompt, groups):
        self.prompt = prompt
        if prompt:
            self.n_seq, self.chunk, self.pad, self.total, self.seq_len = groups[0], CHUNK, PAD, BATCH, SEQ
            self.grid = (BATCH // self.n_seq, SEQ_CHUNKS)
        else:
            self.n_seq, self.chunk, self.pad, self.total, self.seq_len = groups[1], DEC_SEQ, 0, DEC_BATCH, DEC_SEQ
            self.grid = (DEC_BATCH // self.n_seq, 1)

    def p_specs(self, width, col):
        cb = col // width
        if not self.prompt:
            rows = self.n_seq * self.chunk
            return [pl.BlockSpec((rows, width), lambda g, c: (ROW_S0 // rows + g, cb))]

        def spec(q):
            def index(g, c):
                seq = g * self.n_seq + q
                return jnp.where(c == 0, ROW_H0 // CHUNK + seq, seq * (SEQ // CHUNK) + c - 1), cb
            return pl.BlockSpec((CHUNK, width), index)
        return [spec(q) for q in range(self.n_seq)]

    def y_spec(self, dim):
        return pl.BlockSpec((self.n_seq, self.chunk, dim), lambda g, c: (g, jnp.maximum(c - 1, 0), 0))

    def y_shape(self, dim):
        return jax.ShapeDtypeStruct((self.total, self.seq_len, dim), BF16)

    def state_spec(self, shape):
        nd = len(shape)
        return pl.BlockSpec((self.n_seq,) + tuple(shape), lambda g, c: (g,) + (0,) * nd)

    def cast_slots(self, cast_weights):
        slots, t_end = _chain_slots(cast_weights)
        assert t_end <= self.grid[0] * self.grid[1]
        specs = [s.spec(lambda g, c: g * self.grid[1] + c) for s in slots]
        return slots, specs, specs


_MIX_PARAMS = dict(dimension_semantics=("arbitrary", "arbitrary"), vmem_limit_bytes=VMEM_LIMIT)


def _rwkv_mix(p_all, hist, s0, wts, *, prompt, cast_weights=()):
    plan = _SeqPlan(prompt, RWKV_GROUPS)
    slots, cast_in_specs, cast_out_specs = plan.cast_slots(cast_weights)
    p_specs = plan.p_specs(RW_COLS, PC_RW)
    const2 = lambda g, c: (0, 0)
    wspecs = [pl.BlockSpec(w.shape, const2) for w in wts]
    shift_spec = plan.state_spec((1, RW_COLS))
    wkv_spec = plan.state_spec((RW_HEADS, HEAD, HEAD))
    return pl.pallas_call(
        functools.partial(_rwkv_kernel, chunk=plan.chunk, pad=plan.pad, n_seq=plan.n_seq, n_p=len(p_specs),
                          slots=slots),
        grid=plan.grid,
        in_specs=p_specs + [shift_spec, wkv_spec] + wspecs + cast_in_specs,
        out_specs=[plan.y_spec(RW_DIM), wkv_spec, shift_spec] + cast_out_specs,
        out_shape=[plan.y_shape(RW_DIM),
                   jax.ShapeDtypeStruct((plan.total, RW_HEADS, HEAD, HEAD), F32),
                   jax.ShapeDtypeStruct((plan.total, 1, RW_COLS), F32)] + [s.out_shape() for s in slots],
        scratch_shapes=[pltpu.VMEM((plan.n_seq * plan.chunk, RW_DIM), F32)],
        compiler_params=pltpu.CompilerParams(**_MIX_PARAMS),
        name="rwkv_mix_c%d" % plan.chunk,
    )(*([p_all] * len(p_specs)), hist, s0, *wts, *[s.w for s in slots])


def _ssd_kernel(*refs, chunk, pad, n_seq, n_p, slots):
    (xs_refs, bc_refs, z_refs, dt_refs, fixed, cast_in, outs, cast_out, (xpad_ref, yacc_ref)) = _split_refs(
        refs, n_p, n_p, n_p, n_p, 9, len(slots), 3, len(slots))
    hist_ref, h0_ref, cw_ref, cb_ref, dtb_ref, alog_ref, dskip_ref, nw_ref, exp_ref = fixed
    y_ref, h_ref, tail_ref = outs
    c = pl.program_id(1)
    _run_cast_slots(pl.program_id(0) * pl.num_programs(1) + c, slots, cast_in, cast_out)
    n_rows = n_seq * chunk
    seqs = range(n_seq)
    rsl = [slice(q * chunk, (q + 1) * chunk) for q in seqs]
    rows = lax.broadcasted_iota(jnp.int32, (n_rows, 1), 0)
    stack = lambda rs: jnp.concatenate([r[...] for r in rs], axis=0)

    @pl.when(c == 0)
    def _():
        xpad_ref[:, 0:CONV_HIST, :] = hist_ref[...]
        h_ref[...] = h0_ref[...]

    u = jnp.concatenate([stack(xs_refs), stack(bc_refs)], axis=1)
    if pad:
        is_pad = jnp.logical_and(c == 0, jnp.bitwise_and(rows, chunk - 1) < pad)
        u = jnp.where(is_pad, 0.0, u)
    convs = []
    for q in seqs:
        u_q = u[rsl[q]]
        xpad_ref[q, CONV_HIST:CONV_HIST + chunk, :] = u_q
        conv = cb_ref[...] + u_q * cw_ref[CONV_W - 1:CONV_W, :]
        for i in range(CONV_W - 1):
            back = CONV_W - 1 - i
            conv = conv + xpad_ref[q, CONV_HIST - back:CONV_HIST - back + chunk, :] * cw_ref[i:i + 1, :]
        hist_next = xpad_ref[q, chunk:chunk + CONV_HIST, :]
        xpad_ref[q, 0:CONV_HIST, :] = hist_next
        tail_ref[q] = hist_next
        convs.append(conv)
    conv = jnp.concatenate(convs, axis=0)
    xbc = conv * _sigmoid(conv)
    xs = xbc[:, 0:SSM_DIM]
    bm = xbc[:, SSM_DIM:SSM_DIM + BC_DIM].astype(BF16)
    cm = xbc[:, SSM_DIM + BC_DIM:CONV_DIM].astype(BF16)

    dt = _softplus(stack(dt_refs) + dtb_ref[...])
    if pad:
        dt = jnp.where(is_pad, 0.0, dt)
    da = dt * (-jnp.exp(alog_ref[...]))
    ci = lax.broadcasted_iota(jnp.int32, (chunk, chunk), 0)
    cj = lax.broadcasted_iota(jnp.int32, (chunk, chunk), 1)
    causal = ci >= cj
    acs = _dot_sel_l(_causal_blocks(n_rows, chunk), da)
    acs_t = acs.T
    acs_end = jnp.concatenate(
        [jnp.broadcast_to(acs[(q + 1) * chunk - 1:(q + 1) * chunk, :], (chunk, DT_COLS)) for q in seqs], axis=0)

    expand = exp_ref[...]
    xdt = xs * _dot_sel_r(dt, expand)
    xdt_b = xdt.astype(BF16)
    xdt_end = (xdt * _dot_sel_r(jnp.exp(acs_end - acs), expand)).astype(BF16)
    e_acs = _dot_sel_r(jnp.exp(acs), expand)
    decay_end = [jnp.exp(acs_t[:, (q + 1) * chunk - 1:(q + 1) * chunk]) for q in seqs]

    gw = SSM_HPG * HEAD
    groups = range(SSM_GROUPS)
    heads = range(SSM_HEADS)
    qg = [(q, g) for q in seqs for g in groups]
    qh = [(q, hd) for q in seqs for hd in heads]
    bm_g = {(q, g): bm[rsl[q], g * SSM_STATE:(g + 1) * SSM_STATE] for q, g in qg}
    cm_g = {(q, g): cm[rsl[q], g * SSM_STATE:(g + 1) * SSM_STATE] for q, g in qg}
    h_old = {(q, hd): h_ref[q, hd] for q, hd in qh}
    cb = {k: _dot_nt(cm_g[k], bm_g[k]) for k in qg}
    y_off = {(q, g): _dot_nt(cm_g[q, g], jnp.concatenate(
        [h_old[q, g * SSM_HPG + hh].astype(BF16) for hh in range(SSM_HPG)], axis=0)) for q, g in qg}
    st = {(q, g): _dot_tn(xdt_end[rsl[q], g * gw:(g + 1) * gw], bm_g[q, g]) for q, g in qg}
    for q, hd in qh:
        sl = slice(hd * HEAD, (hd + 1) * HEAD)
        seg = acs[rsl[q], hd:hd + 1] - acs_t[hd:hd + 1, rsl[q]]
        lmat = jnp.exp(jnp.where(causal, seg, -jnp.inf))
        yacc_ref[rsl[q], sl] = _dot((cb[q, hd // SSM_HPG] * lmat).astype(BF16), xdt_b[rsl[q], sl])
    for q, hd in qh:
        g, hh = divmod(hd, SSM_HPG)
        h_ref[q, hd] = h_old[q, hd] * decay_end[q][hd:hd + 1, :] + st[q, g][hh * HEAD:(hh + 1) * HEAD, :]

    y_off_all = jnp.concatenate(
        [jnp.concatenate([y_off[q, g] for g in groups], axis=1) for q in seqs], axis=0)
    y = yacc_ref[...] + y_off_all * e_acs + xs * dskip_ref[...]
    zz = stack(z_refs)
    y = y * (zz * _sigmoid(zz))
    parts = []
    for g in groups:
        yg = y[:, g * gw:(g + 1) * gw]
        ms = jnp.mean(yg * yg, axis=-1, keepdims=True)
        parts.append(yg * lax.rsqrt(ms + RMS_EPS))
    out = (jnp.concatenate(parts, axis=1) * nw_ref[...]).astype(y_ref.dtype)
    for q in seqs:
        y_ref[q] = out[rsl[q]]


def _ssd_mix(p_all, hist, h0, wts, *, prompt, cast_weights=()):
    plan = _SeqPlan(prompt, SSD_GROUPS)
    slots, cast_in_specs, cast_out_specs = plan.cast_slots(cast_weights)
    p_specs = (plan.p_specs(SSM_DIM, PC_XS) + plan.p_specs(2 * BC_DIM, PC_BC)
               + plan.p_specs(SSM_DIM, PC_Z) + plan.p_specs(DT_COLS, PC_DT))
    const2 = lambda g, c: (0, 0)
    wspecs = [pl.BlockSpec(w.shape, const2) for w in wts]
    conv_spec = plan.state_spec((CONV_HIST, CONV_DIM))
    ssm_spec = plan.state_spec((SSM_HEADS, HEAD, SSM_STATE))
    return pl.pallas_call(
        functools.partial(_ssd_kernel, chunk=plan.chunk, pad=plan.pad, n_seq=plan.n_seq, n_p=len(p_specs) // 4,
                          slots=slots),
        grid=plan.grid,
        in_specs=p_specs + [conv_spec, ssm_spec] + wspecs + cast_in_specs,
        out_specs=[plan.y_spec(SSM_DIM), ssm_spec, conv_spec] + cast_out_specs,
        out_shape=[plan.y_shape(SSM_DIM),
                   jax.ShapeDtypeStruct((plan.total, SSM_HEADS, HEAD, SSM_STATE), F32),
                   jax.ShapeDtypeStruct((plan.total, CONV_HIST, CONV_DIM), F32)] + [s.out_shape() for s in slots],
        scratch_shapes=[pltpu.VMEM((plan.n_seq, CONV_HIST + plan.chunk, CONV_DIM), F32),
                        pltpu.VMEM((plan.n_seq * plan.chunk, SSM_DIM), F32)],
        compiler_params=pltpu.CompilerParams(**_MIX_PARAMS),
        name="ssd_mix_c%d" % plan.chunk,
    )(*([p_all] * len(p_specs)), hist, h0, *wts, *[s.w for s in slots])


MERGE_F_TILES = F_ROWS // MERGE_TM
assert S_ROWS == MERGE_TM


def _merge_kernel(x_ref, yrw_f_ref, yrw_s_ref, yssm_f_ref, yssm_s_ref, ga_ref, gb_ref, bga_ref, bgb_ref,
                  wrw_ref, wssm_ref, wout_ref, g_ref, b_ref, o_ref):
    is_frame = pl.program_id(0) < MERGE_F_TILES
    y_rw = jnp.where(is_frame, yrw_f_ref[...], yrw_s_ref[...])
    y_ssm = jnp.where(is_frame, yssm_f_ref[...], yssm_s_ref[...])
    merged = (_sigmoid(ga_ref[...] + bga_ref[...]) * _dot(y_rw, wrw_ref[...])
              + _sigmoid(gb_ref[...] + bgb_ref[...]) * _dot(y_ssm, wssm_ref[...]))
    s = ALPHA * x_ref[...] + _dot(merged.astype(BF16), wout_ref[...])
    o_ref[...] = _layer_norm(s, g_ref[...], b_ref[...])


def _merge(x, y_rw_f, y_rw_s, y_ssm_f, y_ssm_s, p_all, b_gate, w_rw_out, w_ssm_out, w_out, ln_g, ln_b):
    m, d = x.shape
    tm = MERGE_TM
    row = lambda i: (i, 0)
    frame_row = lambda i: (jnp.minimum(i, MERGE_F_TILES - 1), 0)
    const = lambda i: (0, 0)
    resident = lambda w: pl.BlockSpec(w.shape, const, pipeline_mode=pl.Buffered(1))
    return pl.pallas_call(
        _merge_kernel,
        grid=(m // tm,),
        in_specs=[
            pl.BlockSpec((tm, d), row),
            pl.BlockSpec((tm, RW_DIM), frame_row),
            pl.BlockSpec((tm, RW_DIM), const),
            pl.BlockSpec((tm, SSM_DIM), frame_row),
            pl.BlockSpec((tm, SSM_DIM), const),
            pl.BlockSpec((tm, d), lambda i: (i, PC_GATE // d)),
            pl.BlockSpec((tm, d), lambda i: (i, PC_GATE // d + 1)),
            pl.BlockSpec((1, d), lambda i: (0, 0)),
            pl.BlockSpec((1, d), lambda i: (0, 1)),
            resident(w_rw_out), resident(w_ssm_out), resident(w_out),
            pl.BlockSpec((1, d), const),
            pl.BlockSpec((1, d), const),
        ],
        out_specs=pl.BlockSpec((tm, d), row),
        out_shape=jax.ShapeDtypeStruct((m, d), F32),
        compiler_params=pltpu.CompilerParams(
            dimension_semantics=("arbitrary",), vmem_limit_bytes=VMEM_LIMIT),
        name="merge_ln",
    )(x, y_rw_f, y_rw_s, y_ssm_f, y_ssm_s, p_all, p_all, b_gate, b_gate, w_rw_out, w_ssm_out, w_out,
      ln_g, ln_b)


def _pad_cols(x, width):
    return jnp.concatenate([x, jnp.zeros(x.shape[:-1] + (width - x.shape[-1],), x.dtype)], axis=-1)


def _rw_cols(x):
    o_wd, o_ad, o_gd = 3 * RW_DIM, 3 * RW_DIM + RW_DECAY_LORA, 3 * RW_DIM + RW_DECAY_LORA + RW_AAA_LORA
    return jnp.concatenate([
        x[..., :o_wd],
        _pad_cols(x[..., o_wd:o_ad], LORA_PAD),
        _pad_cols(x[..., o_ad:o_gd], LORA_PAD),
        x[..., o_gd:],
    ], axis=-1)


def _pad_rows(x, height):
    return jnp.concatenate([x, jnp.zeros((height - x.shape[0],) + x.shape[1:], x.dtype)], axis=0)


def _rw_rows(x):
    o_wd, o_ad, o_gd = 3 * RW_DIM, 3 * RW_DIM + RW_DECAY_LORA, 3 * RW_DIM + RW_DECAY_LORA + RW_AAA_LORA
    return jnp.concatenate([
        x[:o_wd], _pad_rows(x[o_wd:o_ad], LORA_PAD), _pad_rows(x[o_ad:o_gd], LORA_PAD), x[o_gd:]], axis=0)


def _rw_cols_inv(x):
    return jnp.concatenate([
        x[..., :RW_OFF_WD],
        x[..., RW_OFF_WD:RW_OFF_WD + RW_DECAY_LORA],
        x[..., RW_OFF_AD:RW_OFF_AD + RW_AAA_LORA],
        x[..., RW_OFF_GD:],
    ], axis=-1)


def _block_ones(n, blk):
    i = jnp.arange(n) // blk
    return (i[:, None] == i[None, :]).astype(BF16)


def kernel(x_prompt, x_sample, state_rwkv_shift, state_wkv, state_conv, state_ssm, meta_tokens, ffn1_gu, ffn1_dn, ln1_g, ln1_b, w_in, b_gate, rw_mu, rw_w0, rw_w2, rw_a0, rw_a2, rw_g2, rw_kk, rw_ka, rw_rk, rw_gn_w, rw_gn_b, conv_w, conv_b, dt_bias, a_log, d_skip, ssm_norm_w, w_rw_out, w_ssm_out, w_out, ln2_g, ln2_b, ffn2_gu, ffn2_dn, ln3_g, ln3_b):
    lyr = 0
    row = lambda t: t[lyr].reshape(1, -1).astype(F32)

    head_rows = jnp.concatenate([jnp.zeros((PAD, D_MODEL), F32), meta_tokens.astype(F32)], axis=0)
    x_extra = jnp.concatenate([x_sample.reshape(S_ROWS, D_MODEL)] + [head_rows] * BATCH, axis=0)
    x1, x1b, wt = _ffn_ln_in(x_prompt.reshape(F_ROWS, D_MODEL), x_extra, ffn1_gu[lyr].astype(BF16),
                             ffn1_dn[lyr].astype(BF16), row(ln1_g), row(ln1_b),
                             [(jnp.swapaxes(w_in[lyr], 0, 1), FFN_CAST_ROWS)])

    o_dt = RW_SHIFT_COLS + SSM_DIM + CONV_DIM
    wt_mixed = jnp.concatenate([
        _rw_rows(wt[:RW_SHIFT_COLS])[3 * RW_DIM:], _pad_rows(wt[o_dt:o_dt + SSM_HEADS], DT_SLOT)], axis=0)
    p_all = _project(x1b, wt, wt_mixed)

    rw_wts = [
        _rw_cols(rw_mu[lyr]).reshape(1, RW_COLS), row(rw_w0),
        _pad_rows(rw_w2[lyr], LORA_PAD).astype(BF16), row(rw_a0),
        _pad_rows(rw_a2[lyr], LORA_PAD).astype(BF16), rw_g2[lyr].astype(BF16),
        row(rw_kk), row(rw_ka), row(rw_rk), row(rw_gn_w), row(rw_gn_b),
        _block_ones(2 * LANES, HEAD),
    ]
    y_rw_s, wkv_s, shift_s = _rwkv_mix(
        p_all, _rw_cols(state_rwkv_shift[lyr]), state_wkv[lyr], rw_wts, prompt=False)
    y_rw_f, wkv_p, shift_p, w_rw_o, w_ssm_o, w_o = _rwkv_mix(
        p_all, jnp.zeros((BATCH, 1, RW_COLS), F32), jnp.zeros((BATCH, RW_HEADS, HEAD, HEAD), F32), rw_wts,
        prompt=True, cast_weights=[(w_rw_out[lyr], MIX_CAST_ROWS), (w_ssm_out[lyr], MIX_CAST_ROWS),
                                   (w_out[lyr], MIX_CAST_ROWS)])

    head_of_lane = jnp.arange(SSM_DIM) // HEAD
    expand = (jnp.arange(DT_COLS)[:, None] == head_of_lane[None, :]).astype(BF16)
    ssd_wts = [
        conv_w[lyr], row(conv_b), _pad_cols(row(dt_bias), DT_COLS), _pad_cols(row(a_log), DT_COLS),
        jnp.repeat(d_skip[lyr], HEAD).reshape(1, SSM_DIM), row(ssm_norm_w),
        expand,
    ]
    hist_rows = lambda t: jnp.pad(t, ((0, 0), (CONV_HIST - (CONV_W - 1), 0), (0, 0)))
    y_ssm_s, ssm_s, conv_s = _ssd_mix(
        p_all, hist_rows(state_conv[lyr]), state_ssm[lyr], ssd_wts, prompt=False)
    y_ssm_f, ssm_p, conv_p, gu2, dn2 = _ssd_mix(
        p_all, jnp.zeros((BATCH, CONV_HIST, CONV_DIM), F32),
        jnp.zeros((BATCH, SSM_HEADS, HEAD, SSM_STATE), F32), ssd_wts, prompt=True,
        cast_weights=[(ffn2_gu[lyr], MIX_CAST_ROWS // 4), (ffn2_dn[lyr], MIX_CAST_ROWS)])

    flat = lambda t: t.reshape(-1, t.shape[-1])
    x2 = _merge(x1, flat(y_rw_f), flat(y_rw_s), flat(y_ssm_f), flat(y_ssm_s), p_all, row(b_gate),
                w_rw_o, w_ssm_o, w_o, row(ln2_g), row(ln2_b))
    y_frames, y_extra = _ffn_ln_out(x2, gu2, dn2, row(ln3_g), row(ln3_b))

    y_prompt = y_frames.reshape(BATCH, SEQ, D_MODEL)
    y_sample = y_extra[:S_ROWS].reshape(DEC_BATCH, DEC_SEQ, D_MODEL)
    conv_of = lambda t: t[:, CONV_HIST - (CONV_W - 1):][None]
    return (y_prompt, y_sample,
            _rw_cols_inv(shift_p)[None], wkv_p[None], conv_of(conv_p), ssm_p[None],
            _rw_cols_inv(shift_s)[None], wkv_s[None], conv_of(conv_s), ssm_s[None])
```

```python
import functools
import math

import jax
import jax.numpy as jnp
from jax import lax
from jax.experimental import pallas as pl
from jax.experimental.pallas import tpu as pltpu

F32 = jnp.float32
BF16 = jnp.bfloat16

D_MODEL = 2048
BATCH = 4
SEQ = 2048
DEC_BATCH = 16
DEC_SEQ = 16
CHUNK = 64
N_META = 16
PAD = (-N_META) % CHUNK
HEAD_ROWS = PAD + N_META
SEQ_CHUNKS = (HEAD_ROWS + SEQ) // CHUNK
F_ROWS = BATCH * SEQ
S_ROWS = DEC_BATCH * DEC_SEQ
ROW_S0 = F_ROWS
ROW_H0 = F_ROWS + S_ROWS
X_ROWS = S_ROWS + BATCH * HEAD_ROWS
M_ROWS = F_ROWS + X_ROWS

HEAD = 64
RW_DIM = D_MODEL // 2
RW_HEADS = RW_DIM // HEAD
RW_DECAY_LORA = 96
RW_AAA_LORA = 96
RW_GATE_LORA = 256
RW_SHIFT_COLS = 3 * RW_DIM + RW_DECAY_LORA + RW_AAA_LORA + RW_GATE_LORA
RW_GN_EPS = 64e-5
SSM_DIM = D_MODEL
SSM_HEADS = SSM_DIM // HEAD
SSM_GROUPS = 4
SSM_HPG = SSM_HEADS // SSM_GROUPS
SSM_STATE = 128
CONV_W = 4
BC_DIM = SSM_GROUPS * SSM_STATE
CONV_DIM = SSM_DIM + 2 * BC_DIM
RMS_EPS = 1e-5
D_FF = 5632
LN_EPS = 1e-5
DEPTH = 1
ALPHA = (2 * DEPTH) ** 0.25

LANES = 128
SUBLANES = 8
VMEM_LIMIT = 56 * 1024 * 1024

LORA_PAD = LANES
RW_OFF_WD = 3 * RW_DIM
RW_OFF_AD = RW_OFF_WD + LORA_PAD
RW_OFF_GD = RW_OFF_AD + LORA_PAD
RW_COLS = RW_OFF_GD + RW_GATE_LORA
DT_COLS = LANES
CONV_HIST = SUBLANES

MIX_CAST_ROWS = 128
FFN_CAST_ROWS = 96
FFN_TM = 512
FFN_TF = 512
MM_TM = M_ROWS // 8
MM_TN = 1024
MERGE_TM = 256

DT_SLOT = 512
PC_RW = 0
PC_DT = PC_RW + RW_COLS
PC_GATE = PC_DT + DT_SLOT
PC_Z = PC_GATE + 2 * D_MODEL
PC_XS = PC_Z + SSM_DIM
PC_BC = PC_XS + SSM_DIM
P_COLS = PC_BC + 2 * BC_DIM
assert PC_DT % DT_COLS == 0 and PC_GATE % D_MODEL == 0 and PC_Z % SSM_DIM == 0
assert PC_XS % SSM_DIM == 0 and PC_BC % (2 * BC_DIM) == 0 and P_COLS % MM_TN == 0
assert F_ROWS % FFN_TM == 0 and X_ROWS == FFN_TM and M_ROWS % MM_TM == 0 and ROW_H0 % MERGE_TM == 0


def _dot(a, b):
    return jnp.dot(a, b, preferred_element_type=F32)


def _dot_nt(a, b):
    return lax.dot_general(a, b, (((1,), (1,)), ((), ())), preferred_element_type=F32)


def _dot_tn(a, b):
    return lax.dot_general(a, b, (((0,), (0,)), ((), ())), preferred_element_type=F32)


def _split2(x):
    hi = x.astype(BF16)
    lo = (x - hi.astype(F32)).astype(BF16)
    return hi, lo


def _dot_sel_r(x, sel):
    hi, lo = _split2(x)
    return _dot(hi, sel) + _dot(lo, sel)


def _dot_sel_l(sel, x):
    hi, lo = _split2(x)
    return _dot(sel, hi) + _dot(sel, lo)


def _causal_blocks(n, blk):
    i = lax.broadcasted_iota(jnp.int32, (n, n), 0)
    j = lax.broadcasted_iota(jnp.int32, (n, n), 1)
    keep = jnp.logical_and(i >= j, jnp.bitwise_xor(i, j) < blk)
    return jnp.where(keep, 1.0, 0.0).astype(BF16)


def _sigmoid(x):
    return jax.nn.sigmoid(x)


def _softplus(x):
    return jnp.maximum(x, 0.0) + jnp.log1p(jnp.exp(-jnp.abs(x)))


def _layer_norm(s, g, b):
    mu = jnp.mean(s, axis=-1, keepdims=True)
    d = s - mu
    var = jnp.mean(d * d, axis=-1, keepdims=True)
    return d * lax.rsqrt(var + LN_EPS) * g + b


class _CastSlot:
    def __init__(self, w, rows, t0):
        assert w.ndim == 2 and w.shape[0] % rows == 0
        self.w, self.rows, self.t0, self.n = w, rows, t0, w.shape[0] // rows

    def spec(self, step_of):
        def index(*ids):
            return jnp.clip(step_of(*ids) - self.t0, 0, self.n - 1), 0
        return pl.BlockSpec((self.rows, self.w.shape[1]), index)

    def out_shape(self):
        return jax.ShapeDtypeStruct(self.w.shape, BF16)


def _run_cast_slots(t, slots, in_refs, out_refs):
    for slot, i_ref, o_ref in zip(slots, in_refs, out_refs, strict=True):
        @pl.when(jnp.logical_and(t >= slot.t0, t < slot.t0 + slot.n))
        def _(i_ref=i_ref, o_ref=o_ref):
            o_ref[...] = i_ref[...].astype(BF16)


def _split_refs(refs, *counts):
    groups, at = [], 0
    for n in counts:
        groups.append(refs[at:at + n])
        at += n
    return (*groups, refs[at:])


def _chain_slots(weights_rows, t0=0):
    slots = []
    for w, rows in weights_rows:
        slots.append(_CastSlot(w, rows, t0))
        t0 += slots[-1].n
    return slots, t0


N_MAIN_TILES = F_ROWS // FFN_TM
N_TILES = M_ROWS // FFN_TM


def _ffn_body(load_x, emit_mid, emit_last, wg_ref, wu_ref, wd_ref, g_ref, b_ref, xb_ref, acc_ref, s_ref):
    i, f = pl.program_id(0), pl.program_id(1)
    last = pl.num_programs(1) - 1
    real = i < N_TILES
    both = jnp.logical_and

    def mlp(xb):
        gate = _dot(xb, wg_ref[...])
        up = _dot(xb, wu_ref[...])
        return _dot((gate * _sigmoid(gate) * up).astype(BF16), wd_ref[...])

    def finish_prev(emit):
        emit(_layer_norm(s_ref[...], g_ref[...], b_ref[...]))

    def first_step(with_prev):
        xb = load_x().astype(BF16)
        xb_ref[...] = xb
        acc_ref[...] = mlp(xb)
        if with_prev:
            finish_prev(emit_mid)

    pl.when(both(f == 0, i == 0))(functools.partial(first_step, False))
    pl.when(both(f == 0, both(i > 0, real)))(functools.partial(first_step, True))
    pl.when(both(f == 0, i == N_TILES))(functools.partial(finish_prev, emit_last))

    @pl.when(both(real, both(f > 0, f < last)))
    def _():
        acc_ref[...] += mlp(xb_ref[...])

    @pl.when(both(real, f == last))
    def _():
        s_ref[...] = ALPHA * load_x() + 0.5 * (acc_ref[...] + mlp(xb_ref[...]))


def _ffn_in_kernel(xm_ref, xe_ref, wg_ref, wu_ref, wd_ref, g_ref, b_ref, *rest, slots):
    cast_in, (o_ref, ob_ref), cast_out, scratch = _split_refs(rest, len(slots), 2, len(slots))
    _run_cast_slots(pl.program_id(0) * pl.num_programs(1) + pl.program_id(1), slots, cast_in, cast_out)
    load_x = lambda: jnp.where(pl.program_id(0) < N_MAIN_TILES, xm_ref[...], xe_ref[...])

    def emit(y):
        o_ref[...] = y
        ob_ref[...] = y.astype(BF16)

    _ffn_body(load_x, emit, emit, wg_ref, wu_ref, wd_ref, g_ref, b_ref, *scratch)


def _ffn_out_kernel(x_ref, wg_ref, wu_ref, wd_ref, g_ref, b_ref, om_ref, oe_ref, *scratch):
    assert N_TILES == N_MAIN_TILES + 1

    def emit_frames(y):
        om_ref[...] = y

    def emit_extra(y):
        oe_ref[...] = y

    _ffn_body(lambda: x_ref[...], emit_frames, emit_extra, wg_ref, wu_ref, wd_ref, g_ref, b_ref, *scratch)


def _ffn_specs(d):
    nf = D_FF // FFN_TF
    step = lambda i, f: jnp.where(i == N_TILES, nf - 1, f)
    return nf, [
        pl.BlockSpec((d, FFN_TF), lambda i, f: (0, step(i, f))),
        pl.BlockSpec((d, FFN_TF), lambda i, f: (0, step(i, f) + nf)),
        pl.BlockSpec((FFN_TF, d), lambda i, f: (step(i, f), 0)),
        pl.BlockSpec((1, d), lambda i, f: (0, 0)),
        pl.BlockSpec((1, d), lambda i, f: (0, 0)),
    ]


_FFN_PARAMS = dict(dimension_semantics=("arbitrary", "arbitrary"), vmem_limit_bytes=VMEM_LIMIT)
_in_frame_tile = lambda i, f: (jnp.minimum(i, N_MAIN_TILES - 1), 0)
_in_any_tile = lambda i, f: (jnp.minimum(i, N_TILES - 1), 0)
_out_frame_tile = lambda i, f: (jnp.clip(i - 1, 0, N_MAIN_TILES - 1), 0)
_out_any_tile = lambda i, f: (jnp.maximum(i - 1, 0), 0)
_only_tile = lambda i, f: (0, 0)


def _ffn_scratch(tile):
    return [pltpu.VMEM(tile, BF16), pltpu.VMEM(tile, F32), pltpu.VMEM(tile, F32)]


def _ffn_ln_in(x_main, x_extra, w_gu, w_dn, ln_g, ln_b, cast_weights):
    d = x_main.shape[1]
    nf, wspecs = _ffn_specs(d)
    tile = (FFN_TM, d)
    grid = (N_TILES + 1, nf)
    slots, t_end = _chain_slots(cast_weights)
    assert t_end <= grid[0] * grid[1]
    cast_specs = [s.spec(lambda i, f: i * nf + f) for s in slots]
    return pl.pallas_call(
        functools.partial(_ffn_in_kernel, slots=slots),
        grid=grid,
        in_specs=[pl.BlockSpec(tile, _in_frame_tile),
                  pl.BlockSpec(tile, _only_tile, pipeline_mode=pl.Buffered(1))] + wspecs + cast_specs,
        out_specs=[pl.BlockSpec(tile, _out_any_tile), pl.BlockSpec(tile, _out_any_tile)] + cast_specs,
        out_shape=[jax.ShapeDtypeStruct((M_ROWS, d), F32), jax.ShapeDtypeStruct((M_ROWS, d), BF16)]
        + [s.out_shape() for s in slots],
        scratch_shapes=_ffn_scratch(tile),
        compiler_params=pltpu.CompilerParams(**_FFN_PARAMS),
        name="ffn_ln_in",
    )(x_main, x_extra, w_gu, w_gu, w_dn, ln_g, ln_b, *[s.w for s in slots])


def _ffn_ln_out(x, w_gu, w_dn, ln_g, ln_b):
    d = x.shape[1]
    nf, wspecs = _ffn_specs(d)
    tile = (FFN_TM, d)
    return pl.pallas_call(
        _ffn_out_kernel,
        grid=(N_TILES + 1, nf),
        in_specs=[pl.BlockSpec(tile, _in_any_tile)] + wspecs,
        out_specs=[pl.BlockSpec(tile, _out_frame_tile), pl.BlockSpec(tile, _only_tile)],
        out_shape=[jax.ShapeDtypeStruct((F_ROWS, d), F32), jax.ShapeDtypeStruct((X_ROWS, d), F32)],
        scratch_shapes=_ffn_scratch(tile),
        compiler_params=pltpu.CompilerParams(**_FFN_PARAMS),
        name="ffn_ln_out",
    )(x, w_gu, w_gu, w_dn, ln_g, ln_b)


def _proj_tiles():
    o_z = RW_SHIFT_COLS
    o_xbc = o_z + SSM_DIM
    o_dt = o_xbc + CONV_DIM
    o_gate = o_dt + SSM_HEADS
    runs = [(PC_RW, 0, 3 * RW_DIM), (PC_GATE, o_gate, 2 * D_MODEL), (PC_Z, o_z, SSM_DIM), (PC_XS, o_xbc, CONV_DIM)]
    src = [None] * (P_COLS // MM_TN)
    for dst0, src0, width in runs:
        assert dst0 % MM_TN == 0 and width % MM_TN == 0 and src0 % (2 * SUBLANES) == 0
        for t in range(width // MM_TN):
            src[dst0 // MM_TN + t] = src0 + t * MM_TN
    assert src.count(None) == 1
    return src


def _mm_kernel(x_ref, wt_ref, wmix_ref, o_ref, *, mixed_tile):
    j = pl.program_id(0)

    @pl.when(j != mixed_tile)
    def _():
        o_ref[...] = _dot_nt(x_ref[...], wt_ref[...])

    @pl.when(j == mixed_tile)
    def _():
        o_ref[...] = _dot_nt(x_ref[...], wmix_ref[...])


def _project(x, wt, wt_mixed):
    m, k = x.shape
    src = _proj_tiles()
    mixed_tile = src.index(None)

    unit = 2 * SUBLANES

    def wt_row(j, i):
        start = 0
        for t, s in enumerate(src):
            if s is not None:
                start = jnp.where(j == t, s // unit, start)
        return start * unit, 0

    return pl.pallas_call(
        functools.partial(_mm_kernel, mixed_tile=mixed_tile),
        grid=(len(src), m // MM_TM),
        in_specs=[pl.BlockSpec((MM_TM, k), lambda j, i: (i, 0)),
                  pl.BlockSpec((pl.Element(MM_TN), pl.Element(k)), wt_row),
                  pl.BlockSpec((MM_TN, k), lambda j, i: (0, 0), pipeline_mode=pl.Buffered(1))],
        out_specs=pl.BlockSpec((MM_TM, MM_TN), lambda j, i: (i, j)),
        out_shape=jax.ShapeDtypeStruct((m, P_COLS), F32),
        compiler_params=pltpu.CompilerParams(
            dimension_semantics=("parallel", "parallel"), vmem_limit_bytes=VMEM_LIMIT),
        name="proj_in",
    )(x, wt, wt_mixed)


def _rwkv_kernel(*refs, chunk, pad, n_seq, n_p, slots, wave):
    p_refs, fixed, cast_in, outs, cast_out, (yacc_ref,) = _split_refs(refs, n_p, 14, len(slots), 3, len(slots))
    (hist_ref, s0_ref, mu_ref, w0_ref, w2_ref, a0_ref, a2_ref, g2_ref, kk_ref, ka_ref, rk_ref,
     gnw_ref, gnb_ref, seg_ref) = fixed
    y_ref, s_ref, shift_ref = outs
    c = pl.program_id(1)
    _run_cast_slots(pl.program_id(0) * pl.num_programs(1) + c, slots, cast_in, cast_out)
    n_rows = n_seq * chunk
    seqs = range(n_seq)
    rsl = [slice(q * chunk, (q + 1) * chunk) for q in seqs]
    rows = lax.broadcasted_iota(jnp.int32, (n_rows, 1), 0)

    @pl.when(c == 0)
    def _():
        shift_ref[...] = hist_ref[...]
        s_ref[...] = s0_ref[...]

    p = jnp.concatenate([r[...] for r in p_refs], axis=0)
    if pad:
        p = jnp.where(jnp.logical_and(c == 0, jnp.bitwise_and(rows, chunk - 1) < pad), 0.0, p)
    prev = pltpu.roll(p, 1, 0)
    for q in seqs:
        prev = jnp.where(rows == q * chunk, shift_ref[q], prev)
    for q in seqs:
        shift_ref[q] = p[(q + 1) * chunk - 1:(q + 1) * chunk, :]
    ps = p + (prev - p) * mu_ref[...]

    r = ps[:, 0:RW_DIM]
    k = ps[:, RW_DIM:2 * RW_DIM]
    v = ps[:, 2 * RW_DIM:3 * RW_DIM]
    wd = ps[:, RW_OFF_WD:RW_OFF_WD + LORA_PAD]
    ad = ps[:, RW_OFF_AD:RW_OFF_AD + LORA_PAD]
    gd = ps[:, RW_OFF_GD:RW_OFF_GD + RW_GATE_LORA]

    lw = -math.exp(-0.5) * _sigmoid(w0_ref[...] + _dot(jnp.tanh(wd).astype(BF16), w2_ref[...]))
    a = _sigmoid(a0_ref[...] + _dot(ad.astype(BF16), a2_ref[...]))
    g = _dot(_sigmoid(gd).astype(BF16), g2_ref[...])

    seg = seg_ref[...]

    def head_sum(x):
        w = seg.shape[0]
        return jnp.concatenate(
            [_dot_sel_r(x[:, j * w:(j + 1) * w], seg) for j in range(RW_DIM // w)], axis=1)

    kk = k * kk_ref[...]
    kk = kk * lax.rsqrt(jnp.maximum(head_sum(kk * kk), 1e-24))
    k = k * (1.0 + (a - 1.0) * ka_ref[...])
    b_neg = -(kk * a)

    cum = _dot_sel_l(_causal_blocks(n_rows, chunk), lw)
    cum_last = [cum[(q + 1) * chunk - 1:(q + 1) * chunk, :] for q in seqs]
    cum_end = jnp.concatenate([jnp.broadcast_to(x, (chunk, RW_DIM)) for x in cum_last], axis=0)
    e_neg = jnp.exp(-cum)
    kt = (kk * jnp.exp(cum - lw)).astype(BF16)
    rt = (r * jnp.exp(cum)).astype(BF16)
    kd = (k * e_neg).astype(BF16)
    bd_neg = (b_neg * e_neg).astype(BF16)
    e_end = jnp.exp(cum_end - cum)
    k_end = (k * e_end).astype(BF16)
    b_end_neg = (b_neg * e_end).astype(BF16)
    p_end = [jnp.exp(x) for x in cum_last]
    vb = v.astype(BF16)

    ri = lax.broadcasted_iota(jnp.int32, (2 * chunk, 2 * chunk), 0)
    rj = lax.broadcasted_iota(jnp.int32, (2 * chunk, 2 * chunk), 1)
    bi = jnp.where(ri >= chunk, ri - chunk + 1, ri)
    bj = jnp.where(rj >= chunk, rj - chunk, rj)
    keep = bi > bj
    ti = lax.broadcasted_iota(jnp.int32, (chunk, 2 * chunk), 0)
    tj = lax.broadcasted_iota(jnp.int32, (chunk, 2 * chunk), 1)
    hi_lane = tj >= chunk
    eye_hi = (tj == ti + chunk).astype(F32)
    zeros_v = jnp.zeros((chunk, HEAD), BF16)

    def run_wave(pairs):
        sls = [(rsl[q], slice(h * HEAD, (h + 1) * HEAD)) for q, h in pairs]
        s_old = [s_ref[q, h] for q, h in pairs]
        s_b = [s.astype(BF16) for s in s_old]
        amats = [jnp.where(keep, _dot_nt(jnp.concatenate([kt[sl], rt[sl]], axis=0),
                                         jnp.concatenate([bd_neg[sl], kd[sl]], axis=0)), 0.0)
                 for sl in sls]
        top = [m[:chunk] for m in amats]
        bot = [m[chunk:].astype(BF16) for m in amats]
        top_b = [x.astype(BF16) for x in top]
        tq = [_dot(xb[:, :chunk], jnp.where(hi_lane, eye_hi, x).astype(BF16)) + eye_hi
              for x, xb in zip(top, top_b)]
        n = 2
        while n < chunk:
            tq = [jnp.where(hi_lane, x, 0.0) + _dot(x[:, :chunk].astype(BF16), x.astype(BF16)) for x in tq]
            n *= 2
        t_b = [x[:, chunk:].astype(BF16) for x in tq]
        v_h = [vb[sl] for sl in sls]
        rhs = [_dot_nt(kt[sl], sb) + _dot(xb, jnp.concatenate([zeros_v, vh], axis=0))
               for sl, sb, xb, vh in zip(sls, s_b, top_b, v_h)]
        ub = [_dot(t, x.astype(BF16)).astype(BF16) for t, x in zip(t_b, rhs)]
        uv = [jnp.concatenate([u, vh], axis=0) for u, vh in zip(ub, v_h)]
        for i, (q, h) in enumerate(pairs):
            sl = sls[i]
            yacc_ref[sl] = _dot_nt(rt[sl], s_b[i]) + _dot(bot[i], uv[i])
            s_ref[q, h] = s_old[i] * p_end[q][:, sl[1]] + _dot_tn(
                uv[i], jnp.concatenate([b_end_neg[sl], k_end[sl]], axis=0))

    all_pairs = [(q, h) for q in seqs for h in range(RW_HEADS)]
    for w0 in range(0, len(all_pairs), wave):
        run_wave(all_pairs[w0:w0 + wave])

    y = yacc_ref[...]
    inv_n = 1.0 / HEAD
    mean = head_sum(y) * inv_n
    yc = y - mean
    var = head_sum(yc * yc) * inv_n
    yn = yc * lax.rsqrt(var + RW_GN_EPS) * gnw_ref[...] + gnb_ref[...]
    bonus = head_sum(r * k * rk_ref[...]) * v
    out = ((yn + bonus) * g).astype(y_ref.dtype)
    for q in seqs:
        y_ref[q] = out[rsl[q]]


RWKV_GROUPS = (4, 8)
RWKV_WAVES = (32, 128)
SSD_GROUPS = (1, 1)


class _SeqPlan:
    def __init__(self, prompt, groups):
        self.prompt = prompt
        if prompt:
            self.n_seq, self.chunk, self.pad, self.total, self.seq_len = groups[0], CHUNK, PAD, BATCH, SEQ
            self.grid = (BATCH // self.n_seq, SEQ_CHUNKS)
        else:
            self.n_seq, self.chunk, self.pad, self.total, self.seq_len = groups[1], DEC_SEQ, 0, DEC_BATCH, DEC_SEQ
            self.grid = (DEC_BATCH // self.n_seq, 1)

    def p_specs(self, width, col):
        cb = col // width
        if not self.prompt:
            rows = self.n_seq * self.chunk
            return [pl.BlockSpec((rows, width), lambda g, c: (ROW_S0 // rows + g, cb))]

        def spec(q):
            def index(g, c):
                seq = g * self.n_seq + q
                return jnp.where(c == 0, ROW_H0 // CHUNK + seq, seq * (SEQ // CHUNK) + c - 1), cb
            return pl.BlockSpec((CHUNK, width), index)
        return [spec(q) for q in range(self.n_seq)]

    def y_spec(self, dim):
        return pl.BlockSpec((self.n_seq, self.chunk, dim), lambda g, c: (g, jnp.maximum(c - 1, 0), 0))

    def y_shape(self, dim):
        return jax.ShapeDtypeStruct((self.total, self.seq_len, dim), BF16)

    def state_spec(self, shape):
        nd = len(shape)
        return pl.BlockSpec((self.n_seq,) + tuple(shape), lambda g, c: (g,) + (0,) * nd)

    def cast_slots(self, cast_weights):
        slots, t_end = _chain_slots(cast_weights)
        assert t_end <= self.grid[0] * self.grid[1]
        specs = [s.spec(lambda g, c: g * self.grid[1] + c) for s in slots]
        return slots, specs, specs


_MIX_PARAMS = dict(dimension_semantics=("arbitrary", "arbitrary"), vmem_limit_bytes=VMEM_LIMIT)


def _rwkv_mix(p_all, hist, s0, wts, *, prompt, cast_weights=()):
    plan = _SeqPlan(prompt, RWKV_GROUPS)
    slots, cast_in_specs, cast_out_specs = plan.cast_slots(cast_weights)
    p_specs = plan.p_specs(RW_COLS, PC_RW)
    const2 = lambda g, c: (0, 0)
    wspecs = [pl.BlockSpec(w.shape, const2) for w in wts]
    shift_spec = plan.state_spec((1, RW_COLS))
    wkv_spec = plan.state_spec((RW_HEADS, HEAD, HEAD))
    return pl.pallas_call(
        functools.partial(_rwkv_kernel, chunk=plan.chunk, pad=plan.pad, n_seq=plan.n_seq, n_p=len(p_specs),
                          slots=slots, wave=RWKV_WAVES[0 if prompt else 1]),
        grid=plan.grid,
        in_specs=p_specs + [shift_spec, wkv_spec] + wspecs + cast_in_specs,
        out_specs=[plan.y_spec(RW_DIM), wkv_spec, shift_spec] + cast_out_specs,
        out_shape=[plan.y_shape(RW_DIM),
                   jax.ShapeDtypeStruct((plan.total, RW_HEADS, HEAD, HEAD), F32),
                   jax.ShapeDtypeStruct((plan.total, 1, RW_COLS), F32)] + [s.out_shape() for s in slots],
        scratch_shapes=[pltpu.VMEM((plan.n_seq * plan.chunk, RW_DIM), F32)],
        compiler_params=pltpu.CompilerParams(**_MIX_PARAMS),
        name="rwkv_mix_c%d" % plan.chunk,
    )(*([p_all] * len(p_specs)), hist, s0, *wts, *[s.w for s in slots])


def _ssd_kernel(*refs, chunk, pad, n_seq, n_p, slots):
    (xs_refs, bc_refs, z_refs, dt_refs, fixed, cast_in, outs, cast_out, (xpad_ref, yacc_ref)) = _split_refs(
        refs, n_p, n_p, n_p, n_p, 9, len(slots), 3, len(slots))
    hist_ref, h0_ref, cw_ref, cb_ref, dtb_ref, alog_ref, dskip_ref, nw_ref, exp_ref = fixed
    y_ref, h_ref, tail_ref = outs
    c = pl.program_id(1)
    _run_cast_slots(pl.program_id(0) * pl.num_programs(1) + c, slots, cast_in, cast_out)
    n_rows = n_seq * chunk
    seqs = range(n_seq)
    rsl = [slice(q * chunk, (q + 1) * chunk) for q in seqs]
    rows = lax.broadcasted_iota(jnp.int32, (n_rows, 1), 0)
    stack = lambda rs: jnp.concatenate([r[...] for r in rs], axis=0)

    @pl.when(c == 0)
    def _():
        xpad_ref[:, 0:CONV_HIST, :] = hist_ref[...]
        h_ref[...] = h0_ref[...]

    u = jnp.concatenate([stack(xs_refs), stack(bc_refs)], axis=1)
    if pad:
        is_pad = jnp.logical_and(c == 0, jnp.bitwise_and(rows, chunk - 1) < pad)
        u = jnp.where(is_pad, 0.0, u)
    convs = []
    for q in seqs:
        u_q = u[rsl[q]]
        xpad_ref[q, CONV_HIST:CONV_HIST + chunk, :] = u_q
        conv = cb_ref[...] + u_q * cw_ref[CONV_W - 1:CONV_W, :]
        for i in range(CONV_W - 1):
            back = CONV_W - 1 - i
            conv = conv + xpad_ref[q, CONV_HIST - back:CONV_HIST - back + chunk, :] * cw_ref[i:i + 1, :]
        hist_next = xpad_ref[q, chunk:chunk + CONV_HIST, :]
        xpad_ref[q, 0:CONV_HIST, :] = hist_next
        tail_ref[q] = hist_next
        convs.append(conv)
    conv = jnp.concatenate(convs, axis=0)
    xbc = conv * _sigmoid(conv)
    xs = xbc[:, 0:SSM_DIM]
    bm = xbc[:, SSM_DIM:SSM_DIM + BC_DIM].astype(BF16)
    cm = xbc[:, SSM_DIM + BC_DIM:CONV_DIM].astype(BF16)

    dt = _softplus(stack(dt_refs) + dtb_ref[...])
    if pad:
        dt = jnp.where(is_pad, 0.0, dt)
    da = dt * (-jnp.exp(alog_ref[...]))
    ci = lax.broadcasted_iota(jnp.int32, (chunk, chunk), 0)
    cj = lax.broadcasted_iota(jnp.int32, (chunk, chunk), 1)
    causal = ci >= cj
    acs = _dot_sel_l(_causal_blocks(n_rows, chunk), da)
    acs_t = acs.T
    acs_end = jnp.concatenate(
        [jnp.broadcast_to(acs[(q + 1) * chunk - 1:(q + 1) * chunk, :], (chunk, DT_COLS)) for q in seqs], axis=0)

    expand = exp_ref[...]
    xdt = xs * _dot_sel_r(dt, expand)
    xdt_b = xdt.astype(BF16)
    xdt_end = (xdt * _dot_sel_r(jnp.exp(acs_end - acs), expand)).astype(BF16)
    e_acs = _dot_sel_r(jnp.exp(acs), expand)
    decay_end = [jnp.exp(acs_t[:, (q + 1) * chunk - 1:(q + 1) * chunk]) for q in seqs]

    gw = SSM_HPG * HEAD
    groups = range(SSM_GROUPS)
    heads = range(SSM_HEADS)
    qg = [(q, g) for q in seqs for g in groups]
    qh = [(q, hd) for q in seqs for hd in heads]
    bm_g = {(q, g): bm[rsl[q], g * SSM_STATE:(g + 1) * SSM_STATE] for q, g in qg}
    cm_g = {(q, g): cm[rsl[q], g * SSM_STATE:(g + 1) * SSM_STATE] for q, g in qg}
    h_old = {(q, hd): h_ref[q, hd] for q, hd in qh}
    cb = {k: _dot_nt(cm_g[k], bm_g[k]) for k in qg}
    y_off = {(q, g): _dot_nt(cm_g[q, g], jnp.concatenate(
        [h_old[q, g * SSM_HPG + hh].astype(BF16) for hh in range(SSM_HPG)], axis=0)) for q, g in qg}
    st = {(q, g): _dot_tn(xdt_end[rsl[q], g * gw:(g + 1) * gw], bm_g[q, g]) for q, g in qg}
    for q, hd in qh:
        sl = slice(hd * HEAD, (hd + 1) * HEAD)
        seg = acs[rsl[q], hd:hd + 1] - acs_t[hd:hd + 1, rsl[q]]
        lmat = jnp.exp(jnp.where(causal, seg, -jnp.inf))
        yacc_ref[rsl[q], sl] = _dot((cb[q, hd // SSM_HPG] * lmat).astype(BF16), xdt_b[rsl[q], sl])
    for q, hd in qh:
        g, hh = divmod(hd, SSM_HPG)
        h_ref[q, hd] = h_old[q, hd] * decay_end[q][hd:hd + 1, :] + st[q, g][hh * HEAD:(hh + 1) * HEAD, :]

    y_off_all = jnp.concatenate(
        [jnp.concatenate([y_off[q, g] for g in groups], axis=1) for q in seqs], axis=0)
    y = yacc_ref[...] + y_off_all * e_acs + xs * dskip_ref[...]
    zz = stack(z_refs)
    y = y * (zz * _sigmoid(zz))
    parts = []
    for g in groups:
        yg = y[:, g * gw:(g + 1) * gw]
        ms = jnp.mean(yg * yg, axis=-1, keepdims=True)
        parts.append(yg * lax.rsqrt(ms + RMS_EPS))
    out = (jnp.concatenate(parts, axis=1) * nw_ref[...]).astype(y_ref.dtype)
    for q in seqs:
        y_ref[q] = out[rsl[q]]


def _ssd_mix(p_all, hist, h0, wts, *, prompt, cast_weights=()):
    plan = _SeqPlan(prompt, SSD_GROUPS)
    slots, cast_in_specs, cast_out_specs = plan.cast_slots(cast_weights)
    p_specs = (plan.p_specs(SSM_DIM, PC_XS) + plan.p_specs(2 * BC_DIM, PC_BC)
               + plan.p_specs(SSM_DIM, PC_Z) + plan.p_specs(DT_COLS, PC_DT))
    const2 = lambda g, c: (0, 0)
    wspecs = [pl.BlockSpec(w.shape, const2) for w in wts]
    conv_spec = plan.state_spec((CONV_HIST, CONV_DIM))
    ssm_spec = plan.state_spec((SSM_HEADS, HEAD, SSM_STATE))
    return pl.pallas_call(
        functools.partial(_ssd_kernel, chunk=plan.chunk, pad=plan.pad, n_seq=plan.n_seq, n_p=len(p_specs) // 4,
                          slots=slots),
        grid=plan.grid,
        in_specs=p_specs + [conv_spec, ssm_spec] + wspecs + cast_in_specs,
        out_specs=[plan.y_spec(SSM_DIM), ssm_spec, conv_spec] + cast_out_specs,
        out_shape=[plan.y_shape(SSM_DIM),
                   jax.ShapeDtypeStruct((plan.total, SSM_HEADS, HEAD, SSM_STATE), F32),
                   jax.ShapeDtypeStruct((plan.total, CONV_HIST, CONV_DIM), F32)] + [s.out_shape() for s in slots],
        scratch_shapes=[pltpu.VMEM((plan.n_seq, CONV_HIST + plan.chunk, CONV_DIM), F32),
                        pltpu.VMEM((plan.n_seq * plan.chunk, SSM_DIM), F32)],
        compiler_params=pltpu.CompilerParams(**_MIX_PARAMS),
        name="ssd_mix_c%d" % plan.chunk,
    )(*([p_all] * len(p_specs)), hist, h0, *wts, *[s.w for s in slots])


MERGE_F_TILES = F_ROWS // MERGE_TM
assert S_ROWS == MERGE_TM


def _merge_kernel(x_ref, yrw_f_ref, yrw_s_ref, yssm_f_ref, yssm_s_ref, ga_ref, gb_ref, bga_ref, bgb_ref,
                  wrw_ref, wssm_ref, wout_ref, g_ref, b_ref, o_ref):
    is_frame = pl.program_id(0) < MERGE_F_TILES
    y_rw = jnp.where(is_frame, yrw_f_ref[...], yrw_s_ref[...])
    y_ssm = jnp.where(is_frame, yssm_f_ref[...], yssm_s_ref[...])
    merged = (_sigmoid(ga_ref[...] + bga_ref[...]) * _dot(y_rw, wrw_ref[...])
              + _sigmoid(gb_ref[...] + bgb_ref[...]) * _dot(y_ssm, wssm_ref[...]))
    s = ALPHA * x_ref[...] + _dot(merged.astype(BF16), wout_ref[...])
    o_ref[...] = _layer_norm(s, g_ref[...], b_ref[...])


def _merge(x, y_rw_f, y_rw_s, y_ssm_f, y_ssm_s, p_all, b_gate, w_rw_out, w_ssm_out, w_out, ln_g, ln_b):
    m, d = x.shape
    tm = MERGE_TM
    row = lambda i: (i, 0)
    frame_row = lambda i: (jnp.minimum(i, MERGE_F_TILES - 1), 0)
    const = lambda i: (0, 0)
    resident = lambda w: pl.BlockSpec(w.shape, const, pipeline_mode=pl.Buffered(1))
    return pl.pallas_call(
        _merge_kernel,
        grid=(m // tm,),
        in_specs=[
            pl.BlockSpec((tm, d), row),
            pl.BlockSpec((tm, RW_DIM), frame_row),
            pl.BlockSpec((tm, RW_DIM), const),
            pl.BlockSpec((tm, SSM_DIM), frame_row),
            pl.BlockSpec((tm, SSM_DIM), const),
            pl.BlockSpec((tm, d), lambda i: (i, PC_GATE // d)),
            pl.BlockSpec((tm, d), lambda i: (i, PC_GATE // d + 1)),
            pl.BlockSpec((1, d), lambda i: (0, 0)),
            pl.BlockSpec((1, d), lambda i: (0, 1)),
            resident(w_rw_out), resident(w_ssm_out), resident(w_out),
            pl.BlockSpec((1, d), const),
            pl.BlockSpec((1, d), const),
        ],
        out_specs=pl.BlockSpec((tm, d), row),
        out_shape=jax.ShapeDtypeStruct((m, d), F32),
        compiler_params=pltpu.CompilerParams(
            dimension_semantics=("arbitrary",), vmem_limit_bytes=VMEM_LIMIT),
        name="merge_ln",
    )(x, y_rw_f, y_rw_s, y_ssm_f, y_ssm_s, p_all, p_all, b_gate, b_gate, w_rw_out, w_ssm_out, w_out,
      ln_g, ln_b)


def _pad_cols(x, width):
    return jnp.concatenate([x, jnp.zeros(x.shape[:-1] + (width - x.shape[-1],), x.dtype)], axis=-1)


def _rw_cols(x):
    o_wd, o_ad, o_gd = 3 * RW_DIM, 3 * RW_DIM + RW_DECAY_LORA, 3 * RW_DIM + RW_DECAY_LORA + RW_AAA_LORA
    return jnp.concatenate([
        x[..., :o_wd],
        _pad_cols(x[..., o_wd:o_ad], LORA_PAD),
        _pad_cols(x[..., o_ad:o_gd], LORA_PAD),
        x[..., o_gd:],
    ], axis=-1)


def _pad_rows(x, height):
    return jnp.concatenate([x, jnp.zeros((height - x.shape[0],) + x.shape[1:], x.dtype)], axis=0)


def _rw_rows(x):
    o_wd, o_ad, o_gd = 3 * RW_DIM, 3 * RW_DIM + RW_DECAY_LORA, 3 * RW_DIM + RW_DECAY_LORA + RW_AAA_LORA
    return jnp.concatenate([
        x[:o_wd], _pad_rows(x[o_wd:o_ad], LORA_PAD), _pad_rows(x[o_ad:o_gd], LORA_PAD), x[o_gd:]], axis=0)


def _rw_cols_inv(x):
    return jnp.concatenate([
        x[..., :RW_OFF_WD],
        x[..., RW_OFF_WD:RW_OFF_WD + RW_DECAY_LORA],
        x[..., RW_OFF_AD:RW_OFF_AD + RW_AAA_LORA],
        x[..., RW_OFF_GD:],
    ], axis=-1)


def _block_ones(n, blk):
    i = jnp.arange(n) // blk
    return (i[:, None] == i[None, :]).astype(BF16)


def kernel(x_prompt, x_sample, state_rwkv_shift, state_wkv, state_conv, state_ssm, meta_tokens, ffn1_gu, ffn1_dn, ln1_g, ln1_b, w_in, b_gate, rw_mu, rw_w0, rw_w2, rw_a0, rw_a2, rw_g2, rw_kk, rw_ka, rw_rk, rw_gn_w, rw_gn_b, conv_w, conv_b, dt_bias, a_log, d_skip, ssm_norm_w, w_rw_out, w_ssm_out, w_out, ln2_g, ln2_b, ffn2_gu, ffn2_dn, ln3_g, ln3_b):
    lyr = 0
    row = lambda t: t[lyr].reshape(1, -1).astype(F32)

    head_rows = jnp.concatenate([jnp.zeros((PAD, D_MODEL), F32), meta_tokens.astype(F32)], axis=0)
    x_extra = jnp.concatenate([x_sample.reshape(S_ROWS, D_MODEL)] + [head_rows] * BATCH, axis=0)
    x1, x1b, wt = _ffn_ln_in(x_prompt.reshape(F_ROWS, D_MODEL), x_extra, ffn1_gu[lyr].astype(BF16),
                             ffn1_dn[lyr].astype(BF16), row(ln1_g), row(ln1_b),
                             [(jnp.swapaxes(w_in[lyr], 0, 1), FFN_CAST_ROWS)])

    o_dt = RW_SHIFT_COLS + SSM_DIM + CONV_DIM
    wt_mixed = jnp.concatenate([
        _rw_rows(wt[:RW_SHIFT_COLS])[3 * RW_DIM:], _pad_rows(wt[o_dt:o_dt + SSM_HEADS], DT_SLOT)], axis=0)
    p_all = _project(x1b, wt, wt_mixed)

    rw_wts = [
        _rw_cols(rw_mu[lyr]).reshape(1, RW_COLS), row(rw_w0),
        _pad_rows(rw_w2[lyr], LORA_PAD).astype(BF16), row(rw_a0),
        _pad_rows(rw_a2[lyr], LORA_PAD).astype(BF16), rw_g2[lyr].astype(BF16),
        row(rw_kk), row(rw_ka), row(rw_rk), row(rw_gn_w), row(rw_gn_b),
        _block_ones(2 * LANES, HEAD),
    ]
    y_rw_s, wkv_s, shift_s = _rwkv_mix(
        p_all, _rw_cols(state_rwkv_shift[lyr]), state_wkv[lyr], rw_wts, prompt=False)
    y_rw_f, wkv_p, shift_p, w_rw_o, w_ssm_o, w_o = _rwkv_mix(
        p_all, jnp.zeros((BATCH, 1, RW_COLS), F32), jnp.zeros((BATCH, RW_HEADS, HEAD, HEAD), F32), rw_wts,
        prompt=True, cast_weights=[(w_rw_out[lyr], 2 * MIX_CAST_ROWS), (w_ssm_out[lyr], 2 * MIX_CAST_ROWS),
                                   (w_out[lyr], 2 * MIX_CAST_ROWS)])

    head_of_lane = jnp.arange(SSM_DIM) // HEAD
    expand = (jnp.arange(DT_COLS)[:, None] == head_of_lane[None, :]).astype(BF16)
    ssd_wts = [
        conv_w[lyr], row(conv_b), _pad_cols(row(dt_bias), DT_COLS), _pad_cols(row(a_log), DT_COLS),
        jnp.repeat(d_skip[lyr], HEAD).reshape(1, SSM_DIM), row(ssm_norm_w),
        expand,
    ]
    hist_rows = lambda t: jnp.pad(t, ((0, 0), (CONV_HIST - (CONV_W - 1), 0), (0, 0)))
    y_ssm_s, ssm_s, conv_s = _ssd_mix(
        p_all, hist_rows(state_conv[lyr]), state_ssm[lyr], ssd_wts, prompt=False)
    y_ssm_f, ssm_p, conv_p, gu2, dn2 = _ssd_mix(
        p_all, jnp.zeros((BATCH, CONV_HIST, CONV_DIM), F32),
        jnp.zeros((BATCH, SSM_HEADS, HEAD, SSM_STATE), F32), ssd_wts, prompt=True,
        cast_weights=[(ffn2_gu[lyr], MIX_CAST_ROWS // 4), (ffn2_dn[lyr], MIX_CAST_ROWS)])

    flat = lambda t: t.reshape(-1, t.shape[-1])
    x2 = _merge(x1, flat(y_rw_f), flat(y_rw_s), flat(y_ssm_f), flat(y_ssm_s), p_all, row(b_gate),
                w_rw_o, w_ssm_o, w_o, row(ln2_g), row(ln2_b))
    y_frames, y_extra = _ffn_ln_out(x2, gu2, dn2, row(ln3_g), row(ln3_b))

    y_prompt = y_frames.reshape(BATCH, SEQ, D_MODEL)
    y_sample = y_extra[:S_ROWS].reshape(DEC_BATCH, DEC_SEQ, D_MODEL)
    conv_of = lambda t: t[:, CONV_HIST - (CONV_W - 1):][None]
    return (y_prompt, y_sample,
            _rw_cols_inv(shift_p)[None], wkv_p[None], conv_of(conv_p), ssm_p[None],
            _rw_cols_inv(shift_s)[None], wkv_s[None], conv_of(conv_s), ssm_s[None])
```
